```python
import math
import jax, jax.numpy as jnp
from jax import lax
import numpy as np

D_MODEL = 1024
BATCH = 8
SEQ = 2048
DEPTH = 1
DEC_BATCH = 128
DEC_SEQ = 4
PAST_LEN = 16384
PAGE_SIZE = 128

D_PLE = 256
EPS = 1e-6
CONV_W = 4
W_LRU = D_MODEL
LRU_BLOCKS = 8
LRU_BW = W_LRU // LRU_BLOCKS
LRU_C = 8.0
D_INNER = 2 * D_MODEL
SSD_HEADDIM = 64
SSD_HEADS = D_INNER // SSD_HEADDIM
SSD_GROUPS = 8
SSD_HPG = SSD_HEADS // SSD_GROUPS
SSD_STATE = 128
SSD_GN = SSD_GROUPS * SSD_STATE
SSD_CONV_DIM = D_INNER + 2 * SSD_GN
SSD_CHUNK = 128
IN_COLS = 2 * W_LRU + D_INNER + SSD_CONV_DIM + SSD_HEADS + 2 * D_MODEL
N_EGROUPS = 4
EXP_PER_GROUP = 4
N_EXPERTS = N_EGROUPS * EXP_PER_GROUP
TOP_K = 2
D_FF_E = 512

kernel_name = 'hybrid_rglru_ssd_hmoe_step'


def grouped_rmsnorm(y, g, groups):
    sh = y.shape
    yf = y.astype(jnp.float32).reshape(sh[:-1] + (groups, sh[-1] // groups))
    yf = yf * lax.rsqrt(jnp.mean(yf * yf, axis=-1, keepdims=True) + EPS)
    return (yf.reshape(sh) * g.astype(jnp.float32)).astype(y.dtype)


def rmsnorm(y, g):
    return grouped_rmsnorm(y, g, 1)


def causal_dwconv(x, buf, w, b):
    L = x.shape[1]
    xx = jnp.concatenate([buf.astype(x.dtype), x], axis=1)
    y = b + sum(xx[:, k:k + L] * w[k] for k in range(CONV_W))
    return y, xx[:, -(CONV_W - 1):]


def rg_lru(x, h0, w_a, b_a, w_x, b_x, lam, reset_first):
    bsz, L, W = x.shape
    xf = x.astype(jnp.float32)
    xb = xf.reshape(bsz, L, LRU_BLOCKS, LRU_BW)
    r = jax.nn.sigmoid(jnp.einsum('blnc,ncd->blnd', xb, w_a.astype(jnp.float32)).reshape(bsz, L, W) + b_a)
    i = jax.nn.sigmoid(jnp.einsum('blnc,ncd->blnd', xb, w_x.astype(jnp.float32)).reshape(bsz, L, W) + b_x)
    log_a = LRU_C * r * jax.nn.log_sigmoid(lam.astype(jnp.float32))
    a = jnp.exp(log_a)
    mult = jnp.sqrt(-jnp.expm1(2.0 * log_a))
    if reset_first:
        reset = (jnp.arange(L) == 0)[None, :, None]
        mult = jnp.where(reset, 1.0, mult)
        a = jnp.where(reset, 0.0, a)
    u = xf * i * mult

    def comb(e1, e2):
        return (e1[0] * e2[0], e2[0] * e1[1] + e2[1])

    a_cum, h_part = lax.associative_scan(comb, (a, u), axis=1)
    h = a_cum * h0.astype(jnp.float32)[:, None, :] + h_part
    return h, h[:, -1]


def ssd_chunked(xs, dt, A, Bm, Cm, s0):
    bsz, L = xs.shape[:2]
    Q = SSD_CHUNK if L % SSD_CHUNK == 0 else L
    nc = L // Q
    x = (xs * dt[..., None]).reshape(bsz, nc, Q, SSD_GROUPS, SSD_HPG, SSD_HEADDIM)
    a = (dt * A).reshape(bsz, nc, Q, SSD_GROUPS, SSD_HPG)
    Bc = Bm.reshape(bsz, nc, Q, SSD_GROUPS, SSD_STATE)
    Cc = Cm.reshape(bsz, nc, Q, SSD_GROUPS, SSD_STATE)
    cum = jnp.cumsum(a, axis=2)
    cum_t = jnp.moveaxis(cum, 2, -1)
    diff = cum_t[..., :, None] - cum_t[..., None, :]
    causal = jnp.tril(jnp.ones((Q, Q), dtype=bool))
    decay = jnp.exp(jnp.where(causal, diff, -jnp.inf))
    cb = jnp.einsum('bcqgn,bcsgn->bcgqs', Cc, Bc)
    y_diag = jnp.einsum('bcgqs,bcghqs,bcsghp->bcqghp', cb, decay, x)
    decay_to_end = jnp.exp(cum[:, :, -1:] - cum)
    chunk_states = jnp.einsum('bcsgn,bcsgh,bcsghp->bcghpn', Bc, decay_to_end, x)
    chunk_decay = jnp.exp(cum[:, :, -1])
    s0g = s0.reshape(bsz, SSD_GROUPS, SSD_HPG, SSD_HEADDIM, SSD_STATE)

    def step(s, inp):
        dec, st = inp
        return dec[..., None, None] * s + st, s

    s_final, s_prev = lax.scan(step, s0g, (jnp.moveaxis(chunk_decay, 1, 0), jnp.moveaxis(chunk_states, 1, 0)))
    s_prev = jnp.moveaxis(s_prev, 0, 1)
    y_off = jnp.einsum('bcqgn,bcghpn,bcqgh->bcqghp', Cc, s_prev, jnp.exp(cum))
    y = (y_diag + y_off).reshape(bsz, L, SSD_HEADS, SSD_HEADDIM)
    return y, s_final.reshape(bsz, SSD_HEADS, SSD_HEADDIM, SSD_STATE)


def hier_moe(h, w_rg, b_rg, w_re, b_re, w1, w3, w2):
    bsz, L, D = h.shape
    t = h.reshape(bsz * L, D)
    g_logits = (t @ w_rg).astype(jnp.float32) + b_rg
    g_idx = jnp.argmax(g_logits, axis=-1)
    g_w = jnp.take_along_axis(jax.nn.softmax(g_logits, axis=-1), g_idx[:, None], axis=-1)[:, 0]
    e_logits = ((t @ w_re).astype(jnp.float32) + b_re).reshape(-1, N_EGROUPS, EXP_PER_GROUP)
    e_sel = jnp.take_along_axis(e_logits, g_idx[:, None, None], axis=1)[:, 0]
    top_v, top_i = lax.top_k(jax.nn.softmax(e_sel, axis=-1), TOP_K)
    w = g_w[:, None] * top_v / jnp.sum(top_v, axis=-1, keepdims=True)
    e_id = g_idx[:, None] * EXP_PER_GROUP + top_i
    combine = jnp.sum(jax.nn.one_hot(e_id, N_EXPERTS, dtype=jnp.float32) * w[..., None], axis=1).astype(h.dtype)
    out = jnp.zeros_like(t)
    for e in range(N_EXPERTS):
        act = jax.nn.silu(t @ w1[e]) * (t @ w3[e])
        out = out + (combine[:, e:e + 1] * act) @ w2[e]
    return out.reshape(bsz, L, D)


def decoder_layer(x, p, lru_h0, lru_buf0, ssd_s0, ssd_buf0, reset_first, lw):
    bsz, L, _ = x.shape
    h = rmsnorm(x, lw['g_mix'])
    proj = h @ lw['w_in']
    o1 = W_LRU
    o2 = o1 + W_LRU
    o3 = o2 + D_INNER
    o4 = o3 + SSD_CONV_DIM
    o5 = o4 + SSD_HEADS
    o6 = o5 + D_MODEL
    lru_in, lru_gate, z = proj[..., :o1], proj[..., o1:o2], proj[..., o2:o3]
    xbc, dt_raw = proj[..., o3:o4], proj[..., o4:o5]
    gate_a, gate_b = proj[..., o5:o6], proj[..., o6:]
    u, lru_buf = causal_dwconv(lru_in, lru_buf0, lw['lru_conv_w'], lw['lru_conv_b'])
    hs, lru_h = rg_lru(u, lru_h0, lw['lru_wa'], lw['lru_ba'], lw['lru_wx'], lw['lru_bx'], lw['lru_lambda'], reset_first)
    y_a = hs.astype(x.dtype) * jax.nn.gelu(lru_gate)
    xbc_c, ssd_buf = causal_dwconv(xbc, ssd_buf0, lw['ssd_conv_w'], lw['ssd_conv_b'])
    xbc_c = jax.nn.silu(xbc_c).astype(jnp.float32)
    xs = xbc_c[..., :D_INNER].reshape(bsz, L, SSD_HEADS, SSD_HEADDIM)
    Bm = xbc_c[..., D_INNER:D_INNER + SSD_GN].reshape(bsz, L, SSD_GROUPS, SSD_STATE)
    Cm = xbc_c[..., D_INNER + SSD_GN:].reshape(bsz, L, SSD_GROUPS, SSD_STATE)
    dt = jax.nn.softplus(dt_raw.astype(jnp.float32) + lw['ssd_dt_bias'])
    A = -jnp.exp(lw['ssd_A_log'].astype(jnp.float32))
    y_ssd, ssd_s = ssd_chunked(xs, dt, A, Bm, Cm, ssd_s0.astype(jnp.float32))
    y_ssd = y_ssd + lw['ssd_D'].astype(jnp.float32)[:, None] * xs
    y_b = y_ssd.reshape(bsz, L, D_INNER) * jax.nn.silu(z.astype(jnp.float32))
    y_b = grouped_rmsnorm(y_b, lw['ssd_norm_g'], SSD_GROUPS).astype(x.dtype)
    merged = jax.nn.sigmoid(gate_a) * (y_a @ lw['w_br_lru']) + jax.nn.sigmoid(gate_b) * (y_b @ lw['w_br_ssd'])
    x = x + merged @ lw['w_out']
    x = x + hier_moe(rmsnorm(x, lw['g_ffn']), lw['w_router_g'], lw['b_router_g'], lw['w_router_e'], lw['b_router_e'], lw['w1'], lw['w3'], lw['w2'])
    e = rmsnorm(p.astype(x.dtype) @ lw['w_ple_proj'], lw['g_ple'])
    x = x + jax.nn.sigmoid(rmsnorm(x, lw['g_ple_gate']) @ lw['w_ple_gate']) * e
    return x, (lru_h, lru_buf, ssd_s, ssd_buf)


def setup_inputs(seed: int = 0) -> dict:
    key = jax.random.key(seed)
    ks = iter(jax.random.split(key, 64))
    f32 = jnp.float32

    def nrm(shape, scale):
        return jax.random.normal(next(ks), shape, f32) * scale

    def gain(shape):
        return 1.0 + 0.02 * jax.random.normal(next(ks), shape, f32)

    def unif(shape, lo, hi):
        return jax.random.uniform(next(ks), shape, f32, lo, hi)

    a0 = unif((DEPTH, W_LRU), 0.9, 0.999)
    s = a0 ** (1.0 / LRU_C)
    lam = jnp.log(s) - jnp.log1p(-s)
    dt0 = jnp.exp(unif((DEPTH, SSD_HEADS), math.log(1e-3), math.log(1e-1)))
    dt_bias = dt0 + jnp.log(-jnp.expm1(-dt0))
    return {
        'x_prompt': nrm((BATCH, SEQ, D_MODEL), 1.0),
        'x_sample': nrm((DEC_BATCH, DEC_SEQ, D_MODEL), 1.0),
        'state_lru_h': nrm((DEPTH, DEC_BATCH, W_LRU), 0.5),
        'state_lru_conv': nrm((DEPTH, DEC_BATCH, CONV_W - 1, W_LRU), 1.0),
        'state_ssd': nrm((DEPTH, DEC_BATCH, SSD_HEADS, SSD_HEADDIM, SSD_STATE), 0.1),
        'state_ssd_conv': nrm((DEPTH, DEC_BATCH, CONV_W - 1, SSD_CONV_DIM), 1.0),
        'p_prompt': nrm((DEPTH, BATCH, SEQ, D_PLE), 1.0),
        'p_sample': nrm((DEPTH, DEC_BATCH, DEC_SEQ, D_PLE), 1.0),
        'g_mix': gain((DEPTH, D_MODEL)),
        'w_in': nrm((DEPTH, D_MODEL, IN_COLS), D_MODEL ** -0.5),
        'lru_conv_w': nrm((DEPTH, CONV_W, W_LRU), 0.5),
        'lru_conv_b': nrm((DEPTH, W_LRU), 0.02),
        'lru_wa': nrm((DEPTH, LRU_BLOCKS, LRU_BW, LRU_BW), LRU_BW ** -0.5),
        'lru_ba': nrm((DEPTH, W_LRU), 0.02),
        'lru_wx': nrm((DEPTH, LRU_BLOCKS, LRU_BW, LRU_BW), LRU_BW ** -0.5),
        'lru_bx': nrm((DEPTH, W_LRU), 0.02),
        'lru_lambda': lam,
        'ssd_conv_w': nrm((DEPTH, CONV_W, SSD_CONV_DIM), 0.5),
        'ssd_conv_b': nrm((DEPTH, SSD_CONV_DIM), 0.02),
        'ssd_dt_bias': dt_bias,
        'ssd_A_log': jnp.log(unif((DEPTH, SSD_HEADS), 1.0, 16.0)),
        'ssd_D': gain((DEPTH, SSD_HEADS)),
        'ssd_norm_g': gain((DEPTH, D_INNER)),
        'w_br_lru': nrm((DEPTH, W_LRU, D_MODEL), W_LRU ** -0.5),
        'w_br_ssd': nrm((DEPTH, D_INNER, D_MODEL), D_INNER ** -0.5),
        'w_out': nrm((DEPTH, D_MODEL, D_MODEL), D_MODEL ** -0.5),
        'g_ffn': gain((DEPTH, D_MODEL)),
        'w_router_g': nrm((DEPTH, D_MODEL, N_EGROUPS), D_MODEL ** -0.5),
        'b_router_g': nrm((DEPTH, N_EGROUPS), 0.01),
        'w_router_e': nrm((DEPTH, D_MODEL, N_EXPERTS), D_MODEL ** -0.5),
        'b_router_e': nrm((DEPTH, N_EXPERTS), 0.01),
        'w1': nrm((DEPTH, N_EXPERTS, D_MODEL, D_FF_E), D_MODEL ** -0.5),
        'w3': nrm((DEPTH, N_EXPERTS, D_MODEL, D_FF_E), D_MODEL ** -0.5),
        'w2': nrm((DEPTH, N_EXPERTS, D_FF_E, D_MODEL), D_FF_E ** -0.5),
        'w_ple_proj': nrm((DEPTH, D_PLE, D_MODEL), D_PLE ** -0.5),
        'g_ple': gain((DEPTH, D_MODEL)),
        'g_ple_gate': gain((DEPTH, D_MODEL)),
        'w_ple_gate': nrm((DEPTH, D_MODEL, D_MODEL), D_MODEL ** -0.5),
        'g_final': gain((D_MODEL,)),
    }


def reference(x_prompt, x_sample, state_lru_h, state_lru_conv, state_ssd, state_ssd_conv, p_prompt, p_sample,
              g_mix, w_in, lru_conv_w, lru_conv_b, lru_wa, lru_ba, lru_wx, lru_bx, lru_lambda,
              ssd_conv_w, ssd_conv_b, ssd_dt_bias, ssd_A_log, ssd_D, ssd_norm_g,
              w_br_lru, w_br_ssd, w_out, g_ffn, w_router_g, b_router_g, w_router_e, b_router_e,
              w1, w3, w2, w_ple_proj, g_ple, g_ple_gate, w_ple_gate, g_final):
    xp, xs = x_prompt, x_sample
    bp = x_prompt.shape[0]
    st_p = ([], [], [], [])
    st_s = ([], [], [], [])
    for i in range(DEPTH):
        lw = {
            'g_mix': g_mix[i], 'w_in': w_in[i], 'lru_conv_w': lru_conv_w[i], 'lru_conv_b': lru_conv_b[i],
            'lru_wa': lru_wa[i], 'lru_ba': lru_ba[i], 'lru_wx': lru_wx[i], 'lru_bx': lru_bx[i],
            'lru_lambda': lru_lambda[i], 'ssd_conv_w': ssd_conv_w[i], 'ssd_conv_b': ssd_conv_b[i],
            'ssd_dt_bias': ssd_dt_bias[i], 'ssd_A_log': ssd_A_log[i], 'ssd_D': ssd_D[i],
            'ssd_norm_g': ssd_norm_g[i], 'w_br_lru': w_br_lru[i], 'w_br_ssd': w_br_ssd[i], 'w_out': w_out[i],
            'g_ffn': g_ffn[i], 'w_router_g': w_router_g[i], 'b_router_g': b_router_g[i],
            'w_router_e': w_router_e[i], 'b_router_e': b_router_e[i], 'w1': w1[i], 'w3': w3[i], 'w2': w2[i],
            'w_ple_proj': w_ple_proj[i], 'g_ple': g_ple[i], 'g_ple_gate': g_ple_gate[i], 'w_ple_gate': w_ple_gate[i],
        }
        z_h = jnp.zeros((bp, W_LRU), jnp.float32)
        z_lb = jnp.zeros((bp, CONV_W - 1, W_LRU), xp.dtype)
        z_s = jnp.zeros((bp, SSD_HEADS, SSD_HEADDIM, SSD_STATE), jnp.float32)
        z_sb = jnp.zeros((bp, CONV_W - 1, SSD_CONV_DIM), xp.dtype)
        xp, sp = decoder_layer(xp, p_prompt[i], z_h, z_lb, z_s, z_sb, True, lw)
        xs, ss = decoder_layer(xs, p_sample[i], state_lru_h[i], state_lru_conv[i], state_ssd[i], state_ssd_conv[i], False, lw)
        for j in range(4):
            st_p[j].append(sp[j])
            st_s[j].append(ss[j])
    y_prompt = rmsnorm(xp, g_final)
    y_sample = rmsnorm(xs, g_final)
    new_lru_h_p = jnp.stack(st_p[0]).astype(state_lru_h.dtype)
    new_lru_conv_p = jnp.stack(st_p[1]).astype(state_lru_conv.dtype)
    new_ssd_p = jnp.stack(st_p[2]).astype(state_ssd.dtype)
    new_ssd_conv_p = jnp.stack(st_p[3]).astype(state_ssd_conv.dtype)
    new_lru_h_s = jnp.stack(st_s[0]).astype(state_lru_h.dtype)
    new_lru_conv_s = jnp.stack(st_s[1]).astype(state_lru_conv.dtype)
    new_ssd_s = jnp.stack(st_s[2]).astype(state_ssd.dtype)
    new_ssd_conv_s = jnp.stack(st_s[3]).astype(state_ssd_conv.dtype)
    return (y_prompt, y_sample, new_lru_h_p, new_lru_conv_p, new_ssd_p, new_ssd_conv_p, new_lru_h_s, new_lru_conv_s, new_ssd_s, new_ssd_conv_s)
```

```python
import functools

import jax
import jax.numpy as jnp
from jax import lax
from jax.experimental import pallas as pl
from jax.experimental.pallas import tpu as pltpu

F32 = jnp.float32
BF16 = jnp.bfloat16

EPS = 1e-6
CONV_W = 4
LRU_BLOCKS = 8
LRU_C = 8.0
SSD_HEADDIM = 64
SSD_GROUPS = 8
SSD_STATE = 128
SSD_CHUNK = 128
N_EGROUPS = 4
EXP_PER_GROUP = 4
N_EXPERTS = N_EGROUPS * EXP_PER_GROUP

LANES = 128
SUBLANES = 8
VMEM_LIMIT = 52 * 1024 * 1024
ROUTER_LANE0 = N_EGROUPS


def _cparams(sem):
    return pltpu.CompilerParams(dimension_semantics=sem, vmem_limit_bytes=VMEM_LIMIT)


def _dot(a, b):
    return jnp.dot(a, b, preferred_element_type=F32)


def _dot_nt(a, b):
    return lax.dot_general(a, b, (((1,), (1,)), ((), ())), preferred_element_type=F32)


def _dot_tn(a, b):
    return lax.dot_general(a, b, (((0,), (0,)), ((), ())), preferred_element_type=F32)


def _dot_f32(a, b):
    return jnp.dot(a, b, precision=lax.Precision.HIGHEST, preferred_element_type=F32)


def _rms(x, g):
    return x * lax.rsqrt(jnp.mean(x * x, axis=-1, keepdims=True) + EPS) * g


def _const_spec(shape):
    nd = len(shape)
    return pl.BlockSpec(shape, lambda *_: (0,) * nd)


def _inproj_body(x_ref, g_ref, w_ref, o_ref, h_scr):
    @pl.when(pl.program_id(1) == 0)
    def _():
        h_scr[...] = _rms(x_ref[...], g_ref[...]).astype(BF16)

    o_ref[...] = _dot(h_scr[...], w_ref[...]).astype(o_ref.dtype)


def _inproj(x, g, w, tm, tn):
    t, d = x.shape
    n = w.shape[1]
    return pl.pallas_call(
        _inproj_body,
        grid=(t // tm, n // tn),
        in_specs=[pl.BlockSpec((tm, d), lambda i, j: (i, 0)),
                  pl.BlockSpec((1, d), lambda i, j: (0, 0)),
                  pl.BlockSpec((d, tn), lambda i, j: (0, j))],
        out_specs=pl.BlockSpec((tm, tn), lambda i, j: (i, j)),
        out_shape=jax.ShapeDtypeStruct((t, n), BF16),
        scratch_shapes=[pltpu.VMEM((tm, d), BF16)],
        compiler_params=_cparams(("parallel", "arbitrary")),
        name="inproj",
    )(x, g, w)


def _lru_gates(u, wax_ref, ba, bx, lam):
    bw = u.shape[1] // LRU_BLOCKS
    r_parts, i_parts = [], []
    for n in range(LRU_BLOCKS):
        ri = _dot(u[:, n * bw:(n + 1) * bw].astype(BF16), wax_ref[n])
        r_parts.append(ri[:, :bw])
        i_parts.append(ri[:, bw:])
    r = jax.nn.sigmoid(jnp.concatenate(r_parts, axis=1) + ba)
    i = jax.nn.sigmoid(jnp.concatenate(i_parts, axis=1) + bx)
    log_a = LRU_C * r * jax.nn.log_sigmoid(lam)
    a = jnp.exp(log_a)
    mult = jnp.sqrt(-jnp.tanh(log_a) * (a * a + 1.0))
    return a, i, mult


def _lru_prompt_body(xin_ref, gate_ref, buf0_ref, h0_ref, cw_ref, cb_ref, wax_ref, ba_ref, bx_ref, lam_ref,
                     ya_ref, hlast_ref, bufout_ref, halo_scr, h_scr, *, reset_first):
    j = pl.program_id(1)
    tt = xin_ref.shape[1]

    @pl.when(j == 0)
    def _():
        halo_scr[...] = jnp.zeros_like(halo_scr)
        halo_scr[SUBLANES - (CONV_W - 1):SUBLANES, :] = buf0_ref[0]
        h_scr[...] = h0_ref[0]

    x = xin_ref[0].astype(F32)
    xpad = jnp.concatenate([halo_scr[...], x], axis=0)
    cw = cw_ref[...]
    u = cb_ref[...] + sum(xpad[SUBLANES - (CONV_W - 1) + k:SUBLANES - (CONV_W - 1) + k + tt] * cw[k:k + 1]
                          for k in range(CONV_W))
    halo_scr[...] = x[tt - SUBLANES:tt]

    a, i, mult = _lru_gates(u, wax_ref, ba_ref[...], bx_ref[...], lam_ref[...])
    row = lax.broadcasted_iota(jnp.int32, a.shape, 0)
    if reset_first:
        first = jnp.logical_and(row == 0, j == 0)
        mult = jnp.where(first, 1.0, mult)
        a = jnp.where(first, 0.0, a)
    v = u * i * mult

    s = 1
    while s < tt:
        keep = row >= s
        v = jnp.where(keep, a * pltpu.roll(v, s, axis=0) + v, v)
        a = jnp.where(keep, a * pltpu.roll(a, s, axis=0), a)
        s *= 2
    h = a * h_scr[...] + v
    h_scr[...] = h[tt - 1:tt]
    ya_ref[0] = (h * jax.nn.gelu(gate_ref[0].astype(F32))).astype(ya_ref.dtype)

    @pl.when(j == pl.num_programs(1) - 1)
    def _():
        hlast_ref[0] = h[tt - 1:tt]
        bufout_ref[0] = x[tt - (CONV_W - 1):tt]


def _lru_prompt(proj3, buf0, h0, lw, tt, reset_first):
    b, l, _ = proj3.shape
    w = h0.shape[-1]
    body = functools.partial(_lru_prompt_body, reset_first=reset_first)
    return pl.pallas_call(
        body,
        grid=(b, l // tt),
        in_specs=[pl.BlockSpec((1, tt, w), lambda i, j: (i, j, 0)),
                  pl.BlockSpec((1, tt, w), lambda i, j: (i, j, 1)),
                  pl.BlockSpec((1, CONV_W - 1, w), lambda i, j: (i, 0, 0)),
                  pl.BlockSpec((1, 1, w), lambda i, j: (i, 0, 0)),
                  _const_spec((CONV_W, w)), _const_spec((1, w)),
                  _const_spec(lw["wax"].shape), _const_spec((1, w)), _const_spec((1, w)), _const_spec((1, w))],
        out_specs=[pl.BlockSpec((1, tt, w), lambda i, j: (i, j, 0)),
                   pl.BlockSpec((1, 1, w), lambda i, j: (i, 0, 0)),
                   pl.BlockSpec((1, CONV_W - 1, w), lambda i, j: (i, 0, 0))],
        out_shape=[jax.ShapeDtypeStruct((b, l, w), BF16),
                   jax.ShapeDtypeStruct((b, 1, w), F32),
                   jax.ShapeDtypeStruct((b, CONV_W - 1, w), F32)],
        scratch_shapes=[pltpu.VMEM((SUBLANES, w), F32), pltpu.VMEM((1, w), F32)],
        compiler_params=_cparams(("parallel", "arbitrary")),
        name="lru_prompt",
    )(proj3, proj3, buf0, h0.reshape(b, 1, w), lw["lru_conv_w"], lw["lru_conv_b"], lw["wax"],
      lw["lru_ba"], lw["lru_bx"], lw["lru_lambda"])


def _lru_step_body(xin_ref, gate_ref, buf_ref, h0_ref, cw_ref, cb_ref, wax_ref, ba_ref, bx_ref, lam_ref,
                   ya_ref, hlast_ref, bufout_ref, *, steps):
    bsz = h0_ref.shape[0]
    n = steps * bsz
    x = xin_ref[...].astype(F32)
    xx = jnp.concatenate([buf_ref[...], x], axis=0)
    cw = cw_ref[...]
    u = cb_ref[...] + sum(xx[k * bsz:k * bsz + n] * cw[k:k + 1] for k in range(CONV_W))
    a, i, mult = _lru_gates(u, wax_ref, ba_ref[...], bx_ref[...], lam_ref[...])
    v = u * i * mult
    h = h0_ref[...]
    for t in range(steps):
        sl = slice(t * bsz, (t + 1) * bsz)
        h = a[sl] * h + v[sl]
        ya_ref[sl, :] = (h * jax.nn.gelu(gate_ref[sl, :].astype(F32))).astype(ya_ref.dtype)
    hlast_ref[...] = h
    bufout_ref[...] = xx[steps * bsz:(steps + CONV_W - 1) * bsz]


def _lru_step(xin_t, gate_t, buf_t, h0, lw, steps):
    bsz, w = h0.shape
    body = functools.partial(_lru_step_body, steps=steps)
    return pl.pallas_call(
        body,
        out_shape=[jax.ShapeDtypeStruct((steps * bsz, w), BF16),
                   jax.ShapeDtypeStruct((bsz, w), F32),
                   jax.ShapeDtypeStruct(((CONV_W - 1) * bsz, w), F32)],
        compiler_params=pltpu.CompilerParams(vmem_limit_bytes=VMEM_LIMIT),
        name="lru_step",
    )(xin_t, gate_t, buf_t, h0, lw["lru_conv_w"], lw["lru_conv_b"], lw["wax"],
      lw["lru_ba"], lw["lru_bx"], lw["lru_lambda"])


def _ssd_chunk(xc, dt, z, p, s_get, s_set, n_valid):
    q = xc.shape[0]
    gn = SSD_GROUPS * SSD_STATE
    d_inner = xc.shape[1] - 2 * gn
    hpg = d_inner // SSD_HEADDIM // SSD_GROUPS
    gw = hpg * SSD_HEADDIM

    row = lax.broadcasted_iota(jnp.int32, (q, q), 0)
    col = lax.broadcasted_iota(jnp.int32, (q, q), 1)
    causal = row >= col
    if n_valid < q:
        dt = jnp.where(lax.broadcasted_iota(jnp.int32, dt.shape, 0) < n_valid, dt, 0.0)
    a = dt * p["A"]
    cum = _dot_f32(causal.astype(F32), a)
    cum_t = cum.T
    dt_t = dt.T
    ecum = jnp.exp(cum)
    w_end = dt * jnp.exp(cum[q - 1:q, :] - cum)
    dec_t = jnp.exp(cum_t[:, q - 1:q])

    xs = xc[:, :d_inner]
    y_groups = []
    for g in range(SSD_GROUPS):
        bg = xc[:, d_inner + g * SSD_STATE:d_inner + (g + 1) * SSD_STATE].astype(BF16)
        cg = xc[:, d_inner + gn + g * SSD_STATE:d_inner + gn + (g + 1) * SSD_STATE].astype(BF16)
        s_old = s_get(g)
        cb = _dot_nt(cg, bg)
        y_off = _dot_nt(cg, s_old.astype(BF16))
        y_heads, xw_heads, s_dec = [], [], []
        for hh in range(hpg):
            h = g * hpg + hh
            x_h = xs[:, h * SSD_HEADDIM:(h + 1) * SSD_HEADDIM]
            decay = jnp.exp(jnp.where(causal, cum[:, h:h + 1] - cum_t[h:h + 1, :], -jnp.inf))
            m = (cb * decay * dt_t[h:h + 1, :]).astype(BF16)
            y_h = _dot(m, x_h.astype(BF16)) + y_off[:, hh * SSD_HEADDIM:(hh + 1) * SSD_HEADDIM] * ecum[:, h:h + 1]
            y_heads.append(y_h)
            xw_heads.append(x_h * w_end[:, h:h + 1])
            s_dec.append(s_old[hh * SSD_HEADDIM:(hh + 1) * SSD_HEADDIM, :] * dec_t[h:h + 1, :])
        xw = jnp.concatenate(xw_heads, axis=1).astype(BF16)
        s_set(g, jnp.concatenate(s_dec, axis=0) + _dot_tn(xw, bg))
        y_groups.append(jnp.concatenate(y_heads, axis=1))

    zf = z.astype(F32)
    out = []
    for g in range(SSD_GROUPS):
        sl = slice(g * gw, (g + 1) * gw)
        y = (y_groups[g] + p["D"][:, sl] * xs[:, sl]) * (zf[:, sl] * jax.nn.sigmoid(zf[:, sl]))
        out.append(y * lax.rsqrt(jnp.mean(y * y, axis=-1, keepdims=True) + EPS) * p["norm_g"][:, sl])
    return jnp.concatenate(out, axis=1)


def _ssd_conv(xpad, q, cw, cb):
    base = SUBLANES - (CONV_W - 1)
    y = cb + sum(xpad[base + k:base + k + q] * cw[k:k + 1] for k in range(CONV_W))
    return y * jax.nn.sigmoid(y)


def _softplus(x):
    return jax.nn.softplus(x)


def _ssd_params(cw_ref, cb_ref, dtb_ref, a_ref, d_ref, ng_ref):
    return {"cw": cw_ref[...], "cb": cb_ref[...], "dt_bias": dtb_ref[...], "A": a_ref[...],
            "D": d_ref[...], "norm_g": ng_ref[...]}


def _ssd_prompt_body(xbc_ref, z_ref, dt_ref, cw_ref, cb_ref, dtb_ref, a_ref, d_ref, ng_ref,
                     yb_ref, sout_ref, bufout_ref, halo_scr, s_scr):
    c = pl.program_id(1)
    q = xbc_ref.shape[1]

    @pl.when(c == 0)
    def _():
        halo_scr[...] = jnp.zeros_like(halo_scr)
        s_scr[...] = jnp.zeros_like(s_scr)

    p = _ssd_params(cw_ref, cb_ref, dtb_ref, a_ref, d_ref, ng_ref)
    x = xbc_ref[0].astype(F32)
    xpad = jnp.concatenate([halo_scr[...], x], axis=0)
    halo_scr[...] = x[q - SUBLANES:q]
    xc = _ssd_conv(xpad, q, p["cw"], p["cb"])
    dt = _softplus(dt_ref[0].astype(F32) + p["dt_bias"])

    def s_set(g, v):
        s_scr[g] = v

    y = _ssd_chunk(xc, dt, z_ref[0], p, lambda g: s_scr[g], s_set, q)
    yb_ref[0] = y.astype(yb_ref.dtype)

    @pl.when(c == pl.num_programs(1) - 1)
    def _():
        sout_ref[0] = s_scr[...]
        bufout_ref[0] = x[q - (CONV_W - 1):q]


def _ssd_prompt(proj3, sp, col_xbc, col_z, col_dt):
    b, l, _ = proj3.shape
    cdim = sp["ssd_conv_w"].shape[1]
    d_inner = sp["ssd_norm_g"].shape[1]
    gw = d_inner // SSD_GROUPS
    q = SSD_CHUNK if l % SSD_CHUNK == 0 else l
    return pl.pallas_call(
        _ssd_prompt_body,
        grid=(b, l // q),
        in_specs=[pl.BlockSpec((1, q, cdim), lambda i, c: (i, c, col_xbc // cdim)),
                  pl.BlockSpec((1, q, d_inner), lambda i, c: (i, c, col_z // d_inner)),
                  pl.BlockSpec((1, q, LANES), lambda i, c: (i, c, col_dt // LANES)),
                  _const_spec((CONV_W, cdim)), _const_spec((1, cdim)), _const_spec((1, LANES)),
                  _const_spec((1, LANES)), _const_spec((1, d_inner)), _const_spec((1, d_inner))],
        out_specs=[pl.BlockSpec((1, q, d_inner), lambda i, c: (i, c, 0)),
                   pl.BlockSpec((1, SSD_GROUPS, gw, SSD_STATE), lambda i, c: (i, 0, 0, 0)),
                   pl.BlockSpec((1, CONV_W - 1, cdim), lambda i, c: (i, 0, 0))],
        out_shape=[jax.ShapeDtypeStruct((b, l, d_inner), BF16),
                   jax.ShapeDtypeStruct((b, SSD_GROUPS, gw, SSD_STATE), F32),
                   jax.ShapeDtypeStruct((b, CONV_W - 1, cdim), F32)],
        scratch_shapes=[pltpu.VMEM((SUBLANES, cdim), F32), pltpu.VMEM((SSD_GROUPS, gw, SSD_STATE), F32)],
        compiler_params=_cparams(("parallel", "arbitrary")),
        name="ssd_prompt",
    )(proj3, proj3, proj3, sp["ssd_conv_w"], sp["ssd_conv_b"], sp["dt_bias"], sp["A"], sp["D"], sp["ssd_norm_g"])


def _ssd_step_body(xbc_ref, z_ref, dt_ref, buf_ref, s_ref, cw_ref, cb_ref, dtb_ref, a_ref, d_ref, ng_ref,
                   yb_ref, sout_ref, bufout_ref, *, steps, nb):
    p = _ssd_params(cw_ref, cb_ref, dtb_ref, a_ref, d_ref, ng_ref)
    q = SUBLANES
    x_all = xbc_ref[...].astype(F32)
    z_all = z_ref[...]
    dt_all = _softplus(dt_ref[...].astype(F32) + p["dt_bias"])
    cdim = x_all.shape[1]
    for j in range(nb):
        x = x_all[j * steps:(j + 1) * steps]
        buf = buf_ref[j]
        xpad = jnp.concatenate([jnp.zeros((SUBLANES - (CONV_W - 1), cdim), F32), buf, x,
                                jnp.zeros((SUBLANES - steps, cdim), F32)], axis=0)
        xc = _ssd_conv(xpad, q, p["cw"], p["cb"])
        pad_rows = jnp.zeros((q - steps, LANES), F32)
        dt = jnp.concatenate([dt_all[j * steps:(j + 1) * steps], pad_rows], axis=0)
        z = jnp.concatenate([z_all[j * steps:(j + 1) * steps],
                             jnp.zeros((q - steps, z_all.shape[1]), z_all.dtype)], axis=0)

        def s_set(g, v, j=j):
            sout_ref[j, g] = v

        y = _ssd_chunk(xc, dt, z, p, lambda g, j=j: s_ref[j, g], s_set, steps)
        yb_ref[j * steps:(j + 1) * steps, :] = y[:steps].astype(yb_ref.dtype)
        bufout_ref[j] = xpad[SUBLANES + steps - (CONV_W - 1):SUBLANES + steps]


def _ssd_step(proj, buf, s0, sp, steps, nb, col_xbc, col_z, col_dt):
    bsz = s0.shape[0]
    cdim = sp["ssd_conv_w"].shape[1]
    d_inner = sp["ssd_norm_g"].shape[1]
    gw = d_inner // SSD_GROUPS
    rows = nb * steps
    body = functools.partial(_ssd_step_body, steps=steps, nb=nb)
    return pl.pallas_call(
        body,
        grid=(bsz // nb,),
        in_specs=[pl.BlockSpec((rows, cdim), lambda i: (i, col_xbc // cdim)),
                  pl.BlockSpec((rows, d_inner), lambda i: (i, col_z // d_inner)),
                  pl.BlockSpec((rows, LANES), lambda i: (i, col_dt // LANES)),
                  pl.BlockSpec((nb, CONV_W - 1, cdim), lambda i: (i, 0, 0)),
                  pl.BlockSpec((nb, SSD_GROUPS, gw, SSD_STATE), lambda i: (i, 0, 0, 0)),
                  _const_spec((CONV_W, cdim)), _const_spec((1, cdim)), _const_spec((1, LANES)),
                  _const_spec((1, LANES)), _const_spec((1, d_inner)), _const_spec((1, d_inner))],
        out_specs=[pl.BlockSpec((rows, d_inner), lambda i: (i, 0)),
                   pl.BlockSpec((nb, SSD_GROUPS, gw, SSD_STATE), lambda i: (i, 0, 0, 0)),
                   pl.BlockSpec((nb, CONV_W - 1, cdim), lambda i: (i, 0, 0))],
        out_shape=[jax.ShapeDtypeStruct((bsz * steps, d_inner), BF16),
                   jax.ShapeDtypeStruct(s0.shape, F32),
                   jax.ShapeDtypeStruct(buf.shape, F32)],
        compiler_params=_cparams(("parallel",)),
        name="ssd_step",
    )(proj, proj, proj, buf, s0, sp["ssd_conv_w"], sp["ssd_conv_b"], sp["dt_bias"], sp["A"], sp["D"],
      sp["ssd_norm_g"])


def _router(t, wr, br):
    logits = _dot_f32(t, wr) + br
    lane = lax.broadcasted_iota(jnp.int32, logits.shape, 1)
    neg = -jnp.inf
    gl = jnp.where(lane < N_EGROUPS, logits, neg)
    gmax = jnp.max(gl, axis=-1, keepdims=True)
    g_idx = jnp.min(jnp.where(gl == gmax, lane, LANES), axis=-1, keepdims=True)
    g_w = 1.0 / jnp.sum(jnp.exp(gl - gmax), axis=-1, keepdims=True)
    in_grp = jnp.logical_and(jnp.logical_and(lane >= ROUTER_LANE0, lane < ROUTER_LANE0 + N_EXPERTS),
                             ((lane - ROUTER_LANE0) >> 2) == g_idx)
    el = jnp.where(in_grp, logits, neg)
    pe = jnp.exp(el - jnp.max(el, axis=-1, keepdims=True))
    pe = pe / jnp.sum(pe, axis=-1, keepdims=True)
    cand = jnp.where(in_grp, pe, -1.0)
    v1 = jnp.max(cand, axis=-1, keepdims=True)
    i1 = jnp.min(jnp.where(cand == v1, lane, LANES), axis=-1, keepdims=True)
    cand2 = jnp.where(lane == i1, -1.0, cand)
    v2 = jnp.max(cand2, axis=-1, keepdims=True)
    i2 = jnp.min(jnp.where(jnp.logical_and(cand2 == v2, in_grp), lane, LANES), axis=-1, keepdims=True)
    den = v1 + v2
    return jnp.where(lane == i1, g_w * v1 / den, 0.0) + jnp.where(lane == i2, g_w * v2 / den, 0.0)


def _merge_body(x_ref, ya_ref, yb_ref, ga_ref, gb_ref, wl_ref, ws_ref, wo_ref, gf_ref, wr_ref, br_ref,
                x1_ref, t_ref, comb_ref):
    a = _dot(ya_ref[...], wl_ref[...])
    b = _dot(yb_ref[...], ws_ref[...])
    merged = jax.nn.sigmoid(ga_ref[...].astype(F32)) * a + jax.nn.sigmoid(gb_ref[...].astype(F32)) * b
    x1 = x_ref[...] + _dot(merged.astype(BF16), wo_ref[...])
    x1_ref[...] = x1
    t = _rms(x1, gf_ref[...])
    t_ref[...] = t.astype(t_ref.dtype)
    comb_ref[...] = _router(t, wr_ref[...], br_ref[...])


def _merge(x, ya, yb, proj, mw, tm, col_ga, col_gb):
    t, d = x.shape
    d_inner = yb.shape[1]
    return pl.pallas_call(
        _merge_body,
        grid=(t // tm,),
        in_specs=[pl.BlockSpec((tm, d), lambda i: (i, 0)),
                  pl.BlockSpec((tm, d), lambda i: (i, 0)),
                  pl.BlockSpec((tm, d_inner), lambda i: (i, 0)),
                  pl.BlockSpec((tm, d), lambda i: (i, col_ga // d)),
                  pl.BlockSpec((tm, d), lambda i: (i, col_gb // d)),
                  _const_spec((d, d)), _const_spec((d_inner, d)), _const_spec((d, d)),
                  _const_spec((1, d)), _const_spec((d, LANES)), _const_spec((1, LANES))],
        out_specs=[pl.BlockSpec((tm, d), lambda i: (i, 0)),
                   pl.BlockSpec((tm, d), lambda i: (i, 0)),
                   pl.BlockSpec((tm, LANES), lambda i: (i, 0))],
        out_shape=[jax.ShapeDtypeStruct((t, d), F32),
                   jax.ShapeDtypeStruct((t, d), BF16),
                   jax.ShapeDtypeStruct((t, LANES), F32)],
        compiler_params=_cparams(("parallel",)),
        name="merge_router",
    )(x, ya, yb, proj, proj, mw["w_br_lru"], mw["w_br_ssd"], mw["w_out"], mw["g_ffn"], mw["w_router"],
      mw["b_router"])


def _moe_body(t_ref, comb_ref, x1_ref, w1_ref, w3_ref, w2_ref, o_ref):
    e = pl.program_id(1)

    @pl.when(e == 0)
    def _():
        o_ref[...] = x1_ref[...]

    t = t_ref[...]
    h1 = _dot(t, w1_ref[0])
    h3 = _dot(t, w3_ref[0])
    comb = comb_ref[...]
    lane = lax.broadcasted_iota(jnp.int32, comb.shape, 1)
    ce = jnp.sum(jnp.where(lane == e + ROUTER_LANE0, comb, 0.0), axis=-1, keepdims=True)
    act = ce * (h1 * jax.nn.sigmoid(h1) * h3)
    o_ref[...] += _dot(act.astype(BF16), w2_ref[0])


def _moe(t, comb, x1, w1, w3, w2, tm):
    n, d = x1.shape
    ne, _, dff = w1.shape
    return pl.pallas_call(
        _moe_body,
        grid=(n // tm, ne),
        in_specs=[pl.BlockSpec((tm, d), lambda i, e: (i, 0)),
                  pl.BlockSpec((tm, LANES), lambda i, e: (i, 0)),
                  pl.BlockSpec((tm, d), lambda i, e: (i, 0)),
                  pl.BlockSpec((1, d, dff), lambda i, e: (e, 0, 0)),
                  pl.BlockSpec((1, d, dff), lambda i, e: (e, 0, 0)),
                  pl.BlockSpec((1, dff, d), lambda i, e: (e, 0, 0))],
        out_specs=pl.BlockSpec((tm, d), lambda i, e: (i, 0)),
        out_shape=jax.ShapeDtypeStruct((n, d), F32),
        compiler_params=_cparams(("parallel", "arbitrary")),
        name="moe_dense",
    )(t, comb, x1, w1, w3, w2)


def _ple_body(x_ref, p_ref, wp_ref, gp_ref, gg_ref, wg_ref, gfin_ref, o_ref):
    x = x_ref[...]
    e = _rms(_dot(p_ref[...].astype(BF16), wp_ref[...]), gp_ref[...])
    gate = jax.nn.sigmoid(_dot(_rms(x, gg_ref[...]).astype(BF16), wg_ref[...]))
    o_ref[...] = _rms(x + gate * e, gfin_ref[...])


def _ple(x2, p, pw, tm):
    n, d = x2.shape
    dp = p.shape[1]
    return pl.pallas_call(
        _ple_body,
        grid=(n // tm,),
        in_specs=[pl.BlockSpec((tm, d), lambda i: (i, 0)),
                  pl.BlockSpec((tm, dp), lambda i: (i, 0)),
                  _const_spec((dp, d)), _const_spec((1, d)), _const_spec((1, d)), _const_spec((d, d)),
                  _const_spec((1, d))],
        out_specs=pl.BlockSpec((tm, d), lambda i: (i, 0)),
        out_shape=jax.ShapeDtypeStruct((n, d), F32),
        compiler_params=_cparams(("parallel",)),
        name="ple_final",
    )(x2, p, pw["w_ple_proj"], pw["g_ple"], pw["g_ple_gate"], pw["w_ple_gate"], pw["g_final"])


def _pick_tile(n, pref):
    t = min(n, pref)
    while n % t:
        t //= 2
    return t


def _token_tail(x1, t, comb, p, lw, tm):
    x2 = _moe(t, comb, x1, lw["w1"], lw["w3"], lw["w2"], tm)
    return _ple(x2, p, lw, tm)


def kernel(x_prompt, x_sample, state_lru_h, state_lru_conv, state_ssd, state_ssd_conv, p_prompt, p_sample, g_mix, w_in, lru_conv_w, lru_conv_b, lru_wa, lru_ba, lru_wx, lru_bx, lru_lambda, ssd_conv_w, ssd_conv_b, ssd_dt_bias, ssd_A_log, ssd_D, ssd_norm_g, w_br_lru, w_br_ssd, w_out, g_ffn, w_router_g, b_router_g, w_router_e, b_router_e, w1, w3, w2, w_ple_proj, g_ple, g_ple_gate, w_ple_gate, g_final):
    depth = w_in.shape[0]
    assert depth == 1, "one decoder layer per call"
    bp, lp, d = x_prompt.shape
    bs, ls, _ = x_sample.shape
    w_lru = state_lru_h.shape[-1]
    heads, hdim, nstate = state_ssd.shape[2:]
    d_inner = heads * hdim
    cdim = state_ssd_conv.shape[-1]
    assert hdim == SSD_HEADDIM and nstate == SSD_STATE and heads <= LANES and ls < SUBLANES
    gw = d_inner // SSD_GROUPS

    o_dt = 2 * w_lru + d_inner + cdim
    wi = w_in[0]
    w_proj = jnp.concatenate(
        [wi[:, :o_dt], wi[:, o_dt + heads:], wi[:, o_dt:o_dt + heads], jnp.zeros((d, LANES - heads), wi.dtype)],
        axis=1).astype(BF16)
    col_z, col_xbc = 2 * w_lru, 2 * w_lru + d_inner
    col_ga, col_gb, col_dt = o_dt, o_dt + d, o_dt + 2 * d
    n_proj = w_proj.shape[1]
    row = lambda v: v.reshape(1, -1).astype(F32)
    pad_heads = lambda v: jnp.pad(v.astype(F32), (0, LANES - heads)).reshape(1, LANES)
    lw = {
        "lru_conv_w": lru_conv_w[0], "lru_conv_b": row(lru_conv_b[0]),
        "wax": jnp.concatenate([lru_wa[0], lru_wx[0]], axis=-1).astype(BF16),
        "lru_ba": row(lru_ba[0]), "lru_bx": row(lru_bx[0]), "lru_lambda": row(lru_lambda[0]),
        "ssd_conv_w": ssd_conv_w[0], "ssd_conv_b": row(ssd_conv_b[0]),
        "dt_bias": pad_heads(ssd_dt_bias[0]), "A": pad_heads(-jnp.exp(ssd_A_log[0].astype(F32))),
        "D": row(jnp.repeat(ssd_D[0], hdim)), "ssd_norm_g": row(ssd_norm_g[0]),
        "w_br_lru": w_br_lru[0].astype(BF16), "w_br_ssd": w_br_ssd[0].astype(BF16), "w_out": w_out[0].astype(BF16),
        "g_ffn": row(g_ffn[0]),
        "w_router": jnp.concatenate([w_router_g[0], w_router_e[0],
                                     jnp.zeros((d, LANES - N_EGROUPS - N_EXPERTS), F32)], axis=1),
        "b_router": jnp.concatenate([b_router_g[0], b_router_e[0],
                                     jnp.zeros((LANES - N_EGROUPS - N_EXPERTS,), F32)]).reshape(1, LANES),
        "w1": w1[0].astype(BF16), "w3": w3[0].astype(BF16), "w2": w2[0].astype(BF16),
        "w_ple_proj": w_ple_proj[0].astype(BF16), "g_ple": row(g_ple[0]), "g_ple_gate": row(g_ple_gate[0]),
        "w_ple_gate": w_ple_gate[0].astype(BF16), "g_final": row(g_final),
    }
    g_mix_r = row(g_mix[0])
    tn = n_proj // 9 if n_proj % (9 * LANES) == 0 else LANES

    tp = bp * lp
    xp = x_prompt.reshape(tp, d)
    tm_p = _pick_tile(tp, 1024)
    proj_p = _inproj(xp, g_mix_r, w_proj, tm_p, tn)
    proj_p3 = proj_p.reshape(bp, lp, n_proj)
    ya_p, hl_p, lbuf_p = _lru_prompt(proj_p3, jnp.zeros((bp, CONV_W - 1, w_lru), F32), jnp.zeros((bp, w_lru), F32),
                                     lw, _pick_tile(lp, 256), True)
    yb_p, s_p, sbuf_p = _ssd_prompt(proj_p3, lw, col_xbc, col_z, col_dt)
    x1_p, t_p, comb_p = _merge(xp, ya_p.reshape(tp, w_lru), yb_p.reshape(tp, d_inner), proj_p, lw,
                               _pick_tile(tp, 512), col_ga, col_gb)
    y_p = _token_tail(x1_p, t_p, comb_p, p_prompt[0].reshape(tp, -1), lw, tm_p)

    ts = bs * ls
    xs = x_sample.reshape(ts, d)
    tm_s = _pick_tile(ts, 512)
    proj_s = _inproj(xs, g_mix_r, w_proj, tm_s, tn)
    to_tmajor = lambda v, n: v.reshape(bs, n, -1).transpose(1, 0, 2).reshape(n * bs, -1)
    from_tmajor = lambda v, n: v.reshape(n, bs, -1).transpose(1, 0, 2)
    ya_t, hl_s, lbuf_t = _lru_step(to_tmajor(proj_s[:, :w_lru], ls), to_tmajor(proj_s[:, w_lru:2 * w_lru], ls),
                                   to_tmajor(state_lru_conv[0], CONV_W - 1), state_lru_h[0], lw, ls)
    ya_s = from_tmajor(ya_t, ls).reshape(ts, w_lru)
    lbuf_s = from_tmajor(lbuf_t, CONV_W - 1)
    yb_s, s_s, sbuf_s = _ssd_step(proj_s, state_ssd_conv[0], state_ssd[0].reshape(bs, SSD_GROUPS, gw, nstate), lw,
                                  ls, _pick_tile(bs, 4), col_xbc, col_z, col_dt)
    x1_s, t_s, comb_s = _merge(xs, ya_s, yb_s, proj_s, lw, tm_s, col_ga, col_gb)
    y_s = _token_tail(x1_s, t_s, comb_s, p_sample[0].reshape(ts, -1), lw, tm_s)

    return (y_p.reshape(bp, lp, d), y_s.reshape(bs, ls, d),
            hl_p.reshape(1, bp, w_lru), lbuf_p[None],
            s_p.reshape(1, bp, heads, hdim, nstate), sbuf_p[None],
            hl_s[None], lbuf_s[None],
            s_s.reshape(1, bs, heads, hdim, nstate), sbuf_s[None])
```

```python
import functools

import jax
import jax.numpy as jnp
from jax import lax
from jax.experimental import pallas as pl
from jax.experimental.pallas import tpu as pltpu

F32 = jnp.float32
BF16 = jnp.bfloat16

EPS = 1e-6
CONV_W = 4
LRU_BLOCKS = 8
LRU_C = 8.0
SSD_HEADDIM = 64
SSD_GROUPS = 8
SSD_STATE = 128
SSD_CHUNK = 128
N_EGROUPS = 4
EXP_PER_GROUP = 4
N_EXPERTS = N_EGROUPS * EXP_PER_GROUP

LANES = 128
SUBLANES = 8
VMEM_LIMIT = 52 * 1024 * 1024
ROUTER_LANE0 = N_EGROUPS


def _cparams(sem):
    return pltpu.CompilerParams(dimension_semantics=sem, vmem_limit_bytes=VMEM_LIMIT)


def _dot(a, b):
    return jnp.dot(a, b, preferred_element_type=F32)


def _dot_nt(a, b):
    return lax.dot_general(a, b, (((1,), (1,)), ((), ())), preferred_element_type=F32)


def _dot_tn(a, b):
    return lax.dot_general(a, b, (((0,), (0,)), ((), ())), preferred_element_type=F32)


def _dot_f32(a, b):
    return jnp.dot(a, b, precision=lax.Precision.HIGHEST, preferred_element_type=F32)


def _rms(x, g):
    return x * lax.rsqrt(jnp.mean(x * x, axis=-1, keepdims=True) + EPS) * g


def _const_spec(shape):
    nd = len(shape)
    return pl.BlockSpec(shape, lambda *_: (0,) * nd)


def _inproj_body(x_ref, g_ref, w_ref, o_ref, h_scr):
    @pl.when(pl.program_id(1) == 0)
    def _():
        h_scr[...] = _rms(x_ref[...], g_ref[...]).astype(BF16)

    o_ref[...] = _dot(h_scr[...], w_ref[...]).astype(o_ref.dtype)


def _inproj(x, g, w, tm, tn):
    t, d = x.shape
    n = w.shape[1]
    return pl.pallas_call(
        _inproj_body,
        grid=(t // tm, n // tn),
        in_specs=[pl.BlockSpec((tm, d), lambda i, j: (i, 0)),
                  pl.BlockSpec((1, d), lambda i, j: (0, 0)),
                  pl.BlockSpec((d, tn), lambda i, j: (0, j))],
        out_specs=pl.BlockSpec((tm, tn), lambda i, j: (i, j)),
        out_shape=jax.ShapeDtypeStruct((t, n), BF16),
        scratch_shapes=[pltpu.VMEM((tm, d), BF16)],
        compiler_params=_cparams(("parallel", "arbitrary")),
        name="inproj",
    )(x, g, w)


def _lru_gates(u, wax_ref, ba, bx, lam):
    bw = u.shape[1] // LRU_BLOCKS
    r_parts, i_parts = [], []
    for n in range(LRU_BLOCKS):
        ri = _dot(u[:, n * bw:(n + 1) * bw].astype(BF16), wax_ref[n])
        r_parts.append(ri[:, :bw])
        i_parts.append(ri[:, bw:])
    r = jax.nn.sigmoid(jnp.concatenate(r_parts, axis=1) + ba)
    i = jax.nn.sigmoid(jnp.concatenate(i_parts, axis=1) + bx)
    log_a = LRU_C * r * jax.nn.log_sigmoid(lam)
    a = jnp.exp(log_a)
    mult = jnp.sqrt(-jnp.tanh(log_a) * (a * a + 1.0))
    return a, i, mult


def _lru_prompt_body(xin_ref, gate_ref, buf0_ref, h0_ref, cw_ref, cb_ref, wax_ref, ba_ref, bx_ref, lam_ref,
                     ya_ref, hlast_ref, bufout_ref, halo_scr, h_scr, *, reset_first):
    j = pl.program_id(1)
    tt = xin_ref.shape[1]

    @pl.when(j == 0)
    def _():
        halo_scr[...] = jnp.zeros_like(halo_scr)
        halo_scr[SUBLANES - (CONV_W - 1):SUBLANES, :] = buf0_ref[0]
        h_scr[...] = h0_ref[0]

    x = xin_ref[0].astype(F32)
    xpad = jnp.concatenate([halo_scr[...], x], axis=0)
    cw = cw_ref[...]
    u = cb_ref[...] + sum(xpad[SUBLANES - (CONV_W - 1) + k:SUBLANES - (CONV_W - 1) + k + tt] * cw[k:k + 1]
                          for k in range(CONV_W))
    halo_scr[...] = x[tt - SUBLANES:tt]

    a, i, mult = _lru_gates(u, wax_ref, ba_ref[...], bx_ref[...], lam_ref[...])
    row = lax.broadcasted_iota(jnp.int32, a.shape, 0)
    if reset_first:
        first = jnp.logical_and(row == 0, j == 0)
        mult = jnp.where(first, 1.0, mult)
        a = jnp.where(first, 0.0, a)
    v = u * i * mult

    s = 1
    while s < tt:
        keep = row >= s
        v = jnp.where(keep, a * pltpu.roll(v, s, axis=0) + v, v)
        a = jnp.where(keep, a * pltpu.roll(a, s, axis=0), a)
        s *= 2
    h = a * h_scr[...] + v
    h_scr[...] = h[tt - 1:tt]
    ya_ref[0] = (h * jax.nn.gelu(gate_ref[0].astype(F32))).astype(ya_ref.dtype)

    @pl.when(j == pl.num_programs(1) - 1)
    def _():
        hlast_ref[0] = h[tt - 1:tt]
        bufout_ref[0] = x[tt - (CONV_W - 1):tt]


def _lru_prompt(proj3, buf0, h0, lw, tt, reset_first):
    b, l, _ = proj3.shape
    w = h0.shape[-1]
    body = functools.partial(_lru_prompt_body, reset_first=reset_first)
    return pl.pallas_call(
        body,
        grid=(b, l // tt),
        in_specs=[pl.BlockSpec((1, tt, w), lambda i, j: (i, j, 0)),
                  pl.BlockSpec((1, tt, w), lambda i, j: (i, j, 1)),
                  pl.BlockSpec((1, CONV_W - 1, w), lambda i, j: (i, 0, 0)),
                  pl.BlockSpec((1, 1, w), lambda i, j: (i, 0, 0)),
                  _const_spec((CONV_W, w)), _const_spec((1, w)),
                  _const_spec(lw["wax"].shape), _const_spec((1, w)), _const_spec((1, w)), _const_spec((1, w))],
        out_specs=[pl.BlockSpec((1, tt, w), lambda i, j: (i, j, 0)),
                   pl.BlockSpec((1, 1, w), lambda i, j: (i, 0, 0)),
                   pl.BlockSpec((1, CONV_W - 1, w), lambda i, j: (i, 0, 0))],
        out_shape=[jax.ShapeDtypeStruct((b, l, w), BF16),
                   jax.ShapeDtypeStruct((b, 1, w), F32),
                   jax.ShapeDtypeStruct((b, CONV_W - 1, w), F32)],
        scratch_shapes=[pltpu.VMEM((SUBLANES, w), F32), pltpu.VMEM((1, w), F32)],
        compiler_params=_cparams(("parallel", "arbitrary")),
        name="lru_prompt",
    )(proj3, proj3, buf0, h0.reshape(b, 1, w), lw["lru_conv_w"], lw["lru_conv_b"], lw["wax"],
      lw["lru_ba"], lw["lru_bx"], lw["lru_lambda"])


def _lru_step_body(xin_ref, gate_ref, buf_ref, h0_ref, cw_ref, cb_ref, wax_ref, ba_ref, bx_ref, lam_ref,
                   ya_ref, hlast_ref, bufout_ref, *, steps):
    bsz = h0_ref.shape[0]
    n = steps * bsz
    x = xin_ref[...].astype(F32)
    xx = jnp.concatenate([buf_ref[...], x], axis=0)
    cw = cw_ref[...]
    u = cb_ref[...] + sum(xx[k * bsz:k * bsz + n] * cw[k:k + 1] for k in range(CONV_W))
    a, i, mult = _lru_gates(u, wax_ref, ba_ref[...], bx_ref[...], lam_ref[...])
    v = u * i * mult
    h = h0_ref[...]
    for t in range(steps):
        sl = slice(t * bsz, (t + 1) * bsz)
        h = a[sl] * h + v[sl]
        ya_ref[sl, :] = (h * jax.nn.gelu(gate_ref[sl, :].astype(F32))).astype(ya_ref.dtype)
    hlast_ref[...] = h
    bufout_ref[...] = xx[steps * bsz:(steps + CONV_W - 1) * bsz]


def _lru_step(xin_t, gate_t, buf_t, h0, lw, steps):
    bsz, w = h0.shape
    body = functools.partial(_lru_step_body, steps=steps)
    return pl.pallas_call(
        body,
        out_shape=[jax.ShapeDtypeStruct((steps * bsz, w), BF16),
                   jax.ShapeDtypeStruct((bsz, w), F32),
                   jax.ShapeDtypeStruct(((CONV_W - 1) * bsz, w), F32)],
        compiler_params=pltpu.CompilerParams(vmem_limit_bytes=VMEM_LIMIT),
        name="lru_step",
    )(xin_t, gate_t, buf_t, h0, lw["lru_conv_w"], lw["lru_conv_b"], lw["wax"],
      lw["lru_ba"], lw["lru_bx"], lw["lru_lambda"])


def _ssd_chunk(xc, dt, z, p, s_get, s_set, n_valid):
    q = xc.shape[0]
    gn = SSD_GROUPS * SSD_STATE
    d_inner = xc.shape[1] - 2 * gn
    hpg = d_inner // SSD_HEADDIM // SSD_GROUPS
    gw = hpg * SSD_HEADDIM

    row = lax.broadcasted_iota(jnp.int32, (q, q), 0)
    col = lax.broadcasted_iota(jnp.int32, (q, q), 1)
    causal = row >= col
    if n_valid < q:
        dt = jnp.where(lax.broadcasted_iota(jnp.int32, dt.shape, 0) < n_valid, dt, 0.0)
    a = dt * p["A"]
    cum = _dot_f32(causal.astype(F32), a)
    cum_t = cum.T
    dt_t = dt.T
    ecum = jnp.exp(cum)
    w_end = dt * jnp.exp(cum[q - 1:q, :] - cum)
    dec_t = jnp.exp(cum_t[:, q - 1:q])

    xs = xc[:, :d_inner]
    y_groups = []
    for g in range(SSD_GROUPS):
        bg = xc[:, d_inner + g * SSD_STATE:d_inner + (g + 1) * SSD_STATE].astype(BF16)
        cg = xc[:, d_inner + gn + g * SSD_STATE:d_inner + gn + (g + 1) * SSD_STATE].astype(BF16)
        s_old = s_get(g)
        cb = _dot_nt(cg, bg)
        y_off = _dot_nt(cg, s_old.astype(BF16))
        y_heads, xw_heads, s_dec = [], [], []
        for hh in range(hpg):
            h = g * hpg + hh
            x_h = xs[:, h * SSD_HEADDIM:(h + 1) * SSD_HEADDIM]
            decay = jnp.exp(jnp.where(causal, cum[:, h:h + 1] - cum_t[h:h + 1, :], -jnp.inf))
            m = (cb * decay * dt_t[h:h + 1, :]).astype(BF16)
            y_h = _dot(m, x_h.astype(BF16)) + y_off[:, hh * SSD_HEADDIM:(hh + 1) * SSD_HEADDIM] * ecum[:, h:h + 1]
            y_heads.append(y_h)
            xw_heads.append(x_h * w_end[:, h:h + 1])
            s_dec.append(s_old[hh * SSD_HEADDIM:(hh + 1) * SSD_HEADDIM, :] * dec_t[h:h + 1, :])
        xw = jnp.concatenate(xw_heads, axis=1).astype(BF16)
        s_set(g, jnp.concatenate(s_dec, axis=0) + _dot_tn(xw, bg))
        y_groups.append(jnp.concatenate(y_heads, axis=1))

    zf = z.astype(F32)
    out = []
    for g in range(SSD_GROUPS):
        sl = slice(g * gw, (g + 1) * gw)
        y = (y_groups[g] + p["D"][:, sl] * xs[:, sl]) * (zf[:, sl] * jax.nn.sigmoid(zf[:, sl]))
        out.append(y * lax.rsqrt(jnp.mean(y * y, axis=-1, keepdims=True) + EPS) * p["norm_g"][:, sl])
    return jnp.concatenate(out, axis=1)


def _ssd_conv(xpad, q, cw, cb):
    base = SUBLANES - (CONV_W - 1)
    y = cb + sum(xpad[base + k:base + k + q] * cw[k:k + 1] for k in range(CONV_W))
    return y * jax.nn.sigmoid(y)


def _softplus(x):
    return jax.nn.softplus(x)


def _ssd_params(cw_ref, cb_ref, dtb_ref, a_ref, d_ref, ng_ref):
    return {"cw": cw_ref[...], "cb": cb_ref[...], "dt_bias": dtb_ref[...], "A": a_ref[...],
            "D": d_ref[...], "norm_g": ng_ref[...]}


def _ssd_prompt_body(xbc_ref, z_ref, dt_ref, cw_ref, cb_ref, dtb_ref, a_ref, d_ref, ng_ref,
                     yb_ref, sout_ref, bufout_ref, halo_scr, s_scr):
    c = pl.program_id(1)
    q = xbc_ref.shape[1]

    @pl.when(c == 0)
    def _():
        halo_scr[...] = jnp.zeros_like(halo_scr)
        s_scr[...] = jnp.zeros_like(s_scr)

    p = _ssd_params(cw_ref, cb_ref, dtb_ref, a_ref, d_ref, ng_ref)
    x = xbc_ref[0].astype(F32)
    xpad = jnp.concatenate([halo_scr[...], x], axis=0)
    halo_scr[...] = x[q - SUBLANES:q]
    xc = _ssd_conv(xpad, q, p["cw"], p["cb"])
    dt = _softplus(dt_ref[0].astype(F32) + p["dt_bias"])

    def s_set(g, v):
        s_scr[g] = v

    y = _ssd_chunk(xc, dt, z_ref[0], p, lambda g: s_scr[g], s_set, q)
    yb_ref[0] = y.astype(yb_ref.dtype)

    @pl.when(c == pl.num_programs(1) - 1)
    def _():
        sout_ref[0] = s_scr[...]
        bufout_ref[0] = x[q - (CONV_W - 1):q]


def _ssd_prompt(proj3, sp, col_xbc, col_z, col_dt):
    b, l, _ = proj3.shape
    cdim = sp["ssd_conv_w"].shape[1]
    d_inner = sp["ssd_norm_g"].shape[1]
    gw = d_inner // SSD_GROUPS
    q = SSD_CHUNK if l % SSD_CHUNK == 0 else l
    return pl.pallas_call(
        _ssd_prompt_body,
        grid=(b, l // q),
        in_specs=[pl.BlockSpec((1, q, cdim), lambda i, c: (i, c, col_xbc // cdim)),
                  pl.BlockSpec((1, q, d_inner), lambda i, c: (i, c, col_z // d_inner)),
                  pl.BlockSpec((1, q, LANES), lambda i, c: (i, c, col_dt // LANES)),
                  _const_spec((CONV_W, cdim)), _const_spec((1, cdim)), _const_spec((1, LANES)),
                  _const_spec((1, LANES)), _const_spec((1, d_inner)), _const_spec((1, d_inner))],
        out_specs=[pl.BlockSpec((1, q, d_inner), lambda i, c: (i, c, 0)),
                   pl.BlockSpec((1, SSD_GROUPS, gw, SSD_STATE), lambda i, c: (i, 0, 0, 0)),
                   pl.BlockSpec((1, CONV_W - 1, cdim), lambda i, c: (i, 0, 0))],
        out_shape=[jax.ShapeDtypeStruct((b, l, d_inner), BF16),
                   jax.ShapeDtypeStruct((b, SSD_GROUPS, gw, SSD_STATE), F32),
                   jax.ShapeDtypeStruct((b, CONV_W - 1, cdim), F32)],
        scratch_shapes=[pltpu.VMEM((SUBLANES, cdim), F32), pltpu.VMEM((SSD_GROUPS, gw, SSD_STATE), F32)],
        compiler_params=_cparams(("parallel", "arbitrary")),
        name="ssd_prompt",
    )(proj3, proj3, proj3, sp["ssd_conv_w"], sp["ssd_conv_b"], sp["dt_bias"], sp["A"], sp["D"], sp["ssd_norm_g"])


def _ssd_step_body(xbc_ref, z_ref, dt_ref, buf_ref, s_ref, cw_ref, cb_ref, dtb_ref, a_ref, d_ref, ng_ref,
                   yb_ref, sout_ref, bufout_ref, *, steps, nb):
    p = _ssd_params(cw_ref, cb_ref, dtb_ref, a_ref, d_ref, ng_ref)
    q = SUBLANES
    x_all = xbc_ref[...].astype(F32)
    z_all = z_ref[...]
    dt_all = _softplus(dt_ref[...].astype(F32) + p["dt_bias"])
    cdim = x_all.shape[1]
    for j in range(nb):
        x = x_all[j * steps:(j + 1) * steps]
        buf = buf_ref[j]
        xpad = jnp.concatenate([jnp.zeros((SUBLANES - (CONV_W - 1), cdim), F32), buf, x,
                                jnp.zeros((SUBLANES - steps, cdim), F32)], axis=0)
        xc = _ssd_conv(xpad, q, p["cw"], p["cb"])
        pad_rows = jnp.zeros((q - steps, LANES), F32)
        dt = jnp.concatenate([dt_all[j * steps:(j + 1) * steps], pad_rows], axis=0)
        z = jnp.concatenate([z_all[j * steps:(j + 1) * steps],
                             jnp.zeros((q - steps, z_all.shape[1]), z_all.dtype)], axis=0)

        def s_set(g, v, j=j):
            sout_ref[j, g] = v

        y = _ssd_chunk(xc, dt, z, p, lambda g, j=j: s_ref[j, g], s_set, steps)
        yb_ref[j * steps:(j + 1) * steps, :] = y[:steps].astype(yb_ref.dtype)
        bufout_ref[j] = xpad[SUBLANES + steps - (CONV_W - 1):SUBLANES + steps]


def _ssd_step(proj, buf, s0, sp, steps, nb, col_xbc, col_z, col_dt):
    bsz = s0.shape[0]
    cdim = sp["ssd_conv_w"].shape[1]
    d_inner = sp["ssd_norm_g"].shape[1]
    gw = d_inner // SSD_GROUPS
    rows = nb * steps
    body = functools.partial(_ssd_step_body, steps=steps, nb=nb)
    return pl.pallas_call(
        body,
        grid=(bsz // nb,),
        in_specs=[pl.BlockSpec((rows, cdim), lambda i: (i, col_xbc // cdim)),
                  pl.BlockSpec((rows, d_inner), lambda i: (i, col_z // d_inner)),
                  pl.BlockSpec((rows, LANES), lambda i: (i, col_dt // LANES)),
                  pl.BlockSpec((nb, CONV_W - 1, cdim), lambda i: (i, 0, 0)),
                  pl.BlockSpec((nb, SSD_GROUPS, gw, SSD_STATE), lambda i: (i, 0, 0, 0)),
                  _const_spec((CONV_W, cdim)), _const_spec((1, cdim)), _const_spec((1, LANES)),
                  _const_spec((1, LANES)), _const_spec((1, d_inner)), _const_spec((1, d_inner))],
        out_specs=[pl.BlockSpec((rows, d_inner), lambda i: (i, 0)),
                   pl.BlockSpec((nb, SSD_GROUPS, gw, SSD_STATE), lambda i: (i, 0, 0, 0)),
                   pl.BlockSpec((nb, CONV_W - 1, cdim), lambda i: (i, 0, 0))],
        out_shape=[jax.ShapeDtypeStruct((bsz * steps, d_inner), BF16),
                   jax.ShapeDtypeStruct(s0.shape, F32),
                   jax.ShapeDtypeStruct(buf.shape, F32)],
        compiler_params=_cparams(("parallel",)),
        name="ssd_step",
    )(proj, proj, proj, buf, s0, sp["ssd_conv_w"], sp["ssd_conv_b"], sp["dt_bias"], sp["A"], sp["D"],
      sp["ssd_norm_g"])


def _router(t, wr, br):
    logits = _dot_f32(t, wr) + br
    lane = lax.broadcasted_iota(jnp.int32, logits.shape, 1)
    neg = -jnp.inf
    gl = jnp.where(lane < N_EGROUPS, logits, neg)
    gmax = jnp.max(gl, axis=-1, keepdims=True)
    g_idx = jnp.min(jnp.where(gl == gmax, lane, LANES), axis=-1, keepdims=True)
    g_w = 1.0 / jnp.sum(jnp.exp(gl - gmax), axis=-1, keepdims=True)
    in_grp = jnp.logical_and(jnp.logical_and(lane >= ROUTER_LANE0, lane < ROUTER_LANE0 + N_EXPERTS),
                             ((lane - ROUTER_LANE0) >> 2) == g_idx)
    el = jnp.where(in_grp, logits, neg)
    pe = jnp.exp(el - jnp.max(el, axis=-1, keepdims=True))
    pe = pe / jnp.sum(pe, axis=-1, keepdims=True)
    cand = jnp.where(in_grp, pe, -1.0)
    v1 = jnp.max(cand, axis=-1, keepdims=True)
    i1 = jnp.min(jnp.where(cand == v1, lane, LANES), axis=-1, keepdims=True)
    cand2 = jnp.where(lane == i1, -1.0, cand)
    v2 = jnp.max(cand2, axis=-1, keepdims=True)
    i2 = jnp.min(jnp.where(jnp.logical_and(cand2 == v2, in_grp), lane, LANES), axis=-1, keepdims=True)
    den = v1 + v2
    return lane, i1, i2, g_w * v1 / den, g_w * v2 / den


def _merge_body(x_ref, ya_ref, yb_ref, ga_ref, gb_ref, wl_ref, ws_ref, wo_ref, gf_ref, wr_ref, br_ref,
                x1_ref, t_ref, rt_ref, cnt_ref, base_scr):
    step = pl.program_id(0)

    @pl.when(step == 0)
    def _():
        base_scr[...] = jnp.zeros_like(base_scr)

    a = _dot(ya_ref[...], wl_ref[...])
    b = _dot(yb_ref[...], ws_ref[...])
    merged = jax.nn.sigmoid(ga_ref[...].astype(F32)) * a + jax.nn.sigmoid(gb_ref[...].astype(F32)) * b
    x1 = x_ref[...] + _dot(merged.astype(BF16), wo_ref[...])
    x1_ref[...] = x1
    t = _rms(x1, gf_ref[...])
    t_ref[...] = t
    lane, i1, i2, wg1, wg2 = _router(t, wr_ref[...], br_ref[...])

    tm = t.shape[0]
    onehot = jnp.where(jnp.logical_or(lane == i1, lane == i2), 1.0, 0.0).astype(BF16)
    tri = (lax.broadcasted_iota(jnp.int32, (tm, tm), 1) <= lax.broadcasted_iota(jnp.int32, (tm, tm), 0)).astype(BF16)
    cum = _dot(tri, onehot) + base_scr[...]
    r1 = jnp.sum(jnp.where(lane == i1, cum, 0.0), axis=-1, keepdims=True) - 1.0
    r2 = jnp.sum(jnp.where(lane == i2, cum, 0.0), axis=-1, keepdims=True) - 1.0
    cols = (wg1, wg2, r1, r2, (i1 - ROUTER_LANE0).astype(F32), (i2 - ROUTER_LANE0).astype(F32))
    rt = jnp.zeros(cum.shape, F32)
    for k, c in enumerate(cols):
        rt = jnp.where(lane == k, c, rt)
    rt_ref[...] = rt
    base_scr[...] = cum[tm - 1:tm, :]
    cnt_ref[...] = cum[tm - 1:tm, :]


def _merge(x, ya, yb, proj, mw, tm, col_ga, col_gb):
    t, d = x.shape
    d_inner = yb.shape[1]
    return pl.pallas_call(
        _merge_body,
        grid=(t // tm,),
        in_specs=[pl.BlockSpec((tm, d), lambda i: (i, 0)),
                  pl.BlockSpec((tm, d), lambda i: (i, 0)),
                  pl.BlockSpec((tm, d_inner), lambda i: (i, 0)),
                  pl.BlockSpec((tm, d), lambda i: (i, col_ga // d)),
                  pl.BlockSpec((tm, d), lambda i: (i, col_gb // d)),
                  _const_spec((d, d)), _const_spec((d_inner, d)), _const_spec((d, d)),
                  _const_spec((1, d)), _const_spec((d, LANES)), _const_spec((1, LANES))],
        out_specs=[pl.BlockSpec((tm, d), lambda i: (i, 0)),
                   pl.BlockSpec((tm, d), lambda i: (i, 0)),
                   pl.BlockSpec((tm, LANES), lambda i: (i, 0)),
                   pl.BlockSpec((1, LANES), lambda i: (0, 0))],
        out_shape=[jax.ShapeDtypeStruct((t, d), F32),
                   jax.ShapeDtypeStruct((t, d), F32),
                   jax.ShapeDtypeStruct((t, LANES), F32),
                   jax.ShapeDtypeStruct((1, LANES), F32)],
        scratch_shapes=[pltpu.VMEM((1, LANES), F32)],
        compiler_params=_cparams(("arbitrary",)),
        name="merge_router",
    )(x, ya, yb, proj, proj, mw["w_br_lru"], mw["w_br_ssd"], mw["w_out"], mw["g_ffn"], mw["w_router"],
      mw["b_router"])


def _dispatch_body(dest_ref, zb_ref, t_ref, o_ref, zero_scr, sem, zsem, *, tmg):
    step = pl.program_id(0)
    tm = t_ref.shape[0]

    @pl.when(step == 0)
    def _():
        zero_scr[...] = jnp.zeros_like(zero_scr)

        def zcopy(j):
            return pltpu.make_async_copy(zero_scr, o_ref.at[pl.ds(pl.multiple_of(zb_ref[j] * tmg, tmg), tmg)], zsem)

        for j in range(zb_ref.shape[0]):
            pl.when(zb_ref[j] >= 0)(lambda j=j: zcopy(j).start())
        for j in range(zb_ref.shape[0]):
            pl.when(zb_ref[j] >= 0)(lambda j=j: zcopy(j).wait())

    def issue(r, carry):
        for k in range(2):
            row = dest_ref[2 * (step * tm + r) + k]
            pltpu.make_async_copy(t_ref.at[pl.ds(r, 1)], o_ref.at[pl.ds(row, 1)], sem).start()
        return carry

    lax.fori_loop(0, tm, issue, 0, unroll=8)
    for k in range(2):
        pltpu.make_async_copy(t_ref, o_ref.at[pl.ds(0, tm)], sem).wait()


def _dispatch(dest, zero_blocks, t, tm, n_rows, tmg):
    n, d = t.shape
    return pl.pallas_call(
        functools.partial(_dispatch_body, tmg=tmg),
        grid_spec=pltpu.PrefetchScalarGridSpec(
            num_scalar_prefetch=2,
            grid=(n // tm,),
            in_specs=[pl.BlockSpec((tm, d), lambda i, *_: (i, 0))],
            out_specs=pl.BlockSpec(memory_space=pl.ANY),
            scratch_shapes=[pltpu.VMEM((tmg, d), F32), pltpu.SemaphoreType.DMA(()), pltpu.SemaphoreType.DMA(())]),
        out_shape=jax.ShapeDtypeStruct((n_rows, d), F32),
        compiler_params=_cparams(("arbitrary",)),
        name="moe_dispatch",
    )(dest, zero_blocks, t)


def _expert_body(te_ref, nt_ref, x_ref, w1_ref, w3_ref, w2_ref, y_ref):
    real = pl.program_id(0) < nt_ref[0]

    @pl.when(real)
    def _():
        x = x_ref[...].astype(BF16)
        h1 = _dot(x, w1_ref[0])
        h3 = _dot(x, w3_ref[0])
        y_ref[...] = _dot((h1 * jax.nn.sigmoid(h1) * h3).astype(BF16), w2_ref[0])

    @pl.when(jnp.logical_not(real))
    def _():
        y_ref[...] = jnp.zeros_like(y_ref)


def _experts(tile_expert, n_tiles, xs, w1, w3, w2, tmg):
    d = xs.shape[1]
    dff = w1.shape[2]
    row_spec = pl.BlockSpec((tmg, d), lambda i, te, nt: (i, 0))
    return pl.pallas_call(
        _expert_body,
        grid_spec=pltpu.PrefetchScalarGridSpec(
            num_scalar_prefetch=2,
            grid=(xs.shape[0] // tmg,),
            in_specs=[row_spec,
                      pl.BlockSpec((1, d, dff), lambda i, te, nt: (te[i], 0, 0)),
                      pl.BlockSpec((1, d, dff), lambda i, te, nt: (te[i], 0, 0)),
                      pl.BlockSpec((1, dff, d), lambda i, te, nt: (te[i], 0, 0))],
            out_specs=row_spec),
        out_shape=jax.ShapeDtypeStruct(xs.shape, F32),
        compiler_params=_cparams(("arbitrary",)),
        name="moe_experts",
    )(tile_expert, n_tiles, xs, w1, w3, w2)


def _ple_body(dest_ref, x_ref, rt_ref, p_ref, wp_ref, gp_ref, gg_ref, wg_ref, gfin_ref, y_hbm, o_ref, gbuf, sem):
    step = pl.program_id(0)
    tm = x_ref.shape[0]

    def gather(tile, slot):
        def issue(r, carry):
            for k in range(2):
                row = dest_ref[2 * (tile * tm + r) + k]
                pltpu.make_async_copy(y_hbm.at[pl.ds(row, 1)], gbuf.at[slot, pl.ds(k * tm + r, 1)],
                                      sem.at[slot]).start()
            return carry

        lax.fori_loop(0, tm, issue, 0, unroll=8)

    @pl.when(step == 0)
    def _():
        gather(0, 0)

    @pl.when(step + 1 < pl.num_programs(0))
    def _():
        gather(step + 1, (step + 1) % 2)

    slot = step % 2
    pltpu.make_async_copy(y_hbm.at[pl.ds(0, 2 * tm)], gbuf.at[slot], sem.at[slot]).wait()
    rt = rt_ref[...]
    x = x_ref[...] + rt[:, 0:1] * gbuf[slot, 0:tm, :] + rt[:, 1:2] * gbuf[slot, tm:2 * tm, :]
    e = _rms(_dot(p_ref[...].astype(BF16), wp_ref[...]), gp_ref[...])
    gate = jax.nn.sigmoid(_dot(_rms(x, gg_ref[...]).astype(BF16), wg_ref[...]))
    o_ref[...] = _rms(x + gate * e, gfin_ref[...])


def _ple(dest, x1, rt, p, y_sorted, pw, tm):
    n, d = x1.shape
    dp = p.shape[1]
    const = lambda shape: pl.BlockSpec(shape, lambda i, *_: (0,) * len(shape))
    return pl.pallas_call(
        _ple_body,
        grid_spec=pltpu.PrefetchScalarGridSpec(
            num_scalar_prefetch=1,
            grid=(n // tm,),
            in_specs=[pl.BlockSpec((tm, d), lambda i, *_: (i, 0)),
                      pl.BlockSpec((tm, LANES), lambda i, *_: (i, 0)),
                      pl.BlockSpec((tm, dp), lambda i, *_: (i, 0)),
                      const((dp, d)), const((1, d)), const((1, d)), const((d, d)), const((1, d)),
                      pl.BlockSpec(memory_space=pl.ANY)],
            out_specs=pl.BlockSpec((tm, d), lambda i, *_: (i, 0)),
            scratch_shapes=[pltpu.VMEM((2, 2 * tm, d), F32), pltpu.SemaphoreType.DMA((2,))]),
        out_shape=jax.ShapeDtypeStruct((n, d), F32),
        compiler_params=_cparams(("arbitrary",)),
        name="combine_ple_final",
    )(dest, x1, rt, p, pw["w_ple_proj"], pw["g_ple"], pw["g_ple_gate"], pw["w_ple_gate"], pw["g_final"], y_sorted)


def _pick_tile(n, pref):
    t = min(n, pref)
    while n % t:
        t //= 2
    return t


MOE_ROW_TILE = 256


def _token_tail(x1, t, rt, cnt, p, lw, tm):
    n = x1.shape[0]
    tmg = MOE_ROW_TILE
    n_blocks = pl.cdiv(2 * n, tmg) + N_EXPERTS
    counts = cnt[0, ROUTER_LANE0:ROUTER_LANE0 + N_EXPERTS].astype(jnp.int32)
    tiles = (counts + tmg - 1) // tmg
    ends = jnp.cumsum(tiles)
    n_tiles = ends[-1]
    dest = (jnp.take((ends - tiles) * tmg, rt[:, 4:6].astype(jnp.int32)) + rt[:, 2:4].astype(jnp.int32)).reshape(2 * n)
    blk = jnp.arange(n_blocks, dtype=jnp.int32)
    tile_expert = jnp.sum((jnp.minimum(blk, n_tiles - 1)[:, None] >= ends[None, :]).astype(jnp.int32), axis=1)
    tail = n_tiles + blk[:N_EXPERTS]
    zero_blocks = jnp.concatenate([jnp.where(tiles > 0, ends - 1, -1),
                                   jnp.where(tail < n_blocks, tail, -1)]).astype(jnp.int32)

    sorted_t = _dispatch(dest, zero_blocks, t, tm, n_blocks * tmg, tmg)
    y_sorted = _experts(tile_expert, n_tiles.reshape(1), sorted_t, lw["w1"], lw["w3"], lw["w2"], tmg)
    return _ple(dest, x1, rt, p, y_sorted, lw, tm)


def kernel(x_prompt, x_sample, state_lru_h, state_lru_conv, state_ssd, state_ssd_conv, p_prompt, p_sample, g_mix, w_in, lru_conv_w, lru_conv_b, lru_wa, lru_ba, lru_wx, lru_bx, lru_lambda, ssd_conv_w, ssd_conv_b, ssd_dt_bias, ssd_A_log, ssd_D, ssd_norm_g, w_br_lru, w_br_ssd, w_out, g_ffn, w_router_g, b_router_g, w_router_e, b_router_e, w1, w3, w2, w_ple_proj, g_ple, g_ple_gate, w_ple_gate, g_final):
    depth = w_in.shape[0]
    assert depth == 1, "one decoder layer per call"
    bp, lp, d = x_prompt.shape
    bs, ls, _ = x_sample.shape
    w_lru = state_lru_h.shape[-1]
    heads, hdim, nstate = state_ssd.shape[2:]
    d_inner = heads * hdim
    cdim = state_ssd_conv.shape[-1]
    assert hdim == SSD_HEADDIM and nstate == SSD_STATE and heads <= LANES and ls < SUBLANES
    gw = d_inner // SSD_GROUPS

    o_dt = 2 * w_lru + d_inner + cdim
    wi = w_in[0]
    w_proj = jnp.concatenate(
        [wi[:, :o_dt], wi[:, o_dt + heads:], wi[:, o_dt:o_dt + heads], jnp.zeros((d, LANES - heads), wi.dtype)],
        axis=1).astype(BF16)
    col_z, col_xbc = 2 * w_lru, 2 * w_lru + d_inner
    col_ga, col_gb, col_dt = o_dt, o_dt + d, o_dt + 2 * d
    n_proj = w_proj.shape[1]
    row = lambda v: v.reshape(1, -1).astype(F32)
    pad_heads = lambda v: jnp.pad(v.astype(F32), (0, LANES - heads)).reshape(1, LANES)
    lw = {
        "lru_conv_w": lru_conv_w[0], "lru_conv_b": row(lru_conv_b[0]),
        "wax": jnp.concatenate([lru_wa[0], lru_wx[0]], axis=-1).astype(BF16),
        "lru_ba": row(lru_ba[0]), "lru_bx": row(lru_bx[0]), "lru_lambda": row(lru_lambda[0]),
        "ssd_conv_w": ssd_conv_w[0], "ssd_conv_b": row(ssd_conv_b[0]),
        "dt_bias": pad_heads(ssd_dt_bias[0]), "A": pad_heads(-jnp.exp(ssd_A_log[0].astype(F32))),
        "D": row(jnp.repeat(ssd_D[0], hdim)), "ssd_norm_g": row(ssd_norm_g[0]),
        "w_br_lru": w_br_lru[0].astype(BF16), "w_br_ssd": w_br_ssd[0].astype(BF16), "w_out": w_out[0].astype(BF16),
        "g_ffn": row(g_ffn[0]),
        "w_router": jnp.concatenate([w_router_g[0], w_router_e[0],
                                     jnp.zeros((d, LANES - N_EGROUPS - N_EXPERTS), F32)], axis=1),
        "b_router": jnp.concatenate([b_router_g[0], b_router_e[0],
                                     jnp.zeros((LANES - N_EGROUPS - N_EXPERTS,), F32)]).reshape(1, LANES),
        "w1": w1[0].astype(BF16), "w3": w3[0].astype(BF16), "w2": w2[0].astype(BF16),
        "w_ple_proj": w_ple_proj[0].astype(BF16), "g_ple": row(g_ple[0]), "g_ple_gate": row(g_ple_gate[0]),
        "w_ple_gate": w_ple_gate[0].astype(BF16), "g_final": row(g_final),
    }
    g_mix_r = row(g_mix[0])
    tn = n_proj // 9 if n_proj % (9 * LANES) == 0 else LANES

    tp = bp * lp
    xp = x_prompt.reshape(tp, d)
    tm_p = _pick_tile(tp, 1024)
    proj_p = _inproj(xp, g_mix_r, w_proj, tm_p, tn)
    proj_p3 = proj_p.reshape(bp, lp, n_proj)
    ya_p, hl_p, lbuf_p = _lru_prompt(proj_p3, jnp.zeros((bp, CONV_W - 1, w_lru), F32), jnp.zeros((bp, w_lru), F32),
                                     lw, _pick_tile(lp, 256), True)
    yb_p, s_p, sbuf_p = _ssd_prompt(proj_p3, lw, col_xbc, col_z, col_dt)
    tm_tail_p = _pick_tile(tp, 512)
    x1_p, t_p, rt_p, cnt_p = _merge(xp, ya_p.reshape(tp, w_lru), yb_p.reshape(tp, d_inner), proj_p, lw,
                                    tm_tail_p, col_ga, col_gb)
    y_p = _token_tail(x1_p, t_p, rt_p, cnt_p, p_prompt[0].reshape(tp, -1), lw, tm_tail_p)

    ts = bs * ls
    xs = x_sample.reshape(ts, d)
    tm_s = _pick_tile(ts, 512)
    proj_s = _inproj(xs, g_mix_r, w_proj, tm_s, tn)
    to_tmajor = lambda v, n: v.reshape(bs, n, -1).transpose(1, 0, 2).reshape(n * bs, -1)
    from_tmajor = lambda v, n: v.reshape(n, bs, -1).transpose(1, 0, 2)
    ya_t, hl_s, lbuf_t = _lru_step(to_tmajor(proj_s[:, :w_lru], ls), to_tmajor(proj_s[:, w_lru:2 * w_lru], ls),
                                   to_tmajor(state_lru_conv[0], CONV_W - 1), state_lru_h[0], lw, ls)
    ya_s = from_tmajor(ya_t, ls).reshape(ts, w_lru)
    lbuf_s = from_tmajor(lbuf_t, CONV_W - 1)
    yb_s, s_s, sbuf_s = _ssd_step(proj_s, state_ssd_conv[0], state_ssd[0].reshape(bs, SSD_GROUPS, gw, nstate), lw,
                                  ls, _pick_tile(bs, 4), col_xbc, col_z, col_dt)
    x1_s, t_s, rt_s, cnt_s = _merge(xs, ya_s, yb_s, proj_s, lw, tm_s, col_ga, col_gb)
    y_s = _token_tail(x1_s, t_s, rt_s, cnt_s, p_sample[0].reshape(ts, -1), lw, tm_s)

    return (y_p.reshape(bp, lp, d), y_s.reshape(bs, ls, d),
            hl_p.reshape(1, bp, w_lru), lbuf_p[None],
            s_p.reshape(1, bp, heads, hdim, nstate), sbuf_p[None],
            hl_s[None], lbuf_s[None],
            s_s.reshape(1, bs, heads, hdim, nstate), sbuf_s[None])
```

```python
import functools

import jax
import jax.numpy as jnp
from jax import lax
from jax.experimental import pallas as pl
from jax.experimental.pallas import tpu as pltpu

F32 = jnp.float32
BF16 = jnp.bfloat16

EPS = 1e-6
CONV_W = 4
LRU_BLOCKS = 8
LRU_C = 8.0
SSD_HEADDIM = 64
SSD_GROUPS = 8
SSD_STATE = 128
SSD_CHUNK = 128
N_EGROUPS = 4
EXP_PER_GROUP = 4
N_EXPERTS = N_EGROUPS * EXP_PER_GROUP

LANES = 128
SUBLANES = 8
VMEM_LIMIT = 52 * 1024 * 1024
ROUTER_LANE0 = N_EGROUPS


def _cparams(sem):
    return pltpu.CompilerParams(dimension_semantics=sem, vmem_limit_bytes=VMEM_LIMIT)


def _dot(a, b):
    return jnp.dot(a, b, preferred_element_type=F32)


def _dot_nt(a, b):
    return lax.dot_general(a, b, (((1,), (1,)), ((), ())), preferred_element_type=F32)


def _dot_tn(a, b):
    return lax.dot_general(a, b, (((0,), (0,)), ((), ())), preferred_element_type=F32)


def _dot_f32(a, b):
    return jnp.dot(a, b, precision=lax.Precision.HIGHEST, preferred_element_type=F32)


def _rms(x, g):
    return x * lax.rsqrt(jnp.mean(x * x, axis=-1, keepdims=True) + EPS) * g


def _const_spec(shape):
    nd = len(shape)
    return pl.BlockSpec(shape, lambda *_: (0,) * nd)


def _inproj_body(x_ref, g_ref, w_ref, o_ref, h_scr):
    @pl.when(pl.program_id(1) == 0)
    def _():
        h_scr[...] = _rms(x_ref[...], g_ref[...]).astype(BF16)

    o_ref[...] = _dot(h_scr[...], w_ref[...]).astype(o_ref.dtype)


def _inproj(x, g, w, tm, tn):
    t, d = x.shape
    n = w.shape[1]
    return pl.pallas_call(
        _inproj_body,
        grid=(t // tm, n // tn),
        in_specs=[pl.BlockSpec((tm, d), lambda i, j: (i, 0)),
                  pl.BlockSpec((1, d), lambda i, j: (0, 0)),
                  pl.BlockSpec((d, tn), lambda i, j: (0, j))],
        out_specs=pl.BlockSpec((tm, tn), lambda i, j: (i, j)),
        out_shape=jax.ShapeDtypeStruct((t, n), BF16),
        scratch_shapes=[pltpu.VMEM((tm, d), BF16)],
        compiler_params=_cparams(("parallel", "arbitrary")),
        name="inproj",
    )(x, g, w)


def _lru_gates(u, wax_ref, ba, bx, lam):
    bw = u.shape[1] // LRU_BLOCKS
    r_parts, i_parts = [], []
    for n in range(LRU_BLOCKS):
        ri = _dot(u[:, n * bw:(n + 1) * bw].astype(BF16), wax_ref[n])
        r_parts.append(ri[:, :bw])
        i_parts.append(ri[:, bw:])
    r = jax.nn.sigmoid(jnp.concatenate(r_parts, axis=1) + ba)
    i = jax.nn.sigmoid(jnp.concatenate(i_parts, axis=1) + bx)
    log_a = LRU_C * r * jax.nn.log_sigmoid(lam)
    a = jnp.exp(log_a)
    m2 = -jnp.tanh(log_a) * (a * a + 1.0)
    mult = jnp.where(m2 > 0.0, m2 * lax.rsqrt(m2), 0.0)
    return a, i, mult


def _lru_prompt_body(xin_ref, gate_ref, buf0_ref, h0_ref, cw_ref, cb_ref, wax_ref, ba_ref, bx_ref, lam_ref,
                     ya_ref, hlast_ref, bufout_ref, halo_scr, h_scr, *, reset_first):
    j = pl.program_id(1)
    tt = xin_ref.shape[1]

    @pl.when(j == 0)
    def _():
        halo_scr[...] = jnp.zeros_like(halo_scr)
        halo_scr[SUBLANES - (CONV_W - 1):SUBLANES, :] = buf0_ref[0]
        h_scr[...] = h0_ref[0]

    x = xin_ref[0].astype(F32)
    xpad = jnp.concatenate([halo_scr[...], x], axis=0)
    cw = cw_ref[...]
    u = cb_ref[...] + sum(xpad[SUBLANES - (CONV_W - 1) + k:SUBLANES - (CONV_W - 1) + k + tt] * cw[k:k + 1]
                          for k in range(CONV_W))
    halo_scr[...] = x[tt - SUBLANES:tt]

    a, i, mult = _lru_gates(u, wax_ref, ba_ref[...], bx_ref[...], lam_ref[...])
    row = lax.broadcasted_iota(jnp.int32, a.shape, 0)
    if reset_first:
        first = jnp.logical_and(row == 0, j == 0)
        mult = jnp.where(first, 1.0, mult)
        a = jnp.where(first, 0.0, a)
    v = u * i * mult

    width = a.shape[1]
    a = a.reshape(tt // SUBLANES, SUBLANES, width)
    v = v.reshape(tt // SUBLANES, SUBLANES, width)
    sub = lax.broadcasted_iota(jnp.int32, a.shape, 1)
    s = 1
    while s < SUBLANES:
        keep = sub >= s
        v = jnp.where(keep, a * pltpu.roll(v, s, axis=1) + v, v)
        a = jnp.where(keep, a * pltpu.roll(a, s, axis=1), a)
        s *= 2
    carry = h_scr[...]
    groups = []
    for g in range(tt // SUBLANES):
        hg = a[g] * carry + v[g]
        carry = hg[SUBLANES - 1:SUBLANES]
        groups.append(hg)
    h = jnp.concatenate(groups, axis=0)
    h_scr[...] = carry
    ya_ref[0] = (h * jax.nn.gelu(gate_ref[0].astype(F32))).astype(ya_ref.dtype)

    @pl.when(j == pl.num_programs(1) - 1)
    def _():
        hlast_ref[0] = h[tt - 1:tt]
        bufout_ref[0] = x[tt - (CONV_W - 1):tt]


def _lru_prompt(proj3, buf0, h0, lw, tt, reset_first):
    b, l, _ = proj3.shape
    w = h0.shape[-1]
    body = functools.partial(_lru_prompt_body, reset_first=reset_first)
    return pl.pallas_call(
        body,
        grid=(b, l // tt),
        in_specs=[pl.BlockSpec((1, tt, w), lambda i, j: (i, j, 0)),
                  pl.BlockSpec((1, tt, w), lambda i, j: (i, j, 1)),
                  pl.BlockSpec((1, CONV_W - 1, w), lambda i, j: (i, 0, 0)),
                  pl.BlockSpec((1, 1, w), lambda i, j: (i, 0, 0)),
                  _const_spec((CONV_W, w)), _const_spec((1, w)),
                  _const_spec(lw["wax"].shape), _const_spec((1, w)), _const_spec((1, w)), _const_spec((1, w))],
        out_specs=[pl.BlockSpec((1, tt, w), lambda i, j: (i, j, 0)),
                   pl.BlockSpec((1, 1, w), lambda i, j: (i, 0, 0)),
                   pl.BlockSpec((1, CONV_W - 1, w), lambda i, j: (i, 0, 0))],
        out_shape=[jax.ShapeDtypeStruct((b, l, w), BF16),
                   jax.ShapeDtypeStruct((b, 1, w), F32),
                   jax.ShapeDtypeStruct((b, CONV_W - 1, w), F32)],
        scratch_shapes=[pltpu.VMEM((SUBLANES, w), F32), pltpu.VMEM((1, w), F32)],
        compiler_params=_cparams(("parallel", "arbitrary")),
        name="lru_prompt",
    )(proj3, proj3, buf0, h0.reshape(b, 1, w), lw["lru_conv_w"], lw["lru_conv_b"], lw["wax"],
      lw["lru_ba"], lw["lru_bx"], lw["lru_lambda"])


def _lru_step_body(xin_ref, gate_ref, buf_ref, h0_ref, cw_ref, cb_ref, wax_ref, ba_ref, bx_ref, lam_ref,
                   ya_ref, hlast_ref, bufout_ref, *, steps):
    bsz = h0_ref.shape[0]
    n = steps * bsz
    x = xin_ref[...].astype(F32)
    xx = jnp.concatenate([buf_ref[...], x], axis=0)
    cw = cw_ref[...]
    u = cb_ref[...] + sum(xx[k * bsz:k * bsz + n] * cw[k:k + 1] for k in range(CONV_W))
    a, i, mult = _lru_gates(u, wax_ref, ba_ref[...], bx_ref[...], lam_ref[...])
    v = u * i * mult
    h = h0_ref[...]
    for t in range(steps):
        sl = slice(t * bsz, (t + 1) * bsz)
        h = a[sl] * h + v[sl]
        ya_ref[sl, :] = (h * jax.nn.gelu(gate_ref[sl, :].astype(F32))).astype(ya_ref.dtype)
    hlast_ref[...] = h
    bufout_ref[...] = xx[steps * bsz:(steps + CONV_W - 1) * bsz]


def _lru_step(xin_t, gate_t, buf_t, h0, lw, steps):
    bsz, w = h0.shape
    body = functools.partial(_lru_step_body, steps=steps)
    return pl.pallas_call(
        body,
        out_shape=[jax.ShapeDtypeStruct((steps * bsz, w), BF16),
                   jax.ShapeDtypeStruct((bsz, w), F32),
                   jax.ShapeDtypeStruct(((CONV_W - 1) * bsz, w), F32)],
        compiler_params=pltpu.CompilerParams(vmem_limit_bytes=VMEM_LIMIT),
        name="lru_step",
    )(xin_t, gate_t, buf_t, h0, lw["lru_conv_w"], lw["lru_conv_b"], lw["wax"],
      lw["lru_ba"], lw["lru_bx"], lw["lru_lambda"])


def _expand_heads(cols, e2):
    q = cols[0].shape[0]
    v = jnp.concatenate(cols, axis=0)
    hi = v.astype(BF16)
    lo = (v - hi.astype(F32)).astype(BF16)
    out = _dot(jnp.concatenate([hi, lo], axis=1), e2)
    return [out[i * q:(i + 1) * q] for i in range(len(cols))]


def _ssd_chunk(xc, dt, p, e2, s_get, s_set, t_col, t_row, n_valid):
    q = xc.shape[0]
    gn = SSD_GROUPS * SSD_STATE
    d_inner = xc.shape[1] - 2 * gn
    hpg = d_inner // SSD_HEADDIM // SSD_GROUPS
    gw = hpg * SSD_HEADDIM

    causal = t_col >= t_row
    if n_valid < q:
        dt = jnp.where(lax.broadcasted_iota(jnp.int32, dt.shape, 0) < n_valid, dt, 0.0)
    a = dt * p["A"]
    cum = _dot_f32(causal.astype(F32), a)
    cum_t = cum.T
    dec_t = jnp.exp(cum_t[:, q - 1:q])
    dt_x, to_end_x, ecum_x = _expand_heads([dt, jnp.exp(cum[q - 1:q, :] - cum), jnp.exp(cum)], e2)

    xs = xc[:, :d_inner]
    xdt = xs * dt_x
    xw = (xdt * to_end_x).astype(BF16)
    xdt_m = xdt.astype(BF16) if q % (2 * SUBLANES) == 0 else xdt
    lane_head = lax.broadcasted_iota(jnp.int32, (1, gw), 1) // SSD_HEADDIM
    y_groups = []
    for g in range(SSD_GROUPS):
        sl = slice(g * gw, (g + 1) * gw)
        bg = xc[:, d_inner + g * SSD_STATE:d_inner + (g + 1) * SSD_STATE].astype(BF16)
        cg = xc[:, d_inner + gn + g * SSD_STATE:d_inner + gn + (g + 1) * SSD_STATE].astype(BF16)
        s_old = s_get(g)
        cb = _dot_nt(cg, bg)
        y_off = _dot_nt(cg, s_old.astype(BF16))
        m_heads, x_heads, s_dec = [], [], []
        for hh in range(hpg):
            h = g * hpg + hh
            decay = jnp.exp(jnp.where(causal, cum[:, h:h + 1] - cum_t[h:h + 1, :], -jnp.inf))
            m_heads.append((cb * decay).astype(BF16))
            x_heads.append(jnp.where(lane_head == hh, xdt_m[:, sl], 0.0))
            s_dec.append(s_old[hh * SSD_HEADDIM:(hh + 1) * SSD_HEADDIM, :] * dec_t[h:h + 1, :])
        y_diag = _dot(jnp.concatenate(m_heads, axis=1), jnp.concatenate(x_heads, axis=0).astype(BF16))
        s_set(g, jnp.concatenate(s_dec, axis=0) + _dot_tn(xw[:, sl], bg))
        y_groups.append(y_diag + y_off * ecum_x[:, sl])
    return jnp.concatenate(y_groups, axis=1) + p["D"] * xs


def _ssd_gate_norm(y, z, p):
    gw = y.shape[1] // SSD_GROUPS
    zf = z.astype(F32)
    out = []
    for g in range(SSD_GROUPS):
        sl = slice(g * gw, (g + 1) * gw)
        v = y[:, sl] * (zf[:, sl] * jax.nn.sigmoid(zf[:, sl]))
        out.append(v * lax.rsqrt(jnp.mean(v * v, axis=-1, keepdims=True) + EPS) * p["norm_g"][:, sl])
    return jnp.concatenate(out, axis=1)


def _ssd_conv(xpad, q, cw, cb):
    base = SUBLANES - (CONV_W - 1)
    y = cb + sum(xpad[base + k:base + k + q] * cw[k:k + 1] for k in range(CONV_W))
    return y * jax.nn.sigmoid(y)


def _softplus(x):
    return jax.nn.softplus(x)


def _ssd_params(cw_ref, cb_ref, dtb_ref, a_ref, d_ref, ng_ref):
    return {"cw": cw_ref[...], "cb": cb_ref[...], "dt_bias": dtb_ref[...], "A": a_ref[...],
            "D": d_ref[...], "norm_g": ng_ref[...]}


def _ssd_prompt_body(xbc_ref, z_ref, dt_ref, cw_ref, cb_ref, dtb_ref, a_ref, d_ref, ng_ref, e2_ref,
                     yb_ref, sout_ref, bufout_ref, x_scr, dt_scr, y_scr, s_scr):
    c = pl.program_id(1)
    q = xbc_ref.shape[1]
    half = q // 2
    n_xslab = x_scr.shape[0]
    base = SUBLANES - (CONV_W - 1)

    @pl.when(c == 0)
    def _():
        x_scr[:, 0:SUBLANES, :] = jnp.zeros((n_xslab, SUBLANES, LANES), F32)
        s_scr[...] = jnp.zeros_like(s_scr)

    @pl.when(c > 0)
    def _():
        x_scr[:, 0:SUBLANES, :] = x_scr[:, q:q + SUBLANES, :]

    p = _ssd_params(cw_ref, cb_ref, dtb_ref, a_ref, d_ref, ng_ref)
    x = xbc_ref[0].astype(F32)
    for j in range(n_xslab):
        x_scr[j, SUBLANES:SUBLANES + q, :] = x[:, j * LANES:(j + 1) * LANES]

    cols = []
    for j in range(n_xslab):
        ls = slice(j * LANES, (j + 1) * LANES)
        halves = []
        for par in range(2):
            acc = p["cb"][:, ls]
            for k in range(CONV_W):
                acc = acc + x_scr[j, pl.ds(base + k + par, half, stride=2), :] * p["cw"][k:k + 1, ls]
            halves.append(acc)
        cols.append(jnp.concatenate(halves, axis=0))
    xc = jnp.concatenate(cols, axis=1)
    xc = xc * jax.nn.sigmoid(xc)

    dt_scr[...] = _softplus(dt_ref[0].astype(F32) + p["dt_bias"])
    dt = jnp.concatenate([dt_scr[pl.ds(par, half, stride=2), :] for par in range(2)], axis=0)

    def times(shape, axis):
        pos = lax.broadcasted_iota(jnp.int32, shape, axis)
        return jnp.where(pos < half, 2 * pos, 2 * (pos - half) + 1)

    def s_set(g, v):
        s_scr[g] = v

    y = _ssd_chunk(xc, dt, p, e2_ref[...], lambda g: s_scr[g], s_set, times((q, 1), 0), times((1, q), 1), q)
    for j in range(y_scr.shape[0]):
        for par in range(2):
            y_scr[j, pl.ds(par, half, stride=2), :] = y[par * half:(par + 1) * half, j * LANES:(j + 1) * LANES]
    y = jnp.concatenate([y_scr[j] for j in range(y_scr.shape[0])], axis=1)
    yb_ref[0] = _ssd_gate_norm(y, z_ref[0], p).astype(yb_ref.dtype)

    @pl.when(c == pl.num_programs(1) - 1)
    def _():
        sout_ref[0] = s_scr[...]
        bufout_ref[0] = x[q - (CONV_W - 1):q]


def _ssd_prompt(proj3, sp, col_xbc, col_z, col_dt):
    b, l, _ = proj3.shape
    cdim = sp["ssd_conv_w"].shape[1]
    d_inner = sp["ssd_norm_g"].shape[1]
    gw = d_inner // SSD_GROUPS
    q = SSD_CHUNK if l % SSD_CHUNK == 0 else l
    assert q % (2 * SUBLANES) == 0
    return pl.pallas_call(
        _ssd_prompt_body,
        grid=(b, l // q),
        in_specs=[pl.BlockSpec((1, q, cdim), lambda i, c: (i, c, col_xbc // cdim)),
                  pl.BlockSpec((1, q, d_inner), lambda i, c: (i, c, col_z // d_inner)),
                  pl.BlockSpec((1, q, LANES), lambda i, c: (i, c, col_dt // LANES)),
                  _const_spec((CONV_W, cdim)), _const_spec((1, cdim)), _const_spec((1, LANES)),
                  _const_spec((1, LANES)), _const_spec((1, d_inner)), _const_spec((1, d_inner)),
                  _const_spec((2 * LANES, d_inner))],
        out_specs=[pl.BlockSpec((1, q, d_inner), lambda i, c: (i, c, 0)),
                   pl.BlockSpec((1, SSD_GROUPS, gw, SSD_STATE), lambda i, c: (i, 0, 0, 0)),
                   pl.BlockSpec((1, CONV_W - 1, cdim), lambda i, c: (i, 0, 0))],
        out_shape=[jax.ShapeDtypeStruct((b, l, d_inner), BF16),
                   jax.ShapeDtypeStruct((b, SSD_GROUPS, gw, SSD_STATE), F32),
                   jax.ShapeDtypeStruct((b, CONV_W - 1, cdim), F32)],
        scratch_shapes=[pltpu.VMEM((cdim // LANES, q + SUBLANES, LANES), F32),
                        pltpu.VMEM((q, LANES), F32),
                        pltpu.VMEM((d_inner // LANES, q, LANES), F32),
                        pltpu.VMEM((SSD_GROUPS, gw, SSD_STATE), F32)],
        compiler_params=_cparams(("parallel", "arbitrary")),
        name="ssd_prompt",
    )(proj3, proj3, proj3, sp["ssd_conv_w"], sp["ssd_conv_b"], sp["dt_bias"], sp["A"], sp["D"], sp["ssd_norm_g"],
      sp["head_expand"])


def _ssd_step_body(xbc_ref, z_ref, dt_ref, buf_ref, s_ref, cw_ref, cb_ref, dtb_ref, a_ref, d_ref, ng_ref, e2_ref,
                   yb_ref, sout_ref, bufout_ref, *, steps, nb):
    p = _ssd_params(cw_ref, cb_ref, dtb_ref, a_ref, d_ref, ng_ref)
    q = SUBLANES
    e2 = e2_ref[...]
    t_col = lax.broadcasted_iota(jnp.int32, (q, 1), 0)
    t_row = lax.broadcasted_iota(jnp.int32, (1, q), 1)
    x_all = xbc_ref[...].astype(F32)
    z_all = z_ref[...]
    dt_all = _softplus(dt_ref[...].astype(F32) + p["dt_bias"])
    cdim = x_all.shape[1]
    for j in range(nb):
        x = x_all[j * steps:(j + 1) * steps]
        buf = buf_ref[j]
        xpad = jnp.concatenate([jnp.zeros((SUBLANES - (CONV_W - 1), cdim), F32), buf, x,
                                jnp.zeros((SUBLANES - steps, cdim), F32)], axis=0)
        xc = _ssd_conv(xpad, q, p["cw"], p["cb"])
        pad_rows = jnp.zeros((q - steps, LANES), F32)
        dt = jnp.concatenate([dt_all[j * steps:(j + 1) * steps], pad_rows], axis=0)
        z = jnp.concatenate([z_all[j * steps:(j + 1) * steps],
                             jnp.zeros((q - steps, z_all.shape[1]), z_all.dtype)], axis=0)

        def s_set(g, v, j=j):
            sout_ref[j, g] = v

        y = _ssd_chunk(xc, dt, p, e2, lambda g, j=j: s_ref[j, g], s_set, t_col, t_row, steps)
        yb_ref[j * steps:(j + 1) * steps, :] = _ssd_gate_norm(y, z, p)[:steps].astype(yb_ref.dtype)
        bufout_ref[j] = xpad[SUBLANES + steps - (CONV_W - 1):SUBLANES + steps]


def _ssd_step(proj, buf, s0, sp, steps, nb, col_xbc, col_z, col_dt):
    bsz = s0.shape[0]
    cdim = sp["ssd_conv_w"].shape[1]
    d_inner = sp["ssd_norm_g"].shape[1]
    gw = d_inner // SSD_GROUPS
    rows = nb * steps
    body = functools.partial(_ssd_step_body, steps=steps, nb=nb)
    return pl.pallas_call(
        body,
        grid=(bsz // nb,),
        in_specs=[pl.BlockSpec((rows, cdim), lambda i: (i, col_xbc // cdim)),
                  pl.BlockSpec((rows, d_inner), lambda i: (i, col_z // d_inner)),
                  pl.BlockSpec((rows, LANES), lambda i: (i, col_dt // LANES)),
                  pl.BlockSpec((nb, CONV_W - 1, cdim), lambda i: (i, 0, 0)),
                  pl.BlockSpec((nb, SSD_GROUPS, gw, SSD_STATE), lambda i: (i, 0, 0, 0)),
                  _const_spec((CONV_W, cdim)), _const_spec((1, cdim)), _const_spec((1, LANES)),
                  _const_spec((1, LANES)), _const_spec((1, d_inner)), _const_spec((1, d_inner)),
                  _const_spec((2 * LANES, d_inner))],
        out_specs=[pl.BlockSpec((rows, d_inner), lambda i: (i, 0)),
                   pl.BlockSpec((nb, SSD_GROUPS, gw, SSD_STATE), lambda i: (i, 0, 0, 0)),
                   pl.BlockSpec((nb, CONV_W - 1, cdim), lambda i: (i, 0, 0))],
        out_shape=[jax.ShapeDtypeStruct((bsz * steps, d_inner), BF16),
                   jax.ShapeDtypeStruct(s0.shape, F32),
                   jax.ShapeDtypeStruct(buf.shape, F32)],
        compiler_params=_cparams(("parallel",)),
        name="ssd_step",
    )(proj, proj, proj, buf, s0, sp["ssd_conv_w"], sp["ssd_conv_b"], sp["dt_bias"], sp["A"], sp["D"],
      sp["ssd_norm_g"], sp["head_expand"])


def _router(t, wr, br):
    t_hi = t.astype(BF16)
    t_lo = (t - t_hi.astype(F32)).astype(BF16)
    both = _dot(jnp.concatenate([t_hi, t_lo], axis=1), wr)
    logits = both[:, :LANES] + both[:, LANES:] + br
    lane = lax.broadcasted_iota(jnp.int32, logits.shape, 1)
    neg = -jnp.inf
    gl = jnp.where(lane < N_EGROUPS, logits, neg)
    gmax = jnp.max(gl, axis=-1, keepdims=True)
    g_idx = jnp.min(jnp.where(gl == gmax, lane, LANES), axis=-1, keepdims=True)
    g_w = 1.0 / jnp.sum(jnp.exp(gl - gmax), axis=-1, keepdims=True)
    in_grp = jnp.logical_and(jnp.logical_and(lane >= ROUTER_LANE0, lane < ROUTER_LANE0 + N_EXPERTS),
                             ((lane - ROUTER_LANE0) >> 2) == g_idx)
    el = jnp.where(in_grp, logits, neg)
    pe = jnp.exp(el - jnp.max(el, axis=-1, keepdims=True))
    pe = pe / jnp.sum(pe, axis=-1, keepdims=True)
    cand = jnp.where(in_grp, pe, -1.0)
    v1 = jnp.max(cand, axis=-1, keepdims=True)
    i1 = jnp.min(jnp.where(cand == v1, lane, LANES), axis=-1, keepdims=True)
    cand2 = jnp.where(lane == i1, -1.0, cand)
    v2 = jnp.max(cand2, axis=-1, keepdims=True)
    i2 = jnp.min(jnp.where(jnp.logical_and(cand2 == v2, in_grp), lane, LANES), axis=-1, keepdims=True)
    den = v1 + v2
    return lane, i1, i2, g_w * v1 / den, g_w * v2 / den


def _merge_body(x_ref, ya_ref, yb_ref, ga_ref, gb_ref, wl_ref, ws_ref, wo_ref, gf_ref, wr_ref, br_ref,
                x1_ref, t_ref, rt_ref, cnt_ref, base_scr):
    step = pl.program_id(0)

    @pl.when(step == 0)
    def _():
        base_scr[...] = jnp.zeros_like(base_scr)

    a = _dot(ya_ref[...], wl_ref[...])
    b = _dot(yb_ref[...], ws_ref[...])
    merged = jax.nn.sigmoid(ga_ref[...].astype(F32)) * a + jax.nn.sigmoid(gb_ref[...].astype(F32)) * b
    x1 = x_ref[...] + _dot(merged.astype(BF16), wo_ref[...])
    x1_ref[...] = x1
    t = _rms(x1, gf_ref[...])
    t_ref[...] = t
    lane, i1, i2, wg1, wg2 = _router(t, wr_ref[...], br_ref[...])

    tm = t.shape[0]
    onehot = jnp.where(jnp.logical_or(lane == i1, lane == i2), 1.0, 0.0).astype(BF16)
    tri = (lax.broadcasted_iota(jnp.int32, (tm, tm), 1) <= lax.broadcasted_iota(jnp.int32, (tm, tm), 0)).astype(BF16)
    cum = _dot(tri, onehot) + base_scr[...]
    r1 = jnp.sum(jnp.where(lane == i1, cum, 0.0), axis=-1, keepdims=True) - 1.0
    r2 = jnp.sum(jnp.where(lane == i2, cum, 0.0), axis=-1, keepdims=True) - 1.0
    cols = (wg1, wg2, r1, r2, (i1 - ROUTER_LANE0).astype(F32), (i2 - ROUTER_LANE0).astype(F32))
    rt = jnp.zeros(cum.shape, F32)
    for k, c in enumerate(cols):
        rt = jnp.where(lane == k, c, rt)
    rt_ref[...] = rt
    base_scr[...] = cum[tm - 1:tm, :]
    cnt_ref[...] = cum[tm - 1:tm, :]


def _merge(x, ya, yb, proj, mw, tm, col_ga, col_gb):
    t, d = x.shape
    d_inner = yb.shape[1]
    return pl.pallas_call(
        _merge_body,
        grid=(t // tm,),
        in_specs=[pl.BlockSpec((tm, d), lambda i: (i, 0)),
                  pl.BlockSpec((tm, d), lambda i: (i, 0)),
                  pl.BlockSpec((tm, d_inner), lambda i: (i, 0)),
                  pl.BlockSpec((tm, d), lambda i: (i, col_ga // d)),
                  pl.BlockSpec((tm, d), lambda i: (i, col_gb // d)),
                  _const_spec((d, d)), _const_spec((d_inner, d)), _const_spec((d, d)),
                  _const_spec((1, d)), _const_spec((2 * d, 2 * LANES)), _const_spec((1, LANES))],
        out_specs=[pl.BlockSpec((tm, d), lambda i: (i, 0)),
                   pl.BlockSpec((tm, d), lambda i: (i, 0)),
                   pl.BlockSpec((tm, LANES), lambda i: (i, 0)),
                   pl.BlockSpec((1, LANES), lambda i: (0, 0))],
        out_shape=[jax.ShapeDtypeStruct((t, d), F32),
                   jax.ShapeDtypeStruct((t, d), F32),
                   jax.ShapeDtypeStruct((t, LANES), F32),
                   jax.ShapeDtypeStruct((1, LANES), F32)],
        scratch_shapes=[pltpu.VMEM((1, LANES), F32)],
        compiler_params=_cparams(("arbitrary",)),
        name="merge_router",
    )(x, ya, yb, proj, proj, mw["w_br_lru"], mw["w_br_ssd"], mw["w_out"], mw["g_ffn"], mw["w_router"],
      mw["b_router"])


def _dispatch_body(dest_ref, zb_ref, t_ref, o_ref, zero_scr, sem, zsem, *, tmg):
    step = pl.program_id(0)
    tm = t_ref.shape[0]

    @pl.when(step == 0)
    def _():
        zero_scr[...] = jnp.zeros_like(zero_scr)

        def zcopy(j):
            return pltpu.make_async_copy(zero_scr, o_ref.at[pl.ds(pl.multiple_of(zb_ref[j] * tmg, tmg), tmg)], zsem)

        for j in range(zb_ref.shape[0]):
            pl.when(zb_ref[j] >= 0)(lambda j=j: zcopy(j).start())
        for j in range(zb_ref.shape[0]):
            pl.when(zb_ref[j] >= 0)(lambda j=j: zcopy(j).wait())

    def issue(r, carry):
        for k in range(2):
            row = dest_ref[2 * (step * tm + r) + k]
            pltpu.make_async_copy(t_ref.at[pl.ds(r, 1)], o_ref.at[pl.ds(row, 1)], sem).start()
        return carry

    lax.fori_loop(0, tm, issue, 0, unroll=8)
    for k in range(2):
        pltpu.make_async_copy(t_ref, o_ref.at[pl.ds(0, tm)], sem).wait()


def _dispatch(dest, zero_blocks, t, tm, n_rows, tmg):
    n, d = t.shape
    return pl.pallas_call(
        functools.partial(_dispatch_body, tmg=tmg),
        grid_spec=pltpu.PrefetchScalarGridSpec(
            num_scalar_prefetch=2,
            grid=(n // tm,),
            in_specs=[pl.BlockSpec((tm, d), lambda i, *_: (i, 0))],
            out_specs=pl.BlockSpec(memory_space=pl.ANY),
            scratch_shapes=[pltpu.VMEM((tmg, d), F32), pltpu.SemaphoreType.DMA(()), pltpu.SemaphoreType.DMA(())]),
        out_shape=jax.ShapeDtypeStruct((n_rows, d), F32),
        compiler_params=_cparams(("arbitrary",)),
        name="moe_dispatch",
    )(dest, zero_blocks, t)


def _expert_body(te_ref, nt_ref, x_ref, w1_ref, w3_ref, w2_ref, y_ref):
    real = pl.program_id(0) < nt_ref[0]

    @pl.when(real)
    def _():
        x = x_ref[...].astype(BF16)
        h1 = _dot(x, w1_ref[0])
        h3 = _dot(x, w3_ref[0])
        y_ref[...] = _dot((h1 * jax.nn.sigmoid(h1) * h3).astype(BF16), w2_ref[0])

    @pl.when(jnp.logical_not(real))
    def _():
        y_ref[...] = jnp.zeros_like(y_ref)


def _experts(tile_expert, n_tiles, xs, w1, w3, w2, tmg):
    d = xs.shape[1]
    dff = w1.shape[2]
    row_spec = pl.BlockSpec((tmg, d), lambda i, te, nt: (i, 0))
    return pl.pallas_call(
        _expert_body,
        grid_spec=pltpu.PrefetchScalarGridSpec(
            num_scalar_prefetch=2,
            grid=(xs.shape[0] // tmg,),
            in_specs=[row_spec,
                      pl.BlockSpec((1, d, dff), lambda i, te, nt: (te[i], 0, 0)),
                      pl.BlockSpec((1, d, dff), lambda i, te, nt: (te[i], 0, 0)),
                      pl.BlockSpec((1, dff, d), lambda i, te, nt: (te[i], 0, 0))],
            out_specs=row_spec),
        out_shape=jax.ShapeDtypeStruct(xs.shape, F32),
        compiler_params=_cparams(("arbitrary",)),
        name="moe_experts",
    )(tile_expert, n_tiles, xs, w1, w3, w2)


def _ple_body(dest_ref, x_ref, rt_ref, p_ref, wp_ref, gp_ref, gg_ref, wg_ref, gfin_ref, y_hbm, o_ref, gbuf, sem):
    step = pl.program_id(0)
    tm = x_ref.shape[0]

    def gather(tile, slot):
        def issue(r, carry):
            for k in range(2):
                row = dest_ref[2 * (tile * tm + r) + k]
                pltpu.make_async_copy(y_hbm.at[pl.ds(row, 1)], gbuf.at[slot, pl.ds(k * tm + r, 1)],
                                      sem.at[slot]).start()
            return carry

        lax.fori_loop(0, tm, issue, 0, unroll=8)

    @pl.when(step == 0)
    def _():
        gather(0, 0)

    @pl.when(step + 1 < pl.num_programs(0))
    def _():
        gather(step + 1, (step + 1) % 2)

    slot = step % 2
    pltpu.make_async_copy(y_hbm.at[pl.ds(0, 2 * tm)], gbuf.at[slot], sem.at[slot]).wait()
    rt = rt_ref[...]
    x = x_ref[...] + rt[:, 0:1] * gbuf[slot, 0:tm, :] + rt[:, 1:2] * gbuf[slot, tm:2 * tm, :]
    e = _rms(_dot(p_ref[...].astype(BF16), wp_ref[...]), gp_ref[...])
    gate = jax.nn.sigmoid(_dot(_rms(x, gg_ref[...]).astype(BF16), wg_ref[...]))
    o_ref[...] = _rms(x + gate * e, gfin_ref[...])


def _ple(dest, x1, rt, p, y_sorted, pw, tm):
    n, d = x1.shape
    dp = p.shape[1]
    const = lambda shape: pl.BlockSpec(shape, lambda i, *_: (0,) * len(shape))
    return pl.pallas_call(
        _ple_body,
        grid_spec=pltpu.PrefetchScalarGridSpec(
            num_scalar_prefetch=1,
            grid=(n // tm,),
            in_specs=[pl.BlockSpec((tm, d), lambda i, *_: (i, 0)),
                      pl.BlockSpec((tm, LANES), lambda i, *_: (i, 0)),
                      pl.BlockSpec((tm, dp), lambda i, *_: (i, 0)),
                      const((dp, d)), const((1, d)), const((1, d)), const((d, d)), const((1, d)),
                      pl.BlockSpec(memory_space=pl.ANY)],
            out_specs=pl.BlockSpec((tm, d), lambda i, *_: (i, 0)),
            scratch_shapes=[pltpu.VMEM((2, 2 * tm, d), F32), pltpu.SemaphoreType.DMA((2,))]),
        out_shape=jax.ShapeDtypeStruct((n, d), F32),
        compiler_params=_cparams(("arbitrary",)),
        name="combine_ple_final",
    )(dest, x1, rt, p, pw["w_ple_proj"], pw["g_ple"], pw["g_ple_gate"], pw["w_ple_gate"], pw["g_final"], y_sorted)


def _pick_tile(n, pref):
    t = min(n, pref)
    while n % t:
        t //= 2
    return t


MOE_ROW_TILE = 256


def _split_router(w):
    hi = w.astype(BF16)
    lo = (w - hi.astype(F32)).astype(BF16)
    return jnp.concatenate([jnp.concatenate([hi, lo], axis=1),
                            jnp.concatenate([hi, jnp.zeros_like(hi)], axis=1)], axis=0)


def _token_tail(x1, t, rt, cnt, p, lw, tm):
    n = x1.shape[0]
    tmg = MOE_ROW_TILE
    n_blocks = pl.cdiv(2 * n, tmg) + N_EXPERTS
    counts = cnt[0, ROUTER_LANE0:ROUTER_LANE0 + N_EXPERTS].astype(jnp.int32)
    tiles = (counts + tmg - 1) // tmg
    ends = jnp.cumsum(tiles)
    n_tiles = ends[-1]
    dest = (jnp.take((ends - tiles) * tmg, rt[:, 4:6].astype(jnp.int32)) + rt[:, 2:4].astype(jnp.int32)).reshape(2 * n)
    blk = jnp.arange(n_blocks, dtype=jnp.int32)
    tile_expert = jnp.sum((jnp.minimum(blk, n_tiles - 1)[:, None] >= ends[None, :]).astype(jnp.int32), axis=1)
    tail = n_tiles + blk[:N_EXPERTS]
    zero_blocks = jnp.concatenate([jnp.where(tiles > 0, ends - 1, -1),
                                   jnp.where(tail < n_blocks, tail, -1)]).astype(jnp.int32)

    sorted_t = _dispatch(dest, zero_blocks, t, tm, n_blocks * tmg, tmg)
    y_sorted = _experts(tile_expert, n_tiles.reshape(1), sorted_t, lw["w1"], lw["w3"], lw["w2"], tmg)
    return _ple(dest, x1, rt, p, y_sorted, lw, tm)


def kernel(x_prompt, x_sample, state_lru_h, state_lru_conv, state_ssd, state_ssd_conv, p_prompt, p_sample, g_mix, w_in, lru_conv_w, lru_conv_b, lru_wa, lru_ba, lru_wx, lru_bx, lru_lambda, ssd_conv_w, ssd_conv_b, ssd_dt_bias, ssd_A_log, ssd_D, ssd_norm_g, w_br_lru, w_br_ssd, w_out, g_ffn, w_router_g, b_router_g, w_router_e, b_router_e, w1, w3, w2, w_ple_proj, g_ple, g_ple_gate, w_ple_gate, g_final):
    depth = w_in.shape[0]
    assert depth == 1, "one decoder layer per call"
    bp, lp, d = x_prompt.shape
    bs, ls, _ = x_sample.shape
    w_lru = state_lru_h.shape[-1]
    heads, hdim, nstate = state_ssd.shape[2:]
    d_inner = heads * hdim
    cdim = state_ssd_conv.shape[-1]
    assert hdim == SSD_HEADDIM and nstate == SSD_STATE and heads <= LANES and ls < SUBLANES
    gw = d_inner // SSD_GROUPS

    o_dt = 2 * w_lru + d_inner + cdim
    wi = w_in[0]
    w_proj = jnp.concatenate(
        [wi[:, :o_dt], wi[:, o_dt + heads:], wi[:, o_dt:o_dt + heads], jnp.zeros((d, LANES - heads), wi.dtype)],
        axis=1).astype(BF16)
    col_z, col_xbc = 2 * w_lru, 2 * w_lru + d_inner
    col_ga, col_gb, col_dt = o_dt, o_dt + d, o_dt + 2 * d
    n_proj = w_proj.shape[1]
    row = lambda v: v.reshape(1, -1).astype(F32)
    pad_heads = lambda v: jnp.pad(v.astype(F32), (0, LANES - heads)).reshape(1, LANES)
    lw = {
        "lru_conv_w": lru_conv_w[0], "lru_conv_b": row(lru_conv_b[0]),
        "wax": jnp.concatenate([lru_wa[0], lru_wx[0]], axis=-1).astype(BF16),
        "lru_ba": row(lru_ba[0]), "lru_bx": row(lru_bx[0]), "lru_lambda": row(lru_lambda[0]),
        "ssd_conv_w": ssd_conv_w[0], "ssd_conv_b": row(ssd_conv_b[0]),
        "dt_bias": pad_heads(ssd_dt_bias[0]), "A": pad_heads(-jnp.exp(ssd_A_log[0].astype(F32))),
        "D": row(jnp.repeat(ssd_D[0], hdim)), "ssd_norm_g": row(ssd_norm_g[0]),
        "head_expand": jnp.tile(jnp.arange(LANES)[:, None] == jnp.arange(d_inner)[None, :] // hdim, (2, 1)).astype(BF16),
        "w_br_lru": w_br_lru[0].astype(BF16), "w_br_ssd": w_br_ssd[0].astype(BF16), "w_out": w_out[0].astype(BF16),
        "g_ffn": row(g_ffn[0]),
        "w_router": _split_router(jnp.concatenate([w_router_g[0], w_router_e[0],
                                                   jnp.zeros((d, LANES - N_EGROUPS - N_EXPERTS), F32)], axis=1)),
        "b_router": jnp.concatenate([b_router_g[0], b_router_e[0],
                                     jnp.zeros((LANES - N_EGROUPS - N_EXPERTS,), F32)]).reshape(1, LANES),
        "w1": w1[0].astype(BF16), "w3": w3[0].astype(BF16), "w2": w2[0].astype(BF16),
        "w_ple_proj": w_ple_proj[0].astype(BF16), "g_ple": row(g_ple[0]), "g_ple_gate": row(g_ple_gate[0]),
        "w_ple_gate": w_ple_gate[0].astype(BF16), "g_final": row(g_final),
    }
    g_mix_r = row(g_mix[0])
    tn = n_proj // 9 if n_proj % (9 * LANES) == 0 else LANES

    tp = bp * lp
    xp = x_prompt.reshape(tp, d)
    tm_p = _pick_tile(tp, 1024)
    proj_p = _inproj(xp, g_mix_r, w_proj, tm_p, tn)
    proj_p3 = proj_p.reshape(bp, lp, n_proj)
    ya_p, hl_p, lbuf_p = _lru_prompt(proj_p3, jnp.zeros((bp, CONV_W - 1, w_lru), F32), jnp.zeros((bp, w_lru), F32),
                                     lw, _pick_tile(lp, 256), True)
    yb_p, s_p, sbuf_p = _ssd_prompt(proj_p3, lw, col_xbc, col_z, col_dt)
    tm_tail_p = _pick_tile(tp, 512)
    x1_p, t_p, rt_p, cnt_p = _merge(xp, ya_p.reshape(tp, w_lru), yb_p.reshape(tp, d_inner), proj_p, lw,
                                    tm_tail_p, col_ga, col_gb)
    y_p = _token_tail(x1_p, t_p, rt_p, cnt_p, p_prompt[0].reshape(tp, -1), lw, tm_tail_p)

    ts = bs * ls
    xs = x_sample.reshape(ts, d)
    tm_s = _pick_tile(ts, 512)
    proj_s = _inproj(xs, g_mix_r, w_proj, tm_s, tn)
    to_tmajor = lambda v, n: v.reshape(bs, n, -1).transpose(1, 0, 2).reshape(n * bs, -1)
    from_tmajor = lambda v, n: v.reshape(n, bs, -1).transpose(1, 0, 2)
    ya_t, hl_s, lbuf_t = _lru_step(to_tmajor(proj_s[:, :w_lru], ls), to_tmajor(proj_s[:, w_lru:2 * w_lru], ls),
                                   to_tmajor(state_lru_conv[0], CONV_W - 1), state_lru_h[0], lw, ls)
    ya_s = from_tmajor(ya_t, ls).reshape(ts, w_lru)
    lbuf_s = from_tmajor(lbuf_t, CONV_W - 1)
    yb_s, s_s, sbuf_s = _ssd_step(proj_s, state_ssd_conv[0], state_ssd[0].reshape(bs, SSD_GROUPS, gw, nstate), lw,
                                  ls, _pick_tile(bs, 4), col_xbc, col_z, col_dt)
    x1_s, t_s, rt_s, cnt_s = _merge(xs, ya_s, yb_s, proj_s, lw, tm_s, col_ga, col_gb)
    y_s = _token_tail(x1_s, t_s, rt_s, cnt_s, p_sample[0].reshape(ts, -1), lw, tm_s)

    return (y_p.reshape(bp, lp, d), y_s.reshape(bs, ls, d),
            hl_p.reshape(1, bp, w_lru), lbuf_p[None],
            s_p.reshape(1, bp, heads, hdim, nstate), sbuf_p[None],
            hl_s[None], lbuf_s[None],
            s_s.reshape(1, bs, heads, hdim, nstate), sbuf_s[None])
```

```python
import functools

import jax
import jax.numpy as jnp
from jax import lax
from jax.experimental import pallas as pl
from jax.experimental.pallas import tpu as pltpu

F32 = jnp.float32
BF16 = jnp.bfloat16

EPS = 1e-6
CONV_W = 4
LRU_BLOCKS = 8
LRU_C = 8.0
SSD_HEADDIM = 64
SSD_GROUPS = 8
SSD_STATE = 128
SSD_CHUNK = 128
N_EGROUPS = 4
EXP_PER_GROUP = 4
N_EXPERTS = N_EGROUPS * EXP_PER_GROUP

LANES = 128
SUBLANES = 8
VMEM_LIMIT = 52 * 1024 * 1024
ROUTER_LANE0 = N_EGROUPS


def _cparams(sem):
    return pltpu.CompilerParams(dimension_semantics=sem, vmem_limit_bytes=VMEM_LIMIT)


def _dot(a, b):
    return jnp.dot(a, b, preferred_element_type=F32)


def _dot_nt(a, b):
    return lax.dot_general(a, b, (((1,), (1,)), ((), ())), preferred_element_type=F32)


def _dot_tn(a, b):
    return lax.dot_general(a, b, (((0,), (0,)), ((), ())), preferred_element_type=F32)


def _dot_f32(a, b):
    return jnp.dot(a, b, precision=lax.Precision.HIGHEST, preferred_element_type=F32)


def _rms(x, g):
    return x * lax.rsqrt(jnp.mean(x * x, axis=-1, keepdims=True) + EPS) * g


def _const_spec(shape):
    nd = len(shape)
    return pl.BlockSpec(shape, lambda *_: (0,) * nd)


def _inproj_body(x_ref, g_ref, wh_ref, wt_ref, o_ref, h_scr, *, n_head):
    j = pl.program_id(1)

    @pl.when(j == 0)
    def _():
        h_scr[...] = _rms(x_ref[...], g_ref[...]).astype(BF16)

    @pl.when(j < n_head)
    def _():
        o_ref[...] = _dot(h_scr[...], wh_ref[0].astype(BF16)).astype(o_ref.dtype)

    @pl.when(j >= n_head)
    def _():
        o_ref[...] = _dot(h_scr[...], wt_ref[...].astype(BF16)).astype(o_ref.dtype)


def _inproj(x, g, w_in, w_tail, tm, tn, n_head):
    t, d = x.shape
    n_tail = w_tail.shape[1] // tn
    return pl.pallas_call(
        functools.partial(_inproj_body, n_head=n_head),
        grid=(t // tm, n_head + n_tail),
        in_specs=[pl.BlockSpec((tm, d), lambda i, j: (i, 0)),
                  pl.BlockSpec((1, d), lambda i, j: (0, 0)),
                  pl.BlockSpec((1, d, tn), lambda i, j: (0, 0, jnp.minimum(j, n_head - 1))),
                  pl.BlockSpec((d, tn), lambda i, j: (0, jnp.maximum(j - n_head, 0)))],
        out_specs=pl.BlockSpec((tm, tn), lambda i, j: (i, j)),
        out_shape=jax.ShapeDtypeStruct((t, (n_head + n_tail) * tn), BF16),
        scratch_shapes=[pltpu.VMEM((tm, d), BF16)],
        compiler_params=_cparams(("parallel", "arbitrary")),
        name="inproj",
    )(x, g, w_in, w_tail)


def _lru_gates(u, wax_ref, ba, bx, lam):
    bw = u.shape[1] // LRU_BLOCKS
    r_parts, i_parts = [], []
    for n in range(LRU_BLOCKS):
        ri = _dot(u[:, n * bw:(n + 1) * bw].astype(BF16), wax_ref[n])
        r_parts.append(ri[:, :bw])
        i_parts.append(ri[:, bw:])
    r = jax.nn.sigmoid(jnp.concatenate(r_parts, axis=1) + ba)
    i = jax.nn.sigmoid(jnp.concatenate(i_parts, axis=1) + bx)
    log_a = LRU_C * r * jax.nn.log_sigmoid(lam)
    a = jnp.exp(log_a)
    m2 = -jnp.tanh(log_a) * (a * a + 1.0)
    mult = jnp.where(m2 > 0.0, m2 * lax.rsqrt(m2), 0.0)
    return a, i, mult


def _lru_prompt_body(xin_ref, gate_ref, buf0_ref, h0_ref, cw_ref, cb_ref, wax_ref, ba_ref, bx_ref, lam_ref,
                     ya_ref, hlast_ref, bufout_ref, halo_scr, h_scr, *, reset_first):
    j = pl.program_id(1)
    tt = xin_ref.shape[1]

    @pl.when(j == 0)
    def _():
        halo_scr[...] = jnp.zeros_like(halo_scr)
        halo_scr[SUBLANES - (CONV_W - 1):SUBLANES, :] = buf0_ref[0]
        h_scr[...] = h0_ref[0]

    x = xin_ref[0].astype(F32)
    xpad = jnp.concatenate([halo_scr[...], x], axis=0)
    cw = cw_ref[...]
    u = cb_ref[...] + sum(xpad[SUBLANES - (CONV_W - 1) + k:SUBLANES - (CONV_W - 1) + k + tt] * cw[k:k + 1]
                          for k in range(CONV_W))
    halo_scr[...] = x[tt - SUBLANES:tt]

    a, i, mult = _lru_gates(u, wax_ref, ba_ref[...], bx_ref[...], lam_ref[...])
    row = lax.broadcasted_iota(jnp.int32, a.shape, 0)
    if reset_first:
        first = jnp.logical_and(row == 0, j == 0)
        mult = jnp.where(first, 1.0, mult)
        a = jnp.where(first, 0.0, a)
    v = u * i * mult

    width = a.shape[1]
    a = a.reshape(tt // SUBLANES, SUBLANES, width)
    v = v.reshape(tt // SUBLANES, SUBLANES, width)
    sub = lax.broadcasted_iota(jnp.int32, a.shape, 1)
    s = 1
    while s < SUBLANES:
        keep = sub >= s
        v = jnp.where(keep, a * pltpu.roll(v, s, axis=1) + v, v)
        a = jnp.where(keep, a * pltpu.roll(a, s, axis=1), a)
        s *= 2
    carry = h_scr[...]
    groups = []
    for g in range(tt // SUBLANES):
        hg = a[g] * carry + v[g]
        carry = hg[SUBLANES - 1:SUBLANES]
        groups.append(hg)
    h = jnp.concatenate(groups, axis=0)
    h_scr[...] = carry
    ya_ref[0] = (h * jax.nn.gelu(gate_ref[0].astype(F32))).astype(ya_ref.dtype)

    @pl.when(j == pl.num_programs(1) - 1)
    def _():
        hlast_ref[0] = h[tt - 1:tt]
        bufout_ref[0] = x[tt - (CONV_W - 1):tt]


def _lru_prompt(proj3, buf0, h0, lw, tt, reset_first):
    b, l, _ = proj3.shape
    w = h0.shape[-1]
    body = functools.partial(_lru_prompt_body, reset_first=reset_first)
    return pl.pallas_call(
        body,
        grid=(b, l // tt),
        in_specs=[pl.BlockSpec((1, tt, w), lambda i, j: (i, j, 0)),
                  pl.BlockSpec((1, tt, w), lambda i, j: (i, j, 1)),
                  pl.BlockSpec((1, CONV_W - 1, w), lambda i, j: (i, 0, 0)),
                  pl.BlockSpec((1, 1, w), lambda i, j: (i, 0, 0)),
                  _const_spec((CONV_W, w)), _const_spec((1, w)),
                  _const_spec(lw["wax"].shape), _const_spec((1, w)), _const_spec((1, w)), _const_spec((1, w))],
        out_specs=[pl.BlockSpec((1, tt, w), lambda i, j: (i, j, 0)),
                   pl.BlockSpec((1, 1, w), lambda i, j: (i, 0, 0)),
                   pl.BlockSpec((1, CONV_W - 1, w), lambda i, j: (i, 0, 0))],
        out_shape=[jax.ShapeDtypeStruct((b, l, w), BF16),
                   jax.ShapeDtypeStruct((b, 1, w), F32),
                   jax.ShapeDtypeStruct((b, CONV_W - 1, w), F32)],
        scratch_shapes=[pltpu.VMEM((SUBLANES, w), F32), pltpu.VMEM((1, w), F32)],
        compiler_params=_cparams(("parallel", "arbitrary")),
        name="lru_prompt",
    )(proj3, proj3, buf0, h0.reshape(b, 1, w), lw["lru_conv_w"], lw["lru_conv_b"], lw["wax"],
      lw["lru_ba"], lw["lru_bx"], lw["lru_lambda"])


def _lru_step_body(xin_ref, gate_ref, buf_ref, h0_ref, cw_ref, cb_ref, wax_ref, ba_ref, bx_ref, lam_ref,
                   ya_ref, hlast_ref, bufout_ref, *, steps):
    bsz = h0_ref.shape[0]
    n = steps * bsz
    x = xin_ref[...].astype(F32)
    xx = jnp.concatenate([buf_ref[...], x], axis=0)
    cw = cw_ref[...]
    u = cb_ref[...] + sum(xx[k * bsz:k * bsz + n] * cw[k:k + 1] for k in range(CONV_W))
    a, i, mult = _lru_gates(u, wax_ref, ba_ref[...], bx_ref[...], lam_ref[...])
    v = u * i * mult
    h = h0_ref[...]
    for t in range(steps):
        sl = slice(t * bsz, (t + 1) * bsz)
        h = a[sl] * h + v[sl]
        ya_ref[sl, :] = (h * jax.nn.gelu(gate_ref[sl, :].astype(F32))).astype(ya_ref.dtype)
    hlast_ref[...] = h
    bufout_ref[...] = xx[steps * bsz:(steps + CONV_W - 1) * bsz]


def _lru_step(xin_t, gate_t, buf_t, h0, lw, steps):
    bsz, w = h0.shape
    body = functools.partial(_lru_step_body, steps=steps)
    return pl.pallas_call(
        body,
        out_shape=[jax.ShapeDtypeStruct((steps * bsz, w), BF16),
                   jax.ShapeDtypeStruct((bsz, w), F32),
                   jax.ShapeDtypeStruct(((CONV_W - 1) * bsz, w), F32)],
        compiler_params=pltpu.CompilerParams(vmem_limit_bytes=VMEM_LIMIT),
        name="lru_step",
    )(xin_t, gate_t, buf_t, h0, lw["lru_conv_w"], lw["lru_conv_b"], lw["wax"],
      lw["lru_ba"], lw["lru_bx"], lw["lru_lambda"])


def _expand_heads(cols, e2):
    q = cols[0].shape[0]
    v = jnp.concatenate(cols, axis=0)
    hi = v.astype(BF16)
    lo = (v - hi.astype(F32)).astype(BF16)
    out = _dot(jnp.concatenate([hi, lo], axis=1), e2)
    return [out[i * q:(i + 1) * q] for i in range(len(cols))]


def _ssd_chunk(xc, dt, p, e2, s_get, s_set, t_col, t_row, n_valid):
    q = xc.shape[0]
    gn = SSD_GROUPS * SSD_STATE
    d_inner = xc.shape[1] - 2 * gn
    hpg = d_inner // SSD_HEADDIM // SSD_GROUPS
    gw = hpg * SSD_HEADDIM

    causal = t_col >= t_row
    if n_valid < q:
        dt = jnp.where(lax.broadcasted_iota(jnp.int32, dt.shape, 0) < n_valid, dt, 0.0)
    a = dt * p["A"]
    cum = _dot_f32(causal.astype(F32), a)
    cum_t = cum.T
    dec_t = jnp.exp(cum_t[:, q - 1:q])
    dt_x, to_end_x, ecum_x = _expand_heads([dt, jnp.exp(cum[q - 1:q, :] - cum), jnp.exp(cum)], e2)

    xs = xc[:, :d_inner]
    xdt = xs * dt_x
    xw = (xdt * to_end_x).astype(BF16)
    xdt_m = xdt.astype(BF16) if q % (2 * SUBLANES) == 0 else xdt
    lane_head = lax.broadcasted_iota(jnp.int32, (1, gw), 1) // SSD_HEADDIM
    y_groups = []
    for g in range(SSD_GROUPS):
        sl = slice(g * gw, (g + 1) * gw)
        bg = xc[:, d_inner + g * SSD_STATE:d_inner + (g + 1) * SSD_STATE].astype(BF16)
        cg = xc[:, d_inner + gn + g * SSD_STATE:d_inner + gn + (g + 1) * SSD_STATE].astype(BF16)
        s_old = s_get(g)
        cb = _dot_nt(cg, bg)
        y_off = _dot_nt(cg, s_old.astype(BF16))
        m_heads, x_heads, s_dec = [], [], []
        for hh in range(hpg):
            h = g * hpg + hh
            decay = jnp.exp(jnp.where(causal, cum[:, h:h + 1] - cum_t[h:h + 1, :], -jnp.inf))
            m_heads.append((cb * decay).astype(BF16))
            x_heads.append(jnp.where(lane_head == hh, xdt_m[:, sl], 0.0))
            s_dec.append(s_old[hh * SSD_HEADDIM:(hh + 1) * SSD_HEADDIM, :] * dec_t[h:h + 1, :])
        y_diag = _dot(jnp.concatenate(m_heads, axis=1), jnp.concatenate(x_heads, axis=0).astype(BF16))
        s_set(g, jnp.concatenate(s_dec, axis=0) + _dot_tn(xw[:, sl], bg))
        y_groups.append(y_diag + y_off * ecum_x[:, sl])
    return jnp.concatenate(y_groups, axis=1) + p["D"] * xs


def _ssd_gate_norm(y, z, p):
    gw = y.shape[1] // SSD_GROUPS
    zf = z.astype(F32)
    out = []
    for g in range(SSD_GROUPS):
        sl = slice(g * gw, (g + 1) * gw)
        v = y[:, sl] * (zf[:, sl] * jax.nn.sigmoid(zf[:, sl]))
        out.append(v * lax.rsqrt(jnp.mean(v * v, axis=-1, keepdims=True) + EPS) * p["norm_g"][:, sl])
    return jnp.concatenate(out, axis=1)


def _ssd_conv(xpad, q, cw, cb):
    base = SUBLANES - (CONV_W - 1)
    y = cb + sum(xpad[base + k:base + k + q] * cw[k:k + 1] for k in range(CONV_W))
    return y * jax.nn.sigmoid(y)


def _softplus(x):
    return jax.nn.softplus(x)


def _ssd_params(cw_ref, cb_ref, dtb_ref, a_ref, d_ref, ng_ref):
    return {"cw": cw_ref[...], "cb": cb_ref[...], "dt_bias": dtb_ref[...], "A": a_ref[...],
            "D": d_ref[...], "norm_g": ng_ref[...]}


def _ssd_prompt_body(xbc_ref, z_ref, dt_ref, cw_ref, cb_ref, dtb_ref, a_ref, d_ref, ng_ref, e2_ref,
                     yb_ref, sout_ref, bufout_ref, x_scr, dt_scr, y_scr, s_scr):
    c = pl.program_id(1)
    q = xbc_ref.shape[1]
    half = q // 2
    n_xslab = x_scr.shape[0]
    base = SUBLANES - (CONV_W - 1)

    @pl.when(c == 0)
    def _():
        x_scr[:, 0:SUBLANES, :] = jnp.zeros((n_xslab, SUBLANES, LANES), F32)
        s_scr[...] = jnp.zeros_like(s_scr)

    @pl.when(c > 0)
    def _():
        x_scr[:, 0:SUBLANES, :] = x_scr[:, q:q + SUBLANES, :]

    p = _ssd_params(cw_ref, cb_ref, dtb_ref, a_ref, d_ref, ng_ref)
    x = xbc_ref[0].astype(F32)
    for j in range(n_xslab):
        x_scr[j, SUBLANES:SUBLANES + q, :] = x[:, j * LANES:(j + 1) * LANES]

    cols = []
    for j in range(n_xslab):
        ls = slice(j * LANES, (j + 1) * LANES)
        halves = []
        for par in range(2):
            acc = p["cb"][:, ls]
            for k in range(CONV_W):
                acc = acc + x_scr[j, pl.ds(base + k + par, half, stride=2), :] * p["cw"][k:k + 1, ls]
            halves.append(acc)
        cols.append(jnp.concatenate(halves, axis=0))
    xc = jnp.concatenate(cols, axis=1)
    xc = xc * jax.nn.sigmoid(xc)

    dt_scr[...] = _softplus(dt_ref[0].astype(F32) + p["dt_bias"])
    dt = jnp.concatenate([dt_scr[pl.ds(par, half, stride=2), :] for par in range(2)], axis=0)

    def times(shape, axis):
        pos = lax.broadcasted_iota(jnp.int32, shape, axis)
        return jnp.where(pos < half, 2 * pos, 2 * (pos - half) + 1)

    def s_set(g, v):
        s_scr[g] = v

    y = _ssd_chunk(xc, dt, p, e2_ref[...], lambda g: s_scr[g], s_set, times((q, 1), 0), times((1, q), 1), q)
    for j in range(y_scr.shape[0]):
        for par in range(2):
            y_scr[j, pl.ds(par, half, stride=2), :] = y[par * half:(par + 1) * half, j * LANES:(j + 1) * LANES]
    y = jnp.concatenate([y_scr[j] for j in range(y_scr.shape[0])], axis=1)
    yb_ref[0] = _ssd_gate_norm(y, z_ref[0], p).astype(yb_ref.dtype)

    @pl.when(c == pl.num_programs(1) - 1)
    def _():
        sout_ref[0] = s_scr[...]
        bufout_ref[0] = x[q - (CONV_W - 1):q]


def _ssd_prompt(proj3, sp, col_xbc, col_z, col_dt):
    b, l, _ = proj3.shape
    cdim = sp["ssd_conv_w"].shape[1]
    d_inner = sp["ssd_norm_g"].shape[1]
    gw = d_inner // SSD_GROUPS
    q = SSD_CHUNK if l % SSD_CHUNK == 0 else l
    assert q % (2 * SUBLANES) == 0
    return pl.pallas_call(
        _ssd_prompt_body,
        grid=(b, l // q),
        in_specs=[pl.BlockSpec((1, q, cdim), lambda i, c: (i, c, col_xbc // cdim)),
                  pl.BlockSpec((1, q, d_inner), lambda i, c: (i, c, col_z // d_inner)),
                  pl.BlockSpec((1, q, LANES), lambda i, c: (i, c, col_dt // LANES)),
                  _const_spec((CONV_W, cdim)), _const_spec((1, cdim)), _const_spec((1, LANES)),
                  _const_spec((1, LANES)), _const_spec((1, d_inner)), _const_spec((1, d_inner)),
                  _const_spec((2 * LANES, d_inner))],
        out_specs=[pl.BlockSpec((1, q, d_inner), lambda i, c: (i, c, 0)),
                   pl.BlockSpec((1, SSD_GROUPS, gw, SSD_STATE), lambda i, c: (i, 0, 0, 0)),
                   pl.BlockSpec((1, CONV_W - 1, cdim), lambda i, c: (i, 0, 0))],
        out_shape=[jax.ShapeDtypeStruct((b, l, d_inner), BF16),
                   jax.ShapeDtypeStruct((b, SSD_GROUPS, gw, SSD_STATE), F32),
                   jax.ShapeDtypeStruct((b, CONV_W - 1, cdim), F32)],
        scratch_shapes=[pltpu.VMEM((cdim // LANES, q + SUBLANES, LANES), F32),
                        pltpu.VMEM((q, LANES), F32),
                        pltpu.VMEM((d_inner // LANES, q, LANES), F32),
                        pltpu.VMEM((SSD_GROUPS, gw, SSD_STATE), F32)],
        compiler_params=_cparams(("parallel", "arbitrary")),
        name="ssd_prompt",
    )(proj3, proj3, proj3, sp["ssd_conv_w"], sp["ssd_conv_b"], sp["dt_bias"], sp["A"], sp["D"], sp["ssd_norm_g"],
      sp["head_expand"])


def _ssd_step_body(xbc_ref, z_ref, dt_ref, buf_ref, s_ref, cw_ref, cb_ref, dtb_ref, a_ref, d_ref, ng_ref, e2_ref,
                   yb_ref, sout_ref, bufout_ref, *, steps, nb):
    p = _ssd_params(cw_ref, cb_ref, dtb_ref, a_ref, d_ref, ng_ref)
    q = SUBLANES
    e2 = e2_ref[...]
    t_col = lax.broadcasted_iota(jnp.int32, (q, 1), 0)
    t_row = lax.broadcasted_iota(jnp.int32, (1, q), 1)
    x_all = xbc_ref[...].astype(F32)
    z_all = z_ref[...]
    dt_all = _softplus(dt_ref[...].astype(F32) + p["dt_bias"])
    cdim = x_all.shape[1]
    for j in range(nb):
        x = x_all[j * steps:(j + 1) * steps]
        buf = buf_ref[j]
        xpad = jnp.concatenate([jnp.zeros((SUBLANES - (CONV_W - 1), cdim), F32), buf, x,
                                jnp.zeros((SUBLANES - steps, cdim), F32)], axis=0)
        xc = _ssd_conv(xpad, q, p["cw"], p["cb"])
        pad_rows = jnp.zeros((q - steps, LANES), F32)
        dt = jnp.concatenate([dt_all[j * steps:(j + 1) * steps], pad_rows], axis=0)
        z = jnp.concatenate([z_all[j * steps:(j + 1) * steps],
                             jnp.zeros((q - steps, z_all.shape[1]), z_all.dtype)], axis=0)

        def s_set(g, v, j=j):
            sout_ref[j, g] = v

        y = _ssd_chunk(xc, dt, p, e2, lambda g, j=j: s_ref[j, g], s_set, t_col, t_row, steps)
        yb_ref[j * steps:(j + 1) * steps, :] = _ssd_gate_norm(y, z, p)[:steps].astype(yb_ref.dtype)
        bufout_ref[j] = xpad[SUBLANES + steps - (CONV_W - 1):SUBLANES + steps]


def _ssd_step(proj, buf, s0, sp, steps, nb, col_xbc, col_z, col_dt):
    bsz = s0.shape[0]
    cdim = sp["ssd_conv_w"].shape[1]
    d_inner = sp["ssd_norm_g"].shape[1]
    gw = d_inner // SSD_GROUPS
    rows = nb * steps
    body = functools.partial(_ssd_step_body, steps=steps, nb=nb)
    return pl.pallas_call(
        body,
        grid=(bsz // nb,),
        in_specs=[pl.BlockSpec((rows, cdim), lambda i: (i, col_xbc // cdim)),
                  pl.BlockSpec((rows, d_inner), lambda i: (i, col_z // d_inner)),
                  pl.BlockSpec((rows, LANES), lambda i: (i, col_dt // LANES)),
                  pl.BlockSpec((None, nb, CONV_W - 1, cdim), lambda i: (0, i, 0, 0)),
                  pl.BlockSpec((nb, SSD_GROUPS, gw, SSD_STATE), lambda i: (i, 0, 0, 0)),
                  _const_spec((CONV_W, cdim)), _const_spec((1, cdim)), _const_spec((1, LANES)),
                  _const_spec((1, LANES)), _const_spec((1, d_inner)), _const_spec((1, d_inner)),
                  _const_spec((2 * LANES, d_inner))],
        out_specs=[pl.BlockSpec((rows, d_inner), lambda i: (i, 0)),
                   pl.BlockSpec((nb, SSD_GROUPS, gw, SSD_STATE), lambda i: (i, 0, 0, 0)),
                   pl.BlockSpec((None, nb, CONV_W - 1, cdim), lambda i: (0, i, 0, 0))],
        out_shape=[jax.ShapeDtypeStruct((bsz * steps, d_inner), BF16),
                   jax.ShapeDtypeStruct(s0.shape, F32),
                   jax.ShapeDtypeStruct(buf.shape, F32)],
        compiler_params=_cparams(("parallel",)),
        name="ssd_step",
    )(proj, proj, proj, buf, s0, sp["ssd_conv_w"], sp["ssd_conv_b"], sp["dt_bias"], sp["A"], sp["D"],
      sp["ssd_norm_g"], sp["head_expand"])


def _router(t, wr, br):
    t_hi = t.astype(BF16)
    t_lo = (t - t_hi.astype(F32)).astype(BF16)
    both = _dot(jnp.concatenate([t_hi, t_lo], axis=1), wr)
    logits = both[:, :LANES] + both[:, LANES:] + br
    lane = lax.broadcasted_iota(jnp.int32, logits.shape, 1)
    neg = -jnp.inf
    gl = jnp.where(lane < N_EGROUPS, logits, neg)
    gmax = jnp.max(gl, axis=-1, keepdims=True)
    g_idx = jnp.min(jnp.where(gl == gmax, lane, LANES), axis=-1, keepdims=True)
    g_w = 1.0 / jnp.sum(jnp.exp(gl - gmax), axis=-1, keepdims=True)
    in_grp = jnp.logical_and(jnp.logical_and(lane >= ROUTER_LANE0, lane < ROUTER_LANE0 + N_EXPERTS),
                             ((lane - ROUTER_LANE0) >> 2) == g_idx)
    el = jnp.where(in_grp, logits, neg)
    pe = jnp.exp(el - jnp.max(el, axis=-1, keepdims=True))
    pe = pe / jnp.sum(pe, axis=-1, keepdims=True)
    cand = jnp.where(in_grp, pe, -1.0)
    v1 = jnp.max(cand, axis=-1, keepdims=True)
    i1 = jnp.min(jnp.where(cand == v1, lane, LANES), axis=-1, keepdims=True)
    cand2 = jnp.where(lane == i1, -1.0, cand)
    v2 = jnp.max(cand2, axis=-1, keepdims=True)
    i2 = jnp.min(jnp.where(jnp.logical_and(cand2 == v2, in_grp), lane, LANES), axis=-1, keepdims=True)
    den = v1 + v2
    return lane, i1, i2, g_w * v1 / den, g_w * v2 / den


def _rows_to_tiles(ref, val):
    n, d = val.shape
    for k in range(d // LANES):
        ref[pl.ds(k, n, stride=d // LANES), :] = val[:, k * LANES:(k + 1) * LANES]


def _tiles_to_rows(ref, n, d, start=0):
    return jnp.concatenate([ref[pl.ds(start + k, n, stride=d // LANES), :] for k in range(d // LANES)], axis=1)


def _merge_body(x_ref, ya_ref, yb_ref, ga_ref, gb_ref, wl_ref, ws_ref, wo_ref, gf_ref, wr_ref, br_ref,
                x1_ref, t_ref, rt_ref, rtt_ref, cnt_ref, base_scr):
    step = pl.program_id(0)

    @pl.when(step == 0)
    def _():
        base_scr[...] = jnp.zeros_like(base_scr)

    a = _dot(ya_ref[...], wl_ref[...])
    b = _dot(yb_ref[...], ws_ref[...])
    merged = jax.nn.sigmoid(ga_ref[...].astype(F32)) * a + jax.nn.sigmoid(gb_ref[...].astype(F32)) * b
    x1 = x_ref[...] + _dot(merged.astype(BF16), wo_ref[...])
    x1_ref[...] = x1
    t = _rms(x1, gf_ref[...])
    _rows_to_tiles(t_ref, t)
    lane, i1, i2, wg1, wg2 = _router(t, wr_ref[...], br_ref[...])

    tm = t.shape[0]
    onehot = jnp.where(jnp.logical_or(lane == i1, lane == i2), 1.0, 0.0).astype(BF16)
    tri = (lax.broadcasted_iota(jnp.int32, (tm, tm), 1) <= lax.broadcasted_iota(jnp.int32, (tm, tm), 0)).astype(BF16)
    cum = _dot(tri, onehot) + base_scr[...]
    r1 = jnp.sum(jnp.where(lane == i1, cum, 0.0), axis=-1, keepdims=True) - 1.0
    r2 = jnp.sum(jnp.where(lane == i2, cum, 0.0), axis=-1, keepdims=True) - 1.0
    cols = (wg1, wg2, r1, r2, (i1 - ROUTER_LANE0).astype(F32), (i2 - ROUTER_LANE0).astype(F32))
    rt = jnp.zeros(cum.shape, F32)
    for k, c in enumerate(cols):
        rt = jnp.where(lane == k, c, rt)
    rt_ref[...] = rt
    rtt_ref[...] = rt.T[0:SUBLANES, :]
    base_scr[...] = cum[tm - 1:tm, :]
    cnt_ref[...] = cum[tm - 1:tm, :]


def _merge(x, ya, yb, proj, mw, tm, col_ga, col_gb):
    t, d = x.shape
    d_inner = yb.shape[1]
    return pl.pallas_call(
        _merge_body,
        grid=(t // tm,),
        in_specs=[pl.BlockSpec((tm, d), lambda i: (i, 0)),
                  pl.BlockSpec((tm, d), lambda i: (i, 0)),
                  pl.BlockSpec((tm, d_inner), lambda i: (i, 0)),
                  pl.BlockSpec((tm, d), lambda i: (i, col_ga // d)),
                  pl.BlockSpec((tm, d), lambda i: (i, col_gb // d)),
                  _const_spec((d, d)), _const_spec((d_inner, d)), _const_spec((d, d)),
                  _const_spec((1, d)), _const_spec((2 * d, 2 * LANES)), _const_spec((1, LANES))],
        out_specs=[pl.BlockSpec((tm, d), lambda i: (i, 0)),
                   pl.BlockSpec((tm * d // LANES, LANES), lambda i: (i, 0)),
                   pl.BlockSpec((tm, LANES), lambda i: (i, 0)),
                   pl.BlockSpec((SUBLANES, tm), lambda i: (0, i)),
                   pl.BlockSpec((1, LANES), lambda i: (0, 0))],
        out_shape=[jax.ShapeDtypeStruct((t, d), F32),
                   jax.ShapeDtypeStruct((t * d // LANES, LANES), F32),
                   jax.ShapeDtypeStruct((t, LANES), F32),
                   jax.ShapeDtypeStruct((SUBLANES, t), F32),
                   jax.ShapeDtypeStruct((1, LANES), F32)],
        scratch_shapes=[pltpu.VMEM((1, LANES), F32)],
        compiler_params=_cparams(("arbitrary",)),
        name="merge_router",
    )(x, ya, yb, proj, proj, mw["w_br_lru"], mw["w_br_ssd"], mw["w_out"], mw["g_ffn"], mw["w_router"],
      mw["b_router"])


def _dispatch_body(dest_ref, zb_ref, t_hbm, o_ref, zero_scr, sem, zsem, *, tm, nk, tmg, n_tok):
    step = pl.program_id(0)

    @pl.when(step == 0)
    def _():
        zero_scr[...] = jnp.zeros_like(zero_scr)
        blk = tmg * nk

        def zcopy(j):
            return pltpu.make_async_copy(zero_scr, o_ref.at[pl.ds(pl.multiple_of(zb_ref[j] * blk, blk), blk)], zsem)

        for j in range(zb_ref.shape[0]):
            pl.when(zb_ref[j] >= 0)(lambda j=j: zcopy(j).start())
        for j in range(zb_ref.shape[0]):
            pl.when(zb_ref[j] >= 0)(lambda j=j: zcopy(j).wait())

    def issue(r, carry):
        tok = step * tm + r
        src = t_hbm.at[pl.ds(pl.multiple_of(tok * nk, nk), nk)]
        for k in range(2):
            row = dest_ref[k * n_tok + tok]
            pltpu.make_async_copy(src, o_ref.at[pl.ds(pl.multiple_of(row * nk, nk), nk)], sem).start()
        return carry

    lax.fori_loop(0, tm, issue, 0, unroll=8)

    def drain():
        for k in range(2):
            pltpu.make_async_copy(t_hbm.at[pl.ds(0, tm * nk)], o_ref.at[pl.ds(0, tm * nk)], sem).wait()

    pl.when(step > 0)(drain)
    pl.when(step == pl.num_programs(0) - 1)(drain)


def _dispatch(dest, zero_blocks, t_tiles, n_tok, tm, n_rows, tmg):
    nk = t_tiles.shape[0] // n_tok
    return pl.pallas_call(
        functools.partial(_dispatch_body, tm=tm, nk=nk, tmg=tmg, n_tok=n_tok),
        grid_spec=pltpu.PrefetchScalarGridSpec(
            num_scalar_prefetch=2,
            grid=(n_tok // tm,),
            in_specs=[pl.BlockSpec(memory_space=pl.ANY)],
            out_specs=pl.BlockSpec(memory_space=pl.ANY),
            scratch_shapes=[pltpu.VMEM((tmg * nk, LANES), F32), pltpu.SemaphoreType.DMA(()),
                            pltpu.SemaphoreType.DMA(())]),
        out_shape=jax.ShapeDtypeStruct((n_rows * nk, LANES), F32),
        compiler_params=_cparams(("arbitrary",)),
        name="moe_dispatch",
    )(dest, zero_blocks, t_tiles)


def _expert_body(te_ref, nt_ref, x_ref, w1_ref, w3_ref, w2_ref, y_ref, w1_scr, w3_scr, w2_scr, *, tmg):
    i = pl.program_id(0)
    real = i < nt_ref[0]
    d = w1_scr.shape[0]

    @pl.when(jnp.logical_or(i == 0, te_ref[i] != te_ref[jnp.maximum(i - 1, 0)]))
    def _():
        w1_scr[...] = w1_ref[0].astype(BF16)
        w3_scr[...] = w3_ref[0].astype(BF16)
        w2_scr[...] = w2_ref[0].astype(BF16)

    @pl.when(real)
    def _():
        x = _tiles_to_rows(x_ref, tmg, d).astype(BF16)
        h1 = _dot(x, w1_scr[...])
        h3 = _dot(x, w3_scr[...])
        _rows_to_tiles(y_ref, _dot((h1 * jax.nn.sigmoid(h1) * h3).astype(BF16), w2_scr[...]))

    @pl.when(jnp.logical_not(real))
    def _():
        y_ref[...] = jnp.zeros_like(y_ref)


def _experts(tile_expert, n_tiles, xs_tiles, w1, w3, w2, tmg):
    _, d, dff = w1.shape
    blk = tmg * d // LANES
    row_spec = pl.BlockSpec((blk, LANES), lambda i, te, nt: (i, 0))
    return pl.pallas_call(
        functools.partial(_expert_body, tmg=tmg),
        grid_spec=pltpu.PrefetchScalarGridSpec(
            num_scalar_prefetch=2,
            grid=(xs_tiles.shape[0] // blk,),
            in_specs=[row_spec,
                      pl.BlockSpec((1, d, dff), lambda i, te, nt: (te[i], 0, 0)),
                      pl.BlockSpec((1, d, dff), lambda i, te, nt: (te[i], 0, 0)),
                      pl.BlockSpec((1, dff, d), lambda i, te, nt: (te[i], 0, 0))],
            out_specs=row_spec,
            scratch_shapes=[pltpu.VMEM((d, dff), BF16), pltpu.VMEM((d, dff), BF16), pltpu.VMEM((dff, d), BF16)]),
        out_shape=jax.ShapeDtypeStruct(xs_tiles.shape, F32),
        compiler_params=_cparams(("arbitrary",)),
        name="moe_experts",
    )(tile_expert, n_tiles, xs_tiles, w1, w3, w2)


def _ple_body(dest_ref, x_ref, rt_ref, p_ref, wp_ref, gp_ref, gg_ref, wg_ref, gfin_ref, y_hbm, o_ref, gbuf, sem,
              *, n_tok):
    step = pl.program_id(0)
    tm, d = x_ref.shape
    nk = d // LANES

    def gather(tile, slot):
        def issue(r, carry):
            tok = tile * tm + r
            for k in range(2):
                row = dest_ref[k * n_tok + tok]
                pltpu.make_async_copy(y_hbm.at[pl.ds(pl.multiple_of(row * nk, nk), nk)],
                                      gbuf.at[slot, pl.ds(pl.multiple_of((k * tm + r) * nk, nk), nk)],
                                      sem.at[slot]).start()
            return carry

        lax.fori_loop(0, tm, issue, 0, unroll=8)

    @pl.when(step == 0)
    def _():
        gather(0, 0)

    @pl.when(step + 1 < pl.num_programs(0))
    def _():
        gather(step + 1, (step + 1) % 2)

    slot = step % 2
    pltpu.make_async_copy(y_hbm.at[pl.ds(0, 2 * tm * nk)], gbuf.at[slot], sem.at[slot]).wait()
    rt = rt_ref[...]
    rows = gbuf.at[slot]
    x = (x_ref[...] + rt[:, 0:1] * _tiles_to_rows(rows, tm, d)
         + rt[:, 1:2] * _tiles_to_rows(rows, tm, d, start=tm * nk))
    e = _rms(_dot(p_ref[...].astype(BF16), wp_ref[...]), gp_ref[...])
    gate = jax.nn.sigmoid(_dot(_rms(x, gg_ref[...]).astype(BF16), wg_ref[...]))
    o_ref[...] = _rms(x + gate * e, gfin_ref[...])


def _ple(dest, x1, rt, p, y_sorted, pw, tm):
    n, d = x1.shape
    dp = p.shape[1]
    const = lambda shape: pl.BlockSpec(shape, lambda i, *_: (0,) * len(shape))
    return pl.pallas_call(
        functools.partial(_ple_body, n_tok=n),
        grid_spec=pltpu.PrefetchScalarGridSpec(
            num_scalar_prefetch=1,
            grid=(n // tm,),
            in_specs=[pl.BlockSpec((tm, d), lambda i, *_: (i, 0)),
                      pl.BlockSpec((tm, LANES), lambda i, *_: (i, 0)),
                      pl.BlockSpec((tm, dp), lambda i, *_: (i, 0)),
                      const((dp, d)), const((1, d)), const((1, d)), const((d, d)), const((1, d)),
                      pl.BlockSpec(memory_space=pl.ANY)],
            out_specs=pl.BlockSpec((tm, d), lambda i, *_: (i, 0)),
            scratch_shapes=[pltpu.VMEM((2, 2 * tm * d // LANES, LANES), F32), pltpu.SemaphoreType.DMA((2,))]),
        out_shape=jax.ShapeDtypeStruct((n, d), F32),
        compiler_params=_cparams(("arbitrary",)),
        name="combine_ple_final",
    )(dest, x1, rt, p, pw["w_ple_proj"], pw["g_ple"], pw["g_ple_gate"], pw["w_ple_gate"], pw["g_final"], y_sorted)


def _pick_tile(n, pref):
    t = min(n, pref)
    while n % t:
        t //= 2
    return t


MOE_ROW_TILE = 256


def _split_router(w):
    hi = w.astype(BF16)
    lo = (w - hi.astype(F32)).astype(BF16)
    return jnp.concatenate([jnp.concatenate([hi, lo], axis=1),
                            jnp.concatenate([hi, jnp.zeros_like(hi)], axis=1)], axis=0)


def _token_tail(x1, t_tiles, rt, rtt, cnt, p, lw, tm):
    n = x1.shape[0]
    tmg = MOE_ROW_TILE
    n_blocks = pl.cdiv(2 * n, tmg) + N_EXPERTS
    counts = cnt[0, ROUTER_LANE0:ROUTER_LANE0 + N_EXPERTS].astype(jnp.int32)
    tiles = (counts + tmg - 1) // tmg
    ends = jnp.cumsum(tiles)
    n_tiles = ends[-1]
    dest = (jnp.take((ends - tiles) * tmg, rtt[4:6].astype(jnp.int32)) + rtt[2:4].astype(jnp.int32)).reshape(2 * n)
    blk = jnp.arange(n_blocks, dtype=jnp.int32)
    tile_expert = jnp.sum((jnp.minimum(blk, n_tiles - 1)[:, None] >= ends[None, :]).astype(jnp.int32), axis=1)
    tail = n_tiles + blk[:N_EXPERTS]
    zero_blocks = jnp.concatenate([jnp.where(tiles > 0, ends - 1, -1),
                                   jnp.where(tail < n_blocks, tail, -1)]).astype(jnp.int32)

    sorted_t = _dispatch(dest, zero_blocks, t_tiles, n, tm, n_blocks * tmg, tmg)
    y_sorted = _experts(tile_expert, n_tiles.reshape(1), sorted_t, lw["w1"], lw["w3"], lw["w2"], tmg)
    return _ple(dest, x1, rt, p, y_sorted, lw, tm)


def kernel(x_prompt, x_sample, state_lru_h, state_lru_conv, state_ssd, state_ssd_conv, p_prompt, p_sample, g_mix, w_in, lru_conv_w, lru_conv_b, lru_wa, lru_ba, lru_wx, lru_bx, lru_lambda, ssd_conv_w, ssd_conv_b, ssd_dt_bias, ssd_A_log, ssd_D, ssd_norm_g, w_br_lru, w_br_ssd, w_out, g_ffn, w_router_g, b_router_g, w_router_e, b_router_e, w1, w3, w2, w_ple_proj, g_ple, g_ple_gate, w_ple_gate, g_final):
    depth = w_in.shape[0]
    assert depth == 1, "one decoder layer per call"
    bp, lp, d = x_prompt.shape
    bs, ls, _ = x_sample.shape
    w_lru = state_lru_h.shape[-1]
    heads, hdim, nstate = state_ssd.shape[2:]
    d_inner = heads * hdim
    cdim = state_ssd_conv.shape[-1]
    assert hdim == SSD_HEADDIM and nstate == SSD_STATE and heads <= LANES and ls < SUBLANES
    gw = d_inner // SSD_GROUPS

    o_dt = 2 * w_lru + d_inner + cdim
    n_proj = o_dt + 2 * d + LANES
    tn = n_proj // 9 if n_proj % (9 * LANES) == 0 else LANES
    n_head = o_dt // tn
    wi = w_in[0]
    w_tail = jnp.concatenate([wi[:, n_head * tn:o_dt], wi[:, o_dt + heads:], wi[:, o_dt:o_dt + heads],
                              jnp.zeros((d, LANES - heads), wi.dtype)], axis=1)
    assert n_head >= 1 and w_tail.shape[1] == n_proj - n_head * tn and w_tail.shape[1] % tn == 0
    col_z, col_xbc = 2 * w_lru, 2 * w_lru + d_inner
    col_ga, col_gb, col_dt = o_dt, o_dt + d, o_dt + 2 * d
    row = lambda v: v.reshape(1, -1).astype(F32)
    pad_heads = lambda v: jnp.pad(v.astype(F32), (0, LANES - heads)).reshape(1, LANES)
    lw = {
        "lru_conv_w": lru_conv_w[0], "lru_conv_b": row(lru_conv_b[0]),
        "wax": jnp.concatenate([lru_wa[0], lru_wx[0]], axis=-1).astype(BF16),
        "lru_ba": row(lru_ba[0]), "lru_bx": row(lru_bx[0]), "lru_lambda": row(lru_lambda[0]),
        "ssd_conv_w": ssd_conv_w[0], "ssd_conv_b": row(ssd_conv_b[0]),
        "dt_bias": pad_heads(ssd_dt_bias[0]), "A": pad_heads(-jnp.exp(ssd_A_log[0].astype(F32))),
        "D": row(jnp.repeat(ssd_D[0], hdim)), "ssd_norm_g": row(ssd_norm_g[0]),
        "head_expand": jnp.tile(jnp.arange(LANES)[:, None] == jnp.arange(d_inner)[None, :] // hdim, (2, 1)).astype(BF16),
        "w_br_lru": w_br_lru[0].astype(BF16), "w_br_ssd": w_br_ssd[0].astype(BF16), "w_out": w_out[0].astype(BF16),
        "g_ffn": row(g_ffn[0]),
        "w_router": _split_router(jnp.concatenate([w_router_g[0], w_router_e[0],
                                                   jnp.zeros((d, LANES - N_EGROUPS - N_EXPERTS), F32)], axis=1)),
        "b_router": jnp.concatenate([b_router_g[0], b_router_e[0],
                                     jnp.zeros((LANES - N_EGROUPS - N_EXPERTS,), F32)]).reshape(1, LANES),
        "w1": w1[0], "w3": w3[0], "w2": w2[0],
        "w_ple_proj": w_ple_proj[0].astype(BF16), "g_ple": row(g_ple[0]), "g_ple_gate": row(g_ple_gate[0]),
        "w_ple_gate": w_ple_gate[0].astype(BF16), "g_final": row(g_final),
    }
    g_mix_r = row(g_mix[0])

    tp = bp * lp
    xp = x_prompt.reshape(tp, d)
    tm_p = _pick_tile(tp, 1024)
    proj_p = _inproj(xp, g_mix_r, w_in, w_tail, tm_p, tn, n_head)
    proj_p3 = proj_p.reshape(bp, lp, n_proj)
    ya_p, hl_p, lbuf_p = _lru_prompt(proj_p3, jnp.zeros((bp, CONV_W - 1, w_lru), F32), jnp.zeros((bp, w_lru), F32),
                                     lw, _pick_tile(lp, 256), True)
    yb_p, s_p, sbuf_p = _ssd_prompt(proj_p3, lw, col_xbc, col_z, col_dt)
    tm_tail_p = _pick_tile(tp, 512)
    x1_p, t_p, rt_p, rtt_p, cnt_p = _merge(xp, ya_p.reshape(tp, w_lru), yb_p.reshape(tp, d_inner), proj_p, lw,
                                           tm_tail_p, col_ga, col_gb)
    y_p = _token_tail(x1_p, t_p, rt_p, rtt_p, cnt_p, p_prompt[0].reshape(tp, -1), lw, tm_tail_p)

    ts = bs * ls
    xs = x_sample.reshape(ts, d)
    tm_s = _pick_tile(ts, 512)
    proj_s = _inproj(xs, g_mix_r, w_in, w_tail, tm_s, tn, n_head)
    to_tmajor = lambda v, n: v.reshape(bs, n, -1).transpose(1, 0, 2).reshape(n * bs, -1)
    from_tmajor = lambda v, n: v.reshape(n, bs, -1).transpose(1, 0, 2)
    ya_t, hl_s, lbuf_t = _lru_step(to_tmajor(proj_s[:, :w_lru], ls), to_tmajor(proj_s[:, w_lru:2 * w_lru], ls),
                                   to_tmajor(state_lru_conv[0], CONV_W - 1), state_lru_h[0], lw, ls)
    ya_s = from_tmajor(ya_t, ls).reshape(ts, w_lru)
    lbuf_s = from_tmajor(lbuf_t, CONV_W - 1)
    yb_s, s_s, sbuf_s = _ssd_step(proj_s, state_ssd_conv, state_ssd[0].reshape(bs, SSD_GROUPS, gw, nstate), lw,
                                  ls, _pick_tile(bs, 4), col_xbc, col_z, col_dt)
    x1_s, t_s, rt_s, rtt_s, cnt_s = _merge(xs, ya_s, yb_s, proj_s, lw, tm_s, col_ga, col_gb)
    y_s = _token_tail(x1_s, t_s, rt_s, rtt_s, cnt_s, p_sample[0].reshape(ts, -1), lw, tm_s)

    return (y_p.reshape(bp, lp, d), y_s.reshape(bs, ls, d),
            hl_p.reshape(1, bp, w_lru), lbuf_p[None],
            s_p.reshape(1, bp, heads, hdim, nstate), sbuf_p[None],
            hl_s[None], lbuf_s[None],
            s_s.reshape(1, bs, heads, hdim, nstate), sbuf_s)
```

```python
import functools

import jax
import jax.numpy as jnp
from jax import lax
from jax.experimental import pallas as pl
from jax.experimental.pallas import tpu as pltpu

F32 = jnp.float32
BF16 = jnp.bfloat16

EPS = 1e-6
CONV_W = 4
LRU_BLOCKS = 8
LRU_C = 8.0
SSD_HEADDIM = 64
SSD_GROUPS = 8
SSD_STATE = 128
SSD_CHUNK = 128
N_EGROUPS = 4
EXP_PER_GROUP = 4
N_EXPERTS = N_EGROUPS * EXP_PER_GROUP

LANES = 128
SUBLANES = 8
VMEM_LIMIT = 52 * 1024 * 1024
ROUTER_LANE0 = N_EGROUPS


def _cparams(sem):
    return pltpu.CompilerParams(dimension_semantics=sem, vmem_limit_bytes=VMEM_LIMIT)


def _dot(a, b):
    return jnp.dot(a, b, preferred_element_type=F32)


def _dot_nt(a, b):
    return lax.dot_general(a, b, (((1,), (1,)), ((), ())), preferred_element_type=F32)


def _dot_tn(a, b):
    return lax.dot_general(a, b, (((0,), (0,)), ((), ())), preferred_element_type=F32)


def _dot_f32(a, b):
    return jnp.dot(a, b, precision=lax.Precision.HIGHEST, preferred_element_type=F32)


def _rms(x, g):
    return x * lax.rsqrt(jnp.mean(x * x, axis=-1, keepdims=True) + EPS) * g


def _const_spec(shape):
    nd = len(shape)
    return pl.BlockSpec(shape, lambda *_: (0,) * nd)


def _inproj_body(x_ref, g_ref, w_ref, o_ref, h_scr):
    @pl.when(pl.program_id(1) == 0)
    def _():
        h_scr[...] = _rms(x_ref[...], g_ref[...]).astype(BF16)

    o_ref[...] = _dot(h_scr[...], w_ref[...]).astype(o_ref.dtype)


def _inproj(x, g, w, tm, tn):
    t, d = x.shape
    n = w.shape[1]
    return pl.pallas_call(
        _inproj_body,
        grid=(t // tm, n // tn),
        in_specs=[pl.BlockSpec((tm, d), lambda i, j: (i, 0)),
                  pl.BlockSpec((1, d), lambda i, j: (0, 0)),
                  pl.BlockSpec((d, tn), lambda i, j: (0, j))],
        out_specs=pl.BlockSpec((tm, tn), lambda i, j: (i, j)),
        out_shape=jax.ShapeDtypeStruct((t, n), BF16),
        scratch_shapes=[pltpu.VMEM((tm, d), BF16)],
        compiler_params=_cparams(("parallel", "arbitrary")),
        name="inproj",
    )(x, g, w)


def _lru_gates(u, wax_ref, ba, bx, lam):
    bw = u.shape[1] // LRU_BLOCKS
    r_parts, i_parts = [], []
    for n in range(LRU_BLOCKS):
        ri = _dot(u[:, n * bw:(n + 1) * bw].astype(BF16), wax_ref[n])
        r_parts.append(ri[:, :bw])
        i_parts.append(ri[:, bw:])
    r = jax.nn.sigmoid(jnp.concatenate(r_parts, axis=1) + ba)
    i = jax.nn.sigmoid(jnp.concatenate(i_parts, axis=1) + bx)
    log_a = LRU_C * r * jax.nn.log_sigmoid(lam)
    a = jnp.exp(log_a)
    m2 = -jnp.tanh(log_a) * (a * a + 1.0)
    mult = jnp.where(m2 > 0.0, m2 * lax.rsqrt(m2), 0.0)
    return a, i, mult


def _lru_prompt_body(xin_ref, gate_ref, buf0_ref, h0_ref, cw_ref, cb_ref, wax_ref, ba_ref, bx_ref, lam_ref,
                     ya_ref, hlast_ref, bufout_ref, halo_scr, h_scr, *, reset_first):
    j = pl.program_id(1)
    tt = xin_ref.shape[1]

    @pl.when(j == 0)
    def _():
        halo_scr[...] = jnp.zeros_like(halo_scr)
        halo_scr[SUBLANES - (CONV_W - 1):SUBLANES, :] = buf0_ref[0]
        h_scr[...] = h0_ref[0]

    x = xin_ref[0].astype(F32)
    xpad = jnp.concatenate([halo_scr[...], x], axis=0)
    cw = cw_ref[...]
    u = cb_ref[...] + sum(xpad[SUBLANES - (CONV_W - 1) + k:SUBLANES - (CONV_W - 1) + k + tt] * cw[k:k + 1]
                          for k in range(CONV_W))
    halo_scr[...] = x[tt - SUBLANES:tt]

    a, i, mult = _lru_gates(u, wax_ref, ba_ref[...], bx_ref[...], lam_ref[...])
    row = lax.broadcasted_iota(jnp.int32, a.shape, 0)
    if reset_first:
        first = jnp.logical_and(row == 0, j == 0)
        mult = jnp.where(first, 1.0, mult)
        a = jnp.where(first, 0.0, a)
    v = u * i * mult

    width = a.shape[1]
    a = a.reshape(tt // SUBLANES, SUBLANES, width)
    v = v.reshape(tt // SUBLANES, SUBLANES, width)
    sub = lax.broadcasted_iota(jnp.int32, a.shape, 1)
    s = 1
    while s < SUBLANES:
        keep = sub >= s
        v = jnp.where(keep, a * pltpu.roll(v, s, axis=1) + v, v)
        a = jnp.where(keep, a * pltpu.roll(a, s, axis=1), a)
        s *= 2
    carry = h_scr[...]
    groups = []
    for g in range(tt // SUBLANES):
        hg = a[g] * carry + v[g]
        carry = hg[SUBLANES - 1:SUBLANES]
        groups.append(hg)
    h = jnp.concatenate(groups, axis=0)
    h_scr[...] = carry
    ya_ref[0] = (h * jax.nn.gelu(gate_ref[0].astype(F32))).astype(ya_ref.dtype)

    @pl.when(j == pl.num_programs(1) - 1)
    def _():
        hlast_ref[0] = h[tt - 1:tt]
        bufout_ref[0] = x[tt - (CONV_W - 1):tt]


def _lru_prompt(proj3, buf0, h0, lw, tt, reset_first):
    b, l, _ = proj3.shape
    w = h0.shape[-1]
    body = functools.partial(_lru_prompt_body, reset_first=reset_first)
    return pl.pallas_call(
        body,
        grid=(b, l // tt),
        in_specs=[pl.BlockSpec((1, tt, w), lambda i, j: (i, j, 0)),
                  pl.BlockSpec((1, tt, w), lambda i, j: (i, j, 1)),
                  pl.BlockSpec((1, CONV_W - 1, w), lambda i, j: (i, 0, 0)),
                  pl.BlockSpec((1, 1, w), lambda i, j: (i, 0, 0)),
                  _const_spec((CONV_W, w)), _const_spec((1, w)),
                  _const_spec(lw["wax"].shape), _const_spec((1, w)), _const_spec((1, w)), _const_spec((1, w))],
        out_specs=[pl.BlockSpec((1, tt, w), lambda i, j: (i, j, 0)),
                   pl.BlockSpec((1, 1, w), lambda i, j: (i, 0, 0)),
                   pl.BlockSpec((1, CONV_W - 1, w), lambda i, j: (i, 0, 0))],
        out_shape=[jax.ShapeDtypeStruct((b, l, w), BF16),
                   jax.ShapeDtypeStruct((b, 1, w), F32),
                   jax.ShapeDtypeStruct((b, CONV_W - 1, w), F32)],
        scratch_shapes=[pltpu.VMEM((SUBLANES, w), F32), pltpu.VMEM((1, w), F32)],
        compiler_params=_cparams(("parallel", "arbitrary")),
        name="lru_prompt",
    )(proj3, proj3, buf0, h0.reshape(b, 1, w), lw["lru_conv_w"], lw["lru_conv_b"], lw["wax"],
      lw["lru_ba"], lw["lru_bx"], lw["lru_lambda"])


def _lru_step_body(xin_ref, gate_ref, buf_ref, h0_ref, cw_ref, cb_ref, wax_ref, ba_ref, bx_ref, lam_ref,
                   ya_ref, hlast_ref, bufout_ref, *, steps):
    bsz = h0_ref.shape[0]
    n = steps * bsz
    x = xin_ref[...].astype(F32)
    xx = jnp.concatenate([buf_ref[...], x], axis=0)
    cw = cw_ref[...]
    u = cb_ref[...] + sum(xx[k * bsz:k * bsz + n] * cw[k:k + 1] for k in range(CONV_W))
    a, i, mult = _lru_gates(u, wax_ref, ba_ref[...], bx_ref[...], lam_ref[...])
    v = u * i * mult
    h = h0_ref[...]
    for t in range(steps):
        sl = slice(t * bsz, (t + 1) * bsz)
        h = a[sl] * h + v[sl]
        ya_ref[sl, :] = (h * jax.nn.gelu(gate_ref[sl, :].astype(F32))).astype(ya_ref.dtype)
    hlast_ref[...] = h
    bufout_ref[...] = xx[steps * bsz:(steps + CONV_W - 1) * bsz]


def _lru_step(xin_t, gate_t, buf_t, h0, lw, steps):
    bsz, w = h0.shape
    body = functools.partial(_lru_step_body, steps=steps)
    return pl.pallas_call(
        body,
        out_shape=[jax.ShapeDtypeStruct((steps * bsz, w), BF16),
                   jax.ShapeDtypeStruct((bsz, w), F32),
                   jax.ShapeDtypeStruct(((CONV_W - 1) * bsz, w), F32)],
        compiler_params=pltpu.CompilerParams(vmem_limit_bytes=VMEM_LIMIT),
        name="lru_step",
    )(xin_t, gate_t, buf_t, h0, lw["lru_conv_w"], lw["lru_conv_b"], lw["wax"],
      lw["lru_ba"], lw["lru_bx"], lw["lru_lambda"])


def _expand_heads(cols, e2):
    q = cols[0].shape[0]
    v = jnp.concatenate(cols, axis=0)
    hi = v.astype(BF16)
    lo = (v - hi.astype(F32)).astype(BF16)
    out = _dot(jnp.concatenate([hi, lo], axis=1), e2)
    return [out[i * q:(i + 1) * q] for i in range(len(cols))]


def _ssd_chunk(xc, dt, p, e2, s_get, s_set, t_col, t_row, n_seg=1, n_valid=None):
    q = xc.shape[0]
    rps = q // n_seg
    gn = SSD_GROUPS * SSD_STATE
    d_inner = xc.shape[1] - 2 * gn
    hpg = d_inner // SSD_HEADDIM // SSD_GROUPS
    gw = hpg * SSD_HEADDIM

    causal = t_col >= t_row
    if n_seg > 1:
        same = (lax.broadcasted_iota(jnp.int32, (q, 1), 0) // rps) == (lax.broadcasted_iota(jnp.int32, (1, q), 1) // rps)
        causal = jnp.logical_and(same, causal)
    if n_valid is not None:
        dt = jnp.where(lax.broadcasted_iota(jnp.int32, dt.shape, 0) % rps < n_valid, dt, 0.0)
    a = dt * p["A"]
    cum = _dot_f32(causal.astype(F32), a)
    cum_t = cum.T
    total = cum[q - 1:q, :] if n_seg == 1 else _dot_f32(same.astype(F32), a)
    dt_x, to_end_x, ecum_x = _expand_heads([dt, jnp.exp(total - cum), jnp.exp(cum)], e2)

    xs = xc[:, :d_inner]
    xdt = xs * dt_x
    xw = xdt * to_end_x
    packed = rps % (2 * SUBLANES) == 0
    xdt_m = xdt.astype(BF16) if packed else xdt
    if packed:
        xw = xw.astype(BF16)
    lane_head = lax.broadcasted_iota(jnp.int32, (1, gw), 1) // SSD_HEADDIM
    y_groups = []
    for g in range(SSD_GROUPS):
        sl = slice(g * gw, (g + 1) * gw)
        bg = xc[:, d_inner + g * SSD_STATE:d_inner + (g + 1) * SSD_STATE]
        cg = xc[:, d_inner + gn + g * SSD_STATE:d_inner + gn + (g + 1) * SSD_STATE]
        if packed:
            bg, cg = bg.astype(BF16), cg.astype(BF16)
        cb = _dot_nt(cg.astype(BF16), bg.astype(BF16))
        m_heads, x_heads = [], []
        for hh in range(hpg):
            h = g * hpg + hh
            decay = jnp.exp(jnp.where(causal, cum[:, h:h + 1] - cum_t[h:h + 1, :], -jnp.inf))
            m_heads.append((cb * decay).astype(BF16))
            x_heads.append(jnp.where(lane_head == hh, xdt_m[:, sl], 0.0))
        y_diag = _dot(jnp.concatenate(m_heads, axis=1), jnp.concatenate(x_heads, axis=0).astype(BF16))
        y_off = []
        for b in range(n_seg):
            rows = slice(b * rps, (b + 1) * rps)
            s_old = s_get(b, g)
            y_off.append(_dot_nt(cg[rows].astype(BF16), s_old.astype(BF16)))
            s_dec = [s_old[hh * SSD_HEADDIM:(hh + 1) * SSD_HEADDIM, :]
                     * jnp.exp(cum_t[g * hpg + hh:g * hpg + hh + 1, (b + 1) * rps - 1:(b + 1) * rps])
                     for hh in range(hpg)]
            s_set(b, g, jnp.concatenate(s_dec, axis=0) + _dot_tn(xw[rows, sl].astype(BF16), bg[rows].astype(BF16)))
        y_off = y_off[0] if n_seg == 1 else jnp.concatenate(y_off, axis=0)
        y_groups.append(y_diag + y_off * ecum_x[:, sl])
    return jnp.concatenate(y_groups, axis=1) + p["D"] * xs


def _ssd_gate_norm(y, z, p):
    gw = y.shape[1] // SSD_GROUPS
    zf = z.astype(F32)
    out = []
    for g in range(SSD_GROUPS):
        sl = slice(g * gw, (g + 1) * gw)
        v = y[:, sl] * (zf[:, sl] * jax.nn.sigmoid(zf[:, sl]))
        out.append(v * lax.rsqrt(jnp.mean(v * v, axis=-1, keepdims=True) + EPS) * p["norm_g"][:, sl])
    return jnp.concatenate(out, axis=1)


def _ssd_conv(xpad, q, cw, cb):
    base = SUBLANES - (CONV_W - 1)
    y = cb + sum(xpad[base + k:base + k + q] * cw[k:k + 1] for k in range(CONV_W))
    return y * jax.nn.sigmoid(y)


def _softplus(x):
    return jax.nn.softplus(x)


def _ssd_params(cw_ref, cb_ref, dtb_ref, a_ref, d_ref, ng_ref):
    return {"cw": cw_ref[...], "cb": cb_ref[...], "dt_bias": dtb_ref[...], "A": a_ref[...],
            "D": d_ref[...], "norm_g": ng_ref[...]}


def _ssd_prompt_body(xbc_ref, z_ref, dt_ref, cw_ref, cb_ref, dtb_ref, a_ref, d_ref, ng_ref, e2_ref,
                     yb_ref, sout_ref, bufout_ref, x_scr, dt_scr, y_scr, s_scr):
    c = pl.program_id(1)
    q = xbc_ref.shape[1]
    half = q // 2
    n_xslab = x_scr.shape[0]
    base = SUBLANES - (CONV_W - 1)

    @pl.when(c == 0)
    def _():
        x_scr[:, 0:SUBLANES, :] = jnp.zeros((n_xslab, SUBLANES, LANES), F32)
        s_scr[...] = jnp.zeros_like(s_scr)

    @pl.when(c > 0)
    def _():
        x_scr[:, 0:SUBLANES, :] = x_scr[:, q:q + SUBLANES, :]

    p = _ssd_params(cw_ref, cb_ref, dtb_ref, a_ref, d_ref, ng_ref)
    x = xbc_ref[0].astype(F32)
    for j in range(n_xslab):
        x_scr[j, SUBLANES:SUBLANES + q, :] = x[:, j * LANES:(j + 1) * LANES]

    cols = []
    for j in range(n_xslab):
        ls = slice(j * LANES, (j + 1) * LANES)
        halves = []
        for par in range(2):
            acc = p["cb"][:, ls]
            for k in range(CONV_W):
                acc = acc + x_scr[j, pl.ds(base + k + par, half, stride=2), :] * p["cw"][k:k + 1, ls]
            halves.append(acc)
        cols.append(jnp.concatenate(halves, axis=0))
    xc = jnp.concatenate(cols, axis=1)
    xc = xc * jax.nn.sigmoid(xc)

    dt_scr[...] = _softplus(dt_ref[0].astype(F32) + p["dt_bias"])
    dt = jnp.concatenate([dt_scr[pl.ds(par, half, stride=2), :] for par in range(2)], axis=0)

    def times(shape, axis):
        pos = lax.broadcasted_iota(jnp.int32, shape, axis)
        return jnp.where(pos < half, 2 * pos, 2 * (pos - half) + 1)

    def s_set(b, g, v):
        s_scr[g] = v

    y = _ssd_chunk(xc, dt, p, e2_ref[...], lambda b, g: s_scr[g], s_set, times((q, 1), 0), times((1, q), 1))
    for j in range(y_scr.shape[0]):
        for par in range(2):
            y_scr[j, pl.ds(par, half, stride=2), :] = y[par * half:(par + 1) * half, j * LANES:(j + 1) * LANES]
    y = jnp.concatenate([y_scr[j] for j in range(y_scr.shape[0])], axis=1)
    yb_ref[0] = _ssd_gate_norm(y, z_ref[0], p).astype(yb_ref.dtype)

    @pl.when(c == pl.num_programs(1) - 1)
    def _():
        sout_ref[0] = s_scr[...]
        bufout_ref[0] = x[q - (CONV_W - 1):q]


def _ssd_prompt(proj3, sp, col_xbc, col_z, col_dt):
    b, l, _ = proj3.shape
    cdim = sp["ssd_conv_w"].shape[1]
    d_inner = sp["ssd_norm_g"].shape[1]
    gw = d_inner // SSD_GROUPS
    q = SSD_CHUNK if l % SSD_CHUNK == 0 else l
    assert q % (2 * SUBLANES) == 0
    return pl.pallas_call(
        _ssd_prompt_body,
        grid=(b, l // q),
        in_specs=[pl.BlockSpec((1, q, cdim), lambda i, c: (i, c, col_xbc // cdim)),
                  pl.BlockSpec((1, q, d_inner), lambda i, c: (i, c, col_z // d_inner)),
                  pl.BlockSpec((1, q, LANES), lambda i, c: (i, c, col_dt // LANES)),
                  _const_spec((CONV_W, cdim)), _const_spec((1, cdim)), _const_spec((1, LANES)),
                  _const_spec((1, LANES)), _const_spec((1, d_inner)), _const_spec((1, d_inner)),
                  _const_spec((2 * LANES, d_inner))],
        out_specs=[pl.BlockSpec((1, q, d_inner), lambda i, c: (i, c, 0)),
                   pl.BlockSpec((1, SSD_GROUPS, gw, SSD_STATE), lambda i, c: (i, 0, 0, 0)),
                   pl.BlockSpec((1, CONV_W - 1, cdim), lambda i, c: (i, 0, 0))],
        out_shape=[jax.ShapeDtypeStruct((b, l, d_inner), BF16),
                   jax.ShapeDtypeStruct((b, SSD_GROUPS, gw, SSD_STATE), F32),
                   jax.ShapeDtypeStruct((b, CONV_W - 1, cdim), F32)],
        scratch_shapes=[pltpu.VMEM((cdim // LANES, q + SUBLANES, LANES), F32),
                        pltpu.VMEM((q, LANES), F32),
                        pltpu.VMEM((d_inner // LANES, q, LANES), F32),
                        pltpu.VMEM((SSD_GROUPS, gw, SSD_STATE), F32)],
        compiler_params=_cparams(("parallel", "arbitrary")),
        name="ssd_prompt",
    )(proj3, proj3, proj3, sp["ssd_conv_w"], sp["ssd_conv_b"], sp["dt_bias"], sp["A"], sp["D"], sp["ssd_norm_g"],
      sp["head_expand"])


def _ssd_step_body(xbc_ref, z_ref, dt_ref, buf_ref, s_ref, cw_ref, cb_ref, dtb_ref, a_ref, d_ref, ng_ref, e2_ref,
                   yb_ref, sout_ref, bufout_ref, *, steps, nb):
    p = _ssd_params(cw_ref, cb_ref, dtb_ref, a_ref, d_ref, ng_ref)
    rps = SUBLANES
    q = nb * rps
    x_all = xbc_ref[...].astype(F32)
    z_all = z_ref[...].astype(F32)
    dt_all = _softplus(dt_ref[...].astype(F32) + p["dt_bias"])
    cdim = x_all.shape[1]

    def padded(v, j):
        return jnp.concatenate([v[j * steps:(j + 1) * steps], jnp.zeros((rps - steps, v.shape[1]), v.dtype)], axis=0)

    xcs = []
    for j in range(nb):
        xpad = jnp.concatenate([jnp.zeros((SUBLANES - (CONV_W - 1), cdim), F32), buf_ref[j], padded(x_all, j)],
                               axis=0)
        xcs.append(_ssd_conv(xpad, rps, p["cw"], p["cb"]))
        bufout_ref[j] = xpad[SUBLANES + steps - (CONV_W - 1):SUBLANES + steps]
    xc = jnp.concatenate(xcs, axis=0)
    dt = jnp.concatenate([padded(dt_all, j) for j in range(nb)], axis=0)
    z = jnp.concatenate([padded(z_all, j) for j in range(nb)], axis=0)

    def s_set(b, g, v):
        sout_ref[b, g] = v

    t_col = lax.broadcasted_iota(jnp.int32, (q, 1), 0) % rps
    t_row = lax.broadcasted_iota(jnp.int32, (1, q), 1) % rps
    y = _ssd_chunk(xc, dt, p, e2_ref[...], lambda b, g: s_ref[b, g], s_set, t_col, t_row, n_seg=nb, n_valid=steps)
    y = _ssd_gate_norm(y, z, p)
    for j in range(nb):
        yb_ref[j * steps:(j + 1) * steps, :] = y[j * rps:j * rps + steps].astype(yb_ref.dtype)


def _ssd_step(proj, buf, s0, sp, steps, nb, col_xbc, col_z, col_dt):
    bsz = s0.shape[0]
    cdim = sp["ssd_conv_w"].shape[1]
    d_inner = sp["ssd_norm_g"].shape[1]
    gw = d_inner // SSD_GROUPS
    rows = nb * steps
    body = functools.partial(_ssd_step_body, steps=steps, nb=nb)
    return pl.pallas_call(
        body,
        grid=(bsz // nb,),
        in_specs=[pl.BlockSpec((rows, cdim), lambda i: (i, col_xbc // cdim)),
                  pl.BlockSpec((rows, d_inner), lambda i: (i, col_z // d_inner)),
                  pl.BlockSpec((rows, LANES), lambda i: (i, col_dt // LANES)),
                  pl.BlockSpec((None, nb, CONV_W - 1, cdim), lambda i: (0, i, 0, 0)),
                  pl.BlockSpec((nb, SSD_GROUPS, gw, SSD_STATE), lambda i: (i, 0, 0, 0)),
                  _const_spec((CONV_W, cdim)), _const_spec((1, cdim)), _const_spec((1, LANES)),
                  _const_spec((1, LANES)), _const_spec((1, d_inner)), _const_spec((1, d_inner)),
                  _const_spec((2 * LANES, d_inner))],
        out_specs=[pl.BlockSpec((rows, d_inner), lambda i: (i, 0)),
                   pl.BlockSpec((nb, SSD_GROUPS, gw, SSD_STATE), lambda i: (i, 0, 0, 0)),
                   pl.BlockSpec((None, nb, CONV_W - 1, cdim), lambda i: (0, i, 0, 0))],
        out_shape=[jax.ShapeDtypeStruct((bsz * steps, d_inner), BF16),
                   jax.ShapeDtypeStruct(s0.shape, F32),
                   jax.ShapeDtypeStruct(buf.shape, F32)],
        compiler_params=_cparams(("parallel",)),
        name="ssd_step",
    )(proj, proj, proj, buf, s0, sp["ssd_conv_w"], sp["ssd_conv_b"], sp["dt_bias"], sp["A"], sp["D"],
      sp["ssd_norm_g"], sp["head_expand"])


def _router(t, wr, br):
    t_hi = t.astype(BF16)
    t_lo = (t - t_hi.astype(F32)).astype(BF16)
    both = _dot(jnp.concatenate([t_hi, t_lo], axis=1), wr)
    logits = both[:, :LANES] + both[:, LANES:] + br
    lane = lax.broadcasted_iota(jnp.int32, logits.shape, 1)
    neg = -jnp.inf
    gl = jnp.where(lane < N_EGROUPS, logits, neg)
    gmax = jnp.max(gl, axis=-1, keepdims=True)
    g_idx = jnp.min(jnp.where(gl == gmax, lane, LANES), axis=-1, keepdims=True)
    g_w = 1.0 / jnp.sum(jnp.exp(gl - gmax), axis=-1, keepdims=True)
    in_grp = jnp.logical_and(jnp.logical_and(lane >= ROUTER_LANE0, lane < ROUTER_LANE0 + N_EXPERTS),
                             ((lane - ROUTER_LANE0) >> 2) == g_idx)
    el = jnp.where(in_grp, logits, neg)
    pe = jnp.exp(el - jnp.max(el, axis=-1, keepdims=True))
    pe = pe / jnp.sum(pe, axis=-1, keepdims=True)
    cand = jnp.where(in_grp, pe, -1.0)
    v1 = jnp.max(cand, axis=-1, keepdims=True)
    i1 = jnp.min(jnp.where(cand == v1, lane, LANES), axis=-1, keepdims=True)
    cand2 = jnp.where(lane == i1, -1.0, cand)
    v2 = jnp.max(cand2, axis=-1, keepdims=True)
    i2 = jnp.min(jnp.where(jnp.logical_and(cand2 == v2, in_grp), lane, LANES), axis=-1, keepdims=True)
    den = v1 + v2
    return lane, i1, i2, g_w * v1 / den, g_w * v2 / den


def _rows_to_tiles(ref, val):
    n, d = val.shape
    for k in range(d // LANES):
        ref[pl.ds(k, n, stride=d // LANES), :] = val[:, k * LANES:(k + 1) * LANES]


def _tiles_to_rows(ref, n, d, start=0):
    return jnp.concatenate([ref[pl.ds(start + k, n, stride=d // LANES), :] for k in range(d // LANES)], axis=1)


def _merge_body(x_ref, ya_ref, yb_ref, ga_ref, gb_ref, wl_ref, ws_ref, wo_ref, gf_ref, wr_ref, br_ref,
                x1_ref, t_ref, rt_ref, rtt_ref, cnt_ref, base_scr):
    step = pl.program_id(0)

    @pl.when(step == 0)
    def _():
        base_scr[...] = jnp.zeros_like(base_scr)

    a = _dot(ya_ref[...], wl_ref[...])
    b = _dot(yb_ref[...], ws_ref[...])
    merged = jax.nn.sigmoid(ga_ref[...].astype(F32)) * a + jax.nn.sigmoid(gb_ref[...].astype(F32)) * b
    x1 = x_ref[...] + _dot(merged.astype(BF16), wo_ref[...])
    x1_ref[...] = x1
    t = _rms(x1, gf_ref[...])
    _rows_to_tiles(t_ref, t)
    lane, i1, i2, wg1, wg2 = _router(t, wr_ref[...], br_ref[...])

    tm = t.shape[0]
    onehot = jnp.where(jnp.logical_or(lane == i1, lane == i2), 1.0, 0.0).astype(BF16)
    tri = (lax.broadcasted_iota(jnp.int32, (tm, tm), 1) <= lax.broadcasted_iota(jnp.int32, (tm, tm), 0)).astype(BF16)
    cum = _dot(tri, onehot) + base_scr[...]
    r1 = jnp.sum(jnp.where(lane == i1, cum, 0.0), axis=-1, keepdims=True) - 1.0
    r2 = jnp.sum(jnp.where(lane == i2, cum, 0.0), axis=-1, keepdims=True) - 1.0
    cols = (wg1, wg2, r1, r2, (i1 - ROUTER_LANE0).astype(F32), (i2 - ROUTER_LANE0).astype(F32))
    rt = jnp.zeros(cum.shape, F32)
    for k, c in enumerate(cols):
        rt = jnp.where(lane == k, c, rt)
    rt_ref[...] = rt
    rtt_ref[...] = rt.T[0:SUBLANES, :]
    base_scr[...] = cum[tm - 1:tm, :]
    cnt_ref[...] = cum[tm - 1:tm, :]


def _merge(x, ya, yb, proj, mw, tm, col_ga, col_gb):
    t, d = x.shape
    d_inner = yb.shape[1]
    return pl.pallas_call(
        _merge_body,
        grid=(t // tm,),
        in_specs=[pl.BlockSpec((tm, d), lambda i: (i, 0)),
                  pl.BlockSpec((tm, d), lambda i: (i, 0)),
                  pl.BlockSpec((tm, d_inner), lambda i: (i, 0)),
                  pl.BlockSpec((tm, d), lambda i: (i, col_ga // d)),
                  pl.BlockSpec((tm, d), lambda i: (i, col_gb // d)),
                  _const_spec((d, d)), _const_spec((d_inner, d)), _const_spec((d, d)),
                  _const_spec((1, d)), _const_spec((2 * d, 2 * LANES)), _const_spec((1, LANES))],
        out_specs=[pl.BlockSpec((tm, d), lambda i: (i, 0)),
                   pl.BlockSpec((tm * d // LANES, LANES), lambda i: (i, 0)),
                   pl.BlockSpec((tm, LANES), lambda i: (i, 0)),
                   pl.BlockSpec((SUBLANES, tm), lambda i: (0, i)),
                   pl.BlockSpec((1, LANES), lambda i: (0, 0))],
        out_shape=[jax.ShapeDtypeStruct((t, d), F32),
                   jax.ShapeDtypeStruct((t * d // LANES, LANES), F32),
                   jax.ShapeDtypeStruct((t, LANES), F32),
                   jax.ShapeDtypeStruct((SUBLANES, t), F32),
                   jax.ShapeDtypeStruct((1, LANES), F32)],
        scratch_shapes=[pltpu.VMEM((1, LANES), F32)],
        compiler_params=_cparams(("arbitrary",)),
        name="merge_router",
    )(x, ya, yb, proj, proj, mw["w_br_lru"], mw["w_br_ssd"], mw["w_out"], mw["g_ffn"], mw["w_router"],
      mw["b_router"])


def _dispatch_body(dest_ref, zb_ref, t_ref, o_ref, zero_scr, sem, zsem, *, tm, nk, tmg, n_tok):
    step = pl.program_id(0)

    @pl.when(step == 0)
    def _():
        zero_scr[...] = jnp.zeros_like(zero_scr)
        blk = tmg * nk

        def zcopy(j):
            return pltpu.make_async_copy(zero_scr, o_ref.at[pl.ds(pl.multiple_of(zb_ref[j] * blk, blk), blk)], zsem)

        for j in range(zb_ref.shape[0]):
            pl.when(zb_ref[j] >= 0)(lambda j=j: zcopy(j).start())
        for j in range(zb_ref.shape[0]):
            pl.when(zb_ref[j] >= 0)(lambda j=j: zcopy(j).wait())

    def issue(r, carry):
        src = t_ref.at[pl.ds(pl.multiple_of(r * nk, nk), nk)]
        for k in range(2):
            row = dest_ref[k * n_tok + step * tm + r]
            pltpu.make_async_copy(src, o_ref.at[pl.ds(pl.multiple_of(row * nk, nk), nk)], sem).start(priority=k)
        return carry

    lax.fori_loop(0, tm, issue, 0, unroll=8)
    for k in range(2):
        pltpu.make_async_copy(t_ref, o_ref.at[pl.ds(0, tm * nk)], sem).wait()


def _dispatch(dest, zero_blocks, t_tiles, n_tok, tm, n_rows, tmg):
    nk = t_tiles.shape[0] // n_tok
    return pl.pallas_call(
        functools.partial(_dispatch_body, tm=tm, nk=nk, tmg=tmg, n_tok=n_tok),
        grid_spec=pltpu.PrefetchScalarGridSpec(
            num_scalar_prefetch=2,
            grid=(n_tok // tm,),
            in_specs=[pl.BlockSpec((tm * nk, LANES), lambda i, *_: (i, 0))],
            out_specs=pl.BlockSpec(memory_space=pl.ANY),
            scratch_shapes=[pltpu.VMEM((tmg * nk, LANES), F32), pltpu.SemaphoreType.DMA(()),
                            pltpu.SemaphoreType.DMA(())]),
        out_shape=jax.ShapeDtypeStruct((n_rows * nk, LANES), F32),
        compiler_params=_cparams(("arbitrary",)),
        name="moe_dispatch",
    )(dest, zero_blocks, t_tiles)


def _expert_body(te_ref, nt_ref, x_ref, w1_ref, w3_ref, w2_ref, y_ref, w1_scr, w3_scr, w2_scr, *, tmg):
    i = pl.program_id(0)
    real = i < nt_ref[0]
    d = w1_scr.shape[0]

    @pl.when(jnp.logical_or(i == 0, te_ref[i] != te_ref[jnp.maximum(i - 1, 0)]))
    def _():
        w1_scr[...] = w1_ref[0].astype(BF16)
        w3_scr[...] = w3_ref[0].astype(BF16)
        w2_scr[...] = w2_ref[0].astype(BF16)

    @pl.when(real)
    def _():
        x = _tiles_to_rows(x_ref, tmg, d).astype(BF16)
        h1 = _dot(x, w1_scr[...])
        h3 = _dot(x, w3_scr[...])
        _rows_to_tiles(y_ref, _dot((h1 * jax.nn.sigmoid(h1) * h3).astype(BF16), w2_scr[...]))

    @pl.when(jnp.logical_not(real))
    def _():
        y_ref[...] = jnp.zeros_like(y_ref)


def _experts(tile_expert, n_tiles, xs_tiles, w1, w3, w2, tmg):
    _, d, dff = w1.shape
    blk = tmg * d // LANES
    row_spec = pl.BlockSpec((blk, LANES), lambda i, te, nt: (i, 0))
    return pl.pallas_call(
        functools.partial(_expert_body, tmg=tmg),
        grid_spec=pltpu.PrefetchScalarGridSpec(
            num_scalar_prefetch=2,
            grid=(xs_tiles.shape[0] // blk,),
            in_specs=[row_spec,
                      pl.BlockSpec((1, d, dff), lambda i, te, nt: (te[i], 0, 0)),
                      pl.BlockSpec((1, d, dff), lambda i, te, nt: (te[i], 0, 0)),
                      pl.BlockSpec((1, dff, d), lambda i, te, nt: (te[i], 0, 0))],
            out_specs=row_spec,
            scratch_shapes=[pltpu.VMEM((d, dff), BF16), pltpu.VMEM((d, dff), BF16), pltpu.VMEM((dff, d), BF16)]),
        out_shape=jax.ShapeDtypeStruct(xs_tiles.shape, F32),
        compiler_params=_cparams(("arbitrary",)),
        name="moe_experts",
    )(tile_expert, n_tiles, xs_tiles, w1, w3, w2)


def _ple_body(dest_ref, x_ref, rt_ref, p_ref, wp_ref, gp_ref, gg_ref, wg_ref, gfin_ref, y_hbm, o_ref, gbuf, sem,
              *, n_tok):
    step = pl.program_id(0)
    tm, d = x_ref.shape
    nk = d // LANES

    def gather(tile, slot):
        def issue(r, carry):
            tok = tile * tm + r
            for k in range(2):
                row = dest_ref[k * n_tok + tok]
                pltpu.make_async_copy(y_hbm.at[pl.ds(pl.multiple_of(row * nk, nk), nk)],
                                      gbuf.at[slot, pl.ds(pl.multiple_of((k * tm + r) * nk, nk), nk)],
                                      sem.at[slot]).start(priority=k)
            return carry

        lax.fori_loop(0, tm, issue, 0, unroll=8)

    @pl.when(step == 0)
    def _():
        gather(0, 0)

    @pl.when(step + 1 < pl.num_programs(0))
    def _():
        gather(step + 1, (step + 1) % 2)

    slot = step % 2
    pltpu.make_async_copy(y_hbm.at[pl.ds(0, 2 * tm * nk)], gbuf.at[slot], sem.at[slot]).wait()
    rt = rt_ref[...]
    rows = gbuf.at[slot]
    x = (x_ref[...] + rt[:, 0:1] * _tiles_to_rows(rows, tm, d)
         + rt[:, 1:2] * _tiles_to_rows(rows, tm, d, start=tm * nk))
    e = _rms(_dot(p_ref[...].astype(BF16), wp_ref[...]), gp_ref[...])
    gate = jax.nn.sigmoid(_dot(_rms(x, gg_ref[...]).astype(BF16), wg_ref[...]))
    o_ref[...] = _rms(x + gate * e, gfin_ref[...])


def _ple(dest, x1, rt, p, y_sorted, pw, tm):
    n, d = x1.shape
    dp = p.shape[1]
    const = lambda shape: pl.BlockSpec(shape, lambda i, *_: (0,) * len(shape))
    return pl.pallas_call(
        functools.partial(_ple_body, n_tok=n),
        grid_spec=pltpu.PrefetchScalarGridSpec(
            num_scalar_prefetch=1,
            grid=(n // tm,),
            in_specs=[pl.BlockSpec((tm, d), lambda i, *_: (i, 0)),
                      pl.BlockSpec((tm, LANES), lambda i, *_: (i, 0)),
                      pl.BlockSpec((tm, dp), lambda i, *_: (i, 0)),
                      const((dp, d)), const((1, d)), const((1, d)), const((d, d)), const((1, d)),
                      pl.BlockSpec(memory_space=pl.ANY)],
            out_specs=pl.BlockSpec((tm, d), lambda i, *_: (i, 0)),
            scratch_shapes=[pltpu.VMEM((2, 2 * tm * d // LANES, LANES), F32), pltpu.SemaphoreType.DMA((2,))]),
        out_shape=jax.ShapeDtypeStruct((n, d), F32),
        compiler_params=_cparams(("arbitrary",)),
        name="combine_ple_final",
    )(dest, x1, rt, p, pw["w_ple_proj"], pw["g_ple"], pw["g_ple_gate"], pw["w_ple_gate"], pw["g_final"], y_sorted)


def _pick_tile(n, pref):
    t = min(n, pref)
    while n % t:
        t //= 2
    return t


MOE_ROW_TILE = 256
MOE_DISPATCH_TILE = 2048


def _split_router(w):
    hi = w.astype(BF16)
    lo = (w - hi.astype(F32)).astype(BF16)
    return jnp.concatenate([jnp.concatenate([hi, lo], axis=1),
                            jnp.concatenate([hi, jnp.zeros_like(hi)], axis=1)], axis=0)


def _token_tail(x1, t_tiles, rt, rtt, cnt, p, lw, tm):
    n = x1.shape[0]
    tmg = MOE_ROW_TILE
    n_blocks = pl.cdiv(2 * n, tmg) + N_EXPERTS
    counts = cnt[0, ROUTER_LANE0:ROUTER_LANE0 + N_EXPERTS].astype(jnp.int32)
    tiles = (counts + tmg - 1) // tmg
    ends = jnp.cumsum(tiles)
    n_tiles = ends[-1]
    e_idx = rtt[4:6].astype(jnp.int32)
    first_row = (ends - tiles) * tmg
    dest = rtt[2:4].astype(jnp.int32) + sum(jnp.where(e_idx == e, first_row[e], 0) for e in range(N_EXPERTS))
    dest = dest.reshape(2 * n)
    blk = jnp.arange(n_blocks, dtype=jnp.int32)
    tile_expert = jnp.sum((jnp.minimum(blk, n_tiles - 1)[:, None] >= ends[None, :]).astype(jnp.int32), axis=1)
    tail = n_tiles + blk[:N_EXPERTS]
    zero_blocks = jnp.concatenate([jnp.where(tiles > 0, ends - 1, -1),
                                   jnp.where(tail < n_blocks, tail, -1)]).astype(jnp.int32)

    sorted_t = _dispatch(dest, zero_blocks, t_tiles, n, _pick_tile(n, MOE_DISPATCH_TILE), n_blocks * tmg, tmg)
    y_sorted = _experts(tile_expert, n_tiles.reshape(1), sorted_t, lw["w1"], lw["w3"], lw["w2"], tmg)
    return _ple(dest, x1, rt, p, y_sorted, lw, tm)


def kernel(x_prompt, x_sample, state_lru_h, state_lru_conv, state_ssd, state_ssd_conv, p_prompt, p_sample, g_mix, w_in, lru_conv_w, lru_conv_b, lru_wa, lru_ba, lru_wx, lru_bx, lru_lambda, ssd_conv_w, ssd_conv_b, ssd_dt_bias, ssd_A_log, ssd_D, ssd_norm_g, w_br_lru, w_br_ssd, w_out, g_ffn, w_router_g, b_router_g, w_router_e, b_router_e, w1, w3, w2, w_ple_proj, g_ple, g_ple_gate, w_ple_gate, g_final):
    depth = w_in.shape[0]
    assert depth == 1, "one decoder layer per call"
    bp, lp, d = x_prompt.shape
    bs, ls, _ = x_sample.shape
    w_lru = state_lru_h.shape[-1]
    heads, hdim, nstate = state_ssd.shape[2:]
    d_inner = heads * hdim
    cdim = state_ssd_conv.shape[-1]
    assert hdim == SSD_HEADDIM and nstate == SSD_STATE and heads <= LANES and ls < SUBLANES
    gw = d_inner // SSD_GROUPS

    o_dt = 2 * w_lru + d_inner + cdim
    n_proj = o_dt + 2 * d + LANES
    tn = n_proj // 9 if n_proj % (9 * LANES) == 0 else LANES
    wi = w_in[0]
    w_proj = jnp.concatenate(
        [wi[:, :o_dt], wi[:, o_dt + heads:], wi[:, o_dt:o_dt + heads], jnp.zeros((d, LANES - heads), wi.dtype)],
        axis=1).astype(BF16)
    col_z, col_xbc = 2 * w_lru, 2 * w_lru + d_inner
    col_ga, col_gb, col_dt = o_dt, o_dt + d, o_dt + 2 * d
    row = lambda v: v.reshape(1, -1).astype(F32)
    pad_heads = lambda v: jnp.pad(v.astype(F32), (0, LANES - heads)).reshape(1, LANES)
    lw = {
        "lru_conv_w": lru_conv_w[0], "lru_conv_b": row(lru_conv_b[0]),
        "wax": jnp.concatenate([lru_wa[0], lru_wx[0]], axis=-1).astype(BF16),
        "lru_ba": row(lru_ba[0]), "lru_bx": row(lru_bx[0]), "lru_lambda": row(lru_lambda[0]),
        "ssd_conv_w": ssd_conv_w[0], "ssd_conv_b": row(ssd_conv_b[0]),
        "dt_bias": pad_heads(ssd_dt_bias[0]), "A": pad_heads(-jnp.exp(ssd_A_log[0].astype(F32))),
        "D": row(jnp.repeat(ssd_D[0], hdim)), "ssd_norm_g": row(ssd_norm_g[0]),
        "head_expand": jnp.tile(jnp.arange(LANES)[:, None] == jnp.arange(d_inner)[None, :] // hdim, (2, 1)).astype(BF16),
        "w_br_lru": w_br_lru[0].astype(BF16), "w_br_ssd": w_br_ssd[0].astype(BF16), "w_out": w_out[0].astype(BF16),
        "g_ffn": row(g_ffn[0]),
        "w_router": _split_router(jnp.concatenate([w_router_g[0], w_router_e[0],
                                                   jnp.zeros((d, LANES - N_EGROUPS - N_EXPERTS), F32)], axis=1)),
        "b_router": jnp.concatenate([b_router_g[0], b_router_e[0],
                                     jnp.zeros((LANES - N_EGROUPS - N_EXPERTS,), F32)]).reshape(1, LANES),
        "w1": w1[0], "w3": w3[0], "w2": w2[0],
        "w_ple_proj": w_ple_proj[0].astype(BF16), "g_ple": row(g_ple[0]), "g_ple_gate": row(g_ple_gate[0]),
        "w_ple_gate": w_ple_gate[0].astype(BF16), "g_final": row(g_final),
    }
    g_mix_r = row(g_mix[0])

    tp = bp * lp
    xp = x_prompt.reshape(tp, d)
    tm_p = _pick_tile(tp, 1024)
    proj_p = _inproj(xp, g_mix_r, w_proj, tm_p, tn)
    proj_p3 = proj_p.reshape(bp, lp, n_proj)
    ya_p, hl_p, lbuf_p = _lru_prompt(proj_p3, jnp.zeros((bp, CONV_W - 1, w_lru), F32), jnp.zeros((bp, w_lru), F32),
                                     lw, _pick_tile(lp, 256), True)
    yb_p, s_p, sbuf_p = _ssd_prompt(proj_p3, lw, col_xbc, col_z, col_dt)
    tm_tail_p = _pick_tile(tp, 512)
    x1_p, t_p, rt_p, rtt_p, cnt_p = _merge(xp, ya_p.reshape(tp, w_lru), yb_p.reshape(tp, d_inner), proj_p, lw,
                                           tm_tail_p, col_ga, col_gb)
    y_p = _token_tail(x1_p, t_p, rt_p, rtt_p, cnt_p, p_prompt[0].reshape(tp, -1), lw, tm_tail_p)

    ts = bs * ls
    xs = x_sample.reshape(ts, d)
    tm_s = _pick_tile(ts, 512)
    proj_s = _inproj(xs, g_mix_r, w_proj, tm_s, tn)
    to_tmajor = lambda v, n: v.reshape(bs, n, -1).transpose(1, 0, 2).reshape(n * bs, -1)
    from_tmajor = lambda v, n: v.reshape(n, bs, -1).transpose(1, 0, 2)
    ya_t, hl_s, lbuf_t = _lru_step(to_tmajor(proj_s[:, :w_lru], ls), to_tmajor(proj_s[:, w_lru:2 * w_lru], ls),
                                   to_tmajor(state_lru_conv[0], CONV_W - 1), state_lru_h[0], lw, ls)
    ya_s = from_tmajor(ya_t, ls).reshape(ts, w_lru)
    lbuf_s = from_tmajor(lbuf_t, CONV_W - 1)
    yb_s, s_s, sbuf_s = _ssd_step(proj_s, state_ssd_conv, state_ssd[0].reshape(bs, SSD_GROUPS, gw, nstate), lw,
                                  ls, _pick_tile(bs, 4), col_xbc, col_z, col_dt)
    x1_s, t_s, rt_s, rtt_s, cnt_s = _merge(xs, ya_s, yb_s, proj_s, lw, tm_s, col_ga, col_gb)
    y_s = _token_tail(x1_s, t_s, rt_s, rtt_s, cnt_s, p_sample[0].reshape(ts, -1), lw, tm_s)

    return (y_p.reshape(bp, lp, d), y_s.reshape(bs, ls, d),
            hl_p.reshape(1, bp, w_lru), lbuf_p[None],
            s_p.reshape(1, bp, heads, hdim, nstate), sbuf_p[None],
            hl_s[None], lbuf_s[None],
            s_s.reshape(1, bs, heads, hdim, nstate), sbuf_s)
```

```python
import functools

import jax
import jax.numpy as jnp
from jax import lax
from jax.experimental import pallas as pl
from jax.experimental.pallas import tpu as pltpu

F32 = jnp.float32
BF16 = jnp.bfloat16

EPS = 1e-6
CONV_W = 4
LRU_BLOCKS = 8
LRU_C = 8.0
SSD_HEADDIM = 64
SSD_GROUPS = 8
SSD_STATE = 128
SSD_CHUNK = 128
N_EGROUPS = 4
EXP_PER_GROUP = 4
N_EXPERTS = N_EGROUPS * EXP_PER_GROUP

LANES = 128
SUBLANES = 8
VMEM_LIMIT = 52 * 1024 * 1024
ROUTER_LANE0 = N_EGROUPS


def _cparams(sem):
    return pltpu.CompilerParams(dimension_semantics=sem, vmem_limit_bytes=VMEM_LIMIT)


def _dot(a, b):
    return jnp.dot(a, b, preferred_element_type=F32)


def _dot_nt(a, b):
    return lax.dot_general(a, b, (((1,), (1,)), ((), ())), preferred_element_type=F32)


def _dot_tn(a, b):
    return lax.dot_general(a, b, (((0,), (0,)), ((), ())), preferred_element_type=F32)


def _dot_f32(a, b):
    return jnp.dot(a, b, precision=lax.Precision.HIGHEST, preferred_element_type=F32)


def _rms(x, g):
    return x * lax.rsqrt(jnp.mean(x * x, axis=-1, keepdims=True) + EPS) * g


def _const_spec(shape):
    nd = len(shape)
    return pl.BlockSpec(shape, lambda *_: (0,) * nd)


def _inproj_body(x_ref, g_ref, wh_ref, wt_ref, o_ref, h_scr, *, n_head):
    j = pl.program_id(1)

    @pl.when(j == 0)
    def _():
        h_scr[...] = _rms(x_ref[...], g_ref[...]).astype(BF16)

    @pl.when(j < n_head)
    def _():
        o_ref[...] = _dot(h_scr[...], wh_ref[0]).astype(o_ref.dtype)

    @pl.when(j >= n_head)
    def _():
        o_ref[...] = _dot(h_scr[...], wt_ref[...]).astype(o_ref.dtype)


def _inproj(x, g, w_head, w_tail, tm, tn, n_head):
    t, d = x.shape
    n_tail = w_tail.shape[1] // tn
    return pl.pallas_call(
        functools.partial(_inproj_body, n_head=n_head),
        grid=(t // tm, n_head + n_tail),
        in_specs=[pl.BlockSpec((tm, d), lambda i, j: (i, 0)),
                  pl.BlockSpec((1, d), lambda i, j: (0, 0)),
                  pl.BlockSpec((1, d, tn), lambda i, j: (0, 0, jnp.minimum(j, n_head - 1))),
                  pl.BlockSpec((d, tn), lambda i, j: (0, jnp.maximum(j - n_head, 0)))],
        out_specs=pl.BlockSpec((tm, tn), lambda i, j: (i, j)),
        out_shape=jax.ShapeDtypeStruct((t, (n_head + n_tail) * tn), BF16),
        scratch_shapes=[pltpu.VMEM((tm, d), BF16)],
        compiler_params=_cparams(("parallel", "arbitrary")),
        name="inproj",
    )(x, g, w_head, w_tail)


def _lru_gates(u, wax_ref, ba, bx, lam):
    bw = u.shape[1] // LRU_BLOCKS
    r_parts, i_parts = [], []
    for n in range(LRU_BLOCKS):
        ri = _dot(u[:, n * bw:(n + 1) * bw].astype(BF16), wax_ref[n])
        r_parts.append(ri[:, :bw])
        i_parts.append(ri[:, bw:])
    r = jax.nn.sigmoid(jnp.concatenate(r_parts, axis=1) + ba)
    i = jax.nn.sigmoid(jnp.concatenate(i_parts, axis=1) + bx)
    log_a = LRU_C * r * jax.nn.log_sigmoid(lam)
    a = jnp.exp(log_a)
    m2 = -jnp.tanh(log_a) * (a * a + 1.0)
    mult = jnp.where(m2 > 0.0, m2 * lax.rsqrt(m2), 0.0)
    return a, i, mult


def _lru_prompt_body(xin_ref, gate_ref, buf0_ref, h0_ref, cw_ref, cb_ref, wax_ref, ba_ref, bx_ref, lam_ref,
                     ya_ref, hlast_ref, bufout_ref, halo_scr, h_scr, *, reset_first):
    j = pl.program_id(1)
    tt = xin_ref.shape[1]

    @pl.when(j == 0)
    def _():
        halo_scr[...] = jnp.zeros_like(halo_scr)
        halo_scr[SUBLANES - (CONV_W - 1):SUBLANES, :] = buf0_ref[0]
        h_scr[...] = h0_ref[0]

    x = xin_ref[0].astype(F32)
    xpad = jnp.concatenate([halo_scr[...], x], axis=0)
    cw = cw_ref[...]
    u = cb_ref[...] + sum(xpad[SUBLANES - (CONV_W - 1) + k:SUBLANES - (CONV_W - 1) + k + tt] * cw[k:k + 1]
                          for k in range(CONV_W))
    halo_scr[...] = x[tt - SUBLANES:tt]

    a, i, mult = _lru_gates(u, wax_ref, ba_ref[...], bx_ref[...], lam_ref[...])
    row = lax.broadcasted_iota(jnp.int32, a.shape, 0)
    if reset_first:
        first = jnp.logical_and(row == 0, j == 0)
        mult = jnp.where(first, 1.0, mult)
        a = jnp.where(first, 0.0, a)
    v = u * i * mult

    width = a.shape[1]
    a = a.reshape(tt // SUBLANES, SUBLANES, width)
    v = v.reshape(tt // SUBLANES, SUBLANES, width)
    sub = lax.broadcasted_iota(jnp.int32, a.shape, 1)
    s = 1
    while s < SUBLANES:
        keep = sub >= s
        v = jnp.where(keep, a * pltpu.roll(v, s, axis=1) + v, v)
        a = jnp.where(keep, a * pltpu.roll(a, s, axis=1), a)
        s *= 2
    carry = h_scr[...]
    groups = []
    for g in range(tt // SUBLANES):
        hg = a[g] * carry + v[g]
        carry = hg[SUBLANES - 1:SUBLANES]
        groups.append(hg)
    h = jnp.concatenate(groups, axis=0)
    h_scr[...] = carry
    ya_ref[0] = (h * jax.nn.gelu(gate_ref[0].astype(F32))).astype(ya_ref.dtype)

    @pl.when(j == pl.num_programs(1) - 1)
    def _():
        hlast_ref[0] = h[tt - 1:tt]
        bufout_ref[0] = x[tt - (CONV_W - 1):tt]


def _lru_prompt(proj3, buf0, h0, lw, tt, reset_first):
    b, l, _ = proj3.shape
    w = h0.shape[-1]
    body = functools.partial(_lru_prompt_body, reset_first=reset_first)
    return pl.pallas_call(
        body,
        grid=(b, l // tt),
        in_specs=[pl.BlockSpec((1, tt, w), lambda i, j: (i, j, 0)),
                  pl.BlockSpec((1, tt, w), lambda i, j: (i, j, 1)),
                  pl.BlockSpec((1, CONV_W - 1, w), lambda i, j: (i, 0, 0)),
                  pl.BlockSpec((1, 1, w), lambda i, j: (i, 0, 0)),
                  _const_spec((CONV_W, w)), _const_spec((1, w)),
                  _const_spec(lw["wax"].shape), _const_spec((1, w)), _const_spec((1, w)), _const_spec((1, w))],
        out_specs=[pl.BlockSpec((1, tt, w), lambda i, j: (i, j, 0)),
                   pl.BlockSpec((1, 1, w), lambda i, j: (i, 0, 0)),
                   pl.BlockSpec((1, CONV_W - 1, w), lambda i, j: (i, 0, 0))],
        out_shape=[jax.ShapeDtypeStruct((b, l, w), BF16),
                   jax.ShapeDtypeStruct((b, 1, w), F32),
                   jax.ShapeDtypeStruct((b, CONV_W - 1, w), F32)],
        scratch_shapes=[pltpu.VMEM((SUBLANES, w), F32), pltpu.VMEM((1, w), F32)],
        compiler_params=_cparams(("parallel", "arbitrary")),
        name="lru_prompt",
    )(proj3, proj3, buf0, h0.reshape(b, 1, w), lw["lru_conv_w"], lw["lru_conv_b"], lw["wax"],
      lw["lru_ba"], lw["lru_bx"], lw["lru_lambda"])


def _lru_step_body(xin_ref, gate_ref, buf_ref, h0_ref, cw_ref, cb_ref, wax_ref, ba_ref, bx_ref, lam_ref,
                   ya_ref, hlast_ref, bufout_ref, *, steps):
    bsz = h0_ref.shape[0]
    n = steps * bsz
    x = xin_ref[...].astype(F32)
    xx = jnp.concatenate([buf_ref[...], x], axis=0)
    cw = cw_ref[...]
    u = cb_ref[...] + sum(xx[k * bsz:k * bsz + n] * cw[k:k + 1] for k in range(CONV_W))
    a, i, mult = _lru_gates(u, wax_ref, ba_ref[...], bx_ref[...], lam_ref[...])
    v = u * i * mult
    h = h0_ref[...]
    for t in range(steps):
        sl = slice(t * bsz, (t + 1) * bsz)
        h = a[sl] * h + v[sl]
        ya_ref[sl, :] = (h * jax.nn.gelu(gate_ref[sl, :].astype(F32))).astype(ya_ref.dtype)
    hlast_ref[...] = h
    bufout_ref[...] = xx[steps * bsz:(steps + CONV_W - 1) * bsz]


def _lru_step(xin_t, gate_t, buf_t, h0, lw, steps):
    bsz, w = h0.shape
    body = functools.partial(_lru_step_body, steps=steps)
    return pl.pallas_call(
        body,
        out_shape=[jax.ShapeDtypeStruct((steps * bsz, w), BF16),
                   jax.ShapeDtypeStruct((bsz, w), F32),
                   jax.ShapeDtypeStruct(((CONV_W - 1) * bsz, w), F32)],
        compiler_params=pltpu.CompilerParams(vmem_limit_bytes=VMEM_LIMIT),
        name="lru_step",
    )(xin_t, gate_t, buf_t, h0, lw["lru_conv_w"], lw["lru_conv_b"], lw["wax"],
      lw["lru_ba"], lw["lru_bx"], lw["lru_lambda"])


def _expand_heads(cols, e2):
    q = cols[0].shape[0]
    v = jnp.concatenate(cols, axis=0)
    hi = v.astype(BF16)
    lo = (v - hi.astype(F32)).astype(BF16)
    out = _dot(jnp.concatenate([hi, lo], axis=1), e2)
    return [out[i * q:(i + 1) * q] for i in range(len(cols))]


def _ssd_chunk(xc, dt, p, e2, s_get, s_set, t_col, t_row, n_seg=1, n_valid=None):
    q = xc.shape[0]
    rps = q // n_seg
    gn = SSD_GROUPS * SSD_STATE
    d_inner = xc.shape[1] - 2 * gn
    hpg = d_inner // SSD_HEADDIM // SSD_GROUPS
    gw = hpg * SSD_HEADDIM

    causal = t_col >= t_row
    if n_seg > 1:
        same = (lax.broadcasted_iota(jnp.int32, (q, 1), 0) // rps) == (lax.broadcasted_iota(jnp.int32, (1, q), 1) // rps)
        causal = jnp.logical_and(same, causal)
    if n_valid is not None:
        dt = jnp.where(lax.broadcasted_iota(jnp.int32, dt.shape, 0) % rps < n_valid, dt, 0.0)
    a = dt * p["A"]
    cum = _dot_f32(causal.astype(F32), a)
    cum_t = cum.T
    total = cum[q - 1:q, :] if n_seg == 1 else _dot_f32(same.astype(F32), a)
    dt_x, to_end_x, ecum_x = _expand_heads([dt, jnp.exp(total - cum), jnp.exp(cum)], e2)

    xs = xc[:, :d_inner]
    xdt = xs * dt_x
    xw = xdt * to_end_x
    packed = rps % (2 * SUBLANES) == 0
    xdt_m = xdt.astype(BF16) if packed else xdt
    if packed:
        xw = xw.astype(BF16)
    lane_head = lax.broadcasted_iota(jnp.int32, (1, gw), 1) // SSD_HEADDIM
    y_groups = []
    for g in range(SSD_GROUPS):
        sl = slice(g * gw, (g + 1) * gw)
        bg = xc[:, d_inner + g * SSD_STATE:d_inner + (g + 1) * SSD_STATE]
        cg = xc[:, d_inner + gn + g * SSD_STATE:d_inner + gn + (g + 1) * SSD_STATE]
        if packed:
            bg, cg = bg.astype(BF16), cg.astype(BF16)
        cb = _dot_nt(cg.astype(BF16), bg.astype(BF16))
        m_heads, x_heads = [], []
        for hh in range(hpg):
            h = g * hpg + hh
            decay = jnp.exp(jnp.where(causal, cum[:, h:h + 1] - cum_t[h:h + 1, :], -jnp.inf))
            m_heads.append((cb * decay).astype(BF16))
            x_heads.append(jnp.where(lane_head == hh, xdt_m[:, sl], 0.0))
        y_diag = _dot(jnp.concatenate(m_heads, axis=1), jnp.concatenate(x_heads, axis=0).astype(BF16))
        y_off = []
        for b in range(n_seg):
            rows = slice(b * rps, (b + 1) * rps)
            s_old = s_get(b, g)
            y_off.append(_dot_nt(cg[rows].astype(BF16), s_old.astype(BF16)))
            s_dec = [s_old[hh * SSD_HEADDIM:(hh + 1) * SSD_HEADDIM, :]
                     * jnp.exp(cum_t[g * hpg + hh:g * hpg + hh + 1, (b + 1) * rps - 1:(b + 1) * rps])
                     for hh in range(hpg)]
            s_set(b, g, jnp.concatenate(s_dec, axis=0) + _dot_tn(xw[rows, sl].astype(BF16), bg[rows].astype(BF16)))
        y_off = y_off[0] if n_seg == 1 else jnp.concatenate(y_off, axis=0)
        y_groups.append(y_diag + y_off * ecum_x[:, sl])
    return jnp.concatenate(y_groups, axis=1) + p["D"] * xs


def _ssd_gate_norm(y, z, p):
    gw = y.shape[1] // SSD_GROUPS
    zf = z.astype(F32)
    out = []
    for g in range(SSD_GROUPS):
        sl = slice(g * gw, (g + 1) * gw)
        v = y[:, sl] * (zf[:, sl] * jax.nn.sigmoid(zf[:, sl]))
        out.append(v * lax.rsqrt(jnp.mean(v * v, axis=-1, keepdims=True) + EPS) * p["norm_g"][:, sl])
    return jnp.concatenate(out, axis=1)


def _ssd_conv(xpad, q, cw, cb):
    base = SUBLANES - (CONV_W - 1)
    y = cb + sum(xpad[base + k:base + k + q] * cw[k:k + 1] for k in range(CONV_W))
    return y * jax.nn.sigmoid(y)


def _softplus(x):
    return jax.nn.softplus(x)


def _ssd_params(cw_ref, cb_ref, dtb_ref, a_ref, d_ref, ng_ref):
    return {"cw": cw_ref[...], "cb": cb_ref[...], "dt_bias": dtb_ref[...], "A": a_ref[...],
            "D": d_ref[...], "norm_g": ng_ref[...]}


def _ssd_prompt_body(xbc_ref, z_ref, dt_ref, cw_ref, cb_ref, dtb_ref, a_ref, d_ref, ng_ref, e2_ref,
                     yb_ref, sout_ref, bufout_ref, x_scr, dt_scr, y_scr, s_scr):
    c = pl.program_id(1)
    q = xbc_ref.shape[1]
    half = q // 2
    n_xslab = x_scr.shape[0]
    base = SUBLANES - (CONV_W - 1)

    @pl.when(c == 0)
    def _():
        x_scr[:, 0:SUBLANES, :] = jnp.zeros((n_xslab, SUBLANES, LANES), F32)
        s_scr[...] = jnp.zeros_like(s_scr)

    @pl.when(c > 0)
    def _():
        x_scr[:, 0:SUBLANES, :] = x_scr[:, q:q + SUBLANES, :]

    p = _ssd_params(cw_ref, cb_ref, dtb_ref, a_ref, d_ref, ng_ref)
    x = xbc_ref[0].astype(F32)
    for j in range(n_xslab):
        x_scr[j, SUBLANES:SUBLANES + q, :] = x[:, j * LANES:(j + 1) * LANES]

    cols = []
    for j in range(n_xslab):
        ls = slice(j * LANES, (j + 1) * LANES)
        halves = []
        for par in range(2):
            acc = p["cb"][:, ls]
            for k in range(CONV_W):
                acc = acc + x_scr[j, pl.ds(base + k + par, half, stride=2), :] * p["cw"][k:k + 1, ls]
            halves.append(acc)
        cols.append(jnp.concatenate(halves, axis=0))
    xc = jnp.concatenate(cols, axis=1)
    xc = xc * jax.nn.sigmoid(xc)

    dt_scr[...] = _softplus(dt_ref[0].astype(F32) + p["dt_bias"])
    dt = jnp.concatenate([dt_scr[pl.ds(par, half, stride=2), :] for par in range(2)], axis=0)

    def times(shape, axis):
        pos = lax.broadcasted_iota(jnp.int32, shape, axis)
        return jnp.where(pos < half, 2 * pos, 2 * (pos - half) + 1)

    def s_set(b, g, v):
        s_scr[g] = v

    y = _ssd_chunk(xc, dt, p, e2_ref[...], lambda b, g: s_scr[g], s_set, times((q, 1), 0), times((1, q), 1))
    for j in range(y_scr.shape[0]):
        for par in range(2):
            y_scr[j, pl.ds(par, half, stride=2), :] = y[par * half:(par + 1) * half, j * LANES:(j + 1) * LANES]
    y = jnp.concatenate([y_scr[j] for j in range(y_scr.shape[0])], axis=1)
    yb_ref[0] = _ssd_gate_norm(y, z_ref[0], p).astype(yb_ref.dtype)

    @pl.when(c == pl.num_programs(1) - 1)
    def _():
        sout_ref[0] = s_scr[...]
        bufout_ref[0] = x[q - (CONV_W - 1):q]


def _ssd_prompt(proj3, sp, col_xbc, col_z, col_dt):
    b, l, _ = proj3.shape
    cdim = sp["ssd_conv_w"].shape[1]
    d_inner = sp["ssd_norm_g"].shape[1]
    gw = d_inner // SSD_GROUPS
    q = SSD_CHUNK if l % SSD_CHUNK == 0 else l
    assert q % (2 * SUBLANES) == 0
    return pl.pallas_call(
        _ssd_prompt_body,
        grid=(b, l // q),
        in_specs=[pl.BlockSpec((1, q, cdim), lambda i, c: (i, c, col_xbc // cdim)),
                  pl.BlockSpec((1, q, d_inner), lambda i, c: (i, c, col_z // d_inner)),
                  pl.BlockSpec((1, q, LANES), lambda i, c: (i, c, col_dt // LANES)),
                  _const_spec((CONV_W, cdim)), _const_spec((1, cdim)), _const_spec((1, LANES)),
                  _const_spec((1, LANES)), _const_spec((1, d_inner)), _const_spec((1, d_inner)),
                  _const_spec((2 * LANES, d_inner))],
        out_specs=[pl.BlockSpec((1, q, d_inner), lambda i, c: (i, c, 0)),
                   pl.BlockSpec((1, SSD_GROUPS, gw, SSD_STATE), lambda i, c: (i, 0, 0, 0)),
                   pl.BlockSpec((1, CONV_W - 1, cdim), lambda i, c: (i, 0, 0))],
        out_shape=[jax.ShapeDtypeStruct((b, l, d_inner), BF16),
                   jax.ShapeDtypeStruct((b, SSD_GROUPS, gw, SSD_STATE), F32),
                   jax.ShapeDtypeStruct((b, CONV_W - 1, cdim), F32)],
        scratch_shapes=[pltpu.VMEM((cdim // LANES, q + SUBLANES, LANES), F32),
                        pltpu.VMEM((q, LANES), F32),
                        pltpu.VMEM((d_inner // LANES, q, LANES), F32),
                        pltpu.VMEM((SSD_GROUPS, gw, SSD_STATE), F32)],
        compiler_params=_cparams(("parallel", "arbitrary")),
        name="ssd_prompt",
    )(proj3, proj3, proj3, sp["ssd_conv_w"], sp["ssd_conv_b"], sp["dt_bias"], sp["A"], sp["D"], sp["ssd_norm_g"],
      sp["head_expand"])


def _ssd_step_body(xbc_ref, z_ref, dt_ref, buf_ref, s_ref, cw_ref, cb_ref, dtb_ref, a_ref, d_ref, ng_ref, e2_ref,
                   yb_ref, sout_ref, bufout_ref, *, steps, nb):
    p = _ssd_params(cw_ref, cb_ref, dtb_ref, a_ref, d_ref, ng_ref)
    rps = SUBLANES
    q = nb * rps
    x_all = xbc_ref[...].astype(F32)
    z_all = z_ref[...].astype(F32)
    dt_all = _softplus(dt_ref[...].astype(F32) + p["dt_bias"])
    cdim = x_all.shape[1]

    def padded(v, j):
        return jnp.concatenate([v[j * steps:(j + 1) * steps], jnp.zeros((rps - steps, v.shape[1]), v.dtype)], axis=0)

    xcs = []
    for j in range(nb):
        xpad = jnp.concatenate([jnp.zeros((SUBLANES - (CONV_W - 1), cdim), F32), buf_ref[j], padded(x_all, j)],
                               axis=0)
        xcs.append(_ssd_conv(xpad, rps, p["cw"], p["cb"]))
        bufout_ref[j] = xpad[SUBLANES + steps - (CONV_W - 1):SUBLANES + steps]
    xc = jnp.concatenate(xcs, axis=0)
    dt = jnp.concatenate([padded(dt_all, j) for j in range(nb)], axis=0)
    z = jnp.concatenate([padded(z_all, j) for j in range(nb)], axis=0)

    def s_set(b, g, v):
        sout_ref[b, g] = v

    t_col = lax.broadcasted_iota(jnp.int32, (q, 1), 0) % rps
    t_row = lax.broadcasted_iota(jnp.int32, (1, q), 1) % rps
    y = _ssd_chunk(xc, dt, p, e2_ref[...], lambda b, g: s_ref[b, g], s_set, t_col, t_row, n_seg=nb, n_valid=steps)
    y = _ssd_gate_norm(y, z, p)
    for j in range(nb):
        yb_ref[j * steps:(j + 1) * steps, :] = y[j * rps:j * rps + steps].astype(yb_ref.dtype)


def _ssd_step(proj, buf, s0, sp, steps, nb, col_xbc, col_z, col_dt):
    bsz = s0.shape[0]
    cdim = sp["ssd_conv_w"].shape[1]
    d_inner = sp["ssd_norm_g"].shape[1]
    gw = d_inner // SSD_GROUPS
    rows = nb * steps
    body = functools.partial(_ssd_step_body, steps=steps, nb=nb)
    return pl.pallas_call(
        body,
        grid=(bsz // nb,),
        in_specs=[pl.BlockSpec((rows, cdim), lambda i: (i, col_xbc // cdim)),
                  pl.BlockSpec((rows, d_inner), lambda i: (i, col_z // d_inner)),
                  pl.BlockSpec((rows, LANES), lambda i: (i, col_dt // LANES)),
                  pl.BlockSpec((None, nb, CONV_W - 1, cdim), lambda i: (0, i, 0, 0)),
                  pl.BlockSpec((nb, SSD_GROUPS, gw, SSD_STATE), lambda i: (i, 0, 0, 0)),
                  _const_spec((CONV_W, cdim)), _const_spec((1, cdim)), _const_spec((1, LANES)),
                  _const_spec((1, LANES)), _const_spec((1, d_inner)), _const_spec((1, d_inner)),
                  _const_spec((2 * LANES, d_inner))],
        out_specs=[pl.BlockSpec((rows, d_inner), lambda i: (i, 0)),
                   pl.BlockSpec((nb, SSD_GROUPS, gw, SSD_STATE), lambda i: (i, 0, 0, 0)),
                   pl.BlockSpec((None, nb, CONV_W - 1, cdim), lambda i: (0, i, 0, 0))],
        out_shape=[jax.ShapeDtypeStruct((bsz * steps, d_inner), BF16),
                   jax.ShapeDtypeStruct(s0.shape, F32),
                   jax.ShapeDtypeStruct(buf.shape, F32)],
        compiler_params=_cparams(("parallel",)),
        name="ssd_step",
    )(proj, proj, proj, buf, s0, sp["ssd_conv_w"], sp["ssd_conv_b"], sp["dt_bias"], sp["A"], sp["D"],
      sp["ssd_norm_g"], sp["head_expand"])


def _router(t, wr, br):
    t_hi = t.astype(BF16)
    t_lo = (t - t_hi.astype(F32)).astype(BF16)
    both = _dot(jnp.concatenate([t_hi, t_lo], axis=1), wr)
    logits = both[:, :LANES] + both[:, LANES:] + br
    lane = lax.broadcasted_iota(jnp.int32, logits.shape, 1)
    neg = -jnp.inf
    gl = jnp.where(lane < N_EGROUPS, logits, neg)
    gmax = jnp.max(gl, axis=-1, keepdims=True)
    g_idx = jnp.min(jnp.where(gl == gmax, lane, LANES), axis=-1, keepdims=True)
    g_w = 1.0 / jnp.sum(jnp.exp(gl - gmax), axis=-1, keepdims=True)
    in_grp = jnp.logical_and(jnp.logical_and(lane >= ROUTER_LANE0, lane < ROUTER_LANE0 + N_EXPERTS),
                             ((lane - ROUTER_LANE0) >> 2) == g_idx)
    el = jnp.where(in_grp, logits, neg)
    pe = jnp.exp(el - jnp.max(el, axis=-1, keepdims=True))
    pe = pe / jnp.sum(pe, axis=-1, keepdims=True)
    cand = jnp.where(in_grp, pe, -1.0)
    v1 = jnp.max(cand, axis=-1, keepdims=True)
    i1 = jnp.min(jnp.where(cand == v1, lane, LANES), axis=-1, keepdims=True)
    cand2 = jnp.where(lane == i1, -1.0, cand)
    v2 = jnp.max(cand2, axis=-1, keepdims=True)
    i2 = jnp.min(jnp.where(jnp.logical_and(cand2 == v2, in_grp), lane, LANES), axis=-1, keepdims=True)
    den = v1 + v2
    return lane, i1, i2, g_w * v1 / den, g_w * v2 / den


def _rows_to_tiles(ref, val):
    n, d = val.shape
    for k in range(d // LANES):
        ref[pl.ds(k, n, stride=d // LANES), :] = val[:, k * LANES:(k + 1) * LANES]


def _tiles_to_rows(ref, n, d, start=0):
    return jnp.concatenate([ref[pl.ds(start + k, n, stride=d // LANES), :] for k in range(d // LANES)], axis=1)


def _merge_body(x_ref, ya_ref, yb_ref, ga_ref, gb_ref, wl_ref, ws_ref, wo_ref, gf_ref, wr_ref, br_ref,
                x1_ref, t_ref, rt_ref, rtt_ref, cnt_ref, base_scr):
    step = pl.program_id(0)

    @pl.when(step == 0)
    def _():
        base_scr[...] = jnp.zeros_like(base_scr)

    a = _dot(ya_ref[...], wl_ref[...])
    b = _dot(yb_ref[...], ws_ref[...])
    merged = jax.nn.sigmoid(ga_ref[...].astype(F32)) * a + jax.nn.sigmoid(gb_ref[...].astype(F32)) * b
    x1 = x_ref[...] + _dot(merged.astype(BF16), wo_ref[...])
    x1_ref[...] = x1
    t = _rms(x1, gf_ref[...])
    _rows_to_tiles(t_ref, t)
    lane, i1, i2, wg1, wg2 = _router(t, wr_ref[...], br_ref[...])

    tm = t.shape[0]
    onehot = jnp.where(jnp.logical_or(lane == i1, lane == i2), 1.0, 0.0).astype(BF16)
    tri = (lax.broadcasted_iota(jnp.int32, (tm, tm), 1) <= lax.broadcasted_iota(jnp.int32, (tm, tm), 0)).astype(BF16)
    cum = _dot(tri, onehot) + base_scr[...]
    r1 = jnp.sum(jnp.where(lane == i1, cum, 0.0), axis=-1, keepdims=True) - 1.0
    r2 = jnp.sum(jnp.where(lane == i2, cum, 0.0), axis=-1, keepdims=True) - 1.0
    cols = (wg1, wg2, r1, r2, (i1 - ROUTER_LANE0).astype(F32), (i2 - ROUTER_LANE0).astype(F32))
    rt = jnp.zeros(cum.shape, F32)
    for k, c in enumerate(cols):
        rt = jnp.where(lane == k, c, rt)
    rt_ref[...] = rt
    rtt_ref[...] = rt.T[0:SUBLANES, :]
    base_scr[...] = cum[tm - 1:tm, :]
    cnt_ref[...] = cum[tm - 1:tm, :]


def _merge(x, ya, yb, proj, mw, tm, col_ga, col_gb):
    t, d = x.shape
    d_inner = yb.shape[1]
    return pl.pallas_call(
        _merge_body,
        grid=(t // tm,),
        in_specs=[pl.BlockSpec((tm, d), lambda i: (i, 0)),
                  pl.BlockSpec((tm, d), lambda i: (i, 0)),
                  pl.BlockSpec((tm, d_inner), lambda i: (i, 0)),
                  pl.BlockSpec((tm, d), lambda i: (i, col_ga // d)),
                  pl.BlockSpec((tm, d), lambda i: (i, col_gb // d)),
                  _const_spec((d, d)), _const_spec((d_inner, d)), _const_spec((d, d)),
                  _const_spec((1, d)), _const_spec((2 * d, 2 * LANES)), _const_spec((1, LANES))],
        out_specs=[pl.BlockSpec((tm, d), lambda i: (i, 0)),
                   pl.BlockSpec((tm * d // LANES, LANES), lambda i: (i, 0)),
                   pl.BlockSpec((tm, LANES), lambda i: (i, 0)),
                   pl.BlockSpec((SUBLANES, tm), lambda i: (0, i)),
                   pl.BlockSpec((1, LANES), lambda i: (0, 0))],
        out_shape=[jax.ShapeDtypeStruct((t, d), F32),
                   jax.ShapeDtypeStruct((t * d // LANES, LANES), F32),
                   jax.ShapeDtypeStruct((t, LANES), F32),
                   jax.ShapeDtypeStruct((SUBLANES, t), F32),
                   jax.ShapeDtypeStruct((1, LANES), F32)],
        scratch_shapes=[pltpu.VMEM((1, LANES), F32)],
        compiler_params=_cparams(("arbitrary",)),
        name="merge_router",
    )(x, ya, yb, proj, proj, mw["w_br_lru"], mw["w_br_ssd"], mw["w_out"], mw["g_ffn"], mw["w_router"],
      mw["b_router"])


def _dispatch_body(dest_ref, zb_ref, t_ref, o_ref, zero_scr, sem, zsem, *, tm, nk, tmg, n_tok):
    step = pl.program_id(0)

    @pl.when(step == 0)
    def _():
        zero_scr[...] = jnp.zeros_like(zero_scr)
        blk = tmg * nk

        def zcopy(j):
            return pltpu.make_async_copy(zero_scr, o_ref.at[pl.ds(pl.multiple_of(zb_ref[j] * blk, blk), blk)], zsem)

        for j in range(zb_ref.shape[0]):
            pl.when(zb_ref[j] >= 0)(lambda j=j: zcopy(j).start())
        for j in range(zb_ref.shape[0]):
            pl.when(zb_ref[j] >= 0)(lambda j=j: zcopy(j).wait())

    def issue(r, carry):
        src = t_ref.at[pl.ds(pl.multiple_of(r * nk, nk), nk)]
        for k in range(2):
            row = dest_ref[k * n_tok + step * tm + r]
            pltpu.make_async_copy(src, o_ref.at[pl.ds(pl.multiple_of(row * nk, nk), nk)], sem).start(priority=k)
        return carry

    lax.fori_loop(0, tm, issue, 0, unroll=8)
    for k in range(2):
        pltpu.make_async_copy(t_ref, o_ref.at[pl.ds(0, tm * nk)], sem).wait()


def _dispatch(dest, zero_blocks, t_tiles, n_tok, tm, n_rows, tmg):
    nk = t_tiles.shape[0] // n_tok
    return pl.pallas_call(
        functools.partial(_dispatch_body, tm=tm, nk=nk, tmg=tmg, n_tok=n_tok),
        grid_spec=pltpu.PrefetchScalarGridSpec(
            num_scalar_prefetch=2,
            grid=(n_tok // tm,),
            in_specs=[pl.BlockSpec((tm * nk, LANES), lambda i, *_: (i, 0))],
            out_specs=pl.BlockSpec(memory_space=pl.ANY),
            scratch_shapes=[pltpu.VMEM((tmg * nk, LANES), F32), pltpu.SemaphoreType.DMA(()),
                            pltpu.SemaphoreType.DMA(())]),
        out_shape=jax.ShapeDtypeStruct((n_rows * nk, LANES), F32),
        compiler_params=_cparams(("arbitrary",)),
        name="moe_dispatch",
    )(dest, zero_blocks, t_tiles)


def _expert_body(te_ref, nt_ref, x_ref, w1_ref, w3_ref, w2_ref, y_ref, w1_scr, w3_scr, w2_scr, *, tmg):
    i = pl.program_id(0)
    real = i < nt_ref[0]
    d = w1_scr.shape[0]

    @pl.when(jnp.logical_or(i == 0, te_ref[i] != te_ref[jnp.maximum(i - 1, 0)]))
    def _():
        w1_scr[...] = w1_ref[0].astype(BF16)
        w3_scr[...] = w3_ref[0].astype(BF16)
        w2_scr[...] = w2_ref[0].astype(BF16)

    @pl.when(real)
    def _():
        x = _tiles_to_rows(x_ref, tmg, d).astype(BF16)
        h1 = _dot(x, w1_scr[...])
        h3 = _dot(x, w3_scr[...])
        _rows_to_tiles(y_ref, _dot((h1 * jax.nn.sigmoid(h1) * h3).astype(BF16), w2_scr[...]))

    @pl.when(jnp.logical_not(real))
    def _():
        y_ref[...] = jnp.zeros_like(y_ref)


def _experts(tile_expert, n_tiles, xs_tiles, w1, w3, w2, tmg):
    _, d, dff = w1.shape
    blk = tmg * d // LANES
    row_spec = pl.BlockSpec((blk, LANES), lambda i, te, nt: (i, 0))
    return pl.pallas_call(
        functools.partial(_expert_body, tmg=tmg),
        grid_spec=pltpu.PrefetchScalarGridSpec(
            num_scalar_prefetch=2,
            grid=(xs_tiles.shape[0] // blk,),
            in_specs=[row_spec,
                      pl.BlockSpec((1, d, dff), lambda i, te, nt: (te[i], 0, 0)),
                      pl.BlockSpec((1, d, dff), lambda i, te, nt: (te[i], 0, 0)),
                      pl.BlockSpec((1, dff, d), lambda i, te, nt: (te[i], 0, 0))],
            out_specs=row_spec,
            scratch_shapes=[pltpu.VMEM((d, dff), BF16), pltpu.VMEM((d, dff), BF16), pltpu.VMEM((dff, d), BF16)]),
        out_shape=jax.ShapeDtypeStruct(xs_tiles.shape, F32),
        compiler_params=_cparams(("arbitrary",)),
        name="moe_experts",
    )(tile_expert, n_tiles, xs_tiles, w1, w3, w2)


def _ple_body(dest_ref, x_ref, rt_ref, p_ref, wp_ref, gp_ref, gg_ref, wg_ref, gfin_ref, y_hbm, o_ref, gbuf, sem,
              *, n_tok):
    step = pl.program_id(0)
    tm, d = x_ref.shape
    nk = d // LANES

    def gather(tile, slot):
        def issue(r, carry):
            tok = tile * tm + r
            for k in range(2):
                row = dest_ref[k * n_tok + tok]
                pltpu.make_async_copy(y_hbm.at[pl.ds(pl.multiple_of(row * nk, nk), nk)],
                                      gbuf.at[slot, pl.ds(pl.multiple_of((k * tm + r) * nk, nk), nk)],
                                      sem.at[slot]).start(priority=k)
            return carry

        lax.fori_loop(0, tm, issue, 0, unroll=8)

    @pl.when(step == 0)
    def _():
        gather(0, 0)

    @pl.when(step + 1 < pl.num_programs(0))
    def _():
        gather(step + 1, (step + 1) % 2)

    slot = step % 2
    pltpu.make_async_copy(y_hbm.at[pl.ds(0, 2 * tm * nk)], gbuf.at[slot], sem.at[slot]).wait()
    rt = rt_ref[...]
    rows = gbuf.at[slot]
    x = (x_ref[...] + rt[:, 0:1] * _tiles_to_rows(rows, tm, d)
         + rt[:, 1:2] * _tiles_to_rows(rows, tm, d, start=tm * nk))
    e = _rms(_dot(p_ref[...].astype(BF16), wp_ref[...]), gp_ref[...])
    gate = jax.nn.sigmoid(_dot(_rms(x, gg_ref[...]).astype(BF16), wg_ref[...]))
    o_ref[...] = _rms(x + gate * e, gfin_ref[...])


def _ple(dest, x1, rt, p, y_sorted, pw, tm):
    n, d = x1.shape
    dp = p.shape[1]
    const = lambda shape: pl.BlockSpec(shape, lambda i, *_: (0,) * len(shape))
    return pl.pallas_call(
        functools.partial(_ple_body, n_tok=n),
        grid_spec=pltpu.PrefetchScalarGridSpec(
            num_scalar_prefetch=1,
            grid=(n // tm,),
            in_specs=[pl.BlockSpec((tm, d), lambda i, *_: (i, 0)),
                      pl.BlockSpec((tm, LANES), lambda i, *_: (i, 0)),
                      pl.BlockSpec((tm, dp), lambda i, *_: (i, 0)),
                      const((dp, d)), const((1, d)), const((1, d)), const((d, d)), const((1, d)),
                      pl.BlockSpec(memory_space=pl.ANY)],
            out_specs=pl.BlockSpec((tm, d), lambda i, *_: (i, 0)),
            scratch_shapes=[pltpu.VMEM((2, 2 * tm * d // LANES, LANES), F32), pltpu.SemaphoreType.DMA((2,))]),
        out_shape=jax.ShapeDtypeStruct((n, d), F32),
        compiler_params=_cparams(("arbitrary",)),
        name="combine_ple_final",
    )(dest, x1, rt, p, pw["w_ple_proj"], pw["g_ple"], pw["g_ple_gate"], pw["w_ple_gate"], pw["g_final"], y_sorted)


def _pick_tile(n, pref):
    t = min(n, pref)
    while n % t:
        t //= 2
    return t


def _moe_row_tile(n):
    return 512 if 2 * n // N_EXPERTS >= 1024 else 128
MOE_DISPATCH_TILE = 2048


def _split_router(w):
    hi = w.astype(BF16)
    lo = (w - hi.astype(F32)).astype(BF16)
    return jnp.concatenate([jnp.concatenate([hi, lo], axis=1),
                            jnp.concatenate([hi, jnp.zeros_like(hi)], axis=1)], axis=0)


def _token_tail(x1, t_tiles, rt, rtt, cnt, p, lw, tm):
    n = x1.shape[0]
    tmg = _moe_row_tile(n)
    n_blocks = pl.cdiv(2 * n, tmg) + N_EXPERTS
    counts = cnt[0, ROUTER_LANE0:ROUTER_LANE0 + N_EXPERTS].astype(jnp.int32)
    tiles = (counts + tmg - 1) // tmg
    ends = jnp.cumsum(tiles)
    n_tiles = ends[-1]
    e_idx = rtt[4:6].astype(jnp.int32)
    first_row = (ends - tiles) * tmg
    dest = rtt[2:4].astype(jnp.int32) + sum(jnp.where(e_idx == e, first_row[e], 0) for e in range(N_EXPERTS))
    dest = dest.reshape(2 * n)
    blk = jnp.arange(n_blocks, dtype=jnp.int32)
    tile_expert = jnp.sum((jnp.minimum(blk, n_tiles - 1)[:, None] >= ends[None, :]).astype(jnp.int32), axis=1)
    tail = n_tiles + blk[:N_EXPERTS]
    zero_blocks = jnp.concatenate([jnp.where(tiles > 0, ends - 1, -1),
                                   jnp.where(tail < n_blocks, tail, -1)]).astype(jnp.int32)

    sorted_t = _dispatch(dest, zero_blocks, t_tiles, n, _pick_tile(n, MOE_DISPATCH_TILE), n_blocks * tmg, tmg)
    y_sorted = _experts(tile_expert, n_tiles.reshape(1), sorted_t, lw["w1"], lw["w3"], lw["w2"], tmg)
    return _ple(dest, x1, rt, p, y_sorted, lw, tm)


def kernel(x_prompt, x_sample, state_lru_h, state_lru_conv, state_ssd, state_ssd_conv, p_prompt, p_sample, g_mix, w_in, lru_conv_w, lru_conv_b, lru_wa, lru_ba, lru_wx, lru_bx, lru_lambda, ssd_conv_w, ssd_conv_b, ssd_dt_bias, ssd_A_log, ssd_D, ssd_norm_g, w_br_lru, w_br_ssd, w_out, g_ffn, w_router_g, b_router_g, w_router_e, b_router_e, w1, w3, w2, w_ple_proj, g_ple, g_ple_gate, w_ple_gate, g_final):
    depth = w_in.shape[0]
    assert depth == 1, "one decoder layer per call"
    bp, lp, d = x_prompt.shape
    bs, ls, _ = x_sample.shape
    w_lru = state_lru_h.shape[-1]
    heads, hdim, nstate = state_ssd.shape[2:]
    d_inner = heads * hdim
    cdim = state_ssd_conv.shape[-1]
    assert hdim == SSD_HEADDIM and nstate == SSD_STATE and heads <= LANES and ls < SUBLANES
    gw = d_inner // SSD_GROUPS

    o_dt = 2 * w_lru + d_inner + cdim
    n_proj = o_dt + 2 * d + LANES
    tn = n_proj // 9 if n_proj % (9 * LANES) == 0 else LANES
    n_head = o_dt // tn
    w_head = w_in.astype(BF16)
    wi = w_head[0]
    w_tail = jnp.concatenate([wi[:, n_head * tn:o_dt], wi[:, o_dt + heads:], wi[:, o_dt:o_dt + heads],
                              jnp.zeros((d, LANES - heads), BF16)], axis=1)
    assert n_head >= 1 and w_tail.shape[1] == n_proj - n_head * tn and w_tail.shape[1] % tn == 0
    col_z, col_xbc = 2 * w_lru, 2 * w_lru + d_inner
    col_ga, col_gb, col_dt = o_dt, o_dt + d, o_dt + 2 * d
    row = lambda v: v.reshape(1, -1).astype(F32)
    pad_heads = lambda v: jnp.pad(v.astype(F32), (0, LANES - heads)).reshape(1, LANES)
    lw = {
        "lru_conv_w": lru_conv_w[0], "lru_conv_b": row(lru_conv_b[0]),
        "wax": jnp.concatenate([lru_wa[0], lru_wx[0]], axis=-1).astype(BF16),
        "lru_ba": row(lru_ba[0]), "lru_bx": row(lru_bx[0]), "lru_lambda": row(lru_lambda[0]),
        "ssd_conv_w": ssd_conv_w[0], "ssd_conv_b": row(ssd_conv_b[0]),
        "dt_bias": pad_heads(ssd_dt_bias[0]), "A": pad_heads(-jnp.exp(ssd_A_log[0].astype(F32))),
        "D": row(jnp.repeat(ssd_D[0], hdim)), "ssd_norm_g": row(ssd_norm_g[0]),
        "head_expand": jnp.tile(jnp.arange(LANES)[:, None] == jnp.arange(d_inner)[None, :] // hdim, (2, 1)).astype(BF16),
        "w_br_lru": w_br_lru[0].astype(BF16), "w_br_ssd": w_br_ssd[0].astype(BF16), "w_out": w_out[0].astype(BF16),
        "g_ffn": row(g_ffn[0]),
        "w_router": _split_router(jnp.concatenate([w_router_g[0], w_router_e[0],
                                                   jnp.zeros((d, LANES - N_EGROUPS - N_EXPERTS), F32)], axis=1)),
        "b_router": jnp.concatenate([b_router_g[0], b_router_e[0],
                                     jnp.zeros((LANES - N_EGROUPS - N_EXPERTS,), F32)]).reshape(1, LANES),
        "w1": w1[0], "w3": w3[0], "w2": w2[0],
        "w_ple_proj": w_ple_proj[0].astype(BF16), "g_ple": row(g_ple[0]), "g_ple_gate": row(g_ple_gate[0]),
        "w_ple_gate": w_ple_gate[0].astype(BF16), "g_final": row(g_final),
    }
    g_mix_r = row(g_mix[0])

    tp = bp * lp
    xp = x_prompt.reshape(tp, d)
    tm_p = _pick_tile(tp, 1024)
    proj_p = _inproj(xp, g_mix_r, w_head, w_tail, tm_p, tn, n_head)
    proj_p3 = proj_p.reshape(bp, lp, n_proj)
    ya_p, hl_p, lbuf_p = _lru_prompt(proj_p3, jnp.zeros((bp, CONV_W - 1, w_lru), F32), jnp.zeros((bp, w_lru), F32),
                                     lw, _pick_tile(lp, 256), True)
    yb_p, s_p, sbuf_p = _ssd_prompt(proj_p3, lw, col_xbc, col_z, col_dt)
    tm_tail_p = _pick_tile(tp, 512)
    x1_p, t_p, rt_p, rtt_p, cnt_p = _merge(xp, ya_p.reshape(tp, w_lru), yb_p.reshape(tp, d_inner), proj_p, lw,
                                           tm_tail_p, col_ga, col_gb)
    y_p = _token_tail(x1_p, t_p, rt_p, rtt_p, cnt_p, p_prompt[0].reshape(tp, -1), lw, tm_tail_p)

    ts = bs * ls
    xs = x_sample.reshape(ts, d)
    tm_s = _pick_tile(ts, 512)
    proj_s = _inproj(xs, g_mix_r, w_head, w_tail, tm_s, tn, n_head)
    to_tmajor = lambda v, n: v.reshape(bs, n, -1).transpose(1, 0, 2).reshape(n * bs, -1)
    from_tmajor = lambda v, n: v.reshape(n, bs, -1).transpose(1, 0, 2)
    ya_t, hl_s, lbuf_t = _lru_step(to_tmajor(proj_s[:, :w_lru], ls), to_tmajor(proj_s[:, w_lru:2 * w_lru], ls),
                                   to_tmajor(state_lru_conv[0], CONV_W - 1), state_lru_h[0], lw, ls)
    ya_s = from_tmajor(ya_t, ls).reshape(ts, w_lru)
    lbuf_s = from_tmajor(lbuf_t, CONV_W - 1)
    yb_s, s_s, sbuf_s = _ssd_step(proj_s, state_ssd_conv, state_ssd[0].reshape(bs, SSD_GROUPS, gw, nstate), lw,
                                  ls, _pick_tile(bs, 8), col_xbc, col_z, col_dt)
    x1_s, t_s, rt_s, rtt_s, cnt_s = _merge(xs, ya_s, yb_s, proj_s, lw, tm_s, col_ga, col_gb)
    y_s = _token_tail(x1_s, t_s, rt_s, rtt_s, cnt_s, p_sample[0].reshape(ts, -1), lw, tm_s)

    return (y_p.reshape(bp, lp, d), y_s.reshape(bs, ls, d),
            hl_p.reshape(1, bp, w_lru), lbuf_p[None],
            s_p.reshape(1, bp, heads, hdim, nstate), sbuf_p[None],
            hl_s[None], lbuf_s[None],
            s_s.reshape(1, bs, heads, hdim, nstate), sbuf_s)
```

```python
import functools

import jax
import jax.numpy as jnp
from jax import lax
from jax.experimental import pallas as pl
from jax.experimental.pallas import tpu as pltpu

F32 = jnp.float32
BF16 = jnp.bfloat16

EPS = 1e-6
CONV_W = 4
LRU_BLOCKS = 8
LRU_C = 8.0
SSD_HEADDIM = 64
SSD_GROUPS = 8
SSD_STATE = 128
SSD_CHUNK = 128
N_EGROUPS = 4
EXP_PER_GROUP = 4
N_EXPERTS = N_EGROUPS * EXP_PER_GROUP

LANES = 128
SUBLANES = 8
VMEM_LIMIT = 52 * 1024 * 1024
ROUTER_LANE0 = N_EGROUPS


def _cparams(sem):
    return pltpu.CompilerParams(dimension_semantics=sem, vmem_limit_bytes=VMEM_LIMIT)


def _dot(a, b):
    return jnp.dot(a, b, preferred_element_type=F32)


def _dot_nt(a, b):
    return lax.dot_general(a, b, (((1,), (1,)), ((), ())), preferred_element_type=F32)


def _dot_tn(a, b):
    return lax.dot_general(a, b, (((0,), (0,)), ((), ())), preferred_element_type=F32)


def _dot_f32(a, b):
    return jnp.dot(a, b, precision=lax.Precision.HIGHEST, preferred_element_type=F32)


def _rms(x, g):
    return x * lax.rsqrt(jnp.mean(x * x, axis=-1, keepdims=True) + EPS) * g


def _const_spec(shape):
    nd = len(shape)
    return pl.BlockSpec(shape, lambda *_: (0,) * nd)


def _inproj_body(x_ref, g_ref, wh_ref, wt_ref, o_ref, h_scr, *, n_head):
    j = pl.program_id(1)

    @pl.when(j == 0)
    def _():
        h_scr[...] = _rms(x_ref[...], g_ref[...]).astype(BF16)

    @pl.when(j < n_head)
    def _():
        o_ref[...] = _dot(h_scr[...], wh_ref[0]).astype(o_ref.dtype)

    @pl.when(j >= n_head)
    def _():
        o_ref[...] = _dot(h_scr[...], wt_ref[...]).astype(o_ref.dtype)


def _inproj(x, g, w_head, w_tail, tm, tn, n_head):
    t, d = x.shape
    n_tail = w_tail.shape[1] // tn
    return pl.pallas_call(
        functools.partial(_inproj_body, n_head=n_head),
        grid=(t // tm, n_head + n_tail),
        in_specs=[pl.BlockSpec((tm, d), lambda i, j: (i, 0)),
                  pl.BlockSpec((1, d), lambda i, j: (0, 0)),
                  pl.BlockSpec((1, d, tn), lambda i, j: (0, 0, jnp.minimum(j, n_head - 1))),
                  pl.BlockSpec((d, tn), lambda i, j: (0, jnp.maximum(j - n_head, 0)))],
        out_specs=pl.BlockSpec((tm, tn), lambda i, j: (i, j)),
        out_shape=jax.ShapeDtypeStruct((t, (n_head + n_tail) * tn), BF16),
        scratch_shapes=[pltpu.VMEM((tm, d), BF16)],
        compiler_params=_cparams(("parallel", "arbitrary")),
        name="inproj",
    )(x, g, w_head, w_tail)


def _lru_gates(u, wax_ref, ba, bx, lam):
    bw = u.shape[1] // LRU_BLOCKS
    r_parts, i_parts = [], []
    for n in range(LRU_BLOCKS):
        ri = _dot(u[:, n * bw:(n + 1) * bw].astype(BF16), wax_ref[n])
        r_parts.append(ri[:, :bw])
        i_parts.append(ri[:, bw:])
    r = jax.nn.sigmoid(jnp.concatenate(r_parts, axis=1) + ba)
    i = jax.nn.sigmoid(jnp.concatenate(i_parts, axis=1) + bx)
    log_a = LRU_C * r * jax.nn.log_sigmoid(lam)
    a = jnp.exp(log_a)
    m2 = -jnp.tanh(log_a) * (a * a + 1.0)
    mult = jnp.where(m2 > 0.0, m2 * lax.rsqrt(m2), 0.0)
    return a, i, mult


def _lru_rows(x, gate, halo, carry, seq_start, cw, cb, wax_ref, ba, bx, lam):
    tt, width = x.shape
    halo = jnp.where(seq_start, 0.0, halo)
    carry = jnp.where(seq_start, 0.0, carry)
    xpad = jnp.concatenate([halo, x], axis=0)
    base = SUBLANES - (CONV_W - 1)
    u = cb + sum(xpad[base + k:base + k + tt] * cw[k:k + 1] for k in range(CONV_W))
    a, i, mult = _lru_gates(u, wax_ref, ba, bx, lam)
    first = jnp.logical_and(lax.broadcasted_iota(jnp.int32, a.shape, 0) == 0, seq_start)
    mult = jnp.where(first, 1.0, mult)
    a = jnp.where(first, 0.0, a)
    v = u * i * mult

    a = a.reshape(tt // SUBLANES, SUBLANES, width)
    v = v.reshape(tt // SUBLANES, SUBLANES, width)
    sub = lax.broadcasted_iota(jnp.int32, a.shape, 1)
    s = 1
    while s < SUBLANES:
        keep = sub >= s
        v = jnp.where(keep, a * pltpu.roll(v, s, axis=1) + v, v)
        a = jnp.where(keep, a * pltpu.roll(a, s, axis=1), a)
        s *= 2
    groups = []
    for g in range(tt // SUBLANES):
        hg = a[g] * carry + v[g]
        carry = hg[SUBLANES - 1:SUBLANES]
        groups.append(hg)
    h = jnp.concatenate(groups, axis=0)
    return h * jax.nn.gelu(gate), carry


def _inproj_lru_body(x_ref, g_ref, wh_ref, wt_ref, cw_ref, cb_ref, wax_ref, ba_ref, bx_ref, lam_ref,
                     o_ref, ya_ref, hlast_ref, bufout_ref, h_scr, lru_new, lru_cur, halo_scr, carry_scr,
                     *, n_head, n_j, n_tiles, tiles_per_seq, sub_rows):
    i = pl.program_id(0)
    j = pl.program_id(1)
    tm, tn = o_ref.shape
    w = ya_ref.shape[1]
    n_sub = tm // sub_rows
    prev_tile = jnp.maximum(i - 1, 0)

    @pl.when(jnp.logical_and(i == 0, j == 0))
    def _():
        lru_cur[...] = jnp.zeros_like(lru_cur)
        halo_scr[...] = jnp.zeros_like(halo_scr)
        carry_scr[...] = jnp.zeros_like(carry_scr)

    n_piece = LRU_PIECES
    piece_rows = sub_rows // n_piece
    mx_w = 2 * LANES
    col_cuts = [min(tn, mx_w * (q * (tn // mx_w) // n_piece)) for q in range(n_piece)] + [tn]

    def lru_piece(q, state):
        r0 = pl.multiple_of(j * sub_rows, sub_rows) + q * piece_rows
        seq_start = jnp.logical_and(jnp.logical_and(prev_tile % tiles_per_seq == 0, j == 0), q == 0)
        x = lru_cur[pl.ds(r0, piece_rows), 0:w].astype(F32)
        gate = lru_cur[pl.ds(r0, piece_rows), w:2 * w].astype(F32)
        halo, carry = state if state is not None else (halo_scr[...], carry_scr[...])
        ya, carry = _lru_rows(x, gate, halo, carry, seq_start, cw_ref[...], cb_ref[...],
                              wax_ref, ba_ref[...], bx_ref[...], lam_ref[...])
        ya_ref[pl.ds(r0, piece_rows), :] = ya.astype(ya_ref.dtype)
        halo = x[piece_rows - SUBLANES:piece_rows]
        if q == n_piece - 1:
            halo_scr[...] = halo
            carry_scr[...] = carry

            @pl.when(jnp.logical_and(prev_tile % tiles_per_seq == tiles_per_seq - 1, j == n_sub - 1))
            def _():
                hlast_ref[0] = carry
                bufout_ref[0] = x[piece_rows - (CONV_W - 1):piece_rows]
        return halo, carry

    def project_piece(q, from_head, lru_lo):
        c0, c1 = col_cuts[q], col_cuts[q + 1]
        if c0 == c1:
            return
        w_cols = wh_ref[0, :, c0:c1] if from_head else wt_ref[:, c0:c1]
        o = _dot(h_scr[...], w_cols).astype(o_ref.dtype)
        o_ref[:, c0:c1] = o
        if lru_lo is not None:
            keep = min(lru_lo + c1, 2 * w) - (lru_lo + c0)
            if keep > 0:
                lru_new[:, lru_lo + c0:lru_lo + c0 + keep] = o[:, 0:keep]

    def column_steps(lo, hi, from_head, prologue=None, lru_lo=None):
        for a, b, with_lru in ((lo, min(hi, n_sub), True), (max(lo, n_sub), hi, False)):
            if a < b:
                @pl.when(jnp.logical_and(i < n_tiles, jnp.logical_and(j >= a, j < b)))
                def _(with_lru=with_lru):
                    if prologue is not None:
                        prologue()
                    state = None
                    for q in range(n_piece):
                        project_piece(q, from_head, lru_lo)
                        if with_lru:
                            state = lru_piece(q, state)
                    if b == n_j:
                        @pl.when(j == n_j - 1)
                        def _():
                            lru_cur[...] = lru_new[...]

    def normalise():
        h_scr[...] = _rms(x_ref[...], g_ref[...]).astype(BF16)

    column_steps(0, 1, True, prologue=normalise, lru_lo=0)
    column_steps(1, 2, True, lru_lo=tn)
    column_steps(2, n_head, True)
    column_steps(n_head, n_j, False)

    @pl.when(jnp.logical_and(i == n_tiles, j < n_sub))
    def _():
        state = None
        for q in range(n_piece):
            state = lru_piece(q, state)


def _inproj_lru(x, g, w_head, w_tail, lw, seq_len, tm, tn, n_head, sub_rows):
    t, d = x.shape
    w = lw["lru_lambda"].shape[1]
    n_tail = w_tail.shape[1] // tn
    n_tiles, n_j = t // tm, n_head + n_tail
    tiles_per_seq = seq_len // tm
    assert seq_len % tm == 0 and tm % sub_rows == 0 and tm // sub_rows < n_j and n_head >= 2
    assert w <= tn and 2 * w <= 2 * tn and 2 * w > tn
    cur_tile = lambda i: jnp.minimum(i, n_tiles - 1)
    lru_tile = lambda i: jnp.maximum(i - 1, 0)
    body = functools.partial(_inproj_lru_body, n_head=n_head, n_j=n_j, n_tiles=n_tiles,
                             tiles_per_seq=tiles_per_seq, sub_rows=sub_rows)
    out_col = lambda i, j: jnp.where(i < n_tiles, j, n_j - 1)
    w_col = lambda i, j: jnp.where(i < n_tiles, j, n_j - 1)
    return pl.pallas_call(
        body,
        grid=(n_tiles + 1, n_j),
        in_specs=[pl.BlockSpec((tm, d), lambda i, j: (cur_tile(i), 0)),
                  pl.BlockSpec((1, d), lambda i, j: (0, 0)),
                  pl.BlockSpec((1, d, tn), lambda i, j: (0, 0, jnp.minimum(w_col(i, j), n_head - 1))),
                  pl.BlockSpec((d, tn), lambda i, j: (0, jnp.maximum(w_col(i, j) - n_head, 0))),
                  _const_spec((CONV_W, w)), _const_spec((1, w)),
                  _const_spec(lw["wax"].shape), _const_spec((1, w)), _const_spec((1, w)), _const_spec((1, w))],
        out_specs=[pl.BlockSpec((tm, tn), lambda i, j: (cur_tile(i), out_col(i, j))),
                   pl.BlockSpec((tm, w), lambda i, j: (lru_tile(i), 0)),
                   pl.BlockSpec((1, 1, w), lambda i, j: (lru_tile(i) // tiles_per_seq, 0, 0)),
                   pl.BlockSpec((1, CONV_W - 1, w), lambda i, j: (lru_tile(i) // tiles_per_seq, 0, 0))],
        out_shape=[jax.ShapeDtypeStruct((t, n_j * tn), BF16),
                   jax.ShapeDtypeStruct((t, w), BF16),
                   jax.ShapeDtypeStruct((t // seq_len, 1, w), F32),
                   jax.ShapeDtypeStruct((t // seq_len, CONV_W - 1, w), F32)],
        scratch_shapes=[pltpu.VMEM((tm, d), BF16), pltpu.VMEM((tm, 2 * w), BF16), pltpu.VMEM((tm, 2 * w), BF16),
                        pltpu.VMEM((SUBLANES, w), F32), pltpu.VMEM((1, w), F32)],
        compiler_params=_cparams(("arbitrary", "arbitrary")),
        name="inproj_lru",
    )(x, g, w_head, w_tail, lw["lru_conv_w"], lw["lru_conv_b"], lw["wax"], lw["lru_ba"], lw["lru_bx"],
      lw["lru_lambda"])


def _lru_step_body(xin_ref, gate_ref, buf_ref, h0_ref, cw_ref, cb_ref, wax_ref, ba_ref, bx_ref, lam_ref,
                   ya_ref, hlast_ref, bufout_ref, *, steps):
    bsz = h0_ref.shape[0]
    n = steps * bsz
    x = xin_ref[...].astype(F32)
    xx = jnp.concatenate([buf_ref[...], x], axis=0)
    cw = cw_ref[...]
    u = cb_ref[...] + sum(xx[k * bsz:k * bsz + n] * cw[k:k + 1] for k in range(CONV_W))
    a, i, mult = _lru_gates(u, wax_ref, ba_ref[...], bx_ref[...], lam_ref[...])
    v = u * i * mult
    h = h0_ref[...]
    for t in range(steps):
        sl = slice(t * bsz, (t + 1) * bsz)
        h = a[sl] * h + v[sl]
        ya_ref[sl, :] = (h * jax.nn.gelu(gate_ref[sl, :].astype(F32))).astype(ya_ref.dtype)
    hlast_ref[...] = h
    bufout_ref[...] = xx[steps * bsz:(steps + CONV_W - 1) * bsz]


def _lru_step(xin_t, gate_t, buf_t, h0, lw, steps):
    bsz, w = h0.shape
    body = functools.partial(_lru_step_body, steps=steps)
    return pl.pallas_call(
        body,
        out_shape=[jax.ShapeDtypeStruct((steps * bsz, w), BF16),
                   jax.ShapeDtypeStruct((bsz, w), F32),
                   jax.ShapeDtypeStruct(((CONV_W - 1) * bsz, w), F32)],
        compiler_params=pltpu.CompilerParams(vmem_limit_bytes=VMEM_LIMIT),
        name="lru_step",
    )(xin_t, gate_t, buf_t, h0, lw["lru_conv_w"], lw["lru_conv_b"], lw["wax"],
      lw["lru_ba"], lw["lru_bx"], lw["lru_lambda"])


def _expand_heads(cols, e2):
    q = cols[0].shape[0]
    v = jnp.concatenate(cols, axis=0)
    hi = v.astype(BF16)
    lo = (v - hi.astype(F32)).astype(BF16)
    out = _dot(jnp.concatenate([hi, lo], axis=1), e2)
    return [out[i * q:(i + 1) * q] for i in range(len(cols))]


def _ssd_chunk(xc, dt, p, e2, s_get, s_set, t_col, t_row, n_seg=1, n_valid=None):
    q = xc.shape[0]
    rps = q // n_seg
    gn = SSD_GROUPS * SSD_STATE
    d_inner = xc.shape[1] - 2 * gn
    hpg = d_inner // SSD_HEADDIM // SSD_GROUPS
    gw = hpg * SSD_HEADDIM

    causal = t_col >= t_row
    if n_seg > 1:
        same = (lax.broadcasted_iota(jnp.int32, (q, 1), 0) // rps) == (lax.broadcasted_iota(jnp.int32, (1, q), 1) // rps)
        causal = jnp.logical_and(same, causal)
    if n_valid is not None:
        dt = jnp.where(lax.broadcasted_iota(jnp.int32, dt.shape, 0) % rps < n_valid, dt, 0.0)
    a = dt * p["A"]
    cum = _dot_f32(causal.astype(F32), a)
    cum_t = cum.T
    total = cum[q - 1:q, :] if n_seg == 1 else _dot_f32(same.astype(F32), a)
    dt_x, to_end_x, ecum_x = _expand_heads([dt, jnp.exp(total - cum), jnp.exp(cum)], e2)

    xs = xc[:, :d_inner]
    xdt = xs * dt_x
    xw = xdt * to_end_x
    packed = rps % (2 * SUBLANES) == 0
    xdt_m = xdt.astype(BF16) if packed else xdt
    if packed:
        xw = xw.astype(BF16)
    lane_head = lax.broadcasted_iota(jnp.int32, (1, gw), 1) // SSD_HEADDIM
    y_groups = []
    for g in range(SSD_GROUPS):
        sl = slice(g * gw, (g + 1) * gw)
        bg = xc[:, d_inner + g * SSD_STATE:d_inner + (g + 1) * SSD_STATE]
        cg = xc[:, d_inner + gn + g * SSD_STATE:d_inner + gn + (g + 1) * SSD_STATE]
        if packed:
            bg, cg = bg.astype(BF16), cg.astype(BF16)
        cb = _dot_nt(cg.astype(BF16), bg.astype(BF16))
        m_heads, x_heads = [], []
        for hh in range(hpg):
            h = g * hpg + hh
            decay = jnp.exp(jnp.where(causal, cum[:, h:h + 1] - cum_t[h:h + 1, :], -jnp.inf))
            m_heads.append((cb * decay).astype(BF16))
            x_heads.append(jnp.where(lane_head == hh, xdt_m[:, sl], 0.0))
        y_diag = _dot(jnp.concatenate(m_heads, axis=1), jnp.concatenate(x_heads, axis=0).astype(BF16))
        y_off = []
        for b in range(n_seg):
            rows = slice(b * rps, (b + 1) * rps)
            s_old = s_get(b, g)
            y_off.append(_dot_nt(cg[rows].astype(BF16), s_old.astype(BF16)))
            s_dec = [s_old[hh * SSD_HEADDIM:(hh + 1) * SSD_HEADDIM, :]
                     * jnp.exp(cum_t[g * hpg + hh:g * hpg + hh + 1, (b + 1) * rps - 1:(b + 1) * rps])
                     for hh in range(hpg)]
            s_set(b, g, jnp.concatenate(s_dec, axis=0) + _dot_tn(xw[rows, sl].astype(BF16), bg[rows].astype(BF16)))
        y_off = y_off[0] if n_seg == 1 else jnp.concatenate(y_off, axis=0)
        y_groups.append(y_diag + y_off * ecum_x[:, sl])
    return jnp.concatenate(y_groups, axis=1) + p["D"] * xs


def _ssd_gate_norm(y, z, p):
    gw = y.shape[1] // SSD_GROUPS
    zf = z.astype(F32)
    out = []
    for g in range(SSD_GROUPS):
        sl = slice(g * gw, (g + 1) * gw)
        v = y[:, sl] * (zf[:, sl] * jax.nn.sigmoid(zf[:, sl]))
        out.append(v * lax.rsqrt(jnp.mean(v * v, axis=-1, keepdims=True) + EPS) * p["norm_g"][:, sl])
    return jnp.concatenate(out, axis=1)


def _ssd_conv(xpad, q, cw, cb):
    base = SUBLANES - (CONV_W - 1)
    y = cb + sum(xpad[base + k:base + k + q] * cw[k:k + 1] for k in range(CONV_W))
    return y * jax.nn.sigmoid(y)


def _softplus(x):
    return jax.nn.softplus(x)


def _ssd_params(cw_ref, cb_ref, dtb_ref, a_ref, d_ref, ng_ref):
    return {"cw": cw_ref[...], "cb": cb_ref[...], "dt_bias": dtb_ref[...], "A": a_ref[...],
            "D": d_ref[...], "norm_g": ng_ref[...]}


def _ssd_prompt_body(xbc_ref, z_ref, dt_ref, cw_ref, cb_ref, dtb_ref, a_ref, d_ref, ng_ref, e2_ref,
                     yb_ref, sout_ref, bufout_ref, x_scr, dt_scr, y_scr, s_scr):
    c = pl.program_id(1)
    q = xbc_ref.shape[1]
    half = q // 2
    n_xslab = x_scr.shape[0]
    base = SUBLANES - (CONV_W - 1)

    @pl.when(c == 0)
    def _():
        x_scr[:, 0:SUBLANES, :] = jnp.zeros((n_xslab, SUBLANES, LANES), F32)
        s_scr[...] = jnp.zeros_like(s_scr)

    @pl.when(c > 0)
    def _():
        x_scr[:, 0:SUBLANES, :] = x_scr[:, q:q + SUBLANES, :]

    p = _ssd_params(cw_ref, cb_ref, dtb_ref, a_ref, d_ref, ng_ref)
    x = xbc_ref[0].astype(F32)
    for j in range(n_xslab):
        x_scr[j, SUBLANES:SUBLANES + q, :] = x[:, j * LANES:(j + 1) * LANES]

    cols = []
    for j in range(n_xslab):
        ls = slice(j * LANES, (j + 1) * LANES)
        halves = []
        for par in range(2):
            acc = p["cb"][:, ls]
            for k in range(CONV_W):
                acc = acc + x_scr[j, pl.ds(base + k + par, half, stride=2), :] * p["cw"][k:k + 1, ls]
            halves.append(acc)
        cols.append(jnp.concatenate(halves, axis=0))
    xc = jnp.concatenate(cols, axis=1)
    xc = xc * jax.nn.sigmoid(xc)

    dt_scr[...] = _softplus(dt_ref[0].astype(F32) + p["dt_bias"])
    dt = jnp.concatenate([dt_scr[pl.ds(par, half, stride=2), :] for par in range(2)], axis=0)

    def times(shape, axis):
        pos = lax.broadcasted_iota(jnp.int32, shape, axis)
        return jnp.where(pos < half, 2 * pos, 2 * (pos - half) + 1)

    def s_set(b, g, v):
        s_scr[g] = v

    y = _ssd_chunk(xc, dt, p, e2_ref[...], lambda b, g: s_scr[g], s_set, times((q, 1), 0), times((1, q), 1))
    for j in range(y_scr.shape[0]):
        for par in range(2):
            y_scr[j, pl.ds(par, half, stride=2), :] = y[par * half:(par + 1) * half, j * LANES:(j + 1) * LANES]
    y = jnp.concatenate([y_scr[j] for j in range(y_scr.shape[0])], axis=1)
    yb_ref[0] = _ssd_gate_norm(y, z_ref[0], p).astype(yb_ref.dtype)

    @pl.when(c == pl.num_programs(1) - 1)
    def _():
        sout_ref[0] = s_scr[...]
        bufout_ref[0] = x[q - (CONV_W - 1):q]


def _ssd_prompt(proj3, sp, col_xbc, col_z, col_dt):
    b, l, _ = proj3.shape
    cdim = sp["ssd_conv_w"].shape[1]
    d_inner = sp["ssd_norm_g"].shape[1]
    gw = d_inner // SSD_GROUPS
    q = SSD_CHUNK if l % SSD_CHUNK == 0 else l
    assert q % (2 * SUBLANES) == 0
    return pl.pallas_call(
        _ssd_prompt_body,
        grid=(b, l // q),
        in_specs=[pl.BlockSpec((1, q, cdim), lambda i, c: (i, c, col_xbc // cdim)),
                  pl.BlockSpec((1, q, d_inner), lambda i, c: (i, c, col_z // d_inner)),
                  pl.BlockSpec((1, q, LANES), lambda i, c: (i, c, col_dt // LANES)),
                  _const_spec((CONV_W, cdim)), _const_spec((1, cdim)), _const_spec((1, LANES)),
                  _const_spec((1, LANES)), _const_spec((1, d_inner)), _const_spec((1, d_inner)),
                  _const_spec((2 * LANES, d_inner))],
        out_specs=[pl.BlockSpec((1, q, d_inner), lambda i, c: (i, c, 0)),
                   pl.BlockSpec((1, SSD_GROUPS, gw, SSD_STATE), lambda i, c: (i, 0, 0, 0)),
                   pl.BlockSpec((1, CONV_W - 1, cdim), lambda i, c: (i, 0, 0))],
        out_shape=[jax.ShapeDtypeStruct((b, l, d_inner), BF16),
                   jax.ShapeDtypeStruct((b, SSD_GROUPS, gw, SSD_STATE), F32),
                   jax.ShapeDtypeStruct((b, CONV_W - 1, cdim), F32)],
        scratch_shapes=[pltpu.VMEM((cdim // LANES, q + SUBLANES, LANES), F32),
                        pltpu.VMEM((q, LANES), F32),
                        pltpu.VMEM((d_inner // LANES, q, LANES), F32),
                        pltpu.VMEM((SSD_GROUPS, gw, SSD_STATE), F32)],
        compiler_params=_cparams(("parallel", "arbitrary")),
        name="ssd_prompt",
    )(proj3, proj3, proj3, sp["ssd_conv_w"], sp["ssd_conv_b"], sp["dt_bias"], sp["A"], sp["D"], sp["ssd_norm_g"],
      sp["head_expand"])


def _ssd_step_body(xbc_ref, z_ref, dt_ref, buf_ref, s_ref, cw_ref, cb_ref, dtb_ref, a_ref, d_ref, ng_ref, e2_ref,
                   yb_ref, sout_ref, bufout_ref, *, steps, nb):
    p = _ssd_params(cw_ref, cb_ref, dtb_ref, a_ref, d_ref, ng_ref)
    rps = SUBLANES
    q = nb * rps
    x_all = xbc_ref[...].astype(F32)
    z_all = z_ref[...].astype(F32)
    dt_all = _softplus(dt_ref[...].astype(F32) + p["dt_bias"])
    cdim = x_all.shape[1]

    def padded(v, j):
        return jnp.concatenate([v[j * steps:(j + 1) * steps], jnp.zeros((rps - steps, v.shape[1]), v.dtype)], axis=0)

    xcs = []
    for j in range(nb):
        xpad = jnp.concatenate([jnp.zeros((SUBLANES - (CONV_W - 1), cdim), F32), buf_ref[j], padded(x_all, j)],
                               axis=0)
        xcs.append(_ssd_conv(xpad, rps, p["cw"], p["cb"]))
        bufout_ref[j] = xpad[SUBLANES + steps - (CONV_W - 1):SUBLANES + steps]
    xc = jnp.concatenate(xcs, axis=0)
    dt = jnp.concatenate([padded(dt_all, j) for j in range(nb)], axis=0)
    z = jnp.concatenate([padded(z_all, j) for j in range(nb)], axis=0)

    def s_set(b, g, v):
        sout_ref[b, g] = v

    t_col = lax.broadcasted_iota(jnp.int32, (q, 1), 0) % rps
    t_row = lax.broadcasted_iota(jnp.int32, (1, q), 1) % rps
    y = _ssd_chunk(xc, dt, p, e2_ref[...], lambda b, g: s_ref[b, g], s_set, t_col, t_row, n_seg=nb, n_valid=steps)
    y = _ssd_gate_norm(y, z, p)
    for j in range(nb):
        yb_ref[j * steps:(j + 1) * steps, :] = y[j * rps:j * rps + steps].astype(yb_ref.dtype)


def _ssd_step(proj, buf, s0, sp, steps, nb, col_xbc, col_z, col_dt):
    bsz = s0.shape[0]
    cdim = sp["ssd_conv_w"].shape[1]
    d_inner = sp["ssd_norm_g"].shape[1]
    gw = d_inner // SSD_GROUPS
    rows = nb * steps
    body = functools.partial(_ssd_step_body, steps=steps, nb=nb)
    return pl.pallas_call(
        body,
        grid=(bsz // nb,),
        in_specs=[pl.BlockSpec((rows, cdim), lambda i: (i, col_xbc // cdim)),
                  pl.BlockSpec((rows, d_inner), lambda i: (i, col_z // d_inner)),
                  pl.BlockSpec((rows, LANES), lambda i: (i, col_dt // LANES)),
                  pl.BlockSpec((None, nb, CONV_W - 1, cdim), lambda i: (0, i, 0, 0)),
                  pl.BlockSpec((nb, SSD_GROUPS, gw, SSD_STATE), lambda i: (i, 0, 0, 0)),
                  _const_spec((CONV_W, cdim)), _const_spec((1, cdim)), _const_spec((1, LANES)),
                  _const_spec((1, LANES)), _const_spec((1, d_inner)), _const_spec((1, d_inner)),
                  _const_spec((2 * LANES, d_inner))],
        out_specs=[pl.BlockSpec((rows, d_inner), lambda i: (i, 0)),
                   pl.BlockSpec((nb, SSD_GROUPS, gw, SSD_STATE), lambda i: (i, 0, 0, 0)),
                   pl.BlockSpec((None, nb, CONV_W - 1, cdim), lambda i: (0, i, 0, 0))],
        out_shape=[jax.ShapeDtypeStruct((bsz * steps, d_inner), BF16),
                   jax.ShapeDtypeStruct(s0.shape, F32),
                   jax.ShapeDtypeStruct(buf.shape, F32)],
        compiler_params=_cparams(("parallel",)),
        name="ssd_step",
    )(proj, proj, proj, buf, s0, sp["ssd_conv_w"], sp["ssd_conv_b"], sp["dt_bias"], sp["A"], sp["D"],
      sp["ssd_norm_g"], sp["head_expand"])


def _router(t, wr, br):
    t_hi = t.astype(BF16)
    t_lo = (t - t_hi.astype(F32)).astype(BF16)
    both = _dot(jnp.concatenate([t_hi, t_lo], axis=1), wr)
    logits = both[:, :LANES] + both[:, LANES:] + br
    lane = lax.broadcasted_iota(jnp.int32, logits.shape, 1)
    neg = -jnp.inf
    gl = jnp.where(lane < N_EGROUPS, logits, neg)
    gmax = jnp.max(gl, axis=-1, keepdims=True)
    g_idx = jnp.min(jnp.where(gl == gmax, lane, LANES), axis=-1, keepdims=True)
    g_w = 1.0 / jnp.sum(jnp.exp(gl - gmax), axis=-1, keepdims=True)
    in_grp = jnp.logical_and(jnp.logical_and(lane >= ROUTER_LANE0, lane < ROUTER_LANE0 + N_EXPERTS),
                             ((lane - ROUTER_LANE0) >> 2) == g_idx)
    el = jnp.where(in_grp, logits, neg)
    pe = jnp.exp(el - jnp.max(el, axis=-1, keepdims=True))
    pe = pe / jnp.sum(pe, axis=-1, keepdims=True)
    cand = jnp.where(in_grp, pe, -1.0)
    v1 = jnp.max(cand, axis=-1, keepdims=True)
    i1 = jnp.min(jnp.where(cand == v1, lane, LANES), axis=-1, keepdims=True)
    cand2 = jnp.where(lane == i1, -1.0, cand)
    v2 = jnp.max(cand2, axis=-1, keepdims=True)
    i2 = jnp.min(jnp.where(jnp.logical_and(cand2 == v2, in_grp), lane, LANES), axis=-1, keepdims=True)
    den = v1 + v2
    return lane, i1, i2, g_w * v1 / den, g_w * v2 / den


def _rows_to_tiles(ref, val):
    n, d = val.shape
    for k in range(d // LANES):
        ref[pl.ds(k, n, stride=d // LANES), :] = val[:, k * LANES:(k + 1) * LANES]


def _tiles_to_rows(ref, n, d, start=0):
    return jnp.concatenate([ref[pl.ds(start + k, n, stride=d // LANES), :] for k in range(d // LANES)], axis=1)


def _merge_body(x_ref, ya_ref, yb_ref, ga_ref, gb_ref, wl_ref, ws_ref, wo_ref, gf_ref, wr_ref, br_ref,
                x1_ref, t_ref, rt_ref, rtt_ref, cnt_ref, base_scr):
    step = pl.program_id(0)

    @pl.when(step == 0)
    def _():
        base_scr[...] = jnp.zeros_like(base_scr)

    a = _dot(ya_ref[...], wl_ref[...])
    b = _dot(yb_ref[...], ws_ref[...])
    merged = jax.nn.sigmoid(ga_ref[...].astype(F32)) * a + jax.nn.sigmoid(gb_ref[...].astype(F32)) * b
    x1 = x_ref[...] + _dot(merged.astype(BF16), wo_ref[...])
    x1_ref[...] = x1
    t = _rms(x1, gf_ref[...])
    _rows_to_tiles(t_ref, t)
    lane, i1, i2, wg1, wg2 = _router(t, wr_ref[...], br_ref[...])

    tm = t.shape[0]
    onehot = jnp.where(jnp.logical_or(lane == i1, lane == i2), 1.0, 0.0).astype(BF16)
    tri = (lax.broadcasted_iota(jnp.int32, (tm, tm), 1) <= lax.broadcasted_iota(jnp.int32, (tm, tm), 0)).astype(BF16)
    cum = _dot(tri, onehot) + base_scr[...]
    r1 = jnp.sum(jnp.where(lane == i1, cum, 0.0), axis=-1, keepdims=True) - 1.0
    r2 = jnp.sum(jnp.where(lane == i2, cum, 0.0), axis=-1, keepdims=True) - 1.0
    cols = (wg1, wg2, r1, r2, (i1 - ROUTER_LANE0).astype(F32), (i2 - ROUTER_LANE0).astype(F32))
    rt = jnp.zeros(cum.shape, F32)
    for k, c in enumerate(cols):
        rt = jnp.where(lane == k, c, rt)
    rt_ref[...] = rt
    rtt_ref[...] = rt.T[0:SUBLANES, :]
    base_scr[...] = cum[tm - 1:tm, :]
    cnt_ref[...] = cum[tm - 1:tm, :]


def _merge(x, ya, yb, proj, mw, tm, col_ga, col_gb):
    t, d = x.shape
    d_inner = yb.shape[1]
    return pl.pallas_call(
        _merge_body,
        grid=(t // tm,),
        in_specs=[pl.BlockSpec((tm, d), lambda i: (i, 0)),
                  pl.BlockSpec((tm, d), lambda i: (i, 0)),
                  pl.BlockSpec((tm, d_inner), lambda i: (i, 0)),
                  pl.BlockSpec((tm, d), lambda i: (i, col_ga // d)),
                  pl.BlockSpec((tm, d), lambda i: (i, col_gb // d)),
                  _const_spec((d, d)), _const_spec((d_inner, d)), _const_spec((d, d)),
                  _const_spec((1, d)), _const_spec((2 * d, 2 * LANES)), _const_spec((1, LANES))],
        out_specs=[pl.BlockSpec((tm, d), lambda i: (i, 0)),
                   pl.BlockSpec((tm * d // LANES, LANES), lambda i: (i, 0)),
                   pl.BlockSpec((tm, LANES), lambda i: (i, 0)),
                   pl.BlockSpec((SUBLANES, tm), lambda i: (0, i)),
                   pl.BlockSpec((1, LANES), lambda i: (0, 0))],
        out_shape=[jax.ShapeDtypeStruct((t, d), F32),
                   jax.ShapeDtypeStruct((t * d // LANES, LANES), F32),
                   jax.ShapeDtypeStruct((t, LANES), F32),
                   jax.ShapeDtypeStruct((SUBLANES, t), F32),
                   jax.ShapeDtypeStruct((1, LANES), F32)],
        scratch_shapes=[pltpu.VMEM((1, LANES), F32)],
        compiler_params=_cparams(("arbitrary",)),
        name="merge_router",
    )(x, ya, yb, proj, proj, mw["w_br_lru"], mw["w_br_ssd"], mw["w_out"], mw["g_ffn"], mw["w_router"],
      mw["b_router"])


def _dispatch_body(dest_ref, zb_ref, t_ref, o_ref, zero_scr, sem, zsem, *, tm, nk, tmg, n_tok):
    step = pl.program_id(0)

    @pl.when(step == 0)
    def _():
        zero_scr[...] = jnp.zeros_like(zero_scr)
        blk = tmg * nk

        def zcopy(j):
            return pltpu.make_async_copy(zero_scr, o_ref.at[pl.ds(pl.multiple_of(zb_ref[j] * blk, blk), blk)], zsem)

        for j in range(zb_ref.shape[0]):
            pl.when(zb_ref[j] >= 0)(lambda j=j: zcopy(j).start())
        for j in range(zb_ref.shape[0]):
            pl.when(zb_ref[j] >= 0)(lambda j=j: zcopy(j).wait())

    def issue(r, carry):
        src = t_ref.at[pl.ds(pl.multiple_of(r * nk, nk), nk)]
        for k in range(2):
            row = dest_ref[k * n_tok + step * tm + r]
            pltpu.make_async_copy(src, o_ref.at[pl.ds(pl.multiple_of(row * nk, nk), nk)], sem).start(priority=k)
        return carry

    lax.fori_loop(0, tm, issue, 0, unroll=8)
    for k in range(2):
        pltpu.make_async_copy(t_ref, o_ref.at[pl.ds(0, tm * nk)], sem).wait()


def _dispatch(dest, zero_blocks, t_tiles, n_tok, tm, n_rows, tmg):
    nk = t_tiles.shape[0] // n_tok
    return pl.pallas_call(
        functools.partial(_dispatch_body, tm=tm, nk=nk, tmg=tmg, n_tok=n_tok),
        grid_spec=pltpu.PrefetchScalarGridSpec(
            num_scalar_prefetch=2,
            grid=(n_tok // tm,),
            in_specs=[pl.BlockSpec((tm * nk, LANES), lambda i, *_: (i, 0))],
            out_specs=pl.BlockSpec(memory_space=pl.ANY),
            scratch_shapes=[pltpu.VMEM((tmg * nk, LANES), F32), pltpu.SemaphoreType.DMA(()),
                            pltpu.SemaphoreType.DMA(())]),
        out_shape=jax.ShapeDtypeStruct((n_rows * nk, LANES), F32),
        compiler_params=_cparams(("arbitrary",)),
        name="moe_dispatch",
    )(dest, zero_blocks, t_tiles)


def _expert_body(te_ref, nt_ref, x_ref, w1_ref, w3_ref, w2_ref, y_ref, w1_scr, w3_scr, w2_scr, *, tmg):
    i = pl.program_id(0)
    real = i < nt_ref[0]
    d = w1_scr.shape[0]

    @pl.when(jnp.logical_or(i == 0, te_ref[i] != te_ref[jnp.maximum(i - 1, 0)]))
    def _():
        w1_scr[...] = w1_ref[0].astype(BF16)
        w3_scr[...] = w3_ref[0].astype(BF16)
        w2_scr[...] = w2_ref[0].astype(BF16)

    @pl.when(real)
    def _():
        x = _tiles_to_rows(x_ref, tmg, d).astype(BF16)
        h1 = _dot(x, w1_scr[...])
        h3 = _dot(x, w3_scr[...])
        _rows_to_tiles(y_ref, _dot((h1 * jax.nn.sigmoid(h1) * h3).astype(BF16), w2_scr[...]))

    @pl.when(jnp.logical_not(real))
    def _():
        y_ref[...] = jnp.zeros_like(y_ref)


def _experts(tile_expert, n_tiles, xs_tiles, w1, w3, w2, tmg):
    _, d, dff = w1.shape
    blk = tmg * d // LANES
    row_spec = pl.BlockSpec((blk, LANES), lambda i, te, nt: (i, 0))
    return pl.pallas_call(
        functools.partial(_expert_body, tmg=tmg),
        grid_spec=pltpu.PrefetchScalarGridSpec(
            num_scalar_prefetch=2,
            grid=(xs_tiles.shape[0] // blk,),
            in_specs=[row_spec,
                      pl.BlockSpec((1, d, dff), lambda i, te, nt: (te[i], 0, 0)),
                      pl.BlockSpec((1, d, dff), lambda i, te, nt: (te[i], 0, 0)),
                      pl.BlockSpec((1, dff, d), lambda i, te, nt: (te[i], 0, 0))],
            out_specs=row_spec,
            scratch_shapes=[pltpu.VMEM((d, dff), BF16), pltpu.VMEM((d, dff), BF16), pltpu.VMEM((dff, d), BF16)]),
        out_shape=jax.ShapeDtypeStruct(xs_tiles.shape, F32),
        compiler_params=_cparams(("arbitrary",)),
        name="moe_experts",
    )(tile_expert, n_tiles, xs_tiles, w1, w3, w2)


def _ple_body(dest_ref, x_ref, rt_ref, p_ref, wp_ref, gp_ref, gg_ref, wg_ref, gfin_ref, y_hbm, o_ref, gbuf, sem,
              *, n_tok):
    step = pl.program_id(0)
    tm, d = x_ref.shape
    nk = d // LANES

    def gather(tile, slot):
        def issue(r, carry):
            tok = tile * tm + r
            for k in range(2):
                row = dest_ref[k * n_tok + tok]
                pltpu.make_async_copy(y_hbm.at[pl.ds(pl.multiple_of(row * nk, nk), nk)],
                                      gbuf.at[slot, pl.ds(pl.multiple_of((k * tm + r) * nk, nk), nk)],
                                      sem.at[slot]).start(priority=k)
            return carry

        lax.fori_loop(0, tm, issue, 0, unroll=8)

    @pl.when(step == 0)
    def _():
        gather(0, 0)

    @pl.when(step + 1 < pl.num_programs(0))
    def _():
        gather(step + 1, (step + 1) % 2)

    slot = step % 2
    pltpu.make_async_copy(y_hbm.at[pl.ds(0, 2 * tm * nk)], gbuf.at[slot], sem.at[slot]).wait()
    rt = rt_ref[...]
    rows = gbuf.at[slot]
    x = (x_ref[...] + rt[:, 0:1] * _tiles_to_rows(rows, tm, d)
         + rt[:, 1:2] * _tiles_to_rows(rows, tm, d, start=tm * nk))
    e = _rms(_dot(p_ref[...].astype(BF16), wp_ref[...]), gp_ref[...])
    gate = jax.nn.sigmoid(_dot(_rms(x, gg_ref[...]).astype(BF16), wg_ref[...]))
    o_ref[...] = _rms(x + gate * e, gfin_ref[...])


def _ple(dest, x1, rt, p, y_sorted, pw, tm):
    n, d = x1.shape
    dp = p.shape[1]
    const = lambda shape: pl.BlockSpec(shape, lambda i, *_: (0,) * len(shape))
    return pl.pallas_call(
        functools.partial(_ple_body, n_tok=n),
        grid_spec=pltpu.PrefetchScalarGridSpec(
            num_scalar_prefetch=1,
            grid=(n // tm,),
            in_specs=[pl.BlockSpec((tm, d), lambda i, *_: (i, 0)),
                      pl.BlockSpec((tm, LANES), lambda i, *_: (i, 0)),
                      pl.BlockSpec((tm, dp), lambda i, *_: (i, 0)),
                      const((dp, d)), const((1, d)), const((1, d)), const((d, d)), const((1, d)),
                      pl.BlockSpec(memory_space=pl.ANY)],
            out_specs=pl.BlockSpec((tm, d), lambda i, *_: (i, 0)),
            scratch_shapes=[pltpu.VMEM((2, 2 * tm * d // LANES, LANES), F32), pltpu.SemaphoreType.DMA((2,))]),
        out_shape=jax.ShapeDtypeStruct((n, d), F32),
        compiler_params=_cparams(("arbitrary",)),
        name="combine_ple_final",
    )(dest, x1, rt, p, pw["w_ple_proj"], pw["g_ple"], pw["g_ple_gate"], pw["w_ple_gate"], pw["g_final"], y_sorted)


def _pick_tile(n, pref):
    t = min(n, pref)
    while n % t:
        t //= 2
    return t


def _moe_row_tile(n):
    return 512 if 2 * n // N_EXPERTS >= 1024 else 128
MOE_DISPATCH_TILE = 2048
LRU_SLICE_ROWS = 128
LRU_PIECES = 4


def _split_router(w):
    hi = w.astype(BF16)
    lo = (w - hi.astype(F32)).astype(BF16)
    return jnp.concatenate([jnp.concatenate([hi, lo], axis=1),
                            jnp.concatenate([hi, jnp.zeros_like(hi)], axis=1)], axis=0)


def _token_tail(x1, t_tiles, rt, rtt, cnt, p, lw, tm):
    n = x1.shape[0]
    tmg = _moe_row_tile(n)
    n_blocks = pl.cdiv(2 * n, tmg) + N_EXPERTS
    counts = cnt[0, ROUTER_LANE0:ROUTER_LANE0 + N_EXPERTS].astype(jnp.int32)
    tiles = (counts + tmg - 1) // tmg
    ends = jnp.cumsum(tiles)
    n_tiles = ends[-1]
    e_idx = rtt[4:6].astype(jnp.int32)
    first_row = (ends - tiles) * tmg
    dest = rtt[2:4].astype(jnp.int32) + sum(jnp.where(e_idx == e, first_row[e], 0) for e in range(N_EXPERTS))
    dest = dest.reshape(2 * n)
    blk = jnp.arange(n_blocks, dtype=jnp.int32)
    tile_expert = jnp.sum((jnp.minimum(blk, n_tiles - 1)[:, None] >= ends[None, :]).astype(jnp.int32), axis=1)
    tail = n_tiles + blk[:N_EXPERTS]
    zero_blocks = jnp.concatenate([jnp.where(tiles > 0, ends - 1, -1),
                                   jnp.where(tail < n_blocks, tail, -1)]).astype(jnp.int32)

    sorted_t = _dispatch(dest, zero_blocks, t_tiles, n, _pick_tile(n, MOE_DISPATCH_TILE), n_blocks * tmg, tmg)
    y_sorted = _experts(tile_expert, n_tiles.reshape(1), sorted_t, lw["w1"], lw["w3"], lw["w2"], tmg)
    return _ple(dest, x1, rt, p, y_sorted, lw, tm)


def kernel(x_prompt, x_sample, state_lru_h, state_lru_conv, state_ssd, state_ssd_conv, p_prompt, p_sample, g_mix, w_in, lru_conv_w, lru_conv_b, lru_wa, lru_ba, lru_wx, lru_bx, lru_lambda, ssd_conv_w, ssd_conv_b, ssd_dt_bias, ssd_A_log, ssd_D, ssd_norm_g, w_br_lru, w_br_ssd, w_out, g_ffn, w_router_g, b_router_g, w_router_e, b_router_e, w1, w3, w2, w_ple_proj, g_ple, g_ple_gate, w_ple_gate, g_final):
    depth = w_in.shape[0]
    assert depth == 1, "one decoder layer per call"
    bp, lp, d = x_prompt.shape
    bs, ls, _ = x_sample.shape
    w_lru = state_lru_h.shape[-1]
    heads, hdim, nstate = state_ssd.shape[2:]
    d_inner = heads * hdim
    cdim = state_ssd_conv.shape[-1]
    assert hdim == SSD_HEADDIM and nstate == SSD_STATE and heads <= LANES and ls < SUBLANES
    gw = d_inner // SSD_GROUPS

    o_dt = 2 * w_lru + d_inner + cdim
    n_proj = o_dt + 2 * d + LANES
    tn = n_proj // 9 if n_proj % (9 * LANES) == 0 else LANES
    n_head = o_dt // tn
    w_head = w_in.astype(BF16)
    wi = w_head[0]
    w_tail = jnp.concatenate([wi[:, n_head * tn:o_dt], wi[:, o_dt + heads:], wi[:, o_dt:o_dt + heads],
                              jnp.zeros((d, LANES - heads), BF16)], axis=1)
    assert n_head >= 1 and w_tail.shape[1] == n_proj - n_head * tn and w_tail.shape[1] % tn == 0
    col_z, col_xbc = 2 * w_lru, 2 * w_lru + d_inner
    col_ga, col_gb, col_dt = o_dt, o_dt + d, o_dt + 2 * d
    row = lambda v: v.reshape(1, -1).astype(F32)
    pad_heads = lambda v: jnp.pad(v.astype(F32), (0, LANES - heads)).reshape(1, LANES)
    lw = {
        "lru_conv_w": lru_conv_w[0], "lru_conv_b": row(lru_conv_b[0]),
        "wax": jnp.concatenate([lru_wa[0], lru_wx[0]], axis=-1).astype(BF16),
        "lru_ba": row(lru_ba[0]), "lru_bx": row(lru_bx[0]), "lru_lambda": row(lru_lambda[0]),
        "ssd_conv_w": ssd_conv_w[0], "ssd_conv_b": row(ssd_conv_b[0]),
        "dt_bias": pad_heads(ssd_dt_bias[0]), "A": pad_heads(-jnp.exp(ssd_A_log[0].astype(F32))),
        "D": row(jnp.repeat(ssd_D[0], hdim)), "ssd_norm_g": row(ssd_norm_g[0]),
        "head_expand": jnp.tile(jnp.arange(LANES)[:, None] == jnp.arange(d_inner)[None, :] // hdim, (2, 1)).astype(BF16),
        "w_br_lru": w_br_lru[0].astype(BF16), "w_br_ssd": w_br_ssd[0].astype(BF16), "w_out": w_out[0].astype(BF16),
        "g_ffn": row(g_ffn[0]),
        "w_router": _split_router(jnp.concatenate([w_router_g[0], w_router_e[0],
                                                   jnp.zeros((d, LANES - N_EGROUPS - N_EXPERTS), F32)], axis=1)),
        "b_router": jnp.concatenate([b_router_g[0], b_router_e[0],
                                     jnp.zeros((LANES - N_EGROUPS - N_EXPERTS,), F32)]).reshape(1, LANES),
        "w1": w1[0], "w3": w3[0], "w2": w2[0],
        "w_ple_proj": w_ple_proj[0].astype(BF16), "g_ple": row(g_ple[0]), "g_ple_gate": row(g_ple_gate[0]),
        "w_ple_gate": w_ple_gate[0].astype(BF16), "g_final": row(g_final),
    }
    g_mix_r = row(g_mix[0])

    tp = bp * lp
    xp = x_prompt.reshape(tp, d)
    tm_p = _pick_tile(lp, 1024)
    proj_p, ya_p, hl_p, lbuf_p = _inproj_lru(xp, g_mix_r, w_head, w_tail, lw, lp, tm_p, tn, n_head,
                                             _pick_tile(tm_p, LRU_SLICE_ROWS))
    proj_p3 = proj_p.reshape(bp, lp, n_proj)
    yb_p, s_p, sbuf_p = _ssd_prompt(proj_p3, lw, col_xbc, col_z, col_dt)
    tm_tail_p = _pick_tile(tp, 512)
    x1_p, t_p, rt_p, rtt_p, cnt_p = _merge(xp, ya_p.reshape(tp, w_lru), yb_p.reshape(tp, d_inner), proj_p, lw,
                                           tm_tail_p, col_ga, col_gb)
    y_p = _token_tail(x1_p, t_p, rt_p, rtt_p, cnt_p, p_prompt[0].reshape(tp, -1), lw, tm_tail_p)

    ts = bs * ls
    xs = x_sample.reshape(ts, d)
    tm_s = _pick_tile(ts, 512)
    proj_s = _inproj(xs, g_mix_r, w_head, w_tail, tm_s, tn, n_head)
    to_tmajor = lambda v, n: v.reshape(bs, n, -1).transpose(1, 0, 2).reshape(n * bs, -1)
    from_tmajor = lambda v, n: v.reshape(n, bs, -1).transpose(1, 0, 2)
    ya_t, hl_s, lbuf_t = _lru_step(to_tmajor(proj_s[:, :w_lru], ls), to_tmajor(proj_s[:, w_lru:2 * w_lru], ls),
                                   to_tmajor(state_lru_conv[0], CONV_W - 1), state_lru_h[0], lw, ls)
    ya_s = from_tmajor(ya_t, ls).reshape(ts, w_lru)
    lbuf_s = from_tmajor(lbuf_t, CONV_W - 1)
    yb_s, s_s, sbuf_s = _ssd_step(proj_s, state_ssd_conv, state_ssd[0].reshape(bs, SSD_GROUPS, gw, nstate), lw,
                                  ls, _pick_tile(bs, 8), col_xbc, col_z, col_dt)
    x1_s, t_s, rt_s, rtt_s, cnt_s = _merge(xs, ya_s, yb_s, proj_s, lw, tm_s, col_ga, col_gb)
    y_s = _token_tail(x1_s, t_s, rt_s, rtt_s, cnt_s, p_sample[0].reshape(ts, -1), lw, tm_s)

    return (y_p.reshape(bp, lp, d), y_s.reshape(bs, ls, d),
            hl_p.reshape(1, bp, w_lru), lbuf_p[None],
            s_p.reshape(1, bp, heads, hdim, nstate), sbuf_p[None],
            hl_s[None], lbuf_s[None],
            s_s.reshape(1, bs, heads, hdim, nstate), sbuf_s)
```

```python
import functools

import jax
import jax.numpy as jnp
from jax import lax
from jax.experimental import pallas as pl
from jax.experimental.pallas import tpu as pltpu

F32 = jnp.float32
BF16 = jnp.bfloat16

EPS = 1e-6
CONV_W = 4
LRU_BLOCKS = 8
LRU_C = 8.0
SSD_HEADDIM = 64
SSD_GROUPS = 8
SSD_STATE = 128
SSD_CHUNK = 128
N_EGROUPS = 4
EXP_PER_GROUP = 4
N_EXPERTS = N_EGROUPS * EXP_PER_GROUP

LANES = 128
SUBLANES = 8
VMEM_LIMIT = 52 * 1024 * 1024
ROUTER_LANE0 = N_EGROUPS


def _cparams(sem):
    return pltpu.CompilerParams(dimension_semantics=sem, vmem_limit_bytes=VMEM_LIMIT)


def _dot(a, b):
    return jnp.dot(a, b, preferred_element_type=F32)


def _dot_nt(a, b):
    return lax.dot_general(a, b, (((1,), (1,)), ((), ())), preferred_element_type=F32)


def _dot_tn(a, b):
    return lax.dot_general(a, b, (((0,), (0,)), ((), ())), preferred_element_type=F32)


def _dot_f32(a, b):
    return jnp.dot(a, b, precision=lax.Precision.HIGHEST, preferred_element_type=F32)


def _rms(x, g):
    return x * lax.rsqrt(jnp.mean(x * x, axis=-1, keepdims=True) + EPS) * g


def _const_spec(shape):
    nd = len(shape)
    return pl.BlockSpec(shape, lambda *_: (0,) * nd)


def _inproj_body(x_ref, g_ref, wh_ref, wt_ref, o_ref, h_scr, *, n_head):
    j = pl.program_id(1)

    @pl.when(j == 0)
    def _():
        h_scr[...] = _rms(x_ref[...], g_ref[...]).astype(BF16)

    @pl.when(j < n_head)
    def _():
        o_ref[...] = _dot(h_scr[...], wh_ref[0]).astype(o_ref.dtype)

    @pl.when(j >= n_head)
    def _():
        o_ref[...] = _dot(h_scr[...], wt_ref[...]).astype(o_ref.dtype)


def _inproj(x, g, w_head, w_tail, tm, tn, n_head):
    t, d = x.shape
    n_tail = w_tail.shape[1] // tn
    return pl.pallas_call(
        functools.partial(_inproj_body, n_head=n_head),
        grid=(t // tm, n_head + n_tail),
        in_specs=[pl.BlockSpec((tm, d), lambda i, j: (i, 0)),
                  pl.BlockSpec((1, d), lambda i, j: (0, 0)),
                  pl.BlockSpec((1, d, tn), lambda i, j: (0, 0, jnp.minimum(j, n_head - 1))),
                  pl.BlockSpec((d, tn), lambda i, j: (0, jnp.maximum(j - n_head, 0)))],
        out_specs=pl.BlockSpec((tm, tn), lambda i, j: (i, j)),
        out_shape=jax.ShapeDtypeStruct((t, (n_head + n_tail) * tn), BF16),
        scratch_shapes=[pltpu.VMEM((tm, d), BF16)],
        compiler_params=_cparams(("parallel", "arbitrary")),
        name="inproj",
    )(x, g, w_head, w_tail)


def _lru_gate_matmuls(u, wax_ref):
    bw = u.shape[1] // LRU_BLOCKS
    r_parts, i_parts = [], []
    for n in range(LRU_BLOCKS):
        ri = _dot(u[:, n * bw:(n + 1) * bw].astype(BF16), wax_ref[n])
        r_parts.append(ri[:, :bw])
        i_parts.append(ri[:, bw:])
    return jnp.concatenate(r_parts, axis=1), jnp.concatenate(i_parts, axis=1)


def _lru_gate_values(r_pre, i_pre, ba, bx, lam):
    r = jax.nn.sigmoid(r_pre + ba)
    i = jax.nn.sigmoid(i_pre + bx)
    log_a = LRU_C * r * jax.nn.log_sigmoid(lam)
    a = jnp.exp(log_a)
    m2 = -jnp.tanh(log_a) * (a * a + 1.0)
    mult = jnp.where(m2 > 0.0, m2 * lax.rsqrt(m2), 0.0)
    return a, i, mult


def _lru_gates(u, wax_ref, ba, bx, lam):
    return _lru_gate_values(*_lru_gate_matmuls(u, wax_ref), ba, bx, lam)


def _lru_conv(x, halo, seq_start, cw, cb):
    tt = x.shape[0]
    xpad = jnp.concatenate([jnp.where(seq_start, 0.0, halo), x], axis=0)
    base = SUBLANES - (CONV_W - 1)
    return cb + sum(xpad[base + k:base + k + tt] * cw[k:k + 1] for k in range(CONV_W))


def _lru_scan(u, r_pre, i_pre, gate, carry, seq_start, ba, bx, lam):
    tt, width = u.shape
    carry = jnp.where(seq_start, 0.0, carry)
    a, i, mult = _lru_gate_values(r_pre, i_pre, ba, bx, lam)
    first = jnp.logical_and(lax.broadcasted_iota(jnp.int32, a.shape, 0) == 0, seq_start)
    mult = jnp.where(first, 1.0, mult)
    a = jnp.where(first, 0.0, a)
    v = u * i * mult

    a = a.reshape(tt // SUBLANES, SUBLANES, width)
    v = v.reshape(tt // SUBLANES, SUBLANES, width)
    sub = lax.broadcasted_iota(jnp.int32, a.shape, 1)
    s = 1
    while s < SUBLANES:
        keep = sub >= s
        v = jnp.where(keep, a * pltpu.roll(v, s, axis=1) + v, v)
        a = jnp.where(keep, a * pltpu.roll(a, s, axis=1), a)
        s *= 2
    groups = []
    for g in range(tt // SUBLANES):
        hg = a[g] * carry + v[g]
        carry = hg[SUBLANES - 1:SUBLANES]
        groups.append(hg)
    h = jnp.concatenate(groups, axis=0)
    return h * jax.nn.gelu(gate), carry


def _inproj_lru_body(x_ref, g_ref, wh_ref, wt_ref, cw_ref, cb_ref, wax_ref, ba_ref, bx_ref, lam_ref,
                     o_ref, ya_ref, hlast_ref, bufout_ref, h_scr, lru_new, lru_cur, halo_scr, carry_scr,
                     *, n_head, n_j, n_tiles, tiles_per_seq, sub_rows):
    i = pl.program_id(0)
    j = pl.program_id(1)
    tm, tn = o_ref.shape
    w = ya_ref.shape[1]
    n_sub = tm // sub_rows
    prev_tile = jnp.maximum(i - 1, 0)

    @pl.when(jnp.logical_and(i == 0, j == 0))
    def _():
        lru_cur[...] = jnp.zeros_like(lru_cur)
        halo_scr[...] = jnp.zeros_like(halo_scr)
        carry_scr[...] = jnp.zeros_like(carry_scr)

    n_piece = LRU_PIECES
    piece_rows = sub_rows // n_piece
    mx_w = 2 * LANES
    col_cuts = [min(tn, mx_w * (q * (tn // mx_w) // n_piece)) for q in range(n_piece)] + [tn]

    def lru_piece(q, state):
        r0 = pl.multiple_of(j * sub_rows, sub_rows) + q * piece_rows
        seq_start = jnp.logical_and(jnp.logical_and(prev_tile % tiles_per_seq == 0, j == 0), q == 0)
        x = lru_cur[pl.ds(r0, piece_rows), 0:w].astype(F32)
        gate = lru_cur[pl.ds(r0, piece_rows), w:2 * w].astype(F32)
        halo, carry = state if state is not None else (halo_scr[...], carry_scr[...])
        u = _lru_conv(x, halo, seq_start, cw_ref[...], cb_ref[...])
        r_pre, i_pre = _lru_gate_matmuls(u, wax_ref)
        ya, carry = _lru_scan(u, r_pre, i_pre, gate, carry, seq_start, ba_ref[...], bx_ref[...], lam_ref[...])
        ya_ref[pl.ds(r0, piece_rows), :] = ya.astype(ya_ref.dtype)
        halo = x[piece_rows - SUBLANES:piece_rows]
        if q == n_piece - 1:
            halo_scr[...] = halo
            carry_scr[...] = carry

            @pl.when(jnp.logical_and(prev_tile % tiles_per_seq == tiles_per_seq - 1, j == n_sub - 1))
            def _():
                hlast_ref[0] = carry
                bufout_ref[0] = x[piece_rows - (CONV_W - 1):piece_rows]
        return halo, carry

    def project_piece(q, from_head, lru_lo):
        c0, c1 = col_cuts[q], col_cuts[q + 1]
        if c0 == c1:
            return
        w_cols = wh_ref[0, :, c0:c1] if from_head else wt_ref[:, c0:c1]
        o = _dot(h_scr[...], w_cols).astype(o_ref.dtype)
        o_ref[:, c0:c1] = o
        if lru_lo is not None:
            keep = min(lru_lo + c1, 2 * w) - (lru_lo + c0)
            if keep > 0:
                lru_new[:, lru_lo + c0:lru_lo + c0 + keep] = o[:, 0:keep]

    def column_steps(lo, hi, from_head, prologue=None, lru_lo=None):
        for a, b, with_lru in ((lo, min(hi, n_sub), True), (max(lo, n_sub), hi, False)):
            if a < b:
                @pl.when(jnp.logical_and(i < n_tiles, jnp.logical_and(j >= a, j < b)))
                def _(with_lru=with_lru):
                    if prologue is not None:
                        prologue()
                    state = None
                    for q in range(n_piece):
                        project_piece(q, from_head, lru_lo)
                        if with_lru:
                            state = lru_piece(q, state)
                    if b == n_j:
                        @pl.when(j == n_j - 1)
                        def _():
                            lru_cur[...] = lru_new[...]

    def normalise():
        h_scr[...] = _rms(x_ref[...], g_ref[...]).astype(BF16)

    column_steps(0, 1, True, prologue=normalise, lru_lo=0)
    column_steps(1, 2, True, lru_lo=tn)
    column_steps(2, n_head, True)
    column_steps(n_head, n_j, False)

    @pl.when(jnp.logical_and(i == n_tiles, j < n_sub))
    def _():
        state = None
        for q in range(n_piece):
            state = lru_piece(q, state)


def _inproj_lru(x, g, w_head, w_tail, lw, seq_len, tm, tn, n_head, sub_rows):
    t, d = x.shape
    w = lw["lru_lambda"].shape[1]
    n_tail = w_tail.shape[1] // tn
    n_tiles, n_j = t // tm, n_head + n_tail
    tiles_per_seq = seq_len // tm
    assert seq_len % tm == 0 and tm % sub_rows == 0 and tm // sub_rows < n_j and n_head >= 2
    assert w <= tn and 2 * w <= 2 * tn and 2 * w > tn
    cur_tile = lambda i: jnp.minimum(i, n_tiles - 1)
    lru_tile = lambda i: jnp.maximum(i - 1, 0)
    body = functools.partial(_inproj_lru_body, n_head=n_head, n_j=n_j, n_tiles=n_tiles,
                             tiles_per_seq=tiles_per_seq, sub_rows=sub_rows)
    out_col = lambda i, j: jnp.where(i < n_tiles, j, n_j - 1)
    w_col = lambda i, j: jnp.where(i < n_tiles, j, n_j - 1)
    return pl.pallas_call(
        body,
        grid=(n_tiles + 1, n_j),
        in_specs=[pl.BlockSpec((tm, d), lambda i, j: (cur_tile(i), 0)),
                  pl.BlockSpec((1, d), lambda i, j: (0, 0)),
                  pl.BlockSpec((1, d, tn), lambda i, j: (0, 0, jnp.minimum(w_col(i, j), n_head - 1))),
                  pl.BlockSpec((d, tn), lambda i, j: (0, jnp.maximum(w_col(i, j) - n_head, 0))),
                  _const_spec((CONV_W, w)), _const_spec((1, w)),
                  _const_spec(lw["wax"].shape), _const_spec((1, w)), _const_spec((1, w)), _const_spec((1, w))],
        out_specs=[pl.BlockSpec((tm, tn), lambda i, j: (cur_tile(i), out_col(i, j))),
                   pl.BlockSpec((tm, w), lambda i, j: (lru_tile(i), 0)),
                   pl.BlockSpec((1, 1, w), lambda i, j: (lru_tile(i) // tiles_per_seq, 0, 0)),
                   pl.BlockSpec((1, CONV_W - 1, w), lambda i, j: (lru_tile(i) // tiles_per_seq, 0, 0))],
        out_shape=[jax.ShapeDtypeStruct((t, n_j * tn), BF16),
                   jax.ShapeDtypeStruct((t, w), BF16),
                   jax.ShapeDtypeStruct((t // seq_len, 1, w), F32),
                   jax.ShapeDtypeStruct((t // seq_len, CONV_W - 1, w), F32)],
        scratch_shapes=[pltpu.VMEM((tm, d), BF16), pltpu.VMEM((tm, 2 * w), BF16), pltpu.VMEM((tm, 2 * w), BF16),
                        pltpu.VMEM((SUBLANES, w), F32), pltpu.VMEM((1, w), F32)],
        compiler_params=_cparams(("arbitrary", "arbitrary")),
        name="inproj_lru",
    )(x, g, w_head, w_tail, lw["lru_conv_w"], lw["lru_conv_b"], lw["wax"], lw["lru_ba"], lw["lru_bx"],
      lw["lru_lambda"])


def _lru_step_body(xin_ref, gate_ref, buf_ref, h0_ref, cw_ref, cb_ref, wax_ref, ba_ref, bx_ref, lam_ref,
                   ya_ref, hlast_ref, bufout_ref, *, steps):
    bsz = h0_ref.shape[0]
    n = steps * bsz
    x = xin_ref[...].astype(F32)
    xx = jnp.concatenate([buf_ref[...], x], axis=0)
    cw = cw_ref[...]
    u = cb_ref[...] + sum(xx[k * bsz:k * bsz + n] * cw[k:k + 1] for k in range(CONV_W))
    a, i, mult = _lru_gates(u, wax_ref, ba_ref[...], bx_ref[...], lam_ref[...])
    v = u * i * mult
    h = h0_ref[...]
    for t in range(steps):
        sl = slice(t * bsz, (t + 1) * bsz)
        h = a[sl] * h + v[sl]
        ya_ref[sl, :] = (h * jax.nn.gelu(gate_ref[sl, :].astype(F32))).astype(ya_ref.dtype)
    hlast_ref[...] = h
    bufout_ref[...] = xx[steps * bsz:(steps + CONV_W - 1) * bsz]


def _lru_step(xin_t, gate_t, buf_t, h0, lw, steps):
    bsz, w = h0.shape
    body = functools.partial(_lru_step_body, steps=steps)
    return pl.pallas_call(
        body,
        out_shape=[jax.ShapeDtypeStruct((steps * bsz, w), BF16),
                   jax.ShapeDtypeStruct((bsz, w), F32),
                   jax.ShapeDtypeStruct(((CONV_W - 1) * bsz, w), F32)],
        compiler_params=pltpu.CompilerParams(vmem_limit_bytes=VMEM_LIMIT),
        name="lru_step",
    )(xin_t, gate_t, buf_t, h0, lw["lru_conv_w"], lw["lru_conv_b"], lw["wax"],
      lw["lru_ba"], lw["lru_bx"], lw["lru_lambda"])


def _expand_heads(cols, e2):
    q = cols[0].shape[0]
    v = jnp.concatenate(cols, axis=0)
    hi = v.astype(BF16)
    lo = (v - hi.astype(F32)).astype(BF16)
    out = _dot(jnp.concatenate([hi, lo], axis=1), e2)
    return [out[i * q:(i + 1) * q] for i in range(len(cols))]


def _ssd_chunk(xc, dt, p, e2, s_get, s_set, t_col, t_row, n_seg=1, n_valid=None):
    q = xc.shape[0]
    rps = q // n_seg
    gn = SSD_GROUPS * SSD_STATE
    d_inner = xc.shape[1] - 2 * gn
    hpg = d_inner // SSD_HEADDIM // SSD_GROUPS
    gw = hpg * SSD_HEADDIM

    causal = t_col >= t_row
    if n_seg > 1:
        same = (lax.broadcasted_iota(jnp.int32, (q, 1), 0) // rps) == (lax.broadcasted_iota(jnp.int32, (1, q), 1) // rps)
        causal = jnp.logical_and(same, causal)
    if n_valid is not None:
        dt = jnp.where(lax.broadcasted_iota(jnp.int32, dt.shape, 0) % rps < n_valid, dt, 0.0)
    a = dt * p["A"]
    cum = _dot_f32(causal.astype(F32), a)
    cum_t = cum.T
    total = cum[q - 1:q, :] if n_seg == 1 else _dot_f32(same.astype(F32), a)
    dt_x, to_end_x, ecum_x = _expand_heads([dt, jnp.exp(total - cum), jnp.exp(cum)], e2)

    xs = xc[:, :d_inner]
    xdt = xs * dt_x
    xw = xdt * to_end_x
    packed = rps % (2 * SUBLANES) == 0
    xdt_m = xdt.astype(BF16) if packed else xdt
    if packed:
        xw = xw.astype(BF16)
    lane_head = lax.broadcasted_iota(jnp.int32, (1, gw), 1) // SSD_HEADDIM
    y_groups = []
    for g in range(SSD_GROUPS):
        sl = slice(g * gw, (g + 1) * gw)
        bg = xc[:, d_inner + g * SSD_STATE:d_inner + (g + 1) * SSD_STATE]
        cg = xc[:, d_inner + gn + g * SSD_STATE:d_inner + gn + (g + 1) * SSD_STATE]
        if packed:
            bg, cg = bg.astype(BF16), cg.astype(BF16)
        cb = _dot_nt(cg.astype(BF16), bg.astype(BF16))
        m_heads, x_heads = [], []
        for hh in range(hpg):
            h = g * hpg + hh
            decay = jnp.exp(jnp.where(causal, cum[:, h:h + 1] - cum_t[h:h + 1, :], -jnp.inf))
            m_heads.append((cb * decay).astype(BF16))
            x_heads.append(jnp.where(lane_head == hh, xdt_m[:, sl], 0.0))
        y_diag = _dot(jnp.concatenate(m_heads, axis=1), jnp.concatenate(x_heads, axis=0).astype(BF16))
        y_off = []
        for b in range(n_seg):
            rows = slice(b * rps, (b + 1) * rps)
            s_old = s_get(b, g)
            y_off.append(_dot_nt(cg[rows].astype(BF16), s_old.astype(BF16)))
            s_dec = [s_old[hh * SSD_HEADDIM:(hh + 1) * SSD_HEADDIM, :]
                     * jnp.exp(cum_t[g * hpg + hh:g * hpg + hh + 1, (b + 1) * rps - 1:(b + 1) * rps])
                     for hh in range(hpg)]
            s_set(b, g, jnp.concatenate(s_dec, axis=0) + _dot_tn(xw[rows, sl].astype(BF16), bg[rows].astype(BF16)))
        y_off = y_off[0] if n_seg == 1 else jnp.concatenate(y_off, axis=0)
        y_groups.append(y_diag + y_off * ecum_x[:, sl])
    return jnp.concatenate(y_groups, axis=1) + p["D"] * xs


def _ssd_gate_norm(y, z, p):
    gw = y.shape[1] // SSD_GROUPS
    zf = z.astype(F32)
    out = []
    for g in range(SSD_GROUPS):
        sl = slice(g * gw, (g + 1) * gw)
        v = y[:, sl] * (zf[:, sl] * jax.nn.sigmoid(zf[:, sl]))
        out.append(v * lax.rsqrt(jnp.mean(v * v, axis=-1, keepdims=True) + EPS) * p["norm_g"][:, sl])
    return jnp.concatenate(out, axis=1)


def _ssd_conv(xpad, q, cw, cb):
    base = SUBLANES - (CONV_W - 1)
    y = cb + sum(xpad[base + k:base + k + q] * cw[k:k + 1] for k in range(CONV_W))
    return y * jax.nn.sigmoid(y)


def _softplus(x):
    return jax.nn.softplus(x)


def _ssd_params(cw_ref, cb_ref, dtb_ref, a_ref, d_ref, ng_ref):
    return {"cw": cw_ref[...], "cb": cb_ref[...], "dt_bias": dtb_ref[...], "A": a_ref[...],
            "D": d_ref[...], "norm_g": ng_ref[...]}


def _ssd_prompt_body(xbc_ref, z_ref, dt_ref, cw_ref, cb_ref, dtb_ref, a_ref, d_ref, ng_ref, e2_ref,
                     yb_ref, sout_ref, bufout_ref, x_scr, dt_scr, y_scr, s_scr, *, q):
    c = pl.program_id(1)
    rows = xbc_ref.shape[1]
    half = q // 2
    n_xslab = x_scr.shape[0]
    base = SUBLANES - (CONV_W - 1)

    @pl.when(c == 0)
    def _():
        x_scr[:, 0:SUBLANES, :] = jnp.zeros((n_xslab, SUBLANES, LANES), F32)
        s_scr[...] = jnp.zeros_like(s_scr)

    @pl.when(c > 0)
    def _():
        x_scr[:, 0:SUBLANES, :] = x_scr[:, rows:rows + SUBLANES, :]

    p = _ssd_params(cw_ref, cb_ref, dtb_ref, a_ref, d_ref, ng_ref)
    x = xbc_ref[0].astype(F32)
    for j in range(n_xslab):
        x_scr[j, SUBLANES:SUBLANES + rows, :] = x[:, j * LANES:(j + 1) * LANES]
    dt_scr[...] = _softplus(dt_ref[0].astype(F32) + p["dt_bias"])

    def times(shape, axis):
        pos = lax.broadcasted_iota(jnp.int32, shape, axis)
        return jnp.where(pos < half, 2 * pos, 2 * (pos - half) + 1)

    def s_set(b, g, v):
        s_scr[g] = v

    for ch in range(rows // q):
        r0 = ch * q
        cols = []
        for j in range(n_xslab):
            ls = slice(j * LANES, (j + 1) * LANES)
            halves = []
            for par in range(2):
                acc = p["cb"][:, ls]
                for k in range(CONV_W):
                    acc = acc + x_scr[j, pl.ds(r0 + base + k + par, half, stride=2), :] * p["cw"][k:k + 1, ls]
                halves.append(acc)
            cols.append(jnp.concatenate(halves, axis=0))
        xc = jnp.concatenate(cols, axis=1)
        xc = xc * jax.nn.sigmoid(xc)
        dt = jnp.concatenate([dt_scr[pl.ds(r0 + par, half, stride=2), :] for par in range(2)], axis=0)
        y = _ssd_chunk(xc, dt, p, e2_ref[...], lambda b, g: s_scr[g], s_set, times((q, 1), 0), times((1, q), 1))
        for j in range(y_scr.shape[0]):
            for par in range(2):
                y_scr[j, pl.ds(par, half, stride=2), :] = y[par * half:(par + 1) * half, j * LANES:(j + 1) * LANES]
        y = jnp.concatenate([y_scr[j] for j in range(y_scr.shape[0])], axis=1)
        yb_ref[0, r0:r0 + q, :] = _ssd_gate_norm(y, z_ref[0, r0:r0 + q, :], p).astype(yb_ref.dtype)

    @pl.when(c == pl.num_programs(1) - 1)
    def _():
        sout_ref[0] = s_scr[...]
        bufout_ref[0] = x[rows - (CONV_W - 1):rows]


def _ssd_prompt(proj3, sp, col_xbc, col_z, col_dt):
    b, l, _ = proj3.shape
    cdim = sp["ssd_conv_w"].shape[1]
    d_inner = sp["ssd_norm_g"].shape[1]
    gw = d_inner // SSD_GROUPS
    q = SSD_CHUNK if l % SSD_CHUNK == 0 else l
    assert q % (2 * SUBLANES) == 0
    rows = q * SSD_CHUNKS_PER_STEP if l % (q * SSD_CHUNKS_PER_STEP) == 0 else q
    return pl.pallas_call(
        functools.partial(_ssd_prompt_body, q=q),
        grid=(b, l // rows),
        in_specs=[pl.BlockSpec((1, rows, cdim), lambda i, c: (i, c, col_xbc // cdim)),
                  pl.BlockSpec((1, rows, d_inner), lambda i, c: (i, c, col_z // d_inner)),
                  pl.BlockSpec((1, rows, LANES), lambda i, c: (i, c, col_dt // LANES)),
                  _const_spec((CONV_W, cdim)), _const_spec((1, cdim)), _const_spec((1, LANES)),
                  _const_spec((1, LANES)), _const_spec((1, d_inner)), _const_spec((1, d_inner)),
                  _const_spec((2 * LANES, d_inner))],
        out_specs=[pl.BlockSpec((1, rows, d_inner), lambda i, c: (i, c, 0)),
                   pl.BlockSpec((1, SSD_GROUPS, gw, SSD_STATE), lambda i, c: (i, 0, 0, 0)),
                   pl.BlockSpec((1, CONV_W - 1, cdim), lambda i, c: (i, 0, 0))],
        out_shape=[jax.ShapeDtypeStruct((b, l, d_inner), BF16),
                   jax.ShapeDtypeStruct((b, SSD_GROUPS, gw, SSD_STATE), F32),
                   jax.ShapeDtypeStruct((b, CONV_W - 1, cdim), F32)],
        scratch_shapes=[pltpu.VMEM((cdim // LANES, rows + SUBLANES, LANES), F32),
                        pltpu.VMEM((rows, LANES), F32),
                        pltpu.VMEM((d_inner // LANES, q, LANES), F32),
                        pltpu.VMEM((SSD_GROUPS, gw, SSD_STATE), F32)],
        compiler_params=_cparams(("parallel", "arbitrary")),
        name="ssd_prompt",
    )(proj3, proj3, proj3, sp["ssd_conv_w"], sp["ssd_conv_b"], sp["dt_bias"], sp["A"], sp["D"], sp["ssd_norm_g"],
      sp["head_expand"])


def _ssd_step_body(xbc_ref, z_ref, dt_ref, buf_ref, s_ref, cw_ref, cb_ref, dtb_ref, a_ref, d_ref, ng_ref, e2_ref,
                   yb_ref, sout_ref, bufout_ref, *, steps, nb):
    p = _ssd_params(cw_ref, cb_ref, dtb_ref, a_ref, d_ref, ng_ref)
    rps = SUBLANES
    q = nb * rps
    x_all = xbc_ref[...].astype(F32)
    z_all = z_ref[...].astype(F32)
    dt_all = _softplus(dt_ref[...].astype(F32) + p["dt_bias"])
    cdim = x_all.shape[1]

    def padded(v, j):
        return jnp.concatenate([v[j * steps:(j + 1) * steps], jnp.zeros((rps - steps, v.shape[1]), v.dtype)], axis=0)

    xcs = []
    for j in range(nb):
        xpad = jnp.concatenate([jnp.zeros((SUBLANES - (CONV_W - 1), cdim), F32), buf_ref[j], padded(x_all, j)],
                               axis=0)
        xcs.append(_ssd_conv(xpad, rps, p["cw"], p["cb"]))
        bufout_ref[j] = xpad[SUBLANES + steps - (CONV_W - 1):SUBLANES + steps]
    xc = jnp.concatenate(xcs, axis=0)
    dt = jnp.concatenate([padded(dt_all, j) for j in range(nb)], axis=0)
    z = jnp.concatenate([padded(z_all, j) for j in range(nb)], axis=0)

    def s_set(b, g, v):
        sout_ref[b, g] = v

    t_col = lax.broadcasted_iota(jnp.int32, (q, 1), 0) % rps
    t_row = lax.broadcasted_iota(jnp.int32, (1, q), 1) % rps
    y = _ssd_chunk(xc, dt, p, e2_ref[...], lambda b, g: s_ref[b, g], s_set, t_col, t_row, n_seg=nb, n_valid=steps)
    y = _ssd_gate_norm(y, z, p)
    for j in range(nb):
        yb_ref[j * steps:(j + 1) * steps, :] = y[j * rps:j * rps + steps].astype(yb_ref.dtype)


def _ssd_step(proj, buf, s0, sp, steps, nb, col_xbc, col_z, col_dt):
    bsz = s0.shape[0]
    cdim = sp["ssd_conv_w"].shape[1]
    d_inner = sp["ssd_norm_g"].shape[1]
    gw = d_inner // SSD_GROUPS
    rows = nb * steps
    body = functools.partial(_ssd_step_body, steps=steps, nb=nb)
    return pl.pallas_call(
        body,
        grid=(bsz // nb,),
        in_specs=[pl.BlockSpec((rows, cdim), lambda i: (i, col_xbc // cdim)),
                  pl.BlockSpec((rows, d_inner), lambda i: (i, col_z // d_inner)),
                  pl.BlockSpec((rows, LANES), lambda i: (i, col_dt // LANES)),
                  pl.BlockSpec((None, nb, CONV_W - 1, cdim), lambda i: (0, i, 0, 0)),
                  pl.BlockSpec((nb, SSD_GROUPS, gw, SSD_STATE), lambda i: (i, 0, 0, 0)),
                  _const_spec((CONV_W, cdim)), _const_spec((1, cdim)), _const_spec((1, LANES)),
                  _const_spec((1, LANES)), _const_spec((1, d_inner)), _const_spec((1, d_inner)),
                  _const_spec((2 * LANES, d_inner))],
        out_specs=[pl.BlockSpec((rows, d_inner), lambda i: (i, 0)),
                   pl.BlockSpec((nb, SSD_GROUPS, gw, SSD_STATE), lambda i: (i, 0, 0, 0)),
                   pl.BlockSpec((None, nb, CONV_W - 1, cdim), lambda i: (0, i, 0, 0))],
        out_shape=[jax.ShapeDtypeStruct((bsz * steps, d_inner), BF16),
                   jax.ShapeDtypeStruct(s0.shape, F32),
                   jax.ShapeDtypeStruct(buf.shape, F32)],
        compiler_params=_cparams(("parallel",)),
        name="ssd_step",
    )(proj, proj, proj, buf, s0, sp["ssd_conv_w"], sp["ssd_conv_b"], sp["dt_bias"], sp["A"], sp["D"],
      sp["ssd_norm_g"], sp["head_expand"])


def _router(t, wr, br):
    t_hi = t.astype(BF16)
    t_lo = (t - t_hi.astype(F32)).astype(BF16)
    both = _dot(jnp.concatenate([t_hi, t_lo], axis=1), wr)
    logits = both[:, :LANES] + both[:, LANES:] + br
    lane = lax.broadcasted_iota(jnp.int32, logits.shape, 1)
    neg = -jnp.inf
    gl = jnp.where(lane < N_EGROUPS, logits, neg)
    gmax = jnp.max(gl, axis=-1, keepdims=True)
    g_idx = jnp.min(jnp.where(gl == gmax, lane, LANES), axis=-1, keepdims=True)
    g_w = 1.0 / jnp.sum(jnp.exp(gl - gmax), axis=-1, keepdims=True)
    in_grp = jnp.logical_and(jnp.logical_and(lane >= ROUTER_LANE0, lane < ROUTER_LANE0 + N_EXPERTS),
                             ((lane - ROUTER_LANE0) >> 2) == g_idx)
    el = jnp.where(in_grp, logits, neg)
    pe = jnp.exp(el - jnp.max(el, axis=-1, keepdims=True))
    pe = pe / jnp.sum(pe, axis=-1, keepdims=True)
    cand = jnp.where(in_grp, pe, -1.0)
    v1 = jnp.max(cand, axis=-1, keepdims=True)
    i1 = jnp.min(jnp.where(cand == v1, lane, LANES), axis=-1, keepdims=True)
    cand2 = jnp.where(lane == i1, -1.0, cand)
    v2 = jnp.max(cand2, axis=-1, keepdims=True)
    i2 = jnp.min(jnp.where(jnp.logical_and(cand2 == v2, in_grp), lane, LANES), axis=-1, keepdims=True)
    den = v1 + v2
    return lane, i1, i2, g_w * v1 / den, g_w * v2 / den


def _rows_to_tiles(ref, val):
    n, d = val.shape
    for k in range(d // LANES):
        ref[pl.ds(k, n, stride=d // LANES), :] = val[:, k * LANES:(k + 1) * LANES]


def _tiles_to_rows(ref, n, d, start=0):
    return jnp.concatenate([ref[pl.ds(start + k, n, stride=d // LANES), :] for k in range(d // LANES)], axis=1)


def _merge_body(x_ref, ya_ref, yb_ref, ga_ref, gb_ref, wl_ref, ws_ref, wo_ref, gf_ref, wr_ref, br_ref,
                x1_ref, t_ref, rt_ref, rtt_ref, cnt_ref, base_scr):
    step = pl.program_id(0)

    @pl.when(step == 0)
    def _():
        base_scr[...] = jnp.zeros_like(base_scr)

    a = _dot(ya_ref[...], wl_ref[...])
    b = _dot(yb_ref[...], ws_ref[...])
    merged = jax.nn.sigmoid(ga_ref[...].astype(F32)) * a + jax.nn.sigmoid(gb_ref[...].astype(F32)) * b
    x1 = x_ref[...] + _dot(merged.astype(BF16), wo_ref[...])
    x1_ref[...] = x1
    t = _rms(x1, gf_ref[...])
    _rows_to_tiles(t_ref, t)
    lane, i1, i2, wg1, wg2 = _router(t, wr_ref[...], br_ref[...])

    tm = t.shape[0]
    onehot = jnp.where(jnp.logical_or(lane == i1, lane == i2), 1.0, 0.0).astype(BF16)
    tri = (lax.broadcasted_iota(jnp.int32, (tm, tm), 1) <= lax.broadcasted_iota(jnp.int32, (tm, tm), 0)).astype(BF16)
    cum = _dot(tri, onehot) + base_scr[...]
    r1 = jnp.sum(jnp.where(lane == i1, cum, 0.0), axis=-1, keepdims=True) - 1.0
    r2 = jnp.sum(jnp.where(lane == i2, cum, 0.0), axis=-1, keepdims=True) - 1.0
    cols = (wg1, wg2, r1, r2, (i1 - ROUTER_LANE0).astype(F32), (i2 - ROUTER_LANE0).astype(F32))
    rt = jnp.zeros(cum.shape, F32)
    for k, c in enumerate(cols):
        rt = jnp.where(lane == k, c, rt)
    rt_ref[...] = rt
    rtt_ref[...] = rt.T[0:SUBLANES, :]
    base_scr[...] = cum[tm - 1:tm, :]
    cnt_ref[...] = cum[tm - 1:tm, :]


def _merge(x, ya, yb, proj, mw, tm, col_ga, col_gb):
    t, d = x.shape
    d_inner = yb.shape[1]
    return pl.pallas_call(
        _merge_body,
        grid=(t // tm,),
        in_specs=[pl.BlockSpec((tm, d), lambda i: (i, 0)),
                  pl.BlockSpec((tm, d), lambda i: (i, 0)),
                  pl.BlockSpec((tm, d_inner), lambda i: (i, 0)),
                  pl.BlockSpec((tm, d), lambda i: (i, col_ga // d)),
                  pl.BlockSpec((tm, d), lambda i: (i, col_gb // d)),
                  _const_spec((d, d)), _const_spec((d_inner, d)), _const_spec((d, d)),
                  _const_spec((1, d)), _const_spec((2 * d, 2 * LANES)), _const_spec((1, LANES))],
        out_specs=[pl.BlockSpec((tm, d), lambda i: (i, 0)),
                   pl.BlockSpec((tm * d // LANES, LANES), lambda i: (i, 0)),
                   pl.BlockSpec((tm, LANES), lambda i: (i, 0)),
                   pl.BlockSpec((SUBLANES, tm), lambda i: (0, i)),
                   pl.BlockSpec((1, LANES), lambda i: (0, 0))],
        out_shape=[jax.ShapeDtypeStruct((t, d), F32),
                   jax.ShapeDtypeStruct((t * d // LANES, LANES), F32),
                   jax.ShapeDtypeStruct((t, LANES), F32),
                   jax.ShapeDtypeStruct((SUBLANES, t), F32),
                   jax.ShapeDtypeStruct((1, LANES), F32)],
        scratch_shapes=[pltpu.VMEM((1, LANES), F32)],
        compiler_params=_cparams(("arbitrary",)),
        name="merge_router",
    )(x, ya, yb, proj, proj, mw["w_br_lru"], mw["w_br_ssd"], mw["w_out"], mw["g_ffn"], mw["w_router"],
      mw["b_router"])


def _dispatch_body(dest_ref, zb_ref, t_ref, o_ref, zero_scr, sem, zsem, *, tm, nk, tmg, n_tok):
    step = pl.program_id(0)

    @pl.when(step == 0)
    def _():
        zero_scr[...] = jnp.zeros_like(zero_scr)
        blk = tmg * nk

        def zcopy(j):
            return pltpu.make_async_copy(zero_scr, o_ref.at[pl.ds(pl.multiple_of(zb_ref[j] * blk, blk), blk)], zsem)

        for j in range(zb_ref.shape[0]):
            pl.when(zb_ref[j] >= 0)(lambda j=j: zcopy(j).start())
        for j in range(zb_ref.shape[0]):
            pl.when(zb_ref[j] >= 0)(lambda j=j: zcopy(j).wait())

    def issue(r, carry):
        src = t_ref.at[pl.ds(pl.multiple_of(r * nk, nk), nk)]
        for k in range(2):
            row = dest_ref[k * n_tok + step * tm + r]
            pltpu.make_async_copy(src, o_ref.at[pl.ds(pl.multiple_of(row * nk, nk), nk)], sem).start(priority=k)
        return carry

    lax.fori_loop(0, tm, issue, 0, unroll=8)
    for k in range(2):
        pltpu.make_async_copy(t_ref, o_ref.at[pl.ds(0, tm * nk)], sem).wait()


def _dispatch(dest, zero_blocks, t_tiles, n_tok, tm, n_rows, tmg):
    nk = t_tiles.shape[0] // n_tok
    return pl.pallas_call(
        functools.partial(_dispatch_body, tm=tm, nk=nk, tmg=tmg, n_tok=n_tok),
        grid_spec=pltpu.PrefetchScalarGridSpec(
            num_scalar_prefetch=2,
            grid=(n_tok // tm,),
            in_specs=[pl.BlockSpec((tm * nk, LANES), lambda i, *_: (i, 0))],
            out_specs=pl.BlockSpec(memory_space=pl.ANY),
            scratch_shapes=[pltpu.VMEM((tmg * nk, LANES), F32), pltpu.SemaphoreType.DMA(()),
                            pltpu.SemaphoreType.DMA(())]),
        out_shape=jax.ShapeDtypeStruct((n_rows * nk, LANES), F32),
        compiler_params=_cparams(("arbitrary",)),
        name="moe_dispatch",
    )(dest, zero_blocks, t_tiles)


def _expert_body(te_ref, nt_ref, x_ref, w1_ref, w3_ref, w2_ref, y_ref, w1_scr, w3_scr, w2_scr, *, tmg):
    i = pl.program_id(0)
    real = i < nt_ref[0]
    d = w1_scr.shape[0]

    @pl.when(jnp.logical_or(i == 0, te_ref[i] != te_ref[jnp.maximum(i - 1, 0)]))
    def _():
        w1_scr[...] = w1_ref[0].astype(BF16)
        w3_scr[...] = w3_ref[0].astype(BF16)
        w2_scr[...] = w2_ref[0].astype(BF16)

    @pl.when(real)
    def _():
        x = _tiles_to_rows(x_ref, tmg, d).astype(BF16)
        h1 = _dot(x, w1_scr[...])
        h3 = _dot(x, w3_scr[...])
        _rows_to_tiles(y_ref, _dot((h1 * jax.nn.sigmoid(h1) * h3).astype(BF16), w2_scr[...]))

    @pl.when(jnp.logical_not(real))
    def _():
        y_ref[...] = jnp.zeros_like(y_ref)


def _experts(tile_expert, n_tiles, xs_tiles, w1, w3, w2, tmg):
    _, d, dff = w1.shape
    blk = tmg * d // LANES
    row_spec = pl.BlockSpec((blk, LANES), lambda i, te, nt: (i, 0))
    return pl.pallas_call(
        functools.partial(_expert_body, tmg=tmg),
        grid_spec=pltpu.PrefetchScalarGridSpec(
            num_scalar_prefetch=2,
            grid=(xs_tiles.shape[0] // blk,),
            in_specs=[row_spec,
                      pl.BlockSpec((1, d, dff), lambda i, te, nt: (te[i], 0, 0)),
                      pl.BlockSpec((1, d, dff), lambda i, te, nt: (te[i], 0, 0)),
                      pl.BlockSpec((1, dff, d), lambda i, te, nt: (te[i], 0, 0))],
            out_specs=row_spec,
            scratch_shapes=[pltpu.VMEM((d, dff), BF16), pltpu.VMEM((d, dff), BF16), pltpu.VMEM((dff, d), BF16)]),
        out_shape=jax.ShapeDtypeStruct(xs_tiles.shape, F32),
        compiler_params=_cparams(("arbitrary",)),
        name="moe_experts",
    )(tile_expert, n_tiles, xs_tiles, w1, w3, w2)


def _ple_body(dest_ref, x_ref, rt_ref, p_ref, wp_ref, gp_ref, gg_ref, wg_ref, gfin_ref, y_hbm, o_ref, gbuf, sem,
              *, n_tok):
    step = pl.program_id(0)
    tm, d = x_ref.shape
    nk = d // LANES

    def gather(tile, slot):
        def issue(r, carry):
            tok = tile * tm + r
            for k in range(2):
                row = dest_ref[k * n_tok + tok]
                pltpu.make_async_copy(y_hbm.at[pl.ds(pl.multiple_of(row * nk, nk), nk)],
                                      gbuf.at[slot, pl.ds(pl.multiple_of((k * tm + r) * nk, nk), nk)],
                                      sem.at[slot]).start(priority=k)
            return carry

        lax.fori_loop(0, tm, issue, 0, unroll=8)

    @pl.when(step == 0)
    def _():
        gather(0, 0)

    @pl.when(step + 1 < pl.num_programs(0))
    def _():
        gather(step + 1, (step + 1) % 2)

    slot = step % 2
    pltpu.make_async_copy(y_hbm.at[pl.ds(0, 2 * tm * nk)], gbuf.at[slot], sem.at[slot]).wait()
    rt = rt_ref[...]
    rows = gbuf.at[slot]
    x = (x_ref[...] + rt[:, 0:1] * _tiles_to_rows(rows, tm, d)
         + rt[:, 1:2] * _tiles_to_rows(rows, tm, d, start=tm * nk))
    e = _rms(_dot(p_ref[...].astype(BF16), wp_ref[...]), gp_ref[...])
    gate = jax.nn.sigmoid(_dot(_rms(x, gg_ref[...]).astype(BF16), wg_ref[...]))
    o_ref[...] = _rms(x + gate * e, gfin_ref[...])


def _ple(dest, x1, rt, p, y_sorted, pw, tm):
    n, d = x1.shape
    dp = p.shape[1]
    const = lambda shape: pl.BlockSpec(shape, lambda i, *_: (0,) * len(shape))
    return pl.pallas_call(
        functools.partial(_ple_body, n_tok=n),
        grid_spec=pltpu.PrefetchScalarGridSpec(
            num_scalar_prefetch=1,
            grid=(n // tm,),
            in_specs=[pl.BlockSpec((tm, d), lambda i, *_: (i, 0)),
                      pl.BlockSpec((tm, LANES), lambda i, *_: (i, 0)),
                      pl.BlockSpec((tm, dp), lambda i, *_: (i, 0)),
                      const((dp, d)), const((1, d)), const((1, d)), const((d, d)), const((1, d)),
                      pl.BlockSpec(memory_space=pl.ANY)],
            out_specs=pl.BlockSpec((tm, d), lambda i, *_: (i, 0)),
            scratch_shapes=[pltpu.VMEM((2, 2 * tm * d // LANES, LANES), F32), pltpu.SemaphoreType.DMA((2,))]),
        out_shape=jax.ShapeDtypeStruct((n, d), F32),
        compiler_params=_cparams(("arbitrary",)),
        name="combine_ple_final",
    )(dest, x1, rt, p, pw["w_ple_proj"], pw["g_ple"], pw["g_ple_gate"], pw["w_ple_gate"], pw["g_final"], y_sorted)


def _pick_tile(n, pref):
    t = min(n, pref)
    while n % t:
        t //= 2
    return t


def _moe_row_tile(n):
    return 512 if 2 * n // N_EXPERTS >= 1024 else 128
MOE_DISPATCH_TILE = 2048
SSD_CHUNKS_PER_STEP = 4
LRU_SLICE_ROWS = 128
LRU_PIECES = 4


def _split_router(w):
    hi = w.astype(BF16)
    lo = (w - hi.astype(F32)).astype(BF16)
    return jnp.concatenate([jnp.concatenate([hi, lo], axis=1),
                            jnp.concatenate([hi, jnp.zeros_like(hi)], axis=1)], axis=0)


def _token_tail(x1, t_tiles, rt, rtt, cnt, p, lw, tm):
    n = x1.shape[0]
    tmg = _moe_row_tile(n)
    n_blocks = pl.cdiv(2 * n, tmg) + N_EXPERTS
    counts = cnt[0, ROUTER_LANE0:ROUTER_LANE0 + N_EXPERTS].astype(jnp.int32)
    tiles = (counts + tmg - 1) // tmg
    ends = jnp.cumsum(tiles)
    n_tiles = ends[-1]
    e_idx = rtt[4:6].astype(jnp.int32)
    first_row = (ends - tiles) * tmg
    dest = rtt[2:4].astype(jnp.int32) + sum(jnp.where(e_idx == e, first_row[e], 0) for e in range(N_EXPERTS))
    dest = dest.reshape(2 * n)
    blk = jnp.arange(n_blocks, dtype=jnp.int32)
    tile_expert = jnp.sum((jnp.minimum(blk, n_tiles - 1)[:, None] >= ends[None, :]).astype(jnp.int32), axis=1)
    tail = n_tiles + blk[:N_EXPERTS]
    zero_blocks = jnp.concatenate([jnp.where(tiles > 0, ends - 1, -1),
                                   jnp.where(tail < n_blocks, tail, -1)]).astype(jnp.int32)

    sorted_t = _dispatch(dest, zero_blocks, t_tiles, n, _pick_tile(n, MOE_DISPATCH_TILE), n_blocks * tmg, tmg)
    y_sorted = _experts(tile_expert, n_tiles.reshape(1), sorted_t, lw["w1"], lw["w3"], lw["w2"], tmg)
    return _ple(dest, x1, rt, p, y_sorted, lw, tm)


def kernel(x_prompt, x_sample, state_lru_h, state_lru_conv, state_ssd, state_ssd_conv, p_prompt, p_sample, g_mix, w_in, lru_conv_w, lru_conv_b, lru_wa, lru_ba, lru_wx, lru_bx, lru_lambda, ssd_conv_w, ssd_conv_b, ssd_dt_bias, ssd_A_log, ssd_D, ssd_norm_g, w_br_lru, w_br_ssd, w_out, g_ffn, w_router_g, b_router_g, w_router_e, b_router_e, w1, w3, w2, w_ple_proj, g_ple, g_ple_gate, w_ple_gate, g_final):
    depth = w_in.shape[0]
    assert depth == 1, "one decoder layer per call"
    bp, lp, d = x_prompt.shape
    bs, ls, _ = x_sample.shape
    w_lru = state_lru_h.shape[-1]
    heads, hdim, nstate = state_ssd.shape[2:]
    d_inner = heads * hdim
    cdim = state_ssd_conv.shape[-1]
    assert hdim == SSD_HEADDIM and nstate == SSD_STATE and heads <= LANES and ls < SUBLANES
    gw = d_inner // SSD_GROUPS

    o_dt = 2 * w_lru + d_inner + cdim
    n_proj = o_dt + 2 * d + LANES
    tn = n_proj // 9 if n_proj % (9 * LANES) == 0 else LANES
    n_head = o_dt // tn
    w_head = w_in.astype(BF16)
    wi = w_head[0]
    w_tail = jnp.concatenate([wi[:, n_head * tn:o_dt], wi[:, o_dt + heads:], wi[:, o_dt:o_dt + heads],
                              jnp.zeros((d, LANES - heads), BF16)], axis=1)
    assert n_head >= 1 and w_tail.shape[1] == n_proj - n_head * tn and w_tail.shape[1] % tn == 0
    col_z, col_xbc = 2 * w_lru, 2 * w_lru + d_inner
    col_ga, col_gb, col_dt = o_dt, o_dt + d, o_dt + 2 * d
    row = lambda v: v.reshape(1, -1).astype(F32)
    pad_heads = lambda v: jnp.pad(v.astype(F32), (0, LANES - heads)).reshape(1, LANES)
    lw = {
        "lru_conv_w": lru_conv_w[0], "lru_conv_b": row(lru_conv_b[0]),
        "wax": jnp.concatenate([lru_wa[0], lru_wx[0]], axis=-1).astype(BF16),
        "lru_ba": row(lru_ba[0]), "lru_bx": row(lru_bx[0]), "lru_lambda": row(lru_lambda[0]),
        "ssd_conv_w": ssd_conv_w[0], "ssd_conv_b": row(ssd_conv_b[0]),
        "dt_bias": pad_heads(ssd_dt_bias[0]), "A": pad_heads(-jnp.exp(ssd_A_log[0].astype(F32))),
        "D": row(jnp.repeat(ssd_D[0], hdim)), "ssd_norm_g": row(ssd_norm_g[0]),
        "head_expand": jnp.tile(jnp.arange(LANES)[:, None] == jnp.arange(d_inner)[None, :] // hdim, (2, 1)).astype(BF16),
        "w_br_lru": w_br_lru[0].astype(BF16), "w_br_ssd": w_br_ssd[0].astype(BF16), "w_out": w_out[0].astype(BF16),
        "g_ffn": row(g_ffn[0]),
        "w_router": _split_router(jnp.concatenate([w_router_g[0], w_router_e[0],
                                                   jnp.zeros((d, LANES - N_EGROUPS - N_EXPERTS), F32)], axis=1)),
        "b_router": jnp.concatenate([b_router_g[0], b_router_e[0],
                                     jnp.zeros((LANES - N_EGROUPS - N_EXPERTS,), F32)]).reshape(1, LANES),
        "w1": w1[0], "w3": w3[0], "w2": w2[0],
        "w_ple_proj": w_ple_proj[0].astype(BF16), "g_ple": row(g_ple[0]), "g_ple_gate": row(g_ple_gate[0]),
        "w_ple_gate": w_ple_gate[0].astype(BF16), "g_final": row(g_final),
    }
    g_mix_r = row(g_mix[0])

    tp = bp * lp
    xp = x_prompt.reshape(tp, d)
    tm_p = _pick_tile(lp, 1024)
    proj_p, ya_p, hl_p, lbuf_p = _inproj_lru(xp, g_mix_r, w_head, w_tail, lw, lp, tm_p, tn, n_head,
                                             _pick_tile(tm_p, LRU_SLICE_ROWS))
    proj_p3 = proj_p.reshape(bp, lp, n_proj)
    yb_p, s_p, sbuf_p = _ssd_prompt(proj_p3, lw, col_xbc, col_z, col_dt)
    tm_tail_p = _pick_tile(tp, 512)
    x1_p, t_p, rt_p, rtt_p, cnt_p = _merge(xp, ya_p.reshape(tp, w_lru), yb_p.reshape(tp, d_inner), proj_p, lw,
                                           tm_tail_p, col_ga, col_gb)
    y_p = _token_tail(x1_p, t_p, rt_p, rtt_p, cnt_p, p_prompt[0].reshape(tp, -1), lw, tm_tail_p)

    ts = bs * ls
    xs = x_sample.reshape(ts, d)
    tm_s = _pick_tile(ts, 512)
    proj_s = _inproj(xs, g_mix_r, w_head, w_tail, tm_s, tn, n_head)
    to_tmajor = lambda v, n: v.reshape(bs, n, -1).transpose(1, 0, 2).reshape(n * bs, -1)
    from_tmajor = lambda v, n: v.reshape(n, bs, -1).transpose(1, 0, 2)
    ya_t, hl_s, lbuf_t = _lru_step(to_tmajor(proj_s[:, :w_lru], ls), to_tmajor(proj_s[:, w_lru:2 * w_lru], ls),
                                   to_tmajor(state_lru_conv[0], CONV_W - 1), state_lru_h[0], lw, ls)
    ya_s = from_tmajor(ya_t, ls).reshape(ts, w_lru)
    lbuf_s = from_tmajor(lbuf_t, CONV_W - 1)
    yb_s, s_s, sbuf_s = _ssd_step(proj_s, state_ssd_conv, state_ssd[0].reshape(bs, SSD_GROUPS, gw, nstate), lw,
                                  ls, _pick_tile(bs, 8), col_xbc, col_z, col_dt)
    x1_s, t_s, rt_s, rtt_s, cnt_s = _merge(xs, ya_s, yb_s, proj_s, lw, tm_s, col_ga, col_gb)
    y_s = _token_tail(x1_s, t_s, rt_s, rtt_s, cnt_s, p_sample[0].reshape(ts, -1), lw, tm_s)

    return (y_p.reshape(bp, lp, d), y_s.reshape(bs, ls, d),
            hl_p.reshape(1, bp, w_lru), lbuf_p[None],
            s_p.reshape(1, bp, heads, hdim, nstate), sbuf_p[None],
            hl_s[None], lbuf_s[None],
            s_s.reshape(1, bs, heads, hdim, nstate), sbuf_s)
```

```python
import functools

import jax
import jax.numpy as jnp
from jax import lax
from jax.experimental import pallas as pl
from jax.experimental.pallas import tpu as pltpu

F32 = jnp.float32
BF16 = jnp.bfloat16

EPS = 1e-6
CONV_W = 4
LRU_BLOCKS = 8
LRU_C = 8.0
SSD_HEADDIM = 64
SSD_GROUPS = 8
SSD_STATE = 128
SSD_CHUNK = 128
N_EGROUPS = 4
EXP_PER_GROUP = 4
N_EXPERTS = N_EGROUPS * EXP_PER_GROUP

LANES = 128
SUBLANES = 8
VMEM_LIMIT = 52 * 1024 * 1024
ROUTER_LANE0 = N_EGROUPS


def _cparams(sem):
    return pltpu.CompilerParams(dimension_semantics=sem, vmem_limit_bytes=VMEM_LIMIT)


def _dot(a, b):
    return jnp.dot(a, b, preferred_element_type=F32)


def _dot_nt(a, b):
    return lax.dot_general(a, b, (((1,), (1,)), ((), ())), preferred_element_type=F32)


def _dot_tn(a, b):
    return lax.dot_general(a, b, (((0,), (0,)), ((), ())), preferred_element_type=F32)


def _dot_f32(a, b):
    return jnp.dot(a, b, precision=lax.Precision.HIGHEST, preferred_element_type=F32)


def _rms(x, g):
    return x * lax.rsqrt(jnp.mean(x * x, axis=-1, keepdims=True) + EPS) * g


def _const_spec(shape):
    nd = len(shape)
    return pl.BlockSpec(shape, lambda *_: (0,) * nd)


def _inproj_body(x_ref, g_ref, wh_ref, wt_ref, o_ref, h_scr, *, n_head):
    j = pl.program_id(1)

    @pl.when(j == 0)
    def _():
        h_scr[...] = _rms(x_ref[...], g_ref[...]).astype(BF16)

    @pl.when(j < n_head)
    def _():
        o_ref[...] = _dot(h_scr[...], wh_ref[0]).astype(o_ref.dtype)

    @pl.when(j >= n_head)
    def _():
        o_ref[...] = _dot(h_scr[...], wt_ref[...]).astype(o_ref.dtype)


def _inproj(x, g, w_head, w_tail, tm, tn, n_head):
    t, d = x.shape
    n_tail = w_tail.shape[1] // tn
    return pl.pallas_call(
        functools.partial(_inproj_body, n_head=n_head),
        grid=(t // tm, n_head + n_tail),
        in_specs=[pl.BlockSpec((tm, d), lambda i, j: (i, 0)),
                  pl.BlockSpec((1, d), lambda i, j: (0, 0)),
                  pl.BlockSpec((1, d, tn), lambda i, j: (0, 0, jnp.minimum(j, n_head - 1))),
                  pl.BlockSpec((d, tn), lambda i, j: (0, jnp.maximum(j - n_head, 0)))],
        out_specs=pl.BlockSpec((tm, tn), lambda i, j: (i, j)),
        out_shape=jax.ShapeDtypeStruct((t, (n_head + n_tail) * tn), BF16),
        scratch_shapes=[pltpu.VMEM((tm, d), BF16)],
        compiler_params=_cparams(("parallel", "arbitrary")),
        name="inproj",
    )(x, g, w_head, w_tail)


def _lru_gate_matmuls(u, wax_ref):
    bw = u.shape[1] // LRU_BLOCKS
    r_parts, i_parts = [], []
    for n in range(LRU_BLOCKS):
        ri = _dot(u[:, n * bw:(n + 1) * bw].astype(BF16), wax_ref[n])
        r_parts.append(ri[:, :bw])
        i_parts.append(ri[:, bw:])
    return jnp.concatenate(r_parts, axis=1), jnp.concatenate(i_parts, axis=1)


def _lru_gate_values(r_pre, i_pre, ba, bx, lam):
    r = jax.nn.sigmoid(r_pre + ba)
    i = jax.nn.sigmoid(i_pre + bx)
    log_a = LRU_C * r * jax.nn.log_sigmoid(lam)
    a = jnp.exp(log_a)
    m2 = -jnp.tanh(log_a) * (a * a + 1.0)
    mult = jnp.where(m2 > 0.0, m2 * lax.rsqrt(m2), 0.0)
    return a, i, mult


def _lru_gates(u, wax_ref, ba, bx, lam):
    return _lru_gate_values(*_lru_gate_matmuls(u, wax_ref), ba, bx, lam)


def _lru_conv(x, halo, seq_start, cw, cb):
    tt = x.shape[0]
    xpad = jnp.concatenate([jnp.where(seq_start, 0.0, halo), x], axis=0)
    base = SUBLANES - (CONV_W - 1)
    return cb + sum(xpad[base + k:base + k + tt] * cw[k:k + 1] for k in range(CONV_W))


def _lru_scan(u, r_pre, i_pre, gate, carry, seq_start, ba, bx, lam):
    tt, width = u.shape
    carry = jnp.where(seq_start, 0.0, carry)
    a, i, mult = _lru_gate_values(r_pre, i_pre, ba, bx, lam)
    first = jnp.logical_and(lax.broadcasted_iota(jnp.int32, a.shape, 0) == 0, seq_start)
    mult = jnp.where(first, 1.0, mult)
    a = jnp.where(first, 0.0, a)
    v = u * i * mult

    a = a.reshape(tt // SUBLANES, SUBLANES, width)
    v = v.reshape(tt // SUBLANES, SUBLANES, width)
    sub = lax.broadcasted_iota(jnp.int32, a.shape, 1)
    s = 1
    while s < SUBLANES:
        keep = sub >= s
        v = jnp.where(keep, a * pltpu.roll(v, s, axis=1) + v, v)
        a = jnp.where(keep, a * pltpu.roll(a, s, axis=1), a)
        s *= 2
    groups = []
    for g in range(tt // SUBLANES):
        hg = a[g] * carry + v[g]
        carry = hg[SUBLANES - 1:SUBLANES]
        groups.append(hg)
    h = jnp.concatenate(groups, axis=0)
    return h * jax.nn.gelu(gate), carry


def _inproj_lru_body(x_ref, g_ref, wh_ref, wt_ref, cw_ref, cb_ref, wax_ref, ba_ref, bx_ref, lam_ref,
                     o_ref, ya_ref, hlast_ref, bufout_ref, h_scr, lru_new, lru_cur, halo_scr, carry_scr,
                     *, n_head, n_j, n_tiles, tiles_per_seq, sub_rows):
    i = pl.program_id(0)
    j = pl.program_id(1)
    tm, tn = o_ref.shape
    w = ya_ref.shape[1]
    n_sub = tm // sub_rows
    prev_tile = jnp.maximum(i - 1, 0)

    @pl.when(jnp.logical_and(i == 0, j == 0))
    def _():
        lru_cur[...] = jnp.zeros_like(lru_cur)
        halo_scr[...] = jnp.zeros_like(halo_scr)
        carry_scr[...] = jnp.zeros_like(carry_scr)

    n_piece = LRU_PIECES
    piece_rows = sub_rows // n_piece
    mx_w = 2 * LANES
    col_cuts = [min(tn, mx_w * (q * (tn // mx_w) // n_piece)) for q in range(n_piece)] + [tn]

    def lru_piece(q, state):
        r0 = pl.multiple_of(j * sub_rows, sub_rows) + q * piece_rows
        seq_start = jnp.logical_and(jnp.logical_and(prev_tile % tiles_per_seq == 0, j == 0), q == 0)
        x = lru_cur[pl.ds(r0, piece_rows), 0:w].astype(F32)
        gate = lru_cur[pl.ds(r0, piece_rows), w:2 * w].astype(F32)
        halo, carry = state if state is not None else (halo_scr[...], carry_scr[...])
        u = _lru_conv(x, halo, seq_start, cw_ref[...], cb_ref[...])
        r_pre, i_pre = _lru_gate_matmuls(u, wax_ref)
        ya, carry = _lru_scan(u, r_pre, i_pre, gate, carry, seq_start, ba_ref[...], bx_ref[...], lam_ref[...])
        ya_ref[pl.ds(r0, piece_rows), :] = ya.astype(ya_ref.dtype)
        halo = x[piece_rows - SUBLANES:piece_rows]
        if q == n_piece - 1:
            halo_scr[...] = halo
            carry_scr[...] = carry

            @pl.when(jnp.logical_and(prev_tile % tiles_per_seq == tiles_per_seq - 1, j == n_sub - 1))
            def _():
                hlast_ref[0] = carry
                bufout_ref[0] = x[piece_rows - (CONV_W - 1):piece_rows]
        return halo, carry

    def project_piece(q, from_head, lru_lo):
        c0, c1 = col_cuts[q], col_cuts[q + 1]
        if c0 == c1:
            return
        w_cols = wh_ref[0, :, c0:c1] if from_head else wt_ref[:, c0:c1]
        o = _dot(h_scr[...], w_cols).astype(o_ref.dtype)
        o_ref[:, c0:c1] = o
        if lru_lo is not None:
            keep = min(lru_lo + c1, 2 * w) - (lru_lo + c0)
            if keep > 0:
                lru_new[:, lru_lo + c0:lru_lo + c0 + keep] = o[:, 0:keep]

    def column_steps(lo, hi, from_head, prologue=None, lru_lo=None):
        for a, b, with_lru in ((lo, min(hi, n_sub), True), (max(lo, n_sub), hi, False)):
            if a < b:
                @pl.when(jnp.logical_and(i < n_tiles, jnp.logical_and(j >= a, j < b)))
                def _(with_lru=with_lru):
                    if prologue is not None:
                        prologue()
                    state = None
                    for q in range(n_piece):
                        project_piece(q, from_head, lru_lo)
                        if with_lru:
                            state = lru_piece(q, state)
                    if b == n_j:
                        @pl.when(j == n_j - 1)
                        def _():
                            lru_cur[...] = lru_new[...]

    def normalise():
        h_scr[...] = _rms(x_ref[...], g_ref[...]).astype(BF16)

    column_steps(0, 1, True, prologue=normalise, lru_lo=0)
    column_steps(1, 2, True, lru_lo=tn)
    column_steps(2, n_head, True)
    column_steps(n_head, n_j, False)

    @pl.when(jnp.logical_and(i == n_tiles, j < n_sub))
    def _():
        state = None
        for q in range(n_piece):
            state = lru_piece(q, state)


def _inproj_lru(x, g, w_head, w_tail, lw, seq_len, tm, tn, n_head, sub_rows):
    t, d = x.shape
    w = lw["lru_lambda"].shape[1]
    n_tail = w_tail.shape[1] // tn
    n_tiles, n_j = t // tm, n_head + n_tail
    tiles_per_seq = seq_len // tm
    assert seq_len % tm == 0 and tm % sub_rows == 0 and tm // sub_rows < n_j and n_head >= 2
    assert w <= tn and 2 * w <= 2 * tn and 2 * w > tn
    cur_tile = lambda i: jnp.minimum(i, n_tiles - 1)
    lru_tile = lambda i: jnp.maximum(i - 1, 0)
    body = functools.partial(_inproj_lru_body, n_head=n_head, n_j=n_j, n_tiles=n_tiles,
                             tiles_per_seq=tiles_per_seq, sub_rows=sub_rows)
    out_col = lambda i, j: jnp.where(i < n_tiles, j, n_j - 1)
    w_col = lambda i, j: jnp.where(i < n_tiles, j, n_j - 1)
    return pl.pallas_call(
        body,
        grid=(n_tiles + 1, n_j),
        in_specs=[pl.BlockSpec((tm, d), lambda i, j: (cur_tile(i), 0)),
                  pl.BlockSpec((1, d), lambda i, j: (0, 0)),
                  pl.BlockSpec((1, d, tn), lambda i, j: (0, 0, jnp.minimum(w_col(i, j), n_head - 1))),
                  pl.BlockSpec((d, tn), lambda i, j: (0, jnp.maximum(w_col(i, j) - n_head, 0))),
                  _const_spec((CONV_W, w)), _const_spec((1, w)),
                  _const_spec(lw["wax"].shape), _const_spec((1, w)), _const_spec((1, w)), _const_spec((1, w))],
        out_specs=[pl.BlockSpec((tm, tn), lambda i, j: (cur_tile(i), out_col(i, j))),
                   pl.BlockSpec((tm, w), lambda i, j: (lru_tile(i), 0)),
                   pl.BlockSpec((1, 1, w), lambda i, j: (lru_tile(i) // tiles_per_seq, 0, 0)),
                   pl.BlockSpec((1, CONV_W - 1, w), lambda i, j: (lru_tile(i) // tiles_per_seq, 0, 0))],
        out_shape=[jax.ShapeDtypeStruct((t, n_j * tn), BF16),
                   jax.ShapeDtypeStruct((t, w), BF16),
                   jax.ShapeDtypeStruct((t // seq_len, 1, w), F32),
                   jax.ShapeDtypeStruct((t // seq_len, CONV_W - 1, w), F32)],
        scratch_shapes=[pltpu.VMEM((tm, d), BF16), pltpu.VMEM((tm, 2 * w), BF16), pltpu.VMEM((tm, 2 * w), BF16),
                        pltpu.VMEM((SUBLANES, w), F32), pltpu.VMEM((1, w), F32)],
        compiler_params=_cparams(("arbitrary", "arbitrary")),
        name="inproj_lru",
    )(x, g, w_head, w_tail, lw["lru_conv_w"], lw["lru_conv_b"], lw["wax"], lw["lru_ba"], lw["lru_bx"],
      lw["lru_lambda"])


def _lru_step_body(xin_ref, gate_ref, buf_ref, h0_ref, cw_ref, cb_ref, wax_ref, ba_ref, bx_ref, lam_ref,
                   ya_ref, hlast_ref, bufout_ref, *, steps):
    bsz = h0_ref.shape[0]
    n = steps * bsz
    x = xin_ref[...].astype(F32)
    xx = jnp.concatenate([buf_ref[...], x], axis=0)
    cw = cw_ref[...]
    u = cb_ref[...] + sum(xx[k * bsz:k * bsz + n] * cw[k:k + 1] for k in range(CONV_W))
    a, i, mult = _lru_gates(u, wax_ref, ba_ref[...], bx_ref[...], lam_ref[...])
    v = u * i * mult
    h = h0_ref[...]
    for t in range(steps):
        sl = slice(t * bsz, (t + 1) * bsz)
        h = a[sl] * h + v[sl]
        ya_ref[sl, :] = (h * jax.nn.gelu(gate_ref[sl, :].astype(F32))).astype(ya_ref.dtype)
    hlast_ref[...] = h
    bufout_ref[...] = xx[steps * bsz:(steps + CONV_W - 1) * bsz]


def _lru_step(xin_t, gate_t, buf_t, h0, lw, steps):
    bsz, w = h0.shape
    body = functools.partial(_lru_step_body, steps=steps)
    return pl.pallas_call(
        body,
        out_shape=[jax.ShapeDtypeStruct((steps * bsz, w), BF16),
                   jax.ShapeDtypeStruct((bsz, w), F32),
                   jax.ShapeDtypeStruct(((CONV_W - 1) * bsz, w), F32)],
        compiler_params=pltpu.CompilerParams(vmem_limit_bytes=VMEM_LIMIT),
        name="lru_step",
    )(xin_t, gate_t, buf_t, h0, lw["lru_conv_w"], lw["lru_conv_b"], lw["wax"],
      lw["lru_ba"], lw["lru_bx"], lw["lru_lambda"])


def _expand_heads(cols, e2):
    q = cols[0].shape[0]
    v = jnp.concatenate(cols, axis=0)
    hi = v.astype(BF16)
    lo = (v - hi.astype(F32)).astype(BF16)
    out = _dot(jnp.concatenate([hi, lo], axis=1), e2)
    return [out[i * q:(i + 1) * q] for i in range(len(cols))]


def _ssd_chunk(xc, dt, p, e2, s_get, s_set, t_col, t_row, n_seg=1, n_valid=None):
    q = xc.shape[0]
    rps = q // n_seg
    gn = SSD_GROUPS * SSD_STATE
    d_inner = xc.shape[1] - 2 * gn
    hpg = d_inner // SSD_HEADDIM // SSD_GROUPS
    gw = hpg * SSD_HEADDIM

    causal = t_col >= t_row
    if n_seg > 1:
        same = (lax.broadcasted_iota(jnp.int32, (q, 1), 0) // rps) == (lax.broadcasted_iota(jnp.int32, (1, q), 1) // rps)
        causal = jnp.logical_and(same, causal)
    if n_valid is not None:
        dt = jnp.where(lax.broadcasted_iota(jnp.int32, dt.shape, 0) % rps < n_valid, dt, 0.0)
    a = dt * p["A"]
    cum = _dot_f32(causal.astype(F32), a)
    cum_t = cum.T
    total = cum[q - 1:q, :] if n_seg == 1 else _dot_f32(same.astype(F32), a)
    dt_x, to_end_x, ecum_x = _expand_heads([dt, jnp.exp(total - cum), jnp.exp(cum)], e2)

    xs = xc[:, :d_inner]
    xdt = xs * dt_x
    xw = xdt * to_end_x
    packed = rps % (2 * SUBLANES) == 0
    xdt_m = xdt.astype(BF16) if packed else xdt
    if packed:
        xw = xw.astype(BF16)
    lane_head = lax.broadcasted_iota(jnp.int32, (1, gw), 1) // SSD_HEADDIM
    y_groups = []
    for g in range(SSD_GROUPS):
        sl = slice(g * gw, (g + 1) * gw)
        bg = xc[:, d_inner + g * SSD_STATE:d_inner + (g + 1) * SSD_STATE]
        cg = xc[:, d_inner + gn + g * SSD_STATE:d_inner + gn + (g + 1) * SSD_STATE]
        if packed:
            bg, cg = bg.astype(BF16), cg.astype(BF16)
        cb = _dot_nt(cg.astype(BF16), bg.astype(BF16))
        m_heads, x_heads = [], []
        for hh in range(hpg):
            h = g * hpg + hh
            decay = jnp.exp(jnp.where(causal, cum[:, h:h + 1] - cum_t[h:h + 1, :], -jnp.inf))
            m_heads.append((cb * decay).astype(BF16))
            x_heads.append(jnp.where(lane_head == hh, xdt_m[:, sl], 0.0))
        y_diag = _dot(jnp.concatenate(m_heads, axis=1), jnp.concatenate(x_heads, axis=0).astype(BF16))
        y_off = []
        for b in range(n_seg):
            rows = slice(b * rps, (b + 1) * rps)
            s_old = s_get(b, g)
            y_off.append(_dot_nt(cg[rows].astype(BF16), s_old.astype(BF16)))
            s_dec = [s_old[hh * SSD_HEADDIM:(hh + 1) * SSD_HEADDIM, :]
                     * jnp.exp(cum_t[g * hpg + hh:g * hpg + hh + 1, (b + 1) * rps - 1:(b + 1) * rps])
                     for hh in range(hpg)]
            s_set(b, g, jnp.concatenate(s_dec, axis=0) + _dot_tn(xw[rows, sl].astype(BF16), bg[rows].astype(BF16)))
        y_off = y_off[0] if n_seg == 1 else jnp.concatenate(y_off, axis=0)
        y_groups.append(y_diag + y_off * ecum_x[:, sl])
    return jnp.concatenate(y_groups, axis=1) + p["D"] * xs


def _ssd_gate_norm(y, z, p):
    gw = y.shape[1] // SSD_GROUPS
    zf = z.astype(F32)
    out = []
    for g in range(SSD_GROUPS):
        sl = slice(g * gw, (g + 1) * gw)
        v = y[:, sl] * (zf[:, sl] * jax.nn.sigmoid(zf[:, sl]))
        out.append(v * lax.rsqrt(jnp.mean(v * v, axis=-1, keepdims=True) + EPS) * p["norm_g"][:, sl])
    return jnp.concatenate(out, axis=1)


def _ssd_conv(xpad, q, cw, cb):
    base = SUBLANES - (CONV_W - 1)
    y = cb + sum(xpad[base + k:base + k + q] * cw[k:k + 1] for k in range(CONV_W))
    return y * jax.nn.sigmoid(y)


def _softplus(x):
    return jax.nn.softplus(x)


def _ssd_params(cw_ref, cb_ref, dtb_ref, a_ref, d_ref, ng_ref):
    return {"cw": cw_ref[...], "cb": cb_ref[...], "dt_bias": dtb_ref[...], "A": a_ref[...],
            "D": d_ref[...], "norm_g": ng_ref[...]}


def _ssd_prompt_body(xbc_ref, z_ref, dt_ref, cw_ref, cb_ref, dtb_ref, a_ref, d_ref, ng_ref, e2_ref,
                     yb_ref, sout_ref, bufout_ref, x_scr, dt_scr, y_scr, s_scr, *, q):
    c = pl.program_id(1)
    rows = xbc_ref.shape[1]
    half = q // 2
    n_xslab = x_scr.shape[0]
    base = SUBLANES - (CONV_W - 1)

    @pl.when(c == 0)
    def _():
        x_scr[:, 0:SUBLANES, :] = jnp.zeros((n_xslab, SUBLANES, LANES), F32)
        s_scr[...] = jnp.zeros_like(s_scr)

    @pl.when(c > 0)
    def _():
        x_scr[:, 0:SUBLANES, :] = x_scr[:, rows:rows + SUBLANES, :]

    p = _ssd_params(cw_ref, cb_ref, dtb_ref, a_ref, d_ref, ng_ref)
    x = xbc_ref[0].astype(F32)
    for j in range(n_xslab):
        x_scr[j, SUBLANES:SUBLANES + rows, :] = x[:, j * LANES:(j + 1) * LANES]
    dt_scr[...] = _softplus(dt_ref[0].astype(F32) + p["dt_bias"])

    def times(shape, axis):
        pos = lax.broadcasted_iota(jnp.int32, shape, axis)
        return jnp.where(pos < half, 2 * pos, 2 * (pos - half) + 1)

    def s_set(b, g, v):
        s_scr[g] = v

    for ch in range(rows // q):
        r0 = ch * q
        cols = []
        for j in range(n_xslab):
            ls = slice(j * LANES, (j + 1) * LANES)
            halves = []
            for par in range(2):
                acc = p["cb"][:, ls]
                for k in range(CONV_W):
                    acc = acc + x_scr[j, pl.ds(r0 + base + k + par, half, stride=2), :] * p["cw"][k:k + 1, ls]
                halves.append(acc)
            cols.append(jnp.concatenate(halves, axis=0))
        xc = jnp.concatenate(cols, axis=1)
        xc = xc * jax.nn.sigmoid(xc)
        dt = jnp.concatenate([dt_scr[pl.ds(r0 + par, half, stride=2), :] for par in range(2)], axis=0)
        y = _ssd_chunk(xc, dt, p, e2_ref[...], lambda b, g: s_scr[g], s_set, times((q, 1), 0), times((1, q), 1))
        for j in range(y_scr.shape[0]):
            for par in range(2):
                y_scr[j, pl.ds(par, half, stride=2), :] = y[par * half:(par + 1) * half, j * LANES:(j + 1) * LANES]
        y = jnp.concatenate([y_scr[j] for j in range(y_scr.shape[0])], axis=1)
        yb_ref[0, r0:r0 + q, :] = _ssd_gate_norm(y, z_ref[0, r0:r0 + q, :], p).astype(yb_ref.dtype)

    @pl.when(c == pl.num_programs(1) - 1)
    def _():
        sout_ref[0] = s_scr[...]
        bufout_ref[0] = x[rows - (CONV_W - 1):rows]


def _ssd_prompt(proj3, sp, col_xbc, col_z, col_dt):
    b, l, _ = proj3.shape
    cdim = sp["ssd_conv_w"].shape[1]
    d_inner = sp["ssd_norm_g"].shape[1]
    gw = d_inner // SSD_GROUPS
    q = SSD_CHUNK if l % SSD_CHUNK == 0 else l
    assert q % (2 * SUBLANES) == 0
    rows = q * SSD_CHUNKS_PER_STEP if l % (q * SSD_CHUNKS_PER_STEP) == 0 else q
    return pl.pallas_call(
        functools.partial(_ssd_prompt_body, q=q),
        grid=(b, l // rows),
        in_specs=[pl.BlockSpec((1, rows, cdim), lambda i, c: (i, c, col_xbc // cdim)),
                  pl.BlockSpec((1, rows, d_inner), lambda i, c: (i, c, col_z // d_inner)),
                  pl.BlockSpec((1, rows, LANES), lambda i, c: (i, c, col_dt // LANES)),
                  _const_spec((CONV_W, cdim)), _const_spec((1, cdim)), _const_spec((1, LANES)),
                  _const_spec((1, LANES)), _const_spec((1, d_inner)), _const_spec((1, d_inner)),
                  _const_spec((2 * LANES, d_inner))],
        out_specs=[pl.BlockSpec((1, rows, d_inner), lambda i, c: (i, c, 0)),
                   pl.BlockSpec((1, SSD_GROUPS, gw, SSD_STATE), lambda i, c: (i, 0, 0, 0)),
                   pl.BlockSpec((1, CONV_W - 1, cdim), lambda i, c: (i, 0, 0))],
        out_shape=[jax.ShapeDtypeStruct((b, l, d_inner), BF16),
                   jax.ShapeDtypeStruct((b, SSD_GROUPS, gw, SSD_STATE), F32),
                   jax.ShapeDtypeStruct((b, CONV_W - 1, cdim), F32)],
        scratch_shapes=[pltpu.VMEM((cdim // LANES, rows + SUBLANES, LANES), F32),
                        pltpu.VMEM((rows, LANES), F32),
                        pltpu.VMEM((d_inner // LANES, q, LANES), F32),
                        pltpu.VMEM((SSD_GROUPS, gw, SSD_STATE), F32)],
        compiler_params=_cparams(("parallel", "arbitrary")),
        name="ssd_prompt",
    )(proj3, proj3, proj3, sp["ssd_conv_w"], sp["ssd_conv_b"], sp["dt_bias"], sp["A"], sp["D"], sp["ssd_norm_g"],
      sp["head_expand"])


def _ssd_step_body(xbc_ref, z_ref, dt_ref, buf_ref, s_ref, cw_ref, cb_ref, dtb_ref, a_ref, d_ref, ng_ref, e2_ref,
                   yb_ref, sout_ref, bufout_ref, *, steps, nb):
    p = _ssd_params(cw_ref, cb_ref, dtb_ref, a_ref, d_ref, ng_ref)
    rps = SUBLANES
    q = nb * rps
    x_all = xbc_ref[...].astype(F32)
    z_all = z_ref[...].astype(F32)
    dt_all = _softplus(dt_ref[...].astype(F32) + p["dt_bias"])
    cdim = x_all.shape[1]

    def padded(v, j):
        return jnp.concatenate([v[j * steps:(j + 1) * steps], jnp.zeros((rps - steps, v.shape[1]), v.dtype)], axis=0)

    xcs = []
    for j in range(nb):
        xpad = jnp.concatenate([jnp.zeros((SUBLANES - (CONV_W - 1), cdim), F32), buf_ref[j], padded(x_all, j)],
                               axis=0)
        xcs.append(_ssd_conv(xpad, rps, p["cw"], p["cb"]))
        bufout_ref[j] = xpad[SUBLANES + steps - (CONV_W - 1):SUBLANES + steps]
    xc = jnp.concatenate(xcs, axis=0)
    dt = jnp.concatenate([padded(dt_all, j) for j in range(nb)], axis=0)
    z = jnp.concatenate([padded(z_all, j) for j in range(nb)], axis=0)

    def s_set(b, g, v):
        sout_ref[b, g] = v

    t_col = lax.broadcasted_iota(jnp.int32, (q, 1), 0) % rps
    t_row = lax.broadcasted_iota(jnp.int32, (1, q), 1) % rps
    y = _ssd_chunk(xc, dt, p, e2_ref[...], lambda b, g: s_ref[b, g], s_set, t_col, t_row, n_seg=nb, n_valid=steps)
    y = _ssd_gate_norm(y, z, p)
    for j in range(nb):
        yb_ref[j * steps:(j + 1) * steps, :] = y[j * rps:j * rps + steps].astype(yb_ref.dtype)


def _ssd_step(proj, buf, s0, sp, steps, nb, col_xbc, col_z, col_dt):
    bsz = s0.shape[0]
    cdim = sp["ssd_conv_w"].shape[1]
    d_inner = sp["ssd_norm_g"].shape[1]
    gw = d_inner // SSD_GROUPS
    rows = nb * steps
    body = functools.partial(_ssd_step_body, steps=steps, nb=nb)
    return pl.pallas_call(
        body,
        grid=(bsz // nb,),
        in_specs=[pl.BlockSpec((rows, cdim), lambda i: (i, col_xbc // cdim)),
                  pl.BlockSpec((rows, d_inner), lambda i: (i, col_z // d_inner)),
                  pl.BlockSpec((rows, LANES), lambda i: (i, col_dt // LANES)),
                  pl.BlockSpec((None, nb, CONV_W - 1, cdim), lambda i: (0, i, 0, 0)),
                  pl.BlockSpec((nb, SSD_GROUPS, gw, SSD_STATE), lambda i: (i, 0, 0, 0)),
                  _const_spec((CONV_W, cdim)), _const_spec((1, cdim)), _const_spec((1, LANES)),
                  _const_spec((1, LANES)), _const_spec((1, d_inner)), _const_spec((1, d_inner)),
                  _const_spec((2 * LANES, d_inner))],
        out_specs=[pl.BlockSpec((rows, d_inner), lambda i: (i, 0)),
                   pl.BlockSpec((nb, SSD_GROUPS, gw, SSD_STATE), lambda i: (i, 0, 0, 0)),
                   pl.BlockSpec((None, nb, CONV_W - 1, cdim), lambda i: (0, i, 0, 0))],
        out_shape=[jax.ShapeDtypeStruct((bsz * steps, d_inner), BF16),
                   jax.ShapeDtypeStruct(s0.shape, F32),
                   jax.ShapeDtypeStruct(buf.shape, F32)],
        compiler_params=_cparams(("parallel",)),
        name="ssd_step",
    )(proj, proj, proj, buf, s0, sp["ssd_conv_w"], sp["ssd_conv_b"], sp["dt_bias"], sp["A"], sp["D"],
      sp["ssd_norm_g"], sp["head_expand"])


def _router(t, wr, br):
    t_hi = t.astype(BF16)
    t_lo = (t - t_hi.astype(F32)).astype(BF16)
    both = _dot(jnp.concatenate([t_hi, t_lo], axis=1), wr)
    logits = both[:, :LANES] + both[:, LANES:] + br
    lane = lax.broadcasted_iota(jnp.int32, logits.shape, 1)
    neg = -jnp.inf
    gl = jnp.where(lane < N_EGROUPS, logits, neg)
    gmax = jnp.max(gl, axis=-1, keepdims=True)
    g_idx = jnp.min(jnp.where(gl == gmax, lane, LANES), axis=-1, keepdims=True)
    g_w = 1.0 / jnp.sum(jnp.exp(gl - gmax), axis=-1, keepdims=True)
    in_grp = jnp.logical_and(jnp.logical_and(lane >= ROUTER_LANE0, lane < ROUTER_LANE0 + N_EXPERTS),
                             ((lane - ROUTER_LANE0) >> 2) == g_idx)
    el = jnp.where(in_grp, logits, neg)
    pe = jnp.exp(el - jnp.max(el, axis=-1, keepdims=True))
    pe = pe / jnp.sum(pe, axis=-1, keepdims=True)
    cand = jnp.where(in_grp, pe, -1.0)
    v1 = jnp.max(cand, axis=-1, keepdims=True)
    i1 = jnp.min(jnp.where(cand == v1, lane, LANES), axis=-1, keepdims=True)
    cand2 = jnp.where(lane == i1, -1.0, cand)
    v2 = jnp.max(cand2, axis=-1, keepdims=True)
    i2 = jnp.min(jnp.where(jnp.logical_and(cand2 == v2, in_grp), lane, LANES), axis=-1, keepdims=True)
    den = v1 + v2
    return lane, i1, i2, g_w * v1 / den, g_w * v2 / den


def _rows_to_tiles(ref, val):
    n, d = val.shape
    for k in range(d // LANES):
        ref[pl.ds(k, n, stride=d // LANES), :] = val[:, k * LANES:(k + 1) * LANES]


def _tiles_to_rows(ref, n, d, start=0):
    return jnp.concatenate([ref[pl.ds(start + k, n, stride=d // LANES), :] for k in range(d // LANES)], axis=1)


def _merge_body(x_ref, ya_ref, yb_ref, ga_ref, gb_ref, wl_ref, ws_ref, wo_ref, gf_ref, wr_ref, br_ref,
                x1_ref, t_ref, rt_ref, rtt_ref, cnt_ref, base_scr):
    step = pl.program_id(0)

    @pl.when(step == 0)
    def _():
        base_scr[...] = jnp.zeros_like(base_scr)

    a = _dot(ya_ref[...], wl_ref[...])
    b = _dot(yb_ref[...], ws_ref[...])
    merged = jax.nn.sigmoid(ga_ref[...].astype(F32)) * a + jax.nn.sigmoid(gb_ref[...].astype(F32)) * b
    x1 = x_ref[...] + _dot(merged.astype(BF16), wo_ref[...])
    x1_ref[...] = x1
    t = _rms(x1, gf_ref[...])
    _rows_to_tiles(t_ref, t)
    lane, i1, i2, wg1, wg2 = _router(t, wr_ref[...], br_ref[...])

    tm = t.shape[0]
    onehot = jnp.where(jnp.logical_or(lane == i1, lane == i2), 1.0, 0.0).astype(BF16)
    tri = (lax.broadcasted_iota(jnp.int32, (tm, tm), 1) <= lax.broadcasted_iota(jnp.int32, (tm, tm), 0)).astype(BF16)
    cum = _dot(tri, onehot) + base_scr[...]
    r1 = jnp.sum(jnp.where(lane == i1, cum, 0.0), axis=-1, keepdims=True) - 1.0
    r2 = jnp.sum(jnp.where(lane == i2, cum, 0.0), axis=-1, keepdims=True) - 1.0
    cols = (wg1, wg2, r1, r2, (i1 - ROUTER_LANE0).astype(F32), (i2 - ROUTER_LANE0).astype(F32))
    rt = jnp.zeros(cum.shape, F32)
    for k, c in enumerate(cols):
        rt = jnp.where(lane == k, c, rt)
    rt_ref[...] = rt
    rtt_ref[...] = rt.T[0:SUBLANES, :]
    base_scr[...] = cum[tm - 1:tm, :]
    cnt_ref[...] = cum[tm - 1:tm, :]


def _merge(x, ya, yb, proj, mw, tm, col_ga, col_gb):
    t, d = x.shape
    d_inner = yb.shape[1]
    return pl.pallas_call(
        _merge_body,
        grid=(t // tm,),
        in_specs=[pl.BlockSpec((tm, d), lambda i: (i, 0)),
                  pl.BlockSpec((tm, d), lambda i: (i, 0)),
                  pl.BlockSpec((tm, d_inner), lambda i: (i, 0)),
                  pl.BlockSpec((tm, d), lambda i: (i, col_ga // d)),
                  pl.BlockSpec((tm, d), lambda i: (i, col_gb // d)),
                  _const_spec((d, d)), _const_spec((d_inner, d)), _const_spec((d, d)),
                  _const_spec((1, d)), _const_spec((2 * d, 2 * LANES)), _const_spec((1, LANES))],
        out_specs=[pl.BlockSpec((tm, d), lambda i: (i, 0)),
                   pl.BlockSpec((tm * d // LANES, LANES), lambda i: (i, 0)),
                   pl.BlockSpec((tm, LANES), lambda i: (i, 0)),
                   pl.BlockSpec((SUBLANES, tm), lambda i: (0, i)),
                   pl.BlockSpec((1, LANES), lambda i: (0, 0))],
        out_shape=[jax.ShapeDtypeStruct((t, d), F32),
                   jax.ShapeDtypeStruct((t * d // LANES, LANES), F32),
                   jax.ShapeDtypeStruct((t, LANES), F32),
                   jax.ShapeDtypeStruct((SUBLANES, t), F32),
                   jax.ShapeDtypeStruct((1, LANES), F32)],
        scratch_shapes=[pltpu.VMEM((1, LANES), F32)],
        compiler_params=_cparams(("arbitrary",)),
        name="merge_router",
    )(x, ya, yb, proj, proj, mw["w_br_lru"], mw["w_br_ssd"], mw["w_out"], mw["g_ffn"], mw["w_router"],
      mw["b_router"])


def _dispatch_body(dest_ref, zb_ref, t_ref, o_ref, zero_scr, sem, zsem, *, tm, nk, tmg, n_tok):
    step = pl.program_id(0)

    @pl.when(step == 0)
    def _():
        zero_scr[...] = jnp.zeros_like(zero_scr)
        blk = tmg * nk

        def zcopy(j):
            return pltpu.make_async_copy(zero_scr, o_ref.at[pl.ds(pl.multiple_of(zb_ref[j] * blk, blk), blk)], zsem)

        for j in range(zb_ref.shape[0]):
            pl.when(zb_ref[j] >= 0)(lambda j=j: zcopy(j).start())
        for j in range(zb_ref.shape[0]):
            pl.when(zb_ref[j] >= 0)(lambda j=j: zcopy(j).wait())

    def issue(r, carry):
        src = t_ref.at[pl.ds(pl.multiple_of(r * nk, nk), nk)]
        for k in range(2):
            row = dest_ref[k * n_tok + step * tm + r]
            pltpu.make_async_copy(src, o_ref.at[pl.ds(pl.multiple_of(row * nk, nk), nk)], sem).start(priority=k)
        return carry

    lax.fori_loop(0, tm, issue, 0, unroll=8)
    for k in range(2):
        pltpu.make_async_copy(t_ref, o_ref.at[pl.ds(0, tm * nk)], sem).wait()


def _dispatch(dest, zero_blocks, t_tiles, n_tok, tm, n_rows, tmg):
    nk = t_tiles.shape[0] // n_tok
    return pl.pallas_call(
        functools.partial(_dispatch_body, tm=tm, nk=nk, tmg=tmg, n_tok=n_tok),
        grid_spec=pltpu.PrefetchScalarGridSpec(
            num_scalar_prefetch=2,
            grid=(n_tok // tm,),
            in_specs=[pl.BlockSpec((tm * nk, LANES), lambda i, *_: (i, 0))],
            out_specs=pl.BlockSpec(memory_space=pl.ANY),
            scratch_shapes=[pltpu.VMEM((tmg * nk, LANES), F32), pltpu.SemaphoreType.DMA(()),
                            pltpu.SemaphoreType.DMA(())]),
        out_shape=jax.ShapeDtypeStruct((n_rows * nk, LANES), F32),
        compiler_params=_cparams(("arbitrary",)),
        name="moe_dispatch",
    )(dest, zero_blocks, t_tiles)


def _expert_body(te_ref, nt_ref, x_ref, w1_ref, w3_ref, w2_ref, y_ref, w1_scr, w3_scr, w2_scr, *, tmg):
    i = pl.program_id(0)
    real = i < nt_ref[0]
    d = w1_scr.shape[0]

    @pl.when(jnp.logical_or(i == 0, te_ref[i] != te_ref[jnp.maximum(i - 1, 0)]))
    def _():
        w1_scr[...] = w1_ref[0].astype(BF16)
        w3_scr[...] = w3_ref[0].astype(BF16)
        w2_scr[...] = w2_ref[0].astype(BF16)

    @pl.when(real)
    def _():
        x = _tiles_to_rows(x_ref, tmg, d).astype(BF16)
        h1 = _dot(x, w1_scr[...])
        h3 = _dot(x, w3_scr[...])
        _rows_to_tiles(y_ref, _dot((h1 * jax.nn.sigmoid(h1) * h3).astype(BF16), w2_scr[...]))

    @pl.when(jnp.logical_not(real))
    def _():
        y_ref[...] = jnp.zeros_like(y_ref)


def _experts(tile_expert, n_tiles, xs_tiles, w1, w3, w2, tmg):
    _, d, dff = w1.shape
    blk = tmg * d // LANES
    row_spec = pl.BlockSpec((blk, LANES), lambda i, te, nt: (i, 0))
    return pl.pallas_call(
        functools.partial(_expert_body, tmg=tmg),
        grid_spec=pltpu.PrefetchScalarGridSpec(
            num_scalar_prefetch=2,
            grid=(xs_tiles.shape[0] // blk,),
            in_specs=[row_spec,
                      pl.BlockSpec((1, d, dff), lambda i, te, nt: (te[i], 0, 0)),
                      pl.BlockSpec((1, d, dff), lambda i, te, nt: (te[i], 0, 0)),
                      pl.BlockSpec((1, dff, d), lambda i, te, nt: (te[i], 0, 0))],
            out_specs=row_spec,
            scratch_shapes=[pltpu.VMEM((d, dff), BF16), pltpu.VMEM((d, dff), BF16), pltpu.VMEM((dff, d), BF16)]),
        out_shape=jax.ShapeDtypeStruct(xs_tiles.shape, F32),
        compiler_params=_cparams(("arbitrary",)),
        name="moe_experts",
    )(tile_expert, n_tiles, xs_tiles, w1, w3, w2)


def _ple_body(dest_ref, x_ref, rt_ref, p_ref, wp_ref, gp_ref, gg_ref, wg_ref, gfin_ref, y_hbm, o_ref, gbuf, sem,
              *, n_tok):
    step = pl.program_id(0)
    n_steps = pl.num_programs(0)
    tm, d = x_ref.shape
    nk = d // LANES
    pr = tm // COMBINE_PIECES

    def issue_row(tile, slot, r):
        tok = tile * tm + r
        for k in range(2):
            row = dest_ref[k * n_tok + tok]
            pltpu.make_async_copy(y_hbm.at[pl.ds(pl.multiple_of(row * nk, nk), nk)],
                                  gbuf.at[slot, pl.ds(pl.multiple_of((k * tm + r) * nk, nk), nk)],
                                  sem.at[slot]).start(priority=k)

    def wait_slot(slot):
        pltpu.make_async_copy(y_hbm.at[pl.ds(0, 2 * tm * nk)], gbuf.at[slot], sem.at[slot]).wait()

    @pl.when(step == 0)
    def _():
        def issue(r, carry):
            issue_row(0, 0, r)
            return carry

        lax.fori_loop(0, tm, issue, 0, unroll=8)

    slot = step % 2
    wait_slot(slot)
    nxt_tile = jnp.minimum(step + 1, n_steps - 1)
    rows = gbuf.at[slot]
    for q in range(COMBINE_PIECES):
        for r in range(q * pr, (q + 1) * pr):
            issue_row(nxt_tile, 1 - slot, r)
        sl = slice(q * pr, (q + 1) * pr)
        rt = rt_ref[sl, :]
        x = (x_ref[sl, :] + rt[:, 0:1] * _tiles_to_rows(rows, pr, d, start=q * pr * nk)
             + rt[:, 1:2] * _tiles_to_rows(rows, pr, d, start=(tm + q * pr) * nk))
        e = _rms(_dot(p_ref[sl, :].astype(BF16), wp_ref[...]), gp_ref[...])
        gate = jax.nn.sigmoid(_dot(_rms(x, gg_ref[...]).astype(BF16), wg_ref[...]))
        o_ref[sl, :] = _rms(x + gate * e, gfin_ref[...])

    pl.when(step == n_steps - 1)(lambda: wait_slot(1 - slot))


def _ple(dest, x1, rt, p, y_sorted, pw, tm):
    n, d = x1.shape
    dp = p.shape[1]
    const = lambda shape: pl.BlockSpec(shape, lambda i, *_: (0,) * len(shape))
    return pl.pallas_call(
        functools.partial(_ple_body, n_tok=n),
        grid_spec=pltpu.PrefetchScalarGridSpec(
            num_scalar_prefetch=1,
            grid=(n // tm,),
            in_specs=[pl.BlockSpec((tm, d), lambda i, *_: (i, 0)),
                      pl.BlockSpec((tm, LANES), lambda i, *_: (i, 0)),
                      pl.BlockSpec((tm, dp), lambda i, *_: (i, 0)),
                      const((dp, d)), const((1, d)), const((1, d)), const((d, d)), const((1, d)),
                      pl.BlockSpec(memory_space=pl.ANY)],
            out_specs=pl.BlockSpec((tm, d), lambda i, *_: (i, 0)),
            scratch_shapes=[pltpu.VMEM((2, 2 * tm * d // LANES, LANES), F32), pltpu.SemaphoreType.DMA((2,))]),
        out_shape=jax.ShapeDtypeStruct((n, d), F32),
        compiler_params=_cparams(("arbitrary",)),
        name="combine_ple_final",
    )(dest, x1, rt, p, pw["w_ple_proj"], pw["g_ple"], pw["g_ple_gate"], pw["w_ple_gate"], pw["g_final"], y_sorted)


def _pick_tile(n, pref):
    t = min(n, pref)
    while n % t:
        t //= 2
    return t


def _moe_row_tile(n):
    return 512 if 2 * n // N_EXPERTS >= 1024 else 128
MOE_DISPATCH_TILE = 2048
SSD_CHUNKS_PER_STEP = 4
COMBINE_PIECES = 4
LRU_SLICE_ROWS = 128
LRU_PIECES = 4


def _split_router(w):
    hi = w.astype(BF16)
    lo = (w - hi.astype(F32)).astype(BF16)
    return jnp.concatenate([jnp.concatenate([hi, lo], axis=1),
                            jnp.concatenate([hi, jnp.zeros_like(hi)], axis=1)], axis=0)


def _token_tail(x1, t_tiles, rt, rtt, cnt, p, lw, tm):
    n = x1.shape[0]
    tmg = _moe_row_tile(n)
    n_blocks = pl.cdiv(2 * n, tmg) + N_EXPERTS
    counts = cnt[0, ROUTER_LANE0:ROUTER_LANE0 + N_EXPERTS].astype(jnp.int32)
    tiles = (counts + tmg - 1) // tmg
    ends = jnp.cumsum(tiles)
    n_tiles = ends[-1]
    e_idx = rtt[4:6].astype(jnp.int32)
    first_row = (ends - tiles) * tmg
    dest = rtt[2:4].astype(jnp.int32) + sum(jnp.where(e_idx == e, first_row[e], 0) for e in range(N_EXPERTS))
    dest = dest.reshape(2 * n)
    blk = jnp.arange(n_blocks, dtype=jnp.int32)
    tile_expert = jnp.sum((jnp.minimum(blk, n_tiles - 1)[:, None] >= ends[None, :]).astype(jnp.int32), axis=1)
    tail = n_tiles + blk[:N_EXPERTS]
    zero_blocks = jnp.concatenate([jnp.where(tiles > 0, ends - 1, -1),
                                   jnp.where(tail < n_blocks, tail, -1)]).astype(jnp.int32)

    sorted_t = _dispatch(dest, zero_blocks, t_tiles, n, _pick_tile(n, MOE_DISPATCH_TILE), n_blocks * tmg, tmg)
    y_sorted = _experts(tile_expert, n_tiles.reshape(1), sorted_t, lw["w1"], lw["w3"], lw["w2"], tmg)
    return _ple(dest, x1, rt, p, y_sorted, lw, tm)


def kernel(x_prompt, x_sample, state_lru_h, state_lru_conv, state_ssd, state_ssd_conv, p_prompt, p_sample, g_mix, w_in, lru_conv_w, lru_conv_b, lru_wa, lru_ba, lru_wx, lru_bx, lru_lambda, ssd_conv_w, ssd_conv_b, ssd_dt_bias, ssd_A_log, ssd_D, ssd_norm_g, w_br_lru, w_br_ssd, w_out, g_ffn, w_router_g, b_router_g, w_router_e, b_router_e, w1, w3, w2, w_ple_proj, g_ple, g_ple_gate, w_ple_gate, g_final):
    depth = w_in.shape[0]
    assert depth == 1, "one decoder layer per call"
    bp, lp, d = x_prompt.shape
    bs, ls, _ = x_sample.shape
    w_lru = state_lru_h.shape[-1]
    heads, hdim, nstate = state_ssd.shape[2:]
    d_inner = heads * hdim
    cdim = state_ssd_conv.shape[-1]
    assert hdim == SSD_HEADDIM and nstate == SSD_STATE and heads <= LANES and ls < SUBLANES
    gw = d_inner // SSD_GROUPS

    o_dt = 2 * w_lru + d_inner + cdim
    n_proj = o_dt + 2 * d + LANES
    tn = n_proj // 9 if n_proj % (9 * LANES) == 0 else LANES
    n_head = o_dt // tn
    w_head = w_in.astype(BF16)
    wi = w_head[0]
    w_tail = jnp.concatenate([wi[:, n_head * tn:o_dt], wi[:, o_dt + heads:], wi[:, o_dt:o_dt + heads],
                              jnp.zeros((d, LANES - heads), BF16)], axis=1)
    assert n_head >= 1 and w_tail.shape[1] == n_proj - n_head * tn and w_tail.shape[1] % tn == 0
    col_z, col_xbc = 2 * w_lru, 2 * w_lru + d_inner
    col_ga, col_gb, col_dt = o_dt, o_dt + d, o_dt + 2 * d
    row = lambda v: v.reshape(1, -1).astype(F32)
    pad_heads = lambda v: jnp.pad(v.astype(F32), (0, LANES - heads)).reshape(1, LANES)
    lw = {
        "lru_conv_w": lru_conv_w[0], "lru_conv_b": row(lru_conv_b[0]),
        "wax": jnp.concatenate([lru_wa[0], lru_wx[0]], axis=-1).astype(BF16),
        "lru_ba": row(lru_ba[0]), "lru_bx": row(lru_bx[0]), "lru_lambda": row(lru_lambda[0]),
        "ssd_conv_w": ssd_conv_w[0], "ssd_conv_b": row(ssd_conv_b[0]),
        "dt_bias": pad_heads(ssd_dt_bias[0]), "A": pad_heads(-jnp.exp(ssd_A_log[0].astype(F32))),
        "D": row(jnp.repeat(ssd_D[0], hdim)), "ssd_norm_g": row(ssd_norm_g[0]),
        "head_expand": jnp.tile(jnp.arange(LANES)[:, None] == jnp.arange(d_inner)[None, :] // hdim, (2, 1)).astype(BF16),
        "w_br_lru": w_br_lru[0].astype(BF16), "w_br_ssd": w_br_ssd[0].astype(BF16), "w_out": w_out[0].astype(BF16),
        "g_ffn": row(g_ffn[0]),
        "w_router": _split_router(jnp.concatenate([w_router_g[0], w_router_e[0],
                                                   jnp.zeros((d, LANES - N_EGROUPS - N_EXPERTS), F32)], axis=1)),
        "b_router": jnp.concatenate([b_router_g[0], b_router_e[0],
                                     jnp.zeros((LANES - N_EGROUPS - N_EXPERTS,), F32)]).reshape(1, LANES),
        "w1": w1[0], "w3": w3[0], "w2": w2[0],
        "w_ple_proj": w_ple_proj[0].astype(BF16), "g_ple": row(g_ple[0]), "g_ple_gate": row(g_ple_gate[0]),
        "w_ple_gate": w_ple_gate[0].astype(BF16), "g_final": row(g_final),
    }
    g_mix_r = row(g_mix[0])

    tp = bp * lp
    xp = x_prompt.reshape(tp, d)
    tm_p = _pick_tile(lp, 1024)
    proj_p, ya_p, hl_p, lbuf_p = _inproj_lru(xp, g_mix_r, w_head, w_tail, lw, lp, tm_p, tn, n_head,
                                             _pick_tile(tm_p, LRU_SLICE_ROWS))
    proj_p3 = proj_p.reshape(bp, lp, n_proj)
    yb_p, s_p, sbuf_p = _ssd_prompt(proj_p3, lw, col_xbc, col_z, col_dt)
    tm_tail_p = _pick_tile(tp, 512)
    x1_p, t_p, rt_p, rtt_p, cnt_p = _merge(xp, ya_p.reshape(tp, w_lru), yb_p.reshape(tp, d_inner), proj_p, lw,
                                           tm_tail_p, col_ga, col_gb)
    y_p = _token_tail(x1_p, t_p, rt_p, rtt_p, cnt_p, p_prompt[0].reshape(tp, -1), lw, tm_tail_p)

    ts = bs * ls
    xs = x_sample.reshape(ts, d)
    tm_s = _pick_tile(ts, 512)
    proj_s = _inproj(xs, g_mix_r, w_head, w_tail, tm_s, tn, n_head)
    to_tmajor = lambda v, n: v.reshape(bs, n, -1).transpose(1, 0, 2).reshape(n * bs, -1)
    from_tmajor = lambda v, n: v.reshape(n, bs, -1).transpose(1, 0, 2)
    ya_t, hl_s, lbuf_t = _lru_step(to_tmajor(proj_s[:, :w_lru], ls), to_tmajor(proj_s[:, w_lru:2 * w_lru], ls),
                                   to_tmajor(state_lru_conv[0], CONV_W - 1), state_lru_h[0], lw, ls)
    ya_s = from_tmajor(ya_t, ls).reshape(ts, w_lru)
    lbuf_s = from_tmajor(lbuf_t, CONV_W - 1)
    yb_s, s_s, sbuf_s = _ssd_step(proj_s, state_ssd_conv, state_ssd[0].reshape(bs, SSD_GROUPS, gw, nstate), lw,
                                  ls, _pick_tile(bs, 8), col_xbc, col_z, col_dt)
    x1_s, t_s, rt_s, rtt_s, cnt_s = _merge(xs, ya_s, yb_s, proj_s, lw, tm_s, col_ga, col_gb)
    y_s = _token_tail(x1_s, t_s, rt_s, rtt_s, cnt_s, p_sample[0].reshape(ts, -1), lw, tm_s)

    return (y_p.reshape(bp, lp, d), y_s.reshape(bs, ls, d),
            hl_p.reshape(1, bp, w_lru), lbuf_p[None],
            s_p.reshape(1, bp, heads, hdim, nstate), sbuf_p[None],
            hl_s[None], lbuf_s[None],
            s_s.reshape(1, bs, heads, hdim, nstate), sbuf_s)
```

```python
import functools

import jax
import jax.numpy as jnp
from jax import lax
from jax.experimental import pallas as pl
from jax.experimental.pallas import tpu as pltpu

F32 = jnp.float32
BF16 = jnp.bfloat16

EPS = 1e-6
CONV_W = 4
LRU_BLOCKS = 8
LRU_C = 8.0
SSD_HEADDIM = 64
SSD_GROUPS = 8
SSD_STATE = 128
SSD_CHUNK = 128
N_EGROUPS = 4
EXP_PER_GROUP = 4
N_EXPERTS = N_EGROUPS * EXP_PER_GROUP

LANES = 128
SUBLANES = 8
VMEM_LIMIT = 52 * 1024 * 1024
ROUTER_LANE0 = N_EGROUPS


def _cparams(sem):
    return pltpu.CompilerParams(dimension_semantics=sem, vmem_limit_bytes=VMEM_LIMIT)


def _dot(a, b):
    return jnp.dot(a, b, preferred_element_type=F32)


def _dot_nt(a, b):
    return lax.dot_general(a, b, (((1,), (1,)), ((), ())), preferred_element_type=F32)


def _dot_tn(a, b):
    return lax.dot_general(a, b, (((0,), (0,)), ((), ())), preferred_element_type=F32)


def _dot_f32(a, b):
    return jnp.dot(a, b, precision=lax.Precision.HIGHEST, preferred_element_type=F32)


def _rms(x, g):
    return x * lax.rsqrt(jnp.mean(x * x, axis=-1, keepdims=True) + EPS) * g


def _const_spec(shape):
    nd = len(shape)
    return pl.BlockSpec(shape, lambda *_: (0,) * nd)


def _inproj_body(x_ref, g_ref, wh_ref, wt_ref, o_ref, h_scr, *, n_head):
    j = pl.program_id(1)

    @pl.when(j == 0)
    def _():
        h_scr[...] = _rms(x_ref[...], g_ref[...]).astype(BF16)

    @pl.when(j < n_head)
    def _():
        o_ref[...] = _dot(h_scr[...], wh_ref[0]).astype(o_ref.dtype)

    @pl.when(j >= n_head)
    def _():
        o_ref[...] = _dot(h_scr[...], wt_ref[...]).astype(o_ref.dtype)


def _inproj(x, g, w_head, w_tail, tm, tn, n_head):
    t, d = x.shape
    n_tail = w_tail.shape[1] // tn
    return pl.pallas_call(
        functools.partial(_inproj_body, n_head=n_head),
        grid=(t // tm, n_head + n_tail),
        in_specs=[pl.BlockSpec((tm, d), lambda i, j: (i, 0)),
                  pl.BlockSpec((1, d), lambda i, j: (0, 0)),
                  pl.BlockSpec((1, d, tn), lambda i, j: (0, 0, jnp.minimum(j, n_head - 1))),
                  pl.BlockSpec((d, tn), lambda i, j: (0, jnp.maximum(j - n_head, 0)))],
        out_specs=pl.BlockSpec((tm, tn), lambda i, j: (i, j)),
        out_shape=jax.ShapeDtypeStruct((t, (n_head + n_tail) * tn), BF16),
        scratch_shapes=[pltpu.VMEM((tm, d), BF16)],
        compiler_params=_cparams(("parallel", "arbitrary")),
        name="inproj",
    )(x, g, w_head, w_tail)


def _lru_gate_matmuls(u, wax_ref):
    bw = u.shape[1] // LRU_BLOCKS
    r_parts, i_parts = [], []
    for n in range(LRU_BLOCKS):
        ri = _dot(u[:, n * bw:(n + 1) * bw].astype(BF16), wax_ref[n])
        r_parts.append(ri[:, :bw])
        i_parts.append(ri[:, bw:])
    return jnp.concatenate(r_parts, axis=1), jnp.concatenate(i_parts, axis=1)


def _lru_gate_values(r_pre, i_pre, ba, bx, lam):
    r = jax.nn.sigmoid(r_pre + ba)
    i = jax.nn.sigmoid(i_pre + bx)
    log_a = LRU_C * r * jax.nn.log_sigmoid(lam)
    a = jnp.exp(log_a)
    m2 = -jnp.tanh(log_a) * (a * a + 1.0)
    mult = jnp.where(m2 > 0.0, m2 * lax.rsqrt(m2), 0.0)
    return a, i, mult


def _lru_gates(u, wax_ref, ba, bx, lam):
    return _lru_gate_values(*_lru_gate_matmuls(u, wax_ref), ba, bx, lam)


def _lru_conv(x, halo, seq_start, cw, cb):
    tt = x.shape[0]
    xpad = jnp.concatenate([jnp.where(seq_start, 0.0, halo), x], axis=0)
    base = SUBLANES - (CONV_W - 1)
    return cb + sum(xpad[base + k:base + k + tt] * cw[k:k + 1] for k in range(CONV_W))


def _lru_scan(u, r_pre, i_pre, gate, carry, seq_start, ba, bx, lam):
    tt, width = u.shape
    carry = jnp.where(seq_start, 0.0, carry)
    a, i, mult = _lru_gate_values(r_pre, i_pre, ba, bx, lam)
    first = jnp.logical_and(lax.broadcasted_iota(jnp.int32, a.shape, 0) == 0, seq_start)
    mult = jnp.where(first, 1.0, mult)
    a = jnp.where(first, 0.0, a)
    v = u * i * mult

    a = a.reshape(tt // SUBLANES, SUBLANES, width)
    v = v.reshape(tt // SUBLANES, SUBLANES, width)
    sub = lax.broadcasted_iota(jnp.int32, a.shape, 1)
    s = 1
    while s < SUBLANES:
        keep = sub >= s
        v = jnp.where(keep, a * pltpu.roll(v, s, axis=1) + v, v)
        a = jnp.where(keep, a * pltpu.roll(a, s, axis=1), a)
        s *= 2
    groups = []
    for g in range(tt // SUBLANES):
        hg = a[g] * carry + v[g]
        carry = hg[SUBLANES - 1:SUBLANES]
        groups.append(hg)
    h = jnp.concatenate(groups, axis=0)
    return h * jax.nn.gelu(gate), carry


def _inproj_lru_body(x_ref, g_ref, wh_ref, wt_ref, cw_ref, cb_ref, wax_ref, ba_ref, bx_ref, lam_ref,
                     o_ref, ya_ref, hlast_ref, bufout_ref, h_scr, lru_new, lru_cur, halo_scr, carry_scr,
                     *, n_head, n_j, n_tiles, tiles_per_seq, sub_rows):
    i = pl.program_id(0)
    j = pl.program_id(1)
    tm, tn = o_ref.shape
    w = ya_ref.shape[1]
    n_sub = tm // sub_rows
    prev_tile = jnp.maximum(i - 1, 0)

    @pl.when(jnp.logical_and(i == 0, j == 0))
    def _():
        lru_cur[...] = jnp.zeros_like(lru_cur)
        halo_scr[...] = jnp.zeros_like(halo_scr)
        carry_scr[...] = jnp.zeros_like(carry_scr)

    n_piece = LRU_PIECES
    piece_rows = sub_rows // n_piece
    mx_w = 2 * LANES
    col_cuts = [min(tn, mx_w * (q * (tn // mx_w) // n_piece)) for q in range(n_piece)] + [tn]

    def lru_piece(q, state):
        r0 = pl.multiple_of(j * sub_rows, sub_rows) + q * piece_rows
        seq_start = jnp.logical_and(jnp.logical_and(prev_tile % tiles_per_seq == 0, j == 0), q == 0)
        x = lru_cur[pl.ds(r0, piece_rows), 0:w].astype(F32)
        gate = lru_cur[pl.ds(r0, piece_rows), w:2 * w].astype(F32)
        halo, carry = state if state is not None else (halo_scr[...], carry_scr[...])
        u = _lru_conv(x, halo, seq_start, cw_ref[...], cb_ref[...])
        r_pre, i_pre = _lru_gate_matmuls(u, wax_ref)
        ya, carry = _lru_scan(u, r_pre, i_pre, gate, carry, seq_start, ba_ref[...], bx_ref[...], lam_ref[...])
        ya_ref[pl.ds(r0, piece_rows), :] = ya.astype(ya_ref.dtype)
        halo = x[piece_rows - SUBLANES:piece_rows]
        if q == n_piece - 1:
            halo_scr[...] = halo
            carry_scr[...] = carry

            @pl.when(jnp.logical_and(prev_tile % tiles_per_seq == tiles_per_seq - 1, j == n_sub - 1))
            def _():
                hlast_ref[0] = carry
                bufout_ref[0] = x[piece_rows - (CONV_W - 1):piece_rows]
        return halo, carry

    def project_piece(q, from_head, lru_lo):
        c0, c1 = col_cuts[q], col_cuts[q + 1]
        if c0 == c1:
            return
        w_cols = wh_ref[0, :, c0:c1] if from_head else wt_ref[:, c0:c1]
        o = _dot(h_scr[...], w_cols).astype(o_ref.dtype)
        o_ref[:, c0:c1] = o
        if lru_lo is not None:
            keep = min(lru_lo + c1, 2 * w) - (lru_lo + c0)
            if keep > 0:
                lru_new[:, lru_lo + c0:lru_lo + c0 + keep] = o[:, 0:keep]

    def column_steps(lo, hi, from_head, prologue=None, lru_lo=None):
        for a, b, with_lru in ((lo, min(hi, n_sub), True), (max(lo, n_sub), hi, False)):
            if a < b:
                @pl.when(jnp.logical_and(i < n_tiles, jnp.logical_and(j >= a, j < b)))
                def _(with_lru=with_lru):
                    if prologue is not None:
                        prologue()
                    state = None
                    for q in range(n_piece):
                        project_piece(q, from_head, lru_lo)
                        if with_lru:
                            state = lru_piece(q, state)
                    if b == n_j:
                        @pl.when(j == n_j - 1)
                        def _():
                            lru_cur[...] = lru_new[...]

    def normalise():
        h_scr[...] = _rms(x_ref[...], g_ref[...]).astype(BF16)

    column_steps(0, 1, True, prologue=normalise, lru_lo=0)
    column_steps(1, 2, True, lru_lo=tn)
    column_steps(2, n_head, True)
    column_steps(n_head, n_j, False)

    @pl.when(jnp.logical_and(i == n_tiles, j < n_sub))
    def _():
        state = None
        for q in range(n_piece):
            state = lru_piece(q, state)


def _inproj_lru(x, g, w_head, w_tail, lw, seq_len, tm, tn, n_head, sub_rows):
    t, d = x.shape
    w = lw["lru_lambda"].shape[1]
    n_tail = w_tail.shape[1] // tn
    n_tiles, n_j = t // tm, n_head + n_tail
    tiles_per_seq = seq_len // tm
    assert seq_len % tm == 0 and tm % sub_rows == 0 and tm // sub_rows < n_j and n_head >= 2
    assert w <= tn and 2 * w <= 2 * tn and 2 * w > tn
    cur_tile = lambda i: jnp.minimum(i, n_tiles - 1)
    lru_tile = lambda i: jnp.maximum(i - 1, 0)
    body = functools.partial(_inproj_lru_body, n_head=n_head, n_j=n_j, n_tiles=n_tiles,
                             tiles_per_seq=tiles_per_seq, sub_rows=sub_rows)
    out_col = lambda i, j: jnp.where(i < n_tiles, j, n_j - 1)
    w_col = lambda i, j: jnp.where(i < n_tiles, j, n_j - 1)
    return pl.pallas_call(
        body,
        grid=(n_tiles + 1, n_j),
        in_specs=[pl.BlockSpec((tm, d), lambda i, j: (cur_tile(i), 0)),
                  pl.BlockSpec((1, d), lambda i, j: (0, 0)),
                  pl.BlockSpec((1, d, tn), lambda i, j: (0, 0, jnp.minimum(w_col(i, j), n_head - 1))),
                  pl.BlockSpec((d, tn), lambda i, j: (0, jnp.maximum(w_col(i, j) - n_head, 0))),
                  _const_spec((CONV_W, w)), _const_spec((1, w)),
                  _const_spec(lw["wax"].shape), _const_spec((1, w)), _const_spec((1, w)), _const_spec((1, w))],
        out_specs=[pl.BlockSpec((tm, tn), lambda i, j: (cur_tile(i), out_col(i, j))),
                   pl.BlockSpec((tm, w), lambda i, j: (lru_tile(i), 0)),
                   pl.BlockSpec((1, 1, w), lambda i, j: (lru_tile(i) // tiles_per_seq, 0, 0)),
                   pl.BlockSpec((1, CONV_W - 1, w), lambda i, j: (lru_tile(i) // tiles_per_seq, 0, 0))],
        out_shape=[jax.ShapeDtypeStruct((t, n_j * tn), BF16),
                   jax.ShapeDtypeStruct((t, w), BF16),
                   jax.ShapeDtypeStruct((t // seq_len, 1, w), F32),
                   jax.ShapeDtypeStruct((t // seq_len, CONV_W - 1, w), F32)],
        scratch_shapes=[pltpu.VMEM((tm, d), BF16), pltpu.VMEM((tm, 2 * w), BF16), pltpu.VMEM((tm, 2 * w), BF16),
                        pltpu.VMEM((SUBLANES, w), F32), pltpu.VMEM((1, w), F32)],
        compiler_params=_cparams(("arbitrary", "arbitrary")),
        name="inproj_lru",
    )(x, g, w_head, w_tail, lw["lru_conv_w"], lw["lru_conv_b"], lw["wax"], lw["lru_ba"], lw["lru_bx"],
      lw["lru_lambda"])


def _lru_step_body(xin_ref, gate_ref, buf_ref, h0_ref, cw_ref, cb_ref, wax_ref, ba_ref, bx_ref, lam_ref,
                   ya_ref, hlast_ref, bufout_ref, *, steps):
    bsz = h0_ref.shape[0]
    n = steps * bsz
    x = xin_ref[...].astype(F32)
    xx = jnp.concatenate([buf_ref[...], x], axis=0)
    cw = cw_ref[...]
    u = cb_ref[...] + sum(xx[k * bsz:k * bsz + n] * cw[k:k + 1] for k in range(CONV_W))
    a, i, mult = _lru_gates(u, wax_ref, ba_ref[...], bx_ref[...], lam_ref[...])
    v = u * i * mult
    h = h0_ref[...]
    for t in range(steps):
        sl = slice(t * bsz, (t + 1) * bsz)
        h = a[sl] * h + v[sl]
        ya_ref[sl, :] = (h * jax.nn.gelu(gate_ref[sl, :].astype(F32))).astype(ya_ref.dtype)
    hlast_ref[...] = h
    bufout_ref[...] = xx[steps * bsz:(steps + CONV_W - 1) * bsz]


def _lru_step(xin_t, gate_t, buf_t, h0, lw, steps):
    bsz, w = h0.shape
    body = functools.partial(_lru_step_body, steps=steps)
    return pl.pallas_call(
        body,
        out_shape=[jax.ShapeDtypeStruct((steps * bsz, w), BF16),
                   jax.ShapeDtypeStruct((bsz, w), F32),
                   jax.ShapeDtypeStruct(((CONV_W - 1) * bsz, w), F32)],
        compiler_params=pltpu.CompilerParams(vmem_limit_bytes=VMEM_LIMIT),
        name="lru_step",
    )(xin_t, gate_t, buf_t, h0, lw["lru_conv_w"], lw["lru_conv_b"], lw["wax"],
      lw["lru_ba"], lw["lru_bx"], lw["lru_lambda"])


def _expand_heads(cols, e2):
    q = cols[0].shape[0]
    v = jnp.concatenate(cols, axis=0)
    hi = v.astype(BF16)
    lo = (v - hi.astype(F32)).astype(BF16)
    out = _dot(jnp.concatenate([hi, lo], axis=1), e2)
    return [out[i * q:(i + 1) * q] for i in range(len(cols))]


def _ssd_chunk(xc, dt, p, e2, s_get, s_set, t_col, t_row, n_seg=1, n_valid=None):
    q = xc.shape[0]
    rps = q // n_seg
    gn = SSD_GROUPS * SSD_STATE
    d_inner = xc.shape[1] - 2 * gn
    hpg = d_inner // SSD_HEADDIM // SSD_GROUPS
    gw = hpg * SSD_HEADDIM

    causal = t_col >= t_row
    if n_seg > 1:
        same = (lax.broadcasted_iota(jnp.int32, (q, 1), 0) // rps) == (lax.broadcasted_iota(jnp.int32, (1, q), 1) // rps)
        causal = jnp.logical_and(same, causal)
    if n_valid is not None:
        dt = jnp.where(lax.broadcasted_iota(jnp.int32, dt.shape, 0) % rps < n_valid, dt, 0.0)
    a = dt * p["A"]
    cum = _dot_f32(causal.astype(F32), a)
    cum_t = cum.T
    total = cum[q - 1:q, :] if n_seg == 1 else _dot_f32(same.astype(F32), a)
    dt_x, to_end_x, ecum_x = _expand_heads([dt, jnp.exp(total - cum), jnp.exp(cum)], e2)

    xs = xc[:, :d_inner]
    xdt = xs * dt_x
    xw = xdt * to_end_x
    packed = rps % (2 * SUBLANES) == 0
    xdt_m = xdt.astype(BF16) if packed else xdt
    if packed:
        xw = xw.astype(BF16)
    lane_head = lax.broadcasted_iota(jnp.int32, (1, gw), 1) // SSD_HEADDIM
    y_groups = []
    for g in range(SSD_GROUPS):
        sl = slice(g * gw, (g + 1) * gw)
        bg = xc[:, d_inner + g * SSD_STATE:d_inner + (g + 1) * SSD_STATE]
        cg = xc[:, d_inner + gn + g * SSD_STATE:d_inner + gn + (g + 1) * SSD_STATE]
        if packed:
            bg, cg = bg.astype(BF16), cg.astype(BF16)
        cb = _dot_nt(cg.astype(BF16), bg.astype(BF16))
        m_heads, x_heads = [], []
        for hh in range(hpg):
            h = g * hpg + hh
            decay = jnp.exp(jnp.where(causal, cum[:, h:h + 1] - cum_t[h:h + 1, :], -jnp.inf))
            m_heads.append((cb * decay).astype(BF16))
            x_heads.append(jnp.where(lane_head == hh, xdt_m[:, sl], 0.0))
        y_diag = _dot(jnp.concatenate(m_heads, axis=1), jnp.concatenate(x_heads, axis=0).astype(BF16))
        y_off = []
        for b in range(n_seg):
            rows = slice(b * rps, (b + 1) * rps)
            s_old = s_get(b, g)
            y_off.append(_dot_nt(cg[rows].astype(BF16), s_old.astype(BF16)))
            s_dec = [s_old[hh * SSD_HEADDIM:(hh + 1) * SSD_HEADDIM, :]
                     * jnp.exp(cum_t[g * hpg + hh:g * hpg + hh + 1, (b + 1) * rps - 1:(b + 1) * rps])
                     for hh in range(hpg)]
            s_set(b, g, jnp.concatenate(s_dec, axis=0) + _dot_tn(xw[rows, sl].astype(BF16), bg[rows].astype(BF16)))
        y_off = y_off[0] if n_seg == 1 else jnp.concatenate(y_off, axis=0)
        y_groups.append(y_diag + y_off * ecum_x[:, sl])
    return jnp.concatenate(y_groups, axis=1) + p["D"] * xs


def _ssd_gate_norm(y, z, p):
    gw = y.shape[1] // SSD_GROUPS
    zf = z.astype(F32)
    out = []
    for g in range(SSD_GROUPS):
        sl = slice(g * gw, (g + 1) * gw)
        v = y[:, sl] * (zf[:, sl] * jax.nn.sigmoid(zf[:, sl]))
        out.append(v * lax.rsqrt(jnp.mean(v * v, axis=-1, keepdims=True) + EPS) * p["norm_g"][:, sl])
    return jnp.concatenate(out, axis=1)


def _ssd_conv(xpad, q, cw, cb):
    base = SUBLANES - (CONV_W - 1)
    y = cb + sum(xpad[base + k:base + k + q] * cw[k:k + 1] for k in range(CONV_W))
    return y * jax.nn.sigmoid(y)


def _softplus(x):
    return jax.nn.softplus(x)


def _ssd_params(cw_ref, cb_ref, dtb_ref, a_ref, d_ref, ng_ref):
    return {"cw": cw_ref[...], "cb": cb_ref[...], "dt_bias": dtb_ref[...], "A": a_ref[...],
            "D": d_ref[...], "norm_g": ng_ref[...]}


def _ssd_prompt_body(xbc_ref, z_ref, dt_ref, cw_ref, cb_ref, dtb_ref, a_ref, d_ref, ng_ref, e2_ref,
                     yb_ref, sout_ref, bufout_ref, x_scr, dt_scr, y_scr, s_scr, *, q):
    c = pl.program_id(1)
    rows = xbc_ref.shape[1]
    half = q // 2
    n_xslab = x_scr.shape[0]
    base = SUBLANES - (CONV_W - 1)

    @pl.when(c == 0)
    def _():
        x_scr[:, 0:SUBLANES, :] = jnp.zeros((n_xslab, SUBLANES, LANES), F32)
        s_scr[...] = jnp.zeros_like(s_scr)

    @pl.when(c > 0)
    def _():
        x_scr[:, 0:SUBLANES, :] = x_scr[:, rows:rows + SUBLANES, :]

    p = _ssd_params(cw_ref, cb_ref, dtb_ref, a_ref, d_ref, ng_ref)
    x = xbc_ref[0].astype(F32)
    for j in range(n_xslab):
        x_scr[j, SUBLANES:SUBLANES + rows, :] = x[:, j * LANES:(j + 1) * LANES]
    dt_scr[...] = _softplus(dt_ref[0].astype(F32) + p["dt_bias"])

    def times(shape, axis):
        pos = lax.broadcasted_iota(jnp.int32, shape, axis)
        return jnp.where(pos < half, 2 * pos, 2 * (pos - half) + 1)

    def s_set(b, g, v):
        s_scr[g] = v

    for ch in range(rows // q):
        r0 = ch * q
        cols = []
        for j in range(n_xslab):
            ls = slice(j * LANES, (j + 1) * LANES)
            halves = []
            for par in range(2):
                acc = p["cb"][:, ls]
                for k in range(CONV_W):
                    acc = acc + x_scr[j, pl.ds(r0 + base + k + par, half, stride=2), :] * p["cw"][k:k + 1, ls]
                halves.append(acc)
            cols.append(jnp.concatenate(halves, axis=0))
        xc = jnp.concatenate(cols, axis=1)
        xc = xc * jax.nn.sigmoid(xc)
        dt = jnp.concatenate([dt_scr[pl.ds(r0 + par, half, stride=2), :] for par in range(2)], axis=0)
        y = _ssd_chunk(xc, dt, p, e2_ref[...], lambda b, g: s_scr[g], s_set, times((q, 1), 0), times((1, q), 1))
        for j in range(y_scr.shape[0]):
            for par in range(2):
                y_scr[j, pl.ds(par, half, stride=2), :] = y[par * half:(par + 1) * half, j * LANES:(j + 1) * LANES]
        y = jnp.concatenate([y_scr[j] for j in range(y_scr.shape[0])], axis=1)
        yb_ref[0, r0:r0 + q, :] = _ssd_gate_norm(y, z_ref[0, r0:r0 + q, :], p).astype(yb_ref.dtype)

    @pl.when(c == pl.num_programs(1) - 1)
    def _():
        sout_ref[0] = s_scr[...]
        bufout_ref[0] = x[rows - (CONV_W - 1):rows]


def _ssd_prompt(proj3, sp, col_xbc, col_z, col_dt):
    b, l, _ = proj3.shape
    cdim = sp["ssd_conv_w"].shape[1]
    d_inner = sp["ssd_norm_g"].shape[1]
    gw = d_inner // SSD_GROUPS
    q = SSD_CHUNK if l % SSD_CHUNK == 0 else l
    assert q % (2 * SUBLANES) == 0
    rows = q * SSD_CHUNKS_PER_STEP if l % (q * SSD_CHUNKS_PER_STEP) == 0 else q
    return pl.pallas_call(
        functools.partial(_ssd_prompt_body, q=q),
        grid=(b, l // rows),
        in_specs=[pl.BlockSpec((1, rows, cdim), lambda i, c: (i, c, col_xbc // cdim)),
                  pl.BlockSpec((1, rows, d_inner), lambda i, c: (i, c, col_z // d_inner)),
                  pl.BlockSpec((1, rows, LANES), lambda i, c: (i, c, col_dt // LANES)),
                  _const_spec((CONV_W, cdim)), _const_spec((1, cdim)), _const_spec((1, LANES)),
                  _const_spec((1, LANES)), _const_spec((1, d_inner)), _const_spec((1, d_inner)),
                  _const_spec((2 * LANES, d_inner))],
        out_specs=[pl.BlockSpec((1, rows, d_inner), lambda i, c: (i, c, 0)),
                   pl.BlockSpec((1, SSD_GROUPS, gw, SSD_STATE), lambda i, c: (i, 0, 0, 0)),
                   pl.BlockSpec((1, CONV_W - 1, cdim), lambda i, c: (i, 0, 0))],
        out_shape=[jax.ShapeDtypeStruct((b, l, d_inner), BF16),
                   jax.ShapeDtypeStruct((b, SSD_GROUPS, gw, SSD_STATE), F32),
                   jax.ShapeDtypeStruct((b, CONV_W - 1, cdim), F32)],
        scratch_shapes=[pltpu.VMEM((cdim // LANES, rows + SUBLANES, LANES), F32),
                        pltpu.VMEM((rows, LANES), F32),
                        pltpu.VMEM((d_inner // LANES, q, LANES), F32),
                        pltpu.VMEM((SSD_GROUPS, gw, SSD_STATE), F32)],
        compiler_params=_cparams(("parallel", "arbitrary")),
        name="ssd_prompt",
    )(proj3, proj3, proj3, sp["ssd_conv_w"], sp["ssd_conv_b"], sp["dt_bias"], sp["A"], sp["D"], sp["ssd_norm_g"],
      sp["head_expand"])


def _ssd_step_body(xbc_ref, z_ref, dt_ref, buf_ref, s_ref, cw_ref, cb_ref, dtb_ref, a_ref, d_ref, ng_ref, e2_ref,
                   yb_ref, sout_ref, bufout_ref, *, steps, nb):
    p = _ssd_params(cw_ref, cb_ref, dtb_ref, a_ref, d_ref, ng_ref)
    rps = SUBLANES
    q = nb * rps
    x_all = xbc_ref[...].astype(F32)
    z_all = z_ref[...].astype(F32)
    dt_all = _softplus(dt_ref[...].astype(F32) + p["dt_bias"])
    cdim = x_all.shape[1]

    def padded(v, j):
        return jnp.concatenate([v[j * steps:(j + 1) * steps], jnp.zeros((rps - steps, v.shape[1]), v.dtype)], axis=0)

    xcs = []
    for j in range(nb):
        xpad = jnp.concatenate([jnp.zeros((SUBLANES - (CONV_W - 1), cdim), F32), buf_ref[j], padded(x_all, j)],
                               axis=0)
        xcs.append(_ssd_conv(xpad, rps, p["cw"], p["cb"]))
        bufout_ref[j] = xpad[SUBLANES + steps - (CONV_W - 1):SUBLANES + steps]
    xc = jnp.concatenate(xcs, axis=0)
    dt = jnp.concatenate([padded(dt_all, j) for j in range(nb)], axis=0)
    z = jnp.concatenate([padded(z_all, j) for j in range(nb)], axis=0)

    def s_set(b, g, v):
        sout_ref[b, g] = v

    t_col = lax.broadcasted_iota(jnp.int32, (q, 1), 0) % rps
    t_row = lax.broadcasted_iota(jnp.int32, (1, q), 1) % rps
    y = _ssd_chunk(xc, dt, p, e2_ref[...], lambda b, g: s_ref[b, g], s_set, t_col, t_row, n_seg=nb, n_valid=steps)
    y = _ssd_gate_norm(y, z, p)
    for j in range(nb):
        yb_ref[j * steps:(j + 1) * steps, :] = y[j * rps:j * rps + steps].astype(yb_ref.dtype)


def _ssd_step(proj, buf, s0, sp, steps, nb, col_xbc, col_z, col_dt):
    bsz = s0.shape[0]
    cdim = sp["ssd_conv_w"].shape[1]
    d_inner = sp["ssd_norm_g"].shape[1]
    gw = d_inner // SSD_GROUPS
    rows = nb * steps
    body = functools.partial(_ssd_step_body, steps=steps, nb=nb)
    return pl.pallas_call(
        body,
        grid=(bsz // nb,),
        in_specs=[pl.BlockSpec((rows, cdim), lambda i: (i, col_xbc // cdim)),
                  pl.BlockSpec((rows, d_inner), lambda i: (i, col_z // d_inner)),
                  pl.BlockSpec((rows, LANES), lambda i: (i, col_dt // LANES)),
                  pl.BlockSpec((None, nb, CONV_W - 1, cdim), lambda i: (0, i, 0, 0)),
                  pl.BlockSpec((nb, SSD_GROUPS, gw, SSD_STATE), lambda i: (i, 0, 0, 0)),
                  _const_spec((CONV_W, cdim)), _const_spec((1, cdim)), _const_spec((1, LANES)),
                  _const_spec((1, LANES)), _const_spec((1, d_inner)), _const_spec((1, d_inner)),
                  _const_spec((2 * LANES, d_inner))],
        out_specs=[pl.BlockSpec((rows, d_inner), lambda i: (i, 0)),
                   pl.BlockSpec((nb, SSD_GROUPS, gw, SSD_STATE), lambda i: (i, 0, 0, 0)),
                   pl.BlockSpec((None, nb, CONV_W - 1, cdim), lambda i: (0, i, 0, 0))],
        out_shape=[jax.ShapeDtypeStruct((bsz * steps, d_inner), BF16),
                   jax.ShapeDtypeStruct(s0.shape, F32),
                   jax.ShapeDtypeStruct(buf.shape, F32)],
        compiler_params=_cparams(("parallel",)),
        name="ssd_step",
    )(proj, proj, proj, buf, s0, sp["ssd_conv_w"], sp["ssd_conv_b"], sp["dt_bias"], sp["A"], sp["D"],
      sp["ssd_norm_g"], sp["head_expand"])


def _router(t, wr, br):
    t_hi = t.astype(BF16)
    t_lo = (t - t_hi.astype(F32)).astype(BF16)
    both = _dot(jnp.concatenate([t_hi, t_lo], axis=1), wr)
    logits = both[:, :LANES] + both[:, LANES:] + br
    lane = lax.broadcasted_iota(jnp.int32, logits.shape, 1)
    neg = -jnp.inf
    gl = jnp.where(lane < N_EGROUPS, logits, neg)
    gmax = jnp.max(gl, axis=-1, keepdims=True)
    g_idx = jnp.min(jnp.where(gl == gmax, lane, LANES), axis=-1, keepdims=True)
    g_w = 1.0 / jnp.sum(jnp.exp(gl - gmax), axis=-1, keepdims=True)
    in_grp = jnp.logical_and(jnp.logical_and(lane >= ROUTER_LANE0, lane < ROUTER_LANE0 + N_EXPERTS),
                             ((lane - ROUTER_LANE0) >> 2) == g_idx)
    el = jnp.where(in_grp, logits, neg)
    pe = jnp.exp(el - jnp.max(el, axis=-1, keepdims=True))
    pe = pe / jnp.sum(pe, axis=-1, keepdims=True)
    cand = jnp.where(in_grp, pe, -1.0)
    v1 = jnp.max(cand, axis=-1, keepdims=True)
    i1 = jnp.min(jnp.where(cand == v1, lane, LANES), axis=-1, keepdims=True)
    cand2 = jnp.where(lane == i1, -1.0, cand)
    v2 = jnp.max(cand2, axis=-1, keepdims=True)
    i2 = jnp.min(jnp.where(jnp.logical_and(cand2 == v2, in_grp), lane, LANES), axis=-1, keepdims=True)
    den = v1 + v2
    return lane, i1, i2, g_w * v1 / den, g_w * v2 / den


def _rows_to_tiles(ref, val):
    n, d = val.shape
    for k in range(d // LANES):
        ref[pl.ds(k, n, stride=d // LANES), :] = val[:, k * LANES:(k + 1) * LANES]


def _tiles_to_rows(ref, n, d, start=0):
    return jnp.concatenate([ref[pl.ds(start + k, n, stride=d // LANES), :] for k in range(d // LANES)], axis=1)


def _merge_body(x_ref, ya_ref, yb_ref, ga_ref, gb_ref, wl_ref, ws_ref, wo_ref, gf_ref, wr_ref, br_ref, cnt0_ref,
                x1_ref, t_ref, rt_ref, rtt_ref, cnt_ref, base_scr):
    step = pl.program_id(0)

    @pl.when(step == 0)
    def _():
        base_scr[...] = cnt0_ref[...]

    a = _dot(ya_ref[...], wl_ref[...])
    b = _dot(yb_ref[...], ws_ref[...])
    merged = jax.nn.sigmoid(ga_ref[...].astype(F32)) * a + jax.nn.sigmoid(gb_ref[...].astype(F32)) * b
    x1 = x_ref[...] + _dot(merged.astype(BF16), wo_ref[...])
    x1_ref[...] = x1
    t = _rms(x1, gf_ref[...])
    _rows_to_tiles(t_ref, t)
    lane, i1, i2, wg1, wg2 = _router(t, wr_ref[...], br_ref[...])

    tm = t.shape[0]
    onehot = jnp.where(jnp.logical_or(lane == i1, lane == i2), 1.0, 0.0).astype(BF16)
    tri = (lax.broadcasted_iota(jnp.int32, (tm, tm), 1) <= lax.broadcasted_iota(jnp.int32, (tm, tm), 0)).astype(BF16)
    cum = _dot(tri, onehot) + base_scr[...]
    r1 = jnp.sum(jnp.where(lane == i1, cum, 0.0), axis=-1, keepdims=True) - 1.0
    r2 = jnp.sum(jnp.where(lane == i2, cum, 0.0), axis=-1, keepdims=True) - 1.0
    cols = (wg1, wg2, r1, r2, (i1 - ROUTER_LANE0).astype(F32), (i2 - ROUTER_LANE0).astype(F32))
    rt = jnp.zeros(cum.shape, F32)
    for k, c in enumerate(cols):
        rt = jnp.where(lane == k, c, rt)
    rt_ref[...] = rt
    rtt_ref[...] = rt.T[0:SUBLANES, :]
    base_scr[...] = cum[tm - 1:tm, :]
    cnt_ref[...] = cum[tm - 1:tm, :]


def _merge(x, ya, yb, proj, mw, tm, col_ga, col_gb, counts_so_far):
    t, d = x.shape
    d_inner = yb.shape[1]
    return pl.pallas_call(
        _merge_body,
        grid=(t // tm,),
        in_specs=[pl.BlockSpec((tm, d), lambda i: (i, 0)),
                  pl.BlockSpec((tm, d), lambda i: (i, 0)),
                  pl.BlockSpec((tm, d_inner), lambda i: (i, 0)),
                  pl.BlockSpec((tm, d), lambda i: (i, col_ga // d)),
                  pl.BlockSpec((tm, d), lambda i: (i, col_gb // d)),
                  _const_spec((d, d)), _const_spec((d_inner, d)), _const_spec((d, d)),
                  _const_spec((1, d)), _const_spec((2 * d, 2 * LANES)), _const_spec((1, LANES)),
                  _const_spec((1, LANES))],
        out_specs=[pl.BlockSpec((tm, d), lambda i: (i, 0)),
                   pl.BlockSpec((tm * d // LANES, LANES), lambda i: (i, 0)),
                   pl.BlockSpec((tm, LANES), lambda i: (i, 0)),
                   pl.BlockSpec((SUBLANES, tm), lambda i: (0, i)),
                   pl.BlockSpec((1, LANES), lambda i: (0, 0))],
        out_shape=[jax.ShapeDtypeStruct((t, d), F32),
                   jax.ShapeDtypeStruct((t * d // LANES, LANES), F32),
                   jax.ShapeDtypeStruct((t, LANES), F32),
                   jax.ShapeDtypeStruct((SUBLANES, t), F32),
                   jax.ShapeDtypeStruct((1, LANES), F32)],
        scratch_shapes=[pltpu.VMEM((1, LANES), F32)],
        compiler_params=_cparams(("arbitrary",)),
        name="merge_router",
    )(x, ya, yb, proj, proj, mw["w_br_lru"], mw["w_br_ssd"], mw["w_out"], mw["g_ffn"], mw["w_router"],
      mw["b_router"], counts_so_far)


def _dispatch_body(zb_ref, *rest, n_groups, tms, n_toks, nk, tmg):
    dest_refs, t_refs = rest[:n_groups], rest[n_groups:2 * n_groups]
    o_ref, zero_scr, sem, zsem = rest[2 * n_groups:]
    step = pl.program_id(0)

    @pl.when(step == 0)
    def _():
        zero_scr[...] = jnp.zeros_like(zero_scr)
        blk = tmg * nk

        def zcopy(j):
            return pltpu.make_async_copy(zero_scr, o_ref.at[pl.ds(pl.multiple_of(zb_ref[j] * blk, blk), blk)], zsem)

        for j in range(zb_ref.shape[0]):
            pl.when(zb_ref[j] >= 0)(lambda j=j: zcopy(j).start())
        for j in range(zb_ref.shape[0]):
            pl.when(zb_ref[j] >= 0)(lambda j=j: zcopy(j).wait())

    first_step = 0
    for g in range(n_groups):
        tm, n_tok, steps = tms[g], n_toks[g], n_toks[g] // tms[g]

        @pl.when(jnp.logical_and(step >= first_step, step < first_step + steps))
        def _(g=g, tm=tm, n_tok=n_tok, first_step=first_step):
            t_ref, dest_ref = t_refs[g], dest_refs[g]

            def issue(r, carry):
                src = t_ref.at[pl.ds(pl.multiple_of(r * nk, nk), nk)]
                for k in range(2):
                    row = dest_ref[k * n_tok + (step - first_step) * tm + r]
                    pltpu.make_async_copy(src, o_ref.at[pl.ds(pl.multiple_of(row * nk, nk), nk)],
                                          sem).start(priority=k)
                return carry

            lax.fori_loop(0, tm, issue, 0, unroll=8)
            for k in range(2):
                pltpu.make_async_copy(t_ref, o_ref.at[pl.ds(0, tm * nk)], sem).wait()

        first_step += steps


def _dispatch(dests, zero_blocks, t_tiles, n_toks, tms, n_rows, tmg):
    n_groups = len(dests)
    nk = t_tiles[0].shape[0] // n_toks[0]
    steps = [n // tm for n, tm in zip(n_toks, tms)]
    starts = [sum(steps[:g]) for g in range(n_groups)]
    block_of = lambda g: (lambda i, *_: (jnp.clip(i - starts[g], 0, steps[g] - 1), 0))
    return pl.pallas_call(
        functools.partial(_dispatch_body, n_groups=n_groups, tms=tuple(tms), n_toks=tuple(n_toks), nk=nk, tmg=tmg),
        grid_spec=pltpu.PrefetchScalarGridSpec(
            num_scalar_prefetch=1 + n_groups,
            grid=(sum(steps),),
            in_specs=[pl.BlockSpec((tms[g] * nk, LANES), block_of(g)) for g in range(n_groups)],
            out_specs=pl.BlockSpec(memory_space=pl.ANY),
            scratch_shapes=[pltpu.VMEM((tmg * nk, LANES), F32), pltpu.SemaphoreType.DMA(()),
                            pltpu.SemaphoreType.DMA(())]),
        out_shape=jax.ShapeDtypeStruct((n_rows * nk, LANES), F32),
        compiler_params=_cparams(("arbitrary",)),
        name="moe_dispatch",
    )(zero_blocks, *dests, *t_tiles)


def _expert_body(te_ref, nt_ref, x_ref, w1_ref, w3_ref, w2_ref, y_ref, w1_scr, w3_scr, w2_scr, *, tmg):
    i = pl.program_id(0)
    real = i < nt_ref[0]
    d = w1_scr.shape[0]

    @pl.when(jnp.logical_or(i == 0, te_ref[i] != te_ref[jnp.maximum(i - 1, 0)]))
    def _():
        w1_scr[...] = w1_ref[0].astype(BF16)
        w3_scr[...] = w3_ref[0].astype(BF16)
        w2_scr[...] = w2_ref[0].astype(BF16)

    @pl.when(real)
    def _():
        x = _tiles_to_rows(x_ref, tmg, d).astype(BF16)
        h1 = _dot(x, w1_scr[...])
        h3 = _dot(x, w3_scr[...])
        _rows_to_tiles(y_ref, _dot((h1 * jax.nn.sigmoid(h1) * h3).astype(BF16), w2_scr[...]))

    @pl.when(jnp.logical_not(real))
    def _():
        y_ref[...] = jnp.zeros_like(y_ref)


def _experts(tile_expert, n_tiles, xs_tiles, w1, w3, w2, tmg):
    _, d, dff = w1.shape
    blk = tmg * d // LANES
    row_spec = pl.BlockSpec((blk, LANES), lambda i, te, nt: (i, 0))
    return pl.pallas_call(
        functools.partial(_expert_body, tmg=tmg),
        grid_spec=pltpu.PrefetchScalarGridSpec(
            num_scalar_prefetch=2,
            grid=(xs_tiles.shape[0] // blk,),
            in_specs=[row_spec,
                      pl.BlockSpec((1, d, dff), lambda i, te, nt: (te[i], 0, 0)),
                      pl.BlockSpec((1, d, dff), lambda i, te, nt: (te[i], 0, 0)),
                      pl.BlockSpec((1, dff, d), lambda i, te, nt: (te[i], 0, 0))],
            out_specs=row_spec,
            scratch_shapes=[pltpu.VMEM((d, dff), BF16), pltpu.VMEM((d, dff), BF16), pltpu.VMEM((dff, d), BF16)]),
        out_shape=jax.ShapeDtypeStruct(xs_tiles.shape, F32),
        compiler_params=_cparams(("arbitrary",)),
        name="moe_experts",
    )(tile_expert, n_tiles, xs_tiles, w1, w3, w2)


def _ple_body(dest_ref, x_ref, rt_ref, p_ref, wp_ref, gp_ref, gg_ref, wg_ref, gfin_ref, y_hbm, o_ref, gbuf, sem,
              *, n_tok):
    step = pl.program_id(0)
    n_steps = pl.num_programs(0)
    tm, d = x_ref.shape
    nk = d // LANES
    pr = tm // COMBINE_PIECES

    def issue_row(tile, slot, r):
        tok = tile * tm + r
        for k in range(2):
            row = dest_ref[k * n_tok + tok]
            pltpu.make_async_copy(y_hbm.at[pl.ds(pl.multiple_of(row * nk, nk), nk)],
                                  gbuf.at[slot, pl.ds(pl.multiple_of((k * tm + r) * nk, nk), nk)],
                                  sem.at[slot]).start(priority=k)

    def wait_slot(slot):
        pltpu.make_async_copy(y_hbm.at[pl.ds(0, 2 * tm * nk)], gbuf.at[slot], sem.at[slot]).wait()

    @pl.when(step == 0)
    def _():
        def issue(r, carry):
            issue_row(0, 0, r)
            return carry

        lax.fori_loop(0, tm, issue, 0, unroll=8)

    slot = step % 2
    wait_slot(slot)
    nxt_tile = jnp.minimum(step + 1, n_steps - 1)
    rows = gbuf.at[slot]
    for q in range(COMBINE_PIECES):
        for r in range(q * pr, (q + 1) * pr):
            issue_row(nxt_tile, 1 - slot, r)
        sl = slice(q * pr, (q + 1) * pr)
        rt = rt_ref[sl, :]
        x = (x_ref[sl, :] + rt[:, 0:1] * _tiles_to_rows(rows, pr, d, start=q * pr * nk)
             + rt[:, 1:2] * _tiles_to_rows(rows, pr, d, start=(tm + q * pr) * nk))
        e = _rms(_dot(p_ref[sl, :].astype(BF16), wp_ref[...]), gp_ref[...])
        gate = jax.nn.sigmoid(_dot(_rms(x, gg_ref[...]).astype(BF16), wg_ref[...]))
        o_ref[sl, :] = _rms(x + gate * e, gfin_ref[...])

    pl.when(step == n_steps - 1)(lambda: wait_slot(1 - slot))


def _ple(dest, x1, rt, p, y_sorted, pw, tm):
    n, d = x1.shape
    dp = p.shape[1]
    const = lambda shape: pl.BlockSpec(shape, lambda i, *_: (0,) * len(shape))
    return pl.pallas_call(
        functools.partial(_ple_body, n_tok=n),
        grid_spec=pltpu.PrefetchScalarGridSpec(
            num_scalar_prefetch=1,
            grid=(n // tm,),
            in_specs=[pl.BlockSpec((tm, d), lambda i, *_: (i, 0)),
                      pl.BlockSpec((tm, LANES), lambda i, *_: (i, 0)),
                      pl.BlockSpec((tm, dp), lambda i, *_: (i, 0)),
                      const((dp, d)), const((1, d)), const((1, d)), const((d, d)), const((1, d)),
                      pl.BlockSpec(memory_space=pl.ANY)],
            out_specs=pl.BlockSpec((tm, d), lambda i, *_: (i, 0)),
            scratch_shapes=[pltpu.VMEM((2, 2 * tm * d // LANES, LANES), F32), pltpu.SemaphoreType.DMA((2,))]),
        out_shape=jax.ShapeDtypeStruct((n, d), F32),
        compiler_params=_cparams(("arbitrary",)),
        name="combine_ple_final",
    )(dest, x1, rt, p, pw["w_ple_proj"], pw["g_ple"], pw["g_ple_gate"], pw["w_ple_gate"], pw["g_final"], y_sorted)


def _pick_tile(n, pref):
    t = min(n, pref)
    while n % t:
        t //= 2
    return t


def _moe_row_tile(n):
    return 512 if 2 * n // N_EXPERTS >= 1024 else 128
MOE_DISPATCH_TILE = 2048
SSD_CHUNKS_PER_STEP = 4
COMBINE_PIECES = 4
LRU_SLICE_ROWS = 128
LRU_PIECES = 4


def _split_router(w):
    hi = w.astype(BF16)
    lo = (w - hi.astype(F32)).astype(BF16)
    return jnp.concatenate([jnp.concatenate([hi, lo], axis=1),
                            jnp.concatenate([hi, jnp.zeros_like(hi)], axis=1)], axis=0)


def _token_tail(groups, cnt, lw):
    n_all = sum(g["x1"].shape[0] for g in groups)
    tmg = _moe_row_tile(n_all)
    n_blocks = pl.cdiv(2 * n_all, tmg) + N_EXPERTS
    counts = cnt[0, ROUTER_LANE0:ROUTER_LANE0 + N_EXPERTS].astype(jnp.int32)
    tiles = (counts + tmg - 1) // tmg
    ends = jnp.cumsum(tiles)
    n_tiles = ends[-1]
    first_row = (ends - tiles) * tmg
    blk = jnp.arange(n_blocks, dtype=jnp.int32)
    tile_expert = jnp.sum((jnp.minimum(blk, n_tiles - 1)[:, None] >= ends[None, :]).astype(jnp.int32), axis=1)
    tail = n_tiles + blk[:N_EXPERTS]
    zero_blocks = jnp.concatenate([jnp.where(tiles > 0, ends - 1, -1),
                                   jnp.where(tail < n_blocks, tail, -1)]).astype(jnp.int32)

    dests = []
    for g in groups:
        e_idx = g["rtt"][4:6].astype(jnp.int32)
        dest = g["rtt"][2:4].astype(jnp.int32) + sum(jnp.where(e_idx == e, first_row[e], 0) for e in range(N_EXPERTS))
        dests.append(dest.reshape(2 * g["x1"].shape[0]))
    n_toks = [g["x1"].shape[0] for g in groups]
    sorted_t = _dispatch(dests, zero_blocks, [g["t_tiles"] for g in groups], n_toks,
                         [_pick_tile(n, MOE_DISPATCH_TILE) for n in n_toks], n_blocks * tmg, tmg)
    y_sorted = _experts(tile_expert, n_tiles.reshape(1), sorted_t, lw["w1"], lw["w3"], lw["w2"], tmg)
    return [_ple(dest, g["x1"], g["rt"], g["p"], y_sorted, lw, g["tm"]) for g, dest in zip(groups, dests)]


def kernel(x_prompt, x_sample, state_lru_h, state_lru_conv, state_ssd, state_ssd_conv, p_prompt, p_sample, g_mix, w_in, lru_conv_w, lru_conv_b, lru_wa, lru_ba, lru_wx, lru_bx, lru_lambda, ssd_conv_w, ssd_conv_b, ssd_dt_bias, ssd_A_log, ssd_D, ssd_norm_g, w_br_lru, w_br_ssd, w_out, g_ffn, w_router_g, b_router_g, w_router_e, b_router_e, w1, w3, w2, w_ple_proj, g_ple, g_ple_gate, w_ple_gate, g_final):
    depth = w_in.shape[0]
    assert depth == 1, "one decoder layer per call"
    bp, lp, d = x_prompt.shape
    bs, ls, _ = x_sample.shape
    w_lru = state_lru_h.shape[-1]
    heads, hdim, nstate = state_ssd.shape[2:]
    d_inner = heads * hdim
    cdim = state_ssd_conv.shape[-1]
    assert hdim == SSD_HEADDIM and nstate == SSD_STATE and heads <= LANES and ls < SUBLANES
    gw = d_inner // SSD_GROUPS

    o_dt = 2 * w_lru + d_inner + cdim
    n_proj = o_dt + 2 * d + LANES
    tn = n_proj // 9 if n_proj % (9 * LANES) == 0 else LANES
    n_head = o_dt // tn
    w_head = w_in.astype(BF16)
    wi = w_head[0]
    w_tail = jnp.concatenate([wi[:, n_head * tn:o_dt], wi[:, o_dt + heads:], wi[:, o_dt:o_dt + heads],
                              jnp.zeros((d, LANES - heads), BF16)], axis=1)
    assert n_head >= 1 and w_tail.shape[1] == n_proj - n_head * tn and w_tail.shape[1] % tn == 0
    col_z, col_xbc = 2 * w_lru, 2 * w_lru + d_inner
    col_ga, col_gb, col_dt = o_dt, o_dt + d, o_dt + 2 * d
    row = lambda v: v.reshape(1, -1).astype(F32)
    pad_heads = lambda v: jnp.pad(v.astype(F32), (0, LANES - heads)).reshape(1, LANES)
    lw = {
        "lru_conv_w": lru_conv_w[0], "lru_conv_b": row(lru_conv_b[0]),
        "wax": jnp.concatenate([lru_wa[0], lru_wx[0]], axis=-1).astype(BF16),
        "lru_ba": row(lru_ba[0]), "lru_bx": row(lru_bx[0]), "lru_lambda": row(lru_lambda[0]),
        "ssd_conv_w": ssd_conv_w[0], "ssd_conv_b": row(ssd_conv_b[0]),
        "dt_bias": pad_heads(ssd_dt_bias[0]), "A": pad_heads(-jnp.exp(ssd_A_log[0].astype(F32))),
        "D": row(jnp.repeat(ssd_D[0], hdim)), "ssd_norm_g": row(ssd_norm_g[0]),
        "head_expand": jnp.tile(jnp.arange(LANES)[:, None] == jnp.arange(d_inner)[None, :] // hdim, (2, 1)).astype(BF16),
        "w_br_lru": w_br_lru[0].astype(BF16), "w_br_ssd": w_br_ssd[0].astype(BF16), "w_out": w_out[0].astype(BF16),
        "g_ffn": row(g_ffn[0]),
        "w_router": _split_router(jnp.concatenate([w_router_g[0], w_router_e[0],
                                                   jnp.zeros((d, LANES - N_EGROUPS - N_EXPERTS), F32)], axis=1)),
        "b_router": jnp.concatenate([b_router_g[0], b_router_e[0],
                                     jnp.zeros((LANES - N_EGROUPS - N_EXPERTS,), F32)]).reshape(1, LANES),
        "w1": w1[0], "w3": w3[0], "w2": w2[0],
        "w_ple_proj": w_ple_proj[0].astype(BF16), "g_ple": row(g_ple[0]), "g_ple_gate": row(g_ple_gate[0]),
        "w_ple_gate": w_ple_gate[0].astype(BF16), "g_final": row(g_final),
    }
    g_mix_r = row(g_mix[0])

    tp = bp * lp
    xp = x_prompt.reshape(tp, d)
    tm_p = _pick_tile(lp, 1024)
    proj_p, ya_p, hl_p, lbuf_p = _inproj_lru(xp, g_mix_r, w_head, w_tail, lw, lp, tm_p, tn, n_head,
                                             _pick_tile(tm_p, LRU_SLICE_ROWS))
    proj_p3 = proj_p.reshape(bp, lp, n_proj)
    yb_p, s_p, sbuf_p = _ssd_prompt(proj_p3, lw, col_xbc, col_z, col_dt)
    tm_tail_p = _pick_tile(tp, 512)
    x1_p, t_p, rt_p, rtt_p, cnt_p = _merge(xp, ya_p.reshape(tp, w_lru), yb_p.reshape(tp, d_inner), proj_p, lw,
                                           tm_tail_p, col_ga, col_gb, jnp.zeros((1, LANES), F32))
    group_p = {"x1": x1_p, "t_tiles": t_p, "rt": rt_p, "rtt": rtt_p, "p": p_prompt[0].reshape(tp, -1),
               "tm": tm_tail_p}

    ts = bs * ls
    xs = x_sample.reshape(ts, d)
    tm_s = _pick_tile(ts, 512)
    proj_s = _inproj(xs, g_mix_r, w_head, w_tail, tm_s, tn, n_head)
    to_tmajor = lambda v, n: v.reshape(bs, n, -1).transpose(1, 0, 2).reshape(n * bs, -1)
    from_tmajor = lambda v, n: v.reshape(n, bs, -1).transpose(1, 0, 2)
    ya_t, hl_s, lbuf_t = _lru_step(to_tmajor(proj_s[:, :w_lru], ls), to_tmajor(proj_s[:, w_lru:2 * w_lru], ls),
                                   to_tmajor(state_lru_conv[0], CONV_W - 1), state_lru_h[0], lw, ls)
    ya_s = from_tmajor(ya_t, ls).reshape(ts, w_lru)
    lbuf_s = from_tmajor(lbuf_t, CONV_W - 1)
    yb_s, s_s, sbuf_s = _ssd_step(proj_s, state_ssd_conv, state_ssd[0].reshape(bs, SSD_GROUPS, gw, nstate), lw,
                                  ls, _pick_tile(bs, 8), col_xbc, col_z, col_dt)
    x1_s, t_s, rt_s, rtt_s, cnt_all = _merge(xs, ya_s, yb_s, proj_s, lw, tm_s, col_ga, col_gb, cnt_p)
    group_s = {"x1": x1_s, "t_tiles": t_s, "rt": rt_s, "rtt": rtt_s, "p": p_sample[0].reshape(ts, -1), "tm": tm_s}

    y_p, y_s = _token_tail([group_p, group_s], cnt_all, lw)

    return (y_p.reshape(bp, lp, d), y_s.reshape(bs, ls, d),
            hl_p.reshape(1, bp, w_lru), lbuf_p[None],
            s_p.reshape(1, bp, heads, hdim, nstate), sbuf_p[None],
            hl_s[None], lbuf_s[None],
            s_s.reshape(1, bs, heads, hdim, nstate), sbuf_s)
```

```python
import functools

import jax
import jax.numpy as jnp
from jax import lax
from jax.experimental import pallas as pl
from jax.experimental.pallas import tpu as pltpu

F32 = jnp.float32
BF16 = jnp.bfloat16

EPS = 1e-6
CONV_W = 4
LRU_BLOCKS = 8
LRU_C = 8.0
SSD_HEADDIM = 64
SSD_GROUPS = 8
SSD_STATE = 128
SSD_CHUNK = 128
N_EGROUPS = 4
EXP_PER_GROUP = 4
N_EXPERTS = N_EGROUPS * EXP_PER_GROUP

LANES = 128
SUBLANES = 8
VMEM_LIMIT = 52 * 1024 * 1024
ROUTER_LANE0 = N_EGROUPS


def _cparams(sem):
    return pltpu.CompilerParams(dimension_semantics=sem, vmem_limit_bytes=VMEM_LIMIT)


def _dot(a, b):
    return jnp.dot(a, b, preferred_element_type=F32)


def _dot_nt(a, b):
    return lax.dot_general(a, b, (((1,), (1,)), ((), ())), preferred_element_type=F32)


def _dot_tn(a, b):
    return lax.dot_general(a, b, (((0,), (0,)), ((), ())), preferred_element_type=F32)


def _dot_f32(a, b):
    return jnp.dot(a, b, precision=lax.Precision.HIGHEST, preferred_element_type=F32)


def _rms(x, g):
    return x * lax.rsqrt(jnp.mean(x * x, axis=-1, keepdims=True) + EPS) * g


def _const_spec(shape):
    nd = len(shape)
    return pl.BlockSpec(shape, lambda *_: (0,) * nd)


def _inproj_body(x_ref, g_ref, wh_ref, wt_ref, o_ref, h_scr, *, n_head):
    j = pl.program_id(1)

    @pl.when(j == 0)
    def _():
        h_scr[...] = _rms(x_ref[...], g_ref[...]).astype(BF16)

    @pl.when(j < n_head)
    def _():
        o_ref[...] = _dot(h_scr[...], wh_ref[0]).astype(o_ref.dtype)

    @pl.when(j >= n_head)
    def _():
        o_ref[...] = _dot(h_scr[...], wt_ref[...]).astype(o_ref.dtype)


def _inproj(x, g, w_head, w_tail, tm, tn, n_head):
    t, d = x.shape
    n_tail = w_tail.shape[1] // tn
    return pl.pallas_call(
        functools.partial(_inproj_body, n_head=n_head),
        grid=(t // tm, n_head + n_tail),
        in_specs=[pl.BlockSpec((tm, d), lambda i, j: (i, 0)),
                  pl.BlockSpec((1, d), lambda i, j: (0, 0)),
                  pl.BlockSpec((1, d, tn), lambda i, j: (0, 0, jnp.minimum(j, n_head - 1))),
                  pl.BlockSpec((d, tn), lambda i, j: (0, jnp.maximum(j - n_head, 0)))],
        out_specs=pl.BlockSpec((tm, tn), lambda i, j: (i, j)),
        out_shape=jax.ShapeDtypeStruct((t, (n_head + n_tail) * tn), BF16),
        scratch_shapes=[pltpu.VMEM((tm, d), BF16)],
        compiler_params=_cparams(("parallel", "arbitrary")),
        name="inproj",
    )(x, g, w_head, w_tail)


def _lru_gate_matmuls(u, wax_ref):
    bw = u.shape[1] // LRU_BLOCKS
    r_parts, i_parts = [], []
    for n in range(LRU_BLOCKS):
        ri = _dot(u[:, n * bw:(n + 1) * bw].astype(BF16), wax_ref[n])
        r_parts.append(ri[:, :bw])
        i_parts.append(ri[:, bw:])
    return jnp.concatenate(r_parts, axis=1), jnp.concatenate(i_parts, axis=1)


def _lru_gate_values(r_pre, i_pre, ba, bx, lam):
    r = jax.nn.sigmoid(r_pre + ba)
    i = jax.nn.sigmoid(i_pre + bx)
    log_a = LRU_C * r * jax.nn.log_sigmoid(lam)
    a = jnp.exp(log_a)
    m2 = -jnp.tanh(log_a) * (a * a + 1.0)
    mult = jnp.where(m2 > 0.0, m2 * lax.rsqrt(m2), 0.0)
    return a, i, mult


def _lru_gates(u, wax_ref, ba, bx, lam):
    return _lru_gate_values(*_lru_gate_matmuls(u, wax_ref), ba, bx, lam)


def _lru_conv(x, halo, seq_start, cw, cb):
    tt = x.shape[0]
    xpad = jnp.concatenate([jnp.where(seq_start, 0.0, halo), x], axis=0)
    base = SUBLANES - (CONV_W - 1)
    return cb + sum(xpad[base + k:base + k + tt] * cw[k:k + 1] for k in range(CONV_W))


def _lru_scan(u, r_pre, i_pre, gate, carry, seq_start, ba, bx, lam):
    tt, width = u.shape
    carry = jnp.where(seq_start, 0.0, carry)
    a, i, mult = _lru_gate_values(r_pre, i_pre, ba, bx, lam)
    first = jnp.logical_and(lax.broadcasted_iota(jnp.int32, a.shape, 0) == 0, seq_start)
    mult = jnp.where(first, 1.0, mult)
    a = jnp.where(first, 0.0, a)
    v = u * i * mult

    a = a.reshape(tt // SUBLANES, SUBLANES, width)
    v = v.reshape(tt // SUBLANES, SUBLANES, width)
    sub = lax.broadcasted_iota(jnp.int32, a.shape, 1)
    s = 1
    while s < SUBLANES:
        keep = sub >= s
        v = jnp.where(keep, a * pltpu.roll(v, s, axis=1) + v, v)
        a = jnp.where(keep, a * pltpu.roll(a, s, axis=1), a)
        s *= 2
    groups = []
    for g in range(tt // SUBLANES):
        hg = a[g] * carry + v[g]
        carry = hg[SUBLANES - 1:SUBLANES]
        groups.append(hg)
    h = jnp.concatenate(groups, axis=0)
    return h * jax.nn.gelu(gate), carry


def _inproj_lru_body(x_ref, g_ref, wh_ref, wt_ref, cw_ref, cb_ref, wax_ref, ba_ref, bx_ref, lam_ref,
                     o_ref, ya_ref, hlast_ref, bufout_ref, h_scr, lru_new, lru_cur, halo_scr, carry_scr,
                     *, n_head, n_j, n_tiles, tiles_per_seq, sub_rows):
    i = pl.program_id(0)
    j = pl.program_id(1)
    tm, tn = o_ref.shape
    w = ya_ref.shape[1]
    n_sub = tm // sub_rows
    prev_tile = jnp.maximum(i - 1, 0)

    @pl.when(jnp.logical_and(i == 0, j == 0))
    def _():
        lru_cur[...] = jnp.zeros_like(lru_cur)
        halo_scr[...] = jnp.zeros_like(halo_scr)
        carry_scr[...] = jnp.zeros_like(carry_scr)

    n_piece = LRU_PIECES
    piece_rows = sub_rows // n_piece
    mx_w = 2 * LANES
    col_cuts = [min(tn, mx_w * (q * (tn // mx_w) // n_piece)) for q in range(n_piece)] + [tn]

    def lru_piece(q, state):
        r0 = pl.multiple_of(j * sub_rows, sub_rows) + q * piece_rows
        seq_start = jnp.logical_and(jnp.logical_and(prev_tile % tiles_per_seq == 0, j == 0), q == 0)
        x = lru_cur[pl.ds(r0, piece_rows), 0:w].astype(F32)
        gate = lru_cur[pl.ds(r0, piece_rows), w:2 * w].astype(F32)
        halo, carry = state if state is not None else (halo_scr[...], carry_scr[...])
        u = _lru_conv(x, halo, seq_start, cw_ref[...], cb_ref[...])
        r_pre, i_pre = _lru_gate_matmuls(u, wax_ref)
        ya, carry = _lru_scan(u, r_pre, i_pre, gate, carry, seq_start, ba_ref[...], bx_ref[...], lam_ref[...])
        ya_ref[pl.ds(r0, piece_rows), :] = ya.astype(ya_ref.dtype)
        halo = x[piece_rows - SUBLANES:piece_rows]
        if q == n_piece - 1:
            halo_scr[...] = halo
            carry_scr[...] = carry

            @pl.when(jnp.logical_and(prev_tile % tiles_per_seq == tiles_per_seq - 1, j == n_sub - 1))
            def _():
                hlast_ref[0] = carry
                bufout_ref[0] = x[piece_rows - (CONV_W - 1):piece_rows]
        return halo, carry

    def project_piece(q, from_head, lru_lo):
        c0, c1 = col_cuts[q], col_cuts[q + 1]
        if c0 == c1:
            return
        w_cols = wh_ref[0, :, c0:c1] if from_head else wt_ref[:, c0:c1]
        o = _dot(h_scr[...], w_cols).astype(o_ref.dtype)
        o_ref[:, c0:c1] = o
        if lru_lo is not None:
            keep = min(lru_lo + c1, 2 * w) - (lru_lo + c0)
            if keep > 0:
                lru_new[:, lru_lo + c0:lru_lo + c0 + keep] = o[:, 0:keep]

    def column_steps(lo, hi, from_head, prologue=None, lru_lo=None):
        for a, b, with_lru in ((lo, min(hi, n_sub), True), (max(lo, n_sub), hi, False)):
            if a < b:
                @pl.when(jnp.logical_and(i < n_tiles, jnp.logical_and(j >= a, j < b)))
                def _(with_lru=with_lru):
                    if prologue is not None:
                        prologue()
                    state = None
                    for q in range(n_piece):
                        project_piece(q, from_head, lru_lo)
                        if with_lru:
                            state = lru_piece(q, state)
                    if b == n_j:
                        @pl.when(j == n_j - 1)
                        def _():
                            lru_cur[...] = lru_new[...]

    def normalise():
        h_scr[...] = _rms(x_ref[...], g_ref[...]).astype(BF16)

    column_steps(0, 1, True, prologue=normalise, lru_lo=0)
    column_steps(1, 2, True, lru_lo=tn)
    column_steps(2, n_head, True)
    column_steps(n_head, n_j, False)

    @pl.when(jnp.logical_and(i == n_tiles, j < n_sub))
    def _():
        state = None
        for q in range(n_piece):
            state = lru_piece(q, state)


def _inproj_lru(x, g, w_head, w_tail, lw, seq_len, tm, tn, n_head, sub_rows):
    t, d = x.shape
    w = lw["lru_lambda"].shape[1]
    n_tail = w_tail.shape[1] // tn
    n_tiles, n_j = t // tm, n_head + n_tail
    tiles_per_seq = seq_len // tm
    assert seq_len % tm == 0 and tm % sub_rows == 0 and tm // sub_rows < n_j and n_head >= 2
    assert w <= tn and 2 * w <= 2 * tn and 2 * w > tn
    cur_tile = lambda i: jnp.minimum(i, n_tiles - 1)
    lru_tile = lambda i: jnp.maximum(i - 1, 0)
    body = functools.partial(_inproj_lru_body, n_head=n_head, n_j=n_j, n_tiles=n_tiles,
                             tiles_per_seq=tiles_per_seq, sub_rows=sub_rows)
    out_col = lambda i, j: jnp.where(i < n_tiles, j, n_j - 1)
    w_col = lambda i, j: jnp.where(i < n_tiles, j, n_j - 1)
    return pl.pallas_call(
        body,
        grid=(n_tiles + 1, n_j),
        in_specs=[pl.BlockSpec((tm, d), lambda i, j: (cur_tile(i), 0)),
                  pl.BlockSpec((1, d), lambda i, j: (0, 0)),
                  pl.BlockSpec((1, d, tn), lambda i, j: (0, 0, jnp.minimum(w_col(i, j), n_head - 1))),
                  pl.BlockSpec((d, tn), lambda i, j: (0, jnp.maximum(w_col(i, j) - n_head, 0))),
                  _const_spec((CONV_W, w)), _const_spec((1, w)),
                  _const_spec(lw["wax"].shape), _const_spec((1, w)), _const_spec((1, w)), _const_spec((1, w))],
        out_specs=[pl.BlockSpec((tm, tn), lambda i, j: (cur_tile(i), out_col(i, j))),
                   pl.BlockSpec((tm, w), lambda i, j: (lru_tile(i), 0)),
                   pl.BlockSpec((1, 1, w), lambda i, j: (lru_tile(i) // tiles_per_seq, 0, 0)),
                   pl.BlockSpec((1, CONV_W - 1, w), lambda i, j: (lru_tile(i) // tiles_per_seq, 0, 0))],
        out_shape=[jax.ShapeDtypeStruct((t, n_j * tn), BF16),
                   jax.ShapeDtypeStruct((t, w), BF16),
                   jax.ShapeDtypeStruct((t // seq_len, 1, w), F32),
                   jax.ShapeDtypeStruct((t // seq_len, CONV_W - 1, w), F32)],
        scratch_shapes=[pltpu.VMEM((tm, d), BF16), pltpu.VMEM((tm, 2 * w), BF16), pltpu.VMEM((tm, 2 * w), BF16),
                        pltpu.VMEM((SUBLANES, w), F32), pltpu.VMEM((1, w), F32)],
        compiler_params=_cparams(("arbitrary", "arbitrary")),
        name="inproj_lru",
    )(x, g, w_head, w_tail, lw["lru_conv_w"], lw["lru_conv_b"], lw["wax"], lw["lru_ba"], lw["lru_bx"],
      lw["lru_lambda"])


def _lru_step_body(xin_ref, gate_ref, buf_ref, h0_ref, cw_ref, cb_ref, wax_ref, ba_ref, bx_ref, lam_ref,
                   ya_ref, hlast_ref, bufout_ref, *, steps):
    bsz = h0_ref.shape[0]
    n = steps * bsz
    x = xin_ref[...].astype(F32)
    xx = jnp.concatenate([buf_ref[...], x], axis=0)
    cw = cw_ref[...]
    u = cb_ref[...] + sum(xx[k * bsz:k * bsz + n] * cw[k:k + 1] for k in range(CONV_W))
    a, i, mult = _lru_gates(u, wax_ref, ba_ref[...], bx_ref[...], lam_ref[...])
    v = u * i * mult
    h = h0_ref[...]
    for t in range(steps):
        sl = slice(t * bsz, (t + 1) * bsz)
        h = a[sl] * h + v[sl]
        ya_ref[sl, :] = (h * jax.nn.gelu(gate_ref[sl, :].astype(F32))).astype(ya_ref.dtype)
    hlast_ref[...] = h
    bufout_ref[...] = xx[steps * bsz:(steps + CONV_W - 1) * bsz]


def _lru_step(xin_t, gate_t, buf_t, h0, lw, steps):
    bsz, w = h0.shape
    body = functools.partial(_lru_step_body, steps=steps)
    return pl.pallas_call(
        body,
        out_shape=[jax.ShapeDtypeStruct((steps * bsz, w), BF16),
                   jax.ShapeDtypeStruct((bsz, w), F32),
                   jax.ShapeDtypeStruct(((CONV_W - 1) * bsz, w), F32)],
        compiler_params=pltpu.CompilerParams(vmem_limit_bytes=VMEM_LIMIT),
        name="lru_step",
    )(xin_t, gate_t, buf_t, h0, lw["lru_conv_w"], lw["lru_conv_b"], lw["wax"],
      lw["lru_ba"], lw["lru_bx"], lw["lru_lambda"])


def _expand_heads(cols, e2):
    q = cols[0].shape[0]
    v = jnp.concatenate(cols, axis=0)
    hi = v.astype(BF16)
    lo = (v - hi.astype(F32)).astype(BF16)
    out = _dot(jnp.concatenate([hi, lo], axis=1), e2)
    return [out[i * q:(i + 1) * q] for i in range(len(cols))]


def _ssd_chunk(xc, dt, p, e2, s_get, s_set, t_col, t_row, n_seg=1, n_valid=None):
    q = xc.shape[0]
    rps = q // n_seg
    gn = SSD_GROUPS * SSD_STATE
    d_inner = xc.shape[1] - 2 * gn
    hpg = d_inner // SSD_HEADDIM // SSD_GROUPS
    gw = hpg * SSD_HEADDIM

    causal = t_col >= t_row
    if n_seg > 1:
        same = (lax.broadcasted_iota(jnp.int32, (q, 1), 0) // rps) == (lax.broadcasted_iota(jnp.int32, (1, q), 1) // rps)
        causal = jnp.logical_and(same, causal)
    if n_valid is not None:
        dt = jnp.where(lax.broadcasted_iota(jnp.int32, dt.shape, 0) % rps < n_valid, dt, 0.0)
    a = dt * p["A"]
    cum = _dot_f32(causal.astype(F32), a)
    cum_t = cum.T
    total = cum[q - 1:q, :] if n_seg == 1 else _dot_f32(same.astype(F32), a)
    dt_x, to_end_x, ecum_x = _expand_heads([dt, jnp.exp(total - cum), jnp.exp(cum)], e2)

    xs = xc[:, :d_inner]
    xdt = xs * dt_x
    xw = xdt * to_end_x
    packed = rps % (2 * SUBLANES) == 0
    xdt_m = xdt.astype(BF16) if packed else xdt
    if packed:
        xw = xw.astype(BF16)
    lane_head = lax.broadcasted_iota(jnp.int32, (1, gw), 1) // SSD_HEADDIM
    y_groups = []
    for g in range(SSD_GROUPS):
        sl = slice(g * gw, (g + 1) * gw)
        bg = xc[:, d_inner + g * SSD_STATE:d_inner + (g + 1) * SSD_STATE]
        cg = xc[:, d_inner + gn + g * SSD_STATE:d_inner + gn + (g + 1) * SSD_STATE]
        if packed:
            bg, cg = bg.astype(BF16), cg.astype(BF16)
        cb = _dot_nt(cg.astype(BF16), bg.astype(BF16))
        m_heads, x_heads = [], []
        for hh in range(hpg):
            h = g * hpg + hh
            decay = jnp.exp(jnp.where(causal, cum[:, h:h + 1] - cum_t[h:h + 1, :], -jnp.inf))
            m_heads.append((cb * decay).astype(BF16))
            x_heads.append(jnp.where(lane_head == hh, xdt_m[:, sl], 0.0))
        y_diag = _dot(jnp.concatenate(m_heads, axis=1), jnp.concatenate(x_heads, axis=0).astype(BF16))
        y_off = []
        for b in range(n_seg):
            rows = slice(b * rps, (b + 1) * rps)
            s_old = s_get(b, g)
            y_off.append(_dot_nt(cg[rows].astype(BF16), s_old.astype(BF16)))
            s_dec = [s_old[hh * SSD_HEADDIM:(hh + 1) * SSD_HEADDIM, :]
                     * jnp.exp(cum_t[g * hpg + hh:g * hpg + hh + 1, (b + 1) * rps - 1:(b + 1) * rps])
                     for hh in range(hpg)]
            s_set(b, g, jnp.concatenate(s_dec, axis=0) + _dot_tn(xw[rows, sl].astype(BF16), bg[rows].astype(BF16)))
        y_off = y_off[0] if n_seg == 1 else jnp.concatenate(y_off, axis=0)
        y_groups.append(y_diag + y_off * ecum_x[:, sl])
    return jnp.concatenate(y_groups, axis=1) + p["D"] * xs


def _ssd_gate_norm(y, z, p):
    gw = y.shape[1] // SSD_GROUPS
    zf = z.astype(F32)
    out = []
    for g in range(SSD_GROUPS):
        sl = slice(g * gw, (g + 1) * gw)
        v = y[:, sl] * (zf[:, sl] * jax.nn.sigmoid(zf[:, sl]))
        out.append(v * lax.rsqrt(jnp.mean(v * v, axis=-1, keepdims=True) + EPS) * p["norm_g"][:, sl])
    return jnp.concatenate(out, axis=1)


def _ssd_conv(xpad, q, cw, cb):
    base = SUBLANES - (CONV_W - 1)
    y = cb + sum(xpad[base + k:base + k + q] * cw[k:k + 1] for k in range(CONV_W))
    return y * jax.nn.sigmoid(y)


def _softplus(x):
    return jax.nn.softplus(x)


def _ssd_params(cw_ref, cb_ref, dtb_ref, a_ref, d_ref, ng_ref):
    return {"cw": cw_ref[...], "cb": cb_ref[...], "dt_bias": dtb_ref[...], "A": a_ref[...],
            "D": d_ref[...], "norm_g": ng_ref[...]}


def _ssd_prompt_body(xbc_ref, z_ref, dt_ref, cw_ref, cb_ref, dtb_ref, a_ref, d_ref, ng_ref, e2_ref,
                     yb_ref, sout_ref, bufout_ref, x_scr, dt_scr, y_scr, s_scr, *, q):
    c = pl.program_id(1)
    rows = xbc_ref.shape[1]
    half = q // 2
    n_xslab = x_scr.shape[0]
    base = SUBLANES - (CONV_W - 1)

    @pl.when(c == 0)
    def _():
        x_scr[:, 0:SUBLANES, :] = jnp.zeros((n_xslab, SUBLANES, LANES), F32)
        s_scr[...] = jnp.zeros_like(s_scr)

    @pl.when(c > 0)
    def _():
        x_scr[:, 0:SUBLANES, :] = x_scr[:, rows:rows + SUBLANES, :]

    p = _ssd_params(cw_ref, cb_ref, dtb_ref, a_ref, d_ref, ng_ref)
    x = xbc_ref[0].astype(F32)
    for j in range(n_xslab):
        x_scr[j, SUBLANES:SUBLANES + rows, :] = x[:, j * LANES:(j + 1) * LANES]
    dt_scr[...] = _softplus(dt_ref[0].astype(F32) + p["dt_bias"])

    def times(shape, axis):
        pos = lax.broadcasted_iota(jnp.int32, shape, axis)
        return jnp.where(pos < half, 2 * pos, 2 * (pos - half) + 1)

    def s_set(b, g, v):
        s_scr[g] = v

    for ch in range(rows // q):
        r0 = ch * q
        cols = []
        for j in range(n_xslab):
            ls = slice(j * LANES, (j + 1) * LANES)
            halves = []
            for par in range(2):
                acc = p["cb"][:, ls]
                for k in range(CONV_W):
                    acc = acc + x_scr[j, pl.ds(r0 + base + k + par, half, stride=2), :] * p["cw"][k:k + 1, ls]
                halves.append(acc)
            cols.append(jnp.concatenate(halves, axis=0))
        xc = jnp.concatenate(cols, axis=1)
        xc = xc * jax.nn.sigmoid(xc)
        dt = jnp.concatenate([dt_scr[pl.ds(r0 + par, half, stride=2), :] for par in range(2)], axis=0)
        y = _ssd_chunk(xc, dt, p, e2_ref[...], lambda b, g: s_scr[g], s_set, times((q, 1), 0), times((1, q), 1))
        for j in range(y_scr.shape[0]):
            for par in range(2):
                y_scr[j, pl.ds(par, half, stride=2), :] = y[par * half:(par + 1) * half, j * LANES:(j + 1) * LANES]
        y = jnp.concatenate([y_scr[j] for j in range(y_scr.shape[0])], axis=1)
        yb_ref[0, r0:r0 + q, :] = _ssd_gate_norm(y, z_ref[0, r0:r0 + q, :], p).astype(yb_ref.dtype)

    @pl.when(c == pl.num_programs(1) - 1)
    def _():
        sout_ref[0] = s_scr[...]
        bufout_ref[0] = x[rows - (CONV_W - 1):rows]


def _ssd_prompt(proj3, sp, col_xbc, col_z, col_dt):
    b, l, _ = proj3.shape
    cdim = sp["ssd_conv_w"].shape[1]
    d_inner = sp["ssd_norm_g"].shape[1]
    gw = d_inner // SSD_GROUPS
    q = SSD_CHUNK if l % SSD_CHUNK == 0 else l
    assert q % (2 * SUBLANES) == 0
    rows = q * SSD_CHUNKS_PER_STEP if l % (q * SSD_CHUNKS_PER_STEP) == 0 else q
    return pl.pallas_call(
        functools.partial(_ssd_prompt_body, q=q),
        grid=(b, l // rows),
        in_specs=[pl.BlockSpec((1, rows, cdim), lambda i, c: (i, c, col_xbc // cdim)),
                  pl.BlockSpec((1, rows, d_inner), lambda i, c: (i, c, col_z // d_inner)),
                  pl.BlockSpec((1, rows, LANES), lambda i, c: (i, c, col_dt // LANES)),
                  _const_spec((CONV_W, cdim)), _const_spec((1, cdim)), _const_spec((1, LANES)),
                  _const_spec((1, LANES)), _const_spec((1, d_inner)), _const_spec((1, d_inner)),
                  _const_spec((2 * LANES, d_inner))],
        out_specs=[pl.BlockSpec((1, rows, d_inner), lambda i, c: (i, c, 0)),
                   pl.BlockSpec((1, SSD_GROUPS, gw, SSD_STATE), lambda i, c: (i, 0, 0, 0)),
                   pl.BlockSpec((1, CONV_W - 1, cdim), lambda i, c: (i, 0, 0))],
        out_shape=[jax.ShapeDtypeStruct((b, l, d_inner), BF16),
                   jax.ShapeDtypeStruct((b, SSD_GROUPS, gw, SSD_STATE), F32),
                   jax.ShapeDtypeStruct((b, CONV_W - 1, cdim), F32)],
        scratch_shapes=[pltpu.VMEM((cdim // LANES, rows + SUBLANES, LANES), F32),
                        pltpu.VMEM((rows, LANES), F32),
                        pltpu.VMEM((d_inner // LANES, q, LANES), F32),
                        pltpu.VMEM((SSD_GROUPS, gw, SSD_STATE), F32)],
        compiler_params=_cparams(("parallel", "arbitrary")),
        name="ssd_prompt",
    )(proj3, proj3, proj3, sp["ssd_conv_w"], sp["ssd_conv_b"], sp["dt_bias"], sp["A"], sp["D"], sp["ssd_norm_g"],
      sp["head_expand"])


def _ssd_step_body(xbc_ref, z_ref, dt_ref, buf_ref, s_ref, cw_ref, cb_ref, dtb_ref, a_ref, d_ref, ng_ref, e2_ref,
                   yb_ref, sout_ref, bufout_ref, *, steps, nb):
    p = _ssd_params(cw_ref, cb_ref, dtb_ref, a_ref, d_ref, ng_ref)
    rps = SUBLANES
    q = nb * rps
    x_all = xbc_ref[...].astype(F32)
    z_all = z_ref[...].astype(F32)
    dt_all = _softplus(dt_ref[...].astype(F32) + p["dt_bias"])
    cdim = x_all.shape[1]

    def padded(v, j):
        return jnp.concatenate([v[j * steps:(j + 1) * steps], jnp.zeros((rps - steps, v.shape[1]), v.dtype)], axis=0)

    xcs = []
    for j in range(nb):
        xpad = jnp.concatenate([jnp.zeros((SUBLANES - (CONV_W - 1), cdim), F32), buf_ref[j], padded(x_all, j)],
                               axis=0)
        xcs.append(_ssd_conv(xpad, rps, p["cw"], p["cb"]))
        bufout_ref[j] = xpad[SUBLANES + steps - (CONV_W - 1):SUBLANES + steps]
    xc = jnp.concatenate(xcs, axis=0)
    dt = jnp.concatenate([padded(dt_all, j) for j in range(nb)], axis=0)
    z = jnp.concatenate([padded(z_all, j) for j in range(nb)], axis=0)

    def s_set(b, g, v):
        sout_ref[b, g] = v

    t_col = lax.broadcasted_iota(jnp.int32, (q, 1), 0) % rps
    t_row = lax.broadcasted_iota(jnp.int32, (1, q), 1) % rps
    y = _ssd_chunk(xc, dt, p, e2_ref[...], lambda b, g: s_ref[b, g], s_set, t_col, t_row, n_seg=nb, n_valid=steps)
    y = _ssd_gate_norm(y, z, p)
    for j in range(nb):
        yb_ref[j * steps:(j + 1) * steps, :] = y[j * rps:j * rps + steps].astype(yb_ref.dtype)


def _ssd_step(proj, buf, s0, sp, steps, nb, col_xbc, col_z, col_dt):
    bsz = s0.shape[0]
    cdim = sp["ssd_conv_w"].shape[1]
    d_inner = sp["ssd_norm_g"].shape[1]
    gw = d_inner // SSD_GROUPS
    rows = nb * steps
    body = functools.partial(_ssd_step_body, steps=steps, nb=nb)
    return pl.pallas_call(
        body,
        grid=(bsz // nb,),
        in_specs=[pl.BlockSpec((rows, cdim), lambda i: (i, col_xbc // cdim)),
                  pl.BlockSpec((rows, d_inner), lambda i: (i, col_z // d_inner)),
                  pl.BlockSpec((rows, LANES), lambda i: (i, col_dt // LANES)),
                  pl.BlockSpec((None, nb, CONV_W - 1, cdim), lambda i: (0, i, 0, 0)),
                  pl.BlockSpec((nb, SSD_GROUPS, gw, SSD_STATE), lambda i: (i, 0, 0, 0)),
                  _const_spec((CONV_W, cdim)), _const_spec((1, cdim)), _const_spec((1, LANES)),
                  _const_spec((1, LANES)), _const_spec((1, d_inner)), _const_spec((1, d_inner)),
                  _const_spec((2 * LANES, d_inner))],
        out_specs=[pl.BlockSpec((rows, d_inner), lambda i: (i, 0)),
                   pl.BlockSpec((nb, SSD_GROUPS, gw, SSD_STATE), lambda i: (i, 0, 0, 0)),
                   pl.BlockSpec((None, nb, CONV_W - 1, cdim), lambda i: (0, i, 0, 0))],
        out_shape=[jax.ShapeDtypeStruct((bsz * steps, d_inner), BF16),
                   jax.ShapeDtypeStruct(s0.shape, F32),
                   jax.ShapeDtypeStruct(buf.shape, F32)],
        compiler_params=_cparams(("parallel",)),
        name="ssd_step",
    )(proj, proj, proj, buf, s0, sp["ssd_conv_w"], sp["ssd_conv_b"], sp["dt_bias"], sp["A"], sp["D"],
      sp["ssd_norm_g"], sp["head_expand"])


def _router(t, wr, br):
    t_hi = t.astype(BF16)
    t_lo = (t - t_hi.astype(F32)).astype(BF16)
    both = _dot(jnp.concatenate([t_hi, t_lo], axis=1), wr)
    logits = both[:, :LANES] + both[:, LANES:] + br
    lane = lax.broadcasted_iota(jnp.int32, logits.shape, 1)
    neg = -jnp.inf
    gl = jnp.where(lane < N_EGROUPS, logits, neg)
    gmax = jnp.max(gl, axis=-1, keepdims=True)
    g_idx = jnp.min(jnp.where(gl == gmax, lane, LANES), axis=-1, keepdims=True)
    g_w = 1.0 / jnp.sum(jnp.exp(gl - gmax), axis=-1, keepdims=True)
    in_grp = jnp.logical_and(jnp.logical_and(lane >= ROUTER_LANE0, lane < ROUTER_LANE0 + N_EXPERTS),
                             ((lane - ROUTER_LANE0) >> 2) == g_idx)
    el = jnp.where(in_grp, logits, neg)
    pe = jnp.exp(el - jnp.max(el, axis=-1, keepdims=True))
    pe = pe / jnp.sum(pe, axis=-1, keepdims=True)
    cand = jnp.where(in_grp, pe, -1.0)
    v1 = jnp.max(cand, axis=-1, keepdims=True)
    i1 = jnp.min(jnp.where(cand == v1, lane, LANES), axis=-1, keepdims=True)
    cand2 = jnp.where(lane == i1, -1.0, cand)
    v2 = jnp.max(cand2, axis=-1, keepdims=True)
    i2 = jnp.min(jnp.where(jnp.logical_and(cand2 == v2, in_grp), lane, LANES), axis=-1, keepdims=True)
    den = v1 + v2
    return lane, i1, i2, g_w * v1 / den, g_w * v2 / den


def _rows_to_tiles(ref, val):
    n, d = val.shape
    for k in range(d // LANES):
        ref[pl.ds(k, n, stride=d // LANES), :] = val[:, k * LANES:(k + 1) * LANES]


def _tiles_to_rows(ref, n, d, start=0):
    return jnp.concatenate([ref[pl.ds(start + k, n, stride=d // LANES), :] for k in range(d // LANES)], axis=1)


def _merge_body(x_ref, ya_ref, yb_ref, ga_ref, gb_ref, wl_ref, ws_ref, wo_ref, gf_ref, wr_ref, br_ref, cnt0_ref,
                x1_ref, t_ref, rt_ref, rtt_ref, cnt_ref, base_scr):
    step = pl.program_id(0)

    @pl.when(step == 0)
    def _():
        base_scr[...] = cnt0_ref[...]

    a = _dot(ya_ref[...], wl_ref[...])
    b = _dot(yb_ref[...], ws_ref[...])
    merged = jax.nn.sigmoid(ga_ref[...].astype(F32)) * a + jax.nn.sigmoid(gb_ref[...].astype(F32)) * b
    x1 = x_ref[...] + _dot(merged.astype(BF16), wo_ref[...])
    x1_ref[...] = x1
    t = _rms(x1, gf_ref[...])
    _rows_to_tiles(t_ref, t)
    lane, i1, i2, wg1, wg2 = _router(t, wr_ref[...], br_ref[...])

    tm = t.shape[0]
    onehot = jnp.where(jnp.logical_or(lane == i1, lane == i2), 1.0, 0.0).astype(BF16)
    tri = (lax.broadcasted_iota(jnp.int32, (tm, tm), 1) <= lax.broadcasted_iota(jnp.int32, (tm, tm), 0)).astype(BF16)
    cum = _dot(tri, onehot) + base_scr[...]
    r1 = jnp.sum(jnp.where(lane == i1, cum, 0.0), axis=-1, keepdims=True) - 1.0
    r2 = jnp.sum(jnp.where(lane == i2, cum, 0.0), axis=-1, keepdims=True) - 1.0
    cols = (wg1, wg2, r1, r2, (i1 - ROUTER_LANE0).astype(F32), (i2 - ROUTER_LANE0).astype(F32))
    rt = jnp.zeros(cum.shape, F32)
    for k, c in enumerate(cols):
        rt = jnp.where(lane == k, c, rt)
    rt_ref[...] = rt
    rtt_ref[...] = rt.T[0:SUBLANES, :]
    base_scr[...] = cum[tm - 1:tm, :]
    cnt_ref[...] = cum[tm - 1:tm, :]


def _merge(x, ya, yb, proj, mw, tm, col_ga, col_gb, counts_so_far):
    t, d = x.shape
    d_inner = yb.shape[1]
    return pl.pallas_call(
        _merge_body,
        grid=(t // tm,),
        in_specs=[pl.BlockSpec((tm, d), lambda i: (i, 0)),
                  pl.BlockSpec((tm, d), lambda i: (i, 0)),
                  pl.BlockSpec((tm, d_inner), lambda i: (i, 0)),
                  pl.BlockSpec((tm, d), lambda i: (i, col_ga // d)),
                  pl.BlockSpec((tm, d), lambda i: (i, col_gb // d)),
                  _const_spec((d, d)), _const_spec((d_inner, d)), _const_spec((d, d)),
                  _const_spec((1, d)), _const_spec((2 * d, 2 * LANES)), _const_spec((1, LANES)),
                  _const_spec((1, LANES))],
        out_specs=[pl.BlockSpec((tm, d), lambda i: (i, 0)),
                   pl.BlockSpec((tm * d // LANES, LANES), lambda i: (i, 0)),
                   pl.BlockSpec((tm, LANES), lambda i: (i, 0)),
                   pl.BlockSpec((SUBLANES, tm), lambda i: (0, i)),
                   pl.BlockSpec((1, LANES), lambda i: (0, 0))],
        out_shape=[jax.ShapeDtypeStruct((t, d), F32),
                   jax.ShapeDtypeStruct((t * d // LANES, LANES), F32),
                   jax.ShapeDtypeStruct((t, LANES), F32),
                   jax.ShapeDtypeStruct((SUBLANES, t), F32),
                   jax.ShapeDtypeStruct((1, LANES), F32)],
        scratch_shapes=[pltpu.VMEM((1, LANES), F32)],
        compiler_params=_cparams(("arbitrary",)),
        name="merge_router",
    )(x, ya, yb, proj, proj, mw["w_br_lru"], mw["w_br_ssd"], mw["w_out"], mw["g_ffn"], mw["w_router"],
      mw["b_router"], counts_so_far)


def _dispatch_body(zb_ref, *rest, n_groups, tms, n_toks, nk, tmg):
    dest_refs, t_refs = rest[:n_groups], rest[n_groups:2 * n_groups]
    o_ref, zero_scr, sem, zsem = rest[2 * n_groups:]
    step = pl.program_id(0)

    @pl.when(step == 0)
    def _():
        zero_scr[...] = jnp.zeros_like(zero_scr)
        blk = tmg * nk

        def zcopy(j):
            return pltpu.make_async_copy(zero_scr, o_ref.at[pl.ds(pl.multiple_of(zb_ref[j] * blk, blk), blk)], zsem)

        for j in range(zb_ref.shape[0]):
            pl.when(zb_ref[j] >= 0)(lambda j=j: zcopy(j).start())
        for j in range(zb_ref.shape[0]):
            pl.when(zb_ref[j] >= 0)(lambda j=j: zcopy(j).wait())

    first_step = 0
    for g in range(n_groups):
        tm, n_tok, steps = tms[g], n_toks[g], n_toks[g] // tms[g]

        @pl.when(jnp.logical_and(step >= first_step, step < first_step + steps))
        def _(g=g, tm=tm, n_tok=n_tok, first_step=first_step):
            t_ref, dest_ref = t_refs[g], dest_refs[g]

            def issue(r, carry):
                src = t_ref.at[pl.ds(pl.multiple_of(r * nk, nk), nk)]
                for k in range(2):
                    row = dest_ref[k * n_tok + (step - first_step) * tm + r]
                    pltpu.make_async_copy(src, o_ref.at[pl.ds(pl.multiple_of(row * nk, nk), nk)],
                                          sem).start(priority=k)
                return carry

            lax.fori_loop(0, tm, issue, 0, unroll=8)
            for k in range(2):
                pltpu.make_async_copy(t_ref, o_ref.at[pl.ds(0, tm * nk)], sem).wait()

        first_step += steps


def _dispatch(dests, zero_blocks, t_tiles, n_toks, tms, n_rows, tmg):
    n_groups = len(dests)
    nk = t_tiles[0].shape[0] // n_toks[0]
    steps = [n // tm for n, tm in zip(n_toks, tms)]
    starts = [sum(steps[:g]) for g in range(n_groups)]
    block_of = lambda g: (lambda i, *_: (jnp.clip(i - starts[g], 0, steps[g] - 1), 0))
    return pl.pallas_call(
        functools.partial(_dispatch_body, n_groups=n_groups, tms=tuple(tms), n_toks=tuple(n_toks), nk=nk, tmg=tmg),
        grid_spec=pltpu.PrefetchScalarGridSpec(
            num_scalar_prefetch=1 + n_groups,
            grid=(sum(steps),),
            in_specs=[pl.BlockSpec((tms[g] * nk, LANES), block_of(g)) for g in range(n_groups)],
            out_specs=pl.BlockSpec(memory_space=pl.ANY),
            scratch_shapes=[pltpu.VMEM((tmg * nk, LANES), F32), pltpu.SemaphoreType.DMA(()),
                            pltpu.SemaphoreType.DMA(())]),
        out_shape=jax.ShapeDtypeStruct((n_rows * nk, LANES), F32),
        compiler_params=_cparams(("arbitrary",)),
        name="moe_dispatch",
    )(zero_blocks, *dests, *t_tiles)


def _expert_body(te_ref, nt_ref, x_ref, w1_ref, w3_ref, w2_ref, y_ref, w1_scr, w3_scr, w2_scr, *, tmg):
    i = pl.program_id(0)
    real = i < nt_ref[0]
    d = w1_scr.shape[0]

    @pl.when(jnp.logical_or(i == 0, te_ref[i] != te_ref[jnp.maximum(i - 1, 0)]))
    def _():
        w1_scr[...] = w1_ref[0].astype(BF16)
        w3_scr[...] = w3_ref[0].astype(BF16)
        w2_scr[...] = w2_ref[0].astype(BF16)

    @pl.when(real)
    def _():
        x = _tiles_to_rows(x_ref, tmg, d).astype(BF16)
        h1 = _dot(x, w1_scr[...])
        h3 = _dot(x, w3_scr[...])
        _rows_to_tiles(y_ref, _dot((h1 * jax.nn.sigmoid(h1) * h3).astype(BF16), w2_scr[...]))

    @pl.when(jnp.logical_not(real))
    def _():
        y_ref[...] = jnp.zeros_like(y_ref)


def _experts(tile_expert, n_tiles, xs_tiles, w1, w3, w2, tmg):
    _, d, dff = w1.shape
    blk = tmg * d // LANES
    row_spec = pl.BlockSpec((blk, LANES), lambda i, te, nt: (i, 0))
    return pl.pallas_call(
        functools.partial(_expert_body, tmg=tmg),
        grid_spec=pltpu.PrefetchScalarGridSpec(
            num_scalar_prefetch=2,
            grid=(xs_tiles.shape[0] // blk,),
            in_specs=[row_spec,
                      pl.BlockSpec((1, d, dff), lambda i, te, nt: (te[i], 0, 0)),
                      pl.BlockSpec((1, d, dff), lambda i, te, nt: (te[i], 0, 0)),
                      pl.BlockSpec((1, dff, d), lambda i, te, nt: (te[i], 0, 0))],
            out_specs=row_spec,
            scratch_shapes=[pltpu.VMEM((d, dff), BF16), pltpu.VMEM((d, dff), BF16), pltpu.VMEM((dff, d), BF16)]),
        out_shape=jax.ShapeDtypeStruct(xs_tiles.shape, F32),
        compiler_params=_cparams(("arbitrary",)),
        name="moe_experts",
    )(tile_expert, n_tiles, xs_tiles, w1, w3, w2)


def _ple_body(dest_ref, x_ref, rt_ref, p_ref, wp_ref, gp_ref, gg_ref, wg_ref, gfin_ref, y_hbm, o_ref, gbuf, sem,
              *, n_tok):
    step = pl.program_id(0)
    n_steps = pl.num_programs(0)
    tm, d = x_ref.shape
    nk = d // LANES
    pr = tm // COMBINE_PIECES

    def issue_row(tile, slot, r):
        tok = tile * tm + r
        for k in range(2):
            row = dest_ref[k * n_tok + tok]
            pltpu.make_async_copy(y_hbm.at[pl.ds(pl.multiple_of(row * nk, nk), nk)],
                                  gbuf.at[slot, pl.ds(pl.multiple_of((k * tm + r) * nk, nk), nk)],
                                  sem.at[slot]).start(priority=k)

    def wait_slot(slot):
        pltpu.make_async_copy(y_hbm.at[pl.ds(0, 2 * tm * nk)], gbuf.at[slot], sem.at[slot]).wait()

    @pl.when(step == 0)
    def _():
        def issue(r, carry):
            issue_row(0, 0, r)
            return carry

        lax.fori_loop(0, tm, issue, 0, unroll=8)

    slot = step % 2
    wait_slot(slot)
    nxt_tile = jnp.minimum(step + 1, n_steps - 1)
    rows = gbuf.at[slot]
    for q in range(COMBINE_PIECES):
        sl = slice(q * pr, (q + 1) * pr)
        rt = rt_ref[sl, :]
        x = (x_ref[sl, :] + rt[:, 0:1] * _tiles_to_rows(rows, pr, d, start=q * pr * nk)
             + rt[:, 1:2] * _tiles_to_rows(rows, pr, d, start=(tm + q * pr) * nk))
        e = _rms(_dot(p_ref[sl, :].astype(BF16), wp_ref[...]), gp_ref[...])
        gate = jax.nn.sigmoid(_dot(_rms(x, gg_ref[...]).astype(BF16), wg_ref[...]))
        o_ref[sl, :] = _rms(x + gate * e, gfin_ref[...])
        for r in range(q * pr, (q + 1) * pr):
            issue_row(nxt_tile, 1 - slot, r)

    pl.when(step == n_steps - 1)(lambda: wait_slot(1 - slot))


def _ple(dest, x1, rt, p, y_sorted, pw, tm):
    n, d = x1.shape
    dp = p.shape[1]
    const = lambda shape: pl.BlockSpec(shape, lambda i, *_: (0,) * len(shape))
    return pl.pallas_call(
        functools.partial(_ple_body, n_tok=n),
        grid_spec=pltpu.PrefetchScalarGridSpec(
            num_scalar_prefetch=1,
            grid=(n // tm,),
            in_specs=[pl.BlockSpec((tm, d), lambda i, *_: (i, 0)),
                      pl.BlockSpec((tm, LANES), lambda i, *_: (i, 0)),
                      pl.BlockSpec((tm, dp), lambda i, *_: (i, 0)),
                      const((dp, d)), const((1, d)), const((1, d)), const((d, d)), const((1, d)),
                      pl.BlockSpec(memory_space=pl.ANY)],
            out_specs=pl.BlockSpec((tm, d), lambda i, *_: (i, 0)),
            scratch_shapes=[pltpu.VMEM((2, 2 * tm * d // LANES, LANES), F32), pltpu.SemaphoreType.DMA((2,))]),
        out_shape=jax.ShapeDtypeStruct((n, d), F32),
        compiler_params=_cparams(("arbitrary",)),
        name="combine_ple_final",
    )(dest, x1, rt, p, pw["w_ple_proj"], pw["g_ple"], pw["g_ple_gate"], pw["w_ple_gate"], pw["g_final"], y_sorted)


def _pick_tile(n, pref):
    t = min(n, pref)
    while n % t:
        t //= 2
    return t


def _moe_row_tile(n):
    return 512 if 2 * n // N_EXPERTS >= 1024 else 128
MOE_DISPATCH_TILE = 2048
SSD_CHUNKS_PER_STEP = 4
COMBINE_PIECES = 1
LRU_SLICE_ROWS = 128
LRU_PIECES = 2


def _split_router(w):
    hi = w.astype(BF16)
    lo = (w - hi.astype(F32)).astype(BF16)
    return jnp.concatenate([jnp.concatenate([hi, lo], axis=1),
                            jnp.concatenate([hi, jnp.zeros_like(hi)], axis=1)], axis=0)


def _token_tail(groups, cnt, lw):
    n_all = sum(g["x1"].shape[0] for g in groups)
    tmg = _moe_row_tile(n_all)
    n_blocks = pl.cdiv(2 * n_all, tmg) + N_EXPERTS
    counts = cnt[0, ROUTER_LANE0:ROUTER_LANE0 + N_EXPERTS].astype(jnp.int32)
    tiles = (counts + tmg - 1) // tmg
    ends = jnp.cumsum(tiles)
    n_tiles = ends[-1]
    first_row = (ends - tiles) * tmg
    blk = jnp.arange(n_blocks, dtype=jnp.int32)
    tile_expert = jnp.sum((jnp.minimum(blk, n_tiles - 1)[:, None] >= ends[None, :]).astype(jnp.int32), axis=1)
    tail = n_tiles + blk[:N_EXPERTS]
    zero_blocks = jnp.concatenate([jnp.where(tiles > 0, ends - 1, -1),
                                   jnp.where(tail < n_blocks, tail, -1)]).astype(jnp.int32)

    dests = []
    for g in groups:
        e_idx = g["rtt"][4:6].astype(jnp.int32)
        dest = g["rtt"][2:4].astype(jnp.int32) + sum(jnp.where(e_idx == e, first_row[e], 0) for e in range(N_EXPERTS))
        dests.append(dest.reshape(2 * g["x1"].shape[0]))
    n_toks = [g["x1"].shape[0] for g in groups]
    sorted_t = _dispatch(dests, zero_blocks, [g["t_tiles"] for g in groups], n_toks,
                         [_pick_tile(n, MOE_DISPATCH_TILE) for n in n_toks], n_blocks * tmg, tmg)
    y_sorted = _experts(tile_expert, n_tiles.reshape(1), sorted_t, lw["w1"], lw["w3"], lw["w2"], tmg)
    return [_ple(dest, g["x1"], g["rt"], g["p"], y_sorted, lw, g["tm"]) for g, dest in zip(groups, dests)]


def kernel(x_prompt, x_sample, state_lru_h, state_lru_conv, state_ssd, state_ssd_conv, p_prompt, p_sample, g_mix, w_in, lru_conv_w, lru_conv_b, lru_wa, lru_ba, lru_wx, lru_bx, lru_lambda, ssd_conv_w, ssd_conv_b, ssd_dt_bias, ssd_A_log, ssd_D, ssd_norm_g, w_br_lru, w_br_ssd, w_out, g_ffn, w_router_g, b_router_g, w_router_e, b_router_e, w1, w3, w2, w_ple_proj, g_ple, g_ple_gate, w_ple_gate, g_final):
    depth = w_in.shape[0]
    assert depth == 1, "one decoder layer per call"
    bp, lp, d = x_prompt.shape
    bs, ls, _ = x_sample.shape
    w_lru = state_lru_h.shape[-1]
    heads, hdim, nstate = state_ssd.shape[2:]
    d_inner = heads * hdim
    cdim = state_ssd_conv.shape[-1]
    assert hdim == SSD_HEADDIM and nstate == SSD_STATE and heads <= LANES and ls < SUBLANES
    gw = d_inner // SSD_GROUPS

    o_dt = 2 * w_lru + d_inner + cdim
    n_proj = o_dt + 2 * d + LANES
    tn = n_proj // 9 if n_proj % (9 * LANES) == 0 else LANES
    n_head = o_dt // tn
    w_head = w_in.astype(BF16)
    wi = w_head[0]
    w_tail = jnp.concatenate([wi[:, n_head * tn:o_dt], wi[:, o_dt + heads:], wi[:, o_dt:o_dt + heads],
                              jnp.zeros((d, LANES - heads), BF16)], axis=1)
    assert n_head >= 1 and w_tail.shape[1] == n_proj - n_head * tn and w_tail.shape[1] % tn == 0
    col_z, col_xbc = 2 * w_lru, 2 * w_lru + d_inner
    col_ga, col_gb, col_dt = o_dt, o_dt + d, o_dt + 2 * d
    row = lambda v: v.reshape(1, -1).astype(F32)
    pad_heads = lambda v: jnp.pad(v.astype(F32), (0, LANES - heads)).reshape(1, LANES)
    lw = {
        "lru_conv_w": lru_conv_w[0], "lru_conv_b": row(lru_conv_b[0]),
        "wax": jnp.concatenate([lru_wa[0], lru_wx[0]], axis=-1).astype(BF16),
        "lru_ba": row(lru_ba[0]), "lru_bx": row(lru_bx[0]), "lru_lambda": row(lru_lambda[0]),
        "ssd_conv_w": ssd_conv_w[0], "ssd_conv_b": row(ssd_conv_b[0]),
        "dt_bias": pad_heads(ssd_dt_bias[0]), "A": pad_heads(-jnp.exp(ssd_A_log[0].astype(F32))),
        "D": row(jnp.repeat(ssd_D[0], hdim)), "ssd_norm_g": row(ssd_norm_g[0]),
        "head_expand": jnp.tile(jnp.arange(LANES)[:, None] == jnp.arange(d_inner)[None, :] // hdim, (2, 1)).astype(BF16),
        "w_br_lru": w_br_lru[0].astype(BF16), "w_br_ssd": w_br_ssd[0].astype(BF16), "w_out": w_out[0].astype(BF16),
        "g_ffn": row(g_ffn[0]),
        "w_router": _split_router(jnp.concatenate([w_router_g[0], w_router_e[0],
                                                   jnp.zeros((d, LANES - N_EGROUPS - N_EXPERTS), F32)], axis=1)),
        "b_router": jnp.concatenate([b_router_g[0], b_router_e[0],
                                     jnp.zeros((LANES - N_EGROUPS - N_EXPERTS,), F32)]).reshape(1, LANES),
        "w1": w1[0], "w3": w3[0], "w2": w2[0],
        "w_ple_proj": w_ple_proj[0].astype(BF16), "g_ple": row(g_ple[0]), "g_ple_gate": row(g_ple_gate[0]),
        "w_ple_gate": w_ple_gate[0].astype(BF16), "g_final": row(g_final),
    }
    g_mix_r = row(g_mix[0])

    tp = bp * lp
    xp = x_prompt.reshape(tp, d)
    tm_p = _pick_tile(lp, 1024)
    proj_p, ya_p, hl_p, lbuf_p = _inproj_lru(xp, g_mix_r, w_head, w_tail, lw, lp, tm_p, tn, n_head,
                                             _pick_tile(tm_p, LRU_SLICE_ROWS))
    proj_p3 = proj_p.reshape(bp, lp, n_proj)
    yb_p, s_p, sbuf_p = _ssd_prompt(proj_p3, lw, col_xbc, col_z, col_dt)
    tm_tail_p = _pick_tile(tp, 512)
    x1_p, t_p, rt_p, rtt_p, cnt_p = _merge(xp, ya_p.reshape(tp, w_lru), yb_p.reshape(tp, d_inner), proj_p, lw,
                                           tm_tail_p, col_ga, col_gb, jnp.zeros((1, LANES), F32))
    group_p = {"x1": x1_p, "t_tiles": t_p, "rt": rt_p, "rtt": rtt_p, "p": p_prompt[0].reshape(tp, -1),
               "tm": tm_tail_p}

    ts = bs * ls
    xs = x_sample.reshape(ts, d)
    tm_s = _pick_tile(ts, 512)
    proj_s = _inproj(xs, g_mix_r, w_head, w_tail, tm_s, tn, n_head)
    to_tmajor = lambda v, n: v.reshape(bs, n, -1).transpose(1, 0, 2).reshape(n * bs, -1)
    from_tmajor = lambda v, n: v.reshape(n, bs, -1).transpose(1, 0, 2)
    ya_t, hl_s, lbuf_t = _lru_step(to_tmajor(proj_s[:, :w_lru], ls), to_tmajor(proj_s[:, w_lru:2 * w_lru], ls),
                                   to_tmajor(state_lru_conv[0], CONV_W - 1), state_lru_h[0], lw, ls)
    ya_s = from_tmajor(ya_t, ls).reshape(ts, w_lru)
    lbuf_s = from_tmajor(lbuf_t, CONV_W - 1)
    yb_s, s_s, sbuf_s = _ssd_step(proj_s, state_ssd_conv, state_ssd[0].reshape(bs, SSD_GROUPS, gw, nstate), lw,
                                  ls, _pick_tile(bs, 8), col_xbc, col_z, col_dt)
    x1_s, t_s, rt_s, rtt_s, cnt_all = _merge(xs, ya_s, yb_s, proj_s, lw, tm_s, col_ga, col_gb, cnt_p)
    group_s = {"x1": x1_s, "t_tiles": t_s, "rt": rt_s, "rtt": rtt_s, "p": p_sample[0].reshape(ts, -1), "tm": tm_s}

    y_p, y_s = _token_tail([group_p, group_s], cnt_all, lw)

    return (y_p.reshape(bp, lp, d), y_s.reshape(bs, ls, d),
            hl_p.reshape(1, bp, w_lru), lbuf_p[None],
            s_p.reshape(1, bp, heads, hdim, nstate), sbuf_p[None],
            hl_s[None], lbuf_s[None],
            s_s.reshape(1, bs, heads, hdim, nstate), sbuf_s)
```

```python
import functools

import jax
import jax.numpy as jnp
from jax import lax
from jax.experimental import pallas as pl
from jax.experimental.pallas import tpu as pltpu

F32 = jnp.float32
BF16 = jnp.bfloat16

EPS = 1e-6
CONV_W = 4
LRU_BLOCKS = 8
LRU_C = 8.0
SSD_HEADDIM = 64
SSD_GROUPS = 8
SSD_STATE = 128
SSD_CHUNK = 128
N_EGROUPS = 4
EXP_PER_GROUP = 4
N_EXPERTS = N_EGROUPS * EXP_PER_GROUP

LANES = 128
SUBLANES = 8
VMEM_LIMIT = 52 * 1024 * 1024
ROUTER_LANE0 = N_EGROUPS


def _cparams(sem):
    return pltpu.CompilerParams(dimension_semantics=sem, vmem_limit_bytes=VMEM_LIMIT)


def _dot(a, b):
    return jnp.dot(a, b, preferred_element_type=F32)


def _dot_nt(a, b):
    return lax.dot_general(a, b, (((1,), (1,)), ((), ())), preferred_element_type=F32)


def _dot_tn(a, b):
    return lax.dot_general(a, b, (((0,), (0,)), ((), ())), preferred_element_type=F32)


def _dot_f32(a, b):
    return jnp.dot(a, b, precision=lax.Precision.HIGHEST, preferred_element_type=F32)


def _rms(x, g):
    return x * lax.rsqrt(jnp.mean(x * x, axis=-1, keepdims=True) + EPS) * g


def _const_spec(shape):
    nd = len(shape)
    return pl.BlockSpec(shape, lambda *_: (0,) * nd)


def _inproj_body(x_ref, g_ref, wh_ref, wt_ref, o_ref, h_scr, *, n_head):
    j = pl.program_id(1)

    @pl.when(j == 0)
    def _():
        h_scr[...] = _rms(x_ref[...], g_ref[...]).astype(BF16)

    @pl.when(j < n_head)
    def _():
        o_ref[...] = _dot(h_scr[...], wh_ref[0]).astype(o_ref.dtype)

    @pl.when(j >= n_head)
    def _():
        o_ref[...] = _dot(h_scr[...], wt_ref[...]).astype(o_ref.dtype)


def _inproj(x, g, w_head, w_tail, tm, tn, n_head):
    t, d = x.shape
    n_tail = w_tail.shape[1] // tn
    return pl.pallas_call(
        functools.partial(_inproj_body, n_head=n_head),
        grid=(t // tm, n_head + n_tail),
        in_specs=[pl.BlockSpec((tm, d), lambda i, j: (i, 0)),
                  pl.BlockSpec((1, d), lambda i, j: (0, 0)),
                  pl.BlockSpec((1, d, tn), lambda i, j: (0, 0, jnp.minimum(j, n_head - 1))),
                  pl.BlockSpec((d, tn), lambda i, j: (0, jnp.maximum(j - n_head, 0)))],
        out_specs=pl.BlockSpec((tm, tn), lambda i, j: (i, j)),
        out_shape=jax.ShapeDtypeStruct((t, (n_head + n_tail) * tn), BF16),
        scratch_shapes=[pltpu.VMEM((tm, d), BF16)],
        compiler_params=_cparams(("parallel", "arbitrary")),
        name="inproj",
    )(x, g, w_head, w_tail)


def _lru_gate_matmuls(u, wax_ref):
    bw = u.shape[1] // LRU_BLOCKS
    r_parts, i_parts = [], []
    for n in range(LRU_BLOCKS):
        ri = _dot(u[:, n * bw:(n + 1) * bw].astype(BF16), wax_ref[n])
        r_parts.append(ri[:, :bw])
        i_parts.append(ri[:, bw:])
    return jnp.concatenate(r_parts, axis=1), jnp.concatenate(i_parts, axis=1)


def _lru_gate_values(r_pre, i_pre, ba, bx, lam):
    r = jax.nn.sigmoid(r_pre + ba)
    i = jax.nn.sigmoid(i_pre + bx)
    log_a = LRU_C * r * jax.nn.log_sigmoid(lam)
    a = jnp.exp(log_a)
    m2 = -jnp.tanh(log_a) * (a * a + 1.0)
    mult = jnp.where(m2 > 0.0, m2 * lax.rsqrt(m2), 0.0)
    return a, i, mult


def _lru_gates(u, wax_ref, ba, bx, lam):
    return _lru_gate_values(*_lru_gate_matmuls(u, wax_ref), ba, bx, lam)


def _lru_conv(x, halo, seq_start, cw, cb):
    tt = x.shape[0]
    xpad = jnp.concatenate([jnp.where(seq_start, 0.0, halo), x], axis=0)
    base = SUBLANES - (CONV_W - 1)
    return cb + sum(xpad[base + k:base + k + tt] * cw[k:k + 1] for k in range(CONV_W))


def _lru_scan(u, r_pre, i_pre, gate, carry, seq_start, ba, bx, lam):
    tt, width = u.shape
    carry = jnp.where(seq_start, 0.0, carry)
    a, i, mult = _lru_gate_values(r_pre, i_pre, ba, bx, lam)
    first = jnp.logical_and(lax.broadcasted_iota(jnp.int32, a.shape, 0) == 0, seq_start)
    mult = jnp.where(first, 1.0, mult)
    a = jnp.where(first, 0.0, a)
    v = u * i * mult

    a = a.reshape(tt // SUBLANES, SUBLANES, width)
    v = v.reshape(tt // SUBLANES, SUBLANES, width)
    sub = lax.broadcasted_iota(jnp.int32, a.shape, 1)
    s = 1
    while s < SUBLANES:
        keep = sub >= s
        v = jnp.where(keep, a * pltpu.roll(v, s, axis=1) + v, v)
        a = jnp.where(keep, a * pltpu.roll(a, s, axis=1), a)
        s *= 2
    groups = []
    for g in range(tt // SUBLANES):
        hg = a[g] * carry + v[g]
        carry = hg[SUBLANES - 1:SUBLANES]
        groups.append(hg)
    h = jnp.concatenate(groups, axis=0)
    return h * jax.nn.gelu(gate), carry


def _inproj_lru_body(x_ref, g_ref, wh_ref, wt_ref, cw_ref, cb_ref, wax_ref, ba_ref, bx_ref, lam_ref,
                     o_ref, ya_ref, hlast_ref, bufout_ref, h_scr, lru_new, lru_cur, halo_scr, carry_scr,
                     *, n_head, n_j, n_tiles, tiles_per_seq, sub_rows):
    i = pl.program_id(0)
    j = pl.program_id(1)
    tm, tn = o_ref.shape
    w = ya_ref.shape[1]
    n_sub = tm // sub_rows
    prev_tile = jnp.maximum(i - 1, 0)

    @pl.when(jnp.logical_and(i == 0, j == 0))
    def _():
        lru_cur[...] = jnp.zeros_like(lru_cur)
        halo_scr[...] = jnp.zeros_like(halo_scr)
        carry_scr[...] = jnp.zeros_like(carry_scr)

    n_piece = LRU_PIECES
    piece_rows = sub_rows // n_piece
    mx_w = 2 * LANES
    col_cuts = [min(tn, mx_w * (q * (tn // mx_w) // n_piece)) for q in range(n_piece)] + [tn]

    def lru_piece(q, state):
        r0 = pl.multiple_of(j * sub_rows, sub_rows) + q * piece_rows
        seq_start = jnp.logical_and(jnp.logical_and(prev_tile % tiles_per_seq == 0, j == 0), q == 0)
        x = lru_cur[pl.ds(r0, piece_rows), 0:w].astype(F32)
        gate = lru_cur[pl.ds(r0, piece_rows), w:2 * w].astype(F32)
        halo, carry = state if state is not None else (halo_scr[...], carry_scr[...])
        u = _lru_conv(x, halo, seq_start, cw_ref[...], cb_ref[...])
        r_pre, i_pre = _lru_gate_matmuls(u, wax_ref)
        ya, carry = _lru_scan(u, r_pre, i_pre, gate, carry, seq_start, ba_ref[...], bx_ref[...], lam_ref[...])
        ya_ref[pl.ds(r0, piece_rows), :] = ya.astype(ya_ref.dtype)
        halo = x[piece_rows - SUBLANES:piece_rows]
        if q == n_piece - 1:
            halo_scr[...] = halo
            carry_scr[...] = carry

            @pl.when(jnp.logical_and(prev_tile % tiles_per_seq == tiles_per_seq - 1, j == n_sub - 1))
            def _():
                hlast_ref[0] = carry
                bufout_ref[0] = x[piece_rows - (CONV_W - 1):piece_rows]
        return halo, carry

    def project_piece(q, from_head, lru_lo):
        c0, c1 = col_cuts[q], col_cuts[q + 1]
        if c0 == c1:
            return
        w_cols = wh_ref[0, :, c0:c1] if from_head else wt_ref[:, c0:c1]
        o = _dot(h_scr[...], w_cols).astype(o_ref.dtype)
        o_ref[:, c0:c1] = o
        if lru_lo is not None:
            keep = min(lru_lo + c1, 2 * w) - (lru_lo + c0)
            if keep > 0:
                lru_new[:, lru_lo + c0:lru_lo + c0 + keep] = o[:, 0:keep]

    def column_steps(lo, hi, from_head, prologue=None, lru_lo=None):
        for a, b, with_lru in ((lo, min(hi, n_sub), True), (max(lo, n_sub), hi, False)):
            if a < b:
                @pl.when(jnp.logical_and(i < n_tiles, jnp.logical_and(j >= a, j < b)))
                def _(with_lru=with_lru):
                    if prologue is not None:
                        prologue()
                    state = None
                    for q in range(n_piece):
                        project_piece(q, from_head, lru_lo)
                        if with_lru:
                            state = lru_piece(q, state)
                    if b == n_j:
                        @pl.when(j == n_j - 1)
                        def _():
                            lru_cur[...] = lru_new[...]

    def normalise():
        h_scr[...] = _rms(x_ref[...], g_ref[...]).astype(BF16)

    column_steps(0, 1, True, prologue=normalise, lru_lo=0)
    column_steps(1, 2, True, lru_lo=tn)
    column_steps(2, n_head, True)
    column_steps(n_head, n_j, False)

    @pl.when(jnp.logical_and(i == n_tiles, j < n_sub))
    def _():
        state = None
        for q in range(n_piece):
            state = lru_piece(q, state)


def _inproj_lru(x, g, w_head, w_tail, lw, seq_len, tm, tn, n_head, sub_rows):
    t, d = x.shape
    w = lw["lru_lambda"].shape[1]
    n_tail = w_tail.shape[1] // tn
    n_tiles, n_j = t // tm, n_head + n_tail
    tiles_per_seq = seq_len // tm
    assert seq_len % tm == 0 and tm % sub_rows == 0 and tm // sub_rows < n_j and n_head >= 2
    assert w <= tn and 2 * w <= 2 * tn and 2 * w > tn
    cur_tile = lambda i: jnp.minimum(i, n_tiles - 1)
    lru_tile = lambda i: jnp.maximum(i - 1, 0)
    body = functools.partial(_inproj_lru_body, n_head=n_head, n_j=n_j, n_tiles=n_tiles,
                             tiles_per_seq=tiles_per_seq, sub_rows=sub_rows)
    out_col = lambda i, j: jnp.where(i < n_tiles, j, n_j - 1)
    w_col = lambda i, j: jnp.where(i < n_tiles, j, n_j - 1)
    return pl.pallas_call(
        body,
        grid=(n_tiles + 1, n_j),
        in_specs=[pl.BlockSpec((tm, d), lambda i, j: (cur_tile(i), 0)),
                  pl.BlockSpec((1, d), lambda i, j: (0, 0)),
                  pl.BlockSpec((1, d, tn), lambda i, j: (0, 0, jnp.minimum(w_col(i, j), n_head - 1))),
                  pl.BlockSpec((d, tn), lambda i, j: (0, jnp.maximum(w_col(i, j) - n_head, 0))),
                  _const_spec((CONV_W, w)), _const_spec((1, w)),
                  _const_spec(lw["wax"].shape), _const_spec((1, w)), _const_spec((1, w)), _const_spec((1, w))],
        out_specs=[pl.BlockSpec((tm, tn), lambda i, j: (cur_tile(i), out_col(i, j))),
                   pl.BlockSpec((tm, w), lambda i, j: (lru_tile(i), 0)),
                   pl.BlockSpec((1, 1, w), lambda i, j: (lru_tile(i) // tiles_per_seq, 0, 0)),
                   pl.BlockSpec((1, CONV_W - 1, w), lambda i, j: (lru_tile(i) // tiles_per_seq, 0, 0))],
        out_shape=[jax.ShapeDtypeStruct((t, n_j * tn), BF16),
                   jax.ShapeDtypeStruct((t, w), BF16),
                   jax.ShapeDtypeStruct((t // seq_len, 1, w), F32),
                   jax.ShapeDtypeStruct((t // seq_len, CONV_W - 1, w), F32)],
        scratch_shapes=[pltpu.VMEM((tm, d), BF16), pltpu.VMEM((tm, 2 * w), BF16), pltpu.VMEM((tm, 2 * w), BF16),
                        pltpu.VMEM((SUBLANES, w), F32), pltpu.VMEM((1, w), F32)],
        compiler_params=_cparams(("arbitrary", "arbitrary")),
        name="inproj_lru",
    )(x, g, w_head, w_tail, lw["lru_conv_w"], lw["lru_conv_b"], lw["wax"], lw["lru_ba"], lw["lru_bx"],
      lw["lru_lambda"])


def _lru_step_body(xin_ref, gate_ref, buf_ref, h0_ref, cw_ref, cb_ref, wax_ref, ba_ref, bx_ref, lam_ref,
                   ya_ref, hlast_ref, bufout_ref, *, steps):
    bsz = h0_ref.shape[0]
    n = steps * bsz
    x = xin_ref[...].astype(F32)
    xx = jnp.concatenate([buf_ref[...], x], axis=0)
    cw = cw_ref[...]
    u = cb_ref[...] + sum(xx[k * bsz:k * bsz + n] * cw[k:k + 1] for k in range(CONV_W))
    a, i, mult = _lru_gates(u, wax_ref, ba_ref[...], bx_ref[...], lam_ref[...])
    v = u * i * mult
    h = h0_ref[...]
    for t in range(steps):
        sl = slice(t * bsz, (t + 1) * bsz)
        h = a[sl] * h + v[sl]
        ya_ref[sl, :] = (h * jax.nn.gelu(gate_ref[sl, :].astype(F32))).astype(ya_ref.dtype)
    hlast_ref[...] = h
    bufout_ref[...] = xx[steps * bsz:(steps + CONV_W - 1) * bsz]


def _lru_step(xin_t, gate_t, buf_t, h0, lw, steps):
    bsz, w = h0.shape
    body = functools.partial(_lru_step_body, steps=steps)
    return pl.pallas_call(
        body,
        out_shape=[jax.ShapeDtypeStruct((steps * bsz, w), BF16),
                   jax.ShapeDtypeStruct((bsz, w), F32),
                   jax.ShapeDtypeStruct(((CONV_W - 1) * bsz, w), F32)],
        compiler_params=pltpu.CompilerParams(vmem_limit_bytes=VMEM_LIMIT),
        name="lru_step",
    )(xin_t, gate_t, buf_t, h0, lw["lru_conv_w"], lw["lru_conv_b"], lw["wax"],
      lw["lru_ba"], lw["lru_bx"], lw["lru_lambda"])


def _expand_heads(cols, e2):
    q = cols[0].shape[0]
    v = jnp.concatenate(cols, axis=0)
    hi = v.astype(BF16)
    lo = (v - hi.astype(F32)).astype(BF16)
    out = _dot(jnp.concatenate([hi, lo], axis=1), e2)
    return [out[i * q:(i + 1) * q] for i in range(len(cols))]


def _ssd_chunk(xc, dt, p, e2, s_get, s_set, t_col, t_row, n_seg=1, n_valid=None):
    q = xc.shape[0]
    rps = q // n_seg
    gn = SSD_GROUPS * SSD_STATE
    d_inner = xc.shape[1] - 2 * gn
    hpg = d_inner // SSD_HEADDIM // SSD_GROUPS
    gw = hpg * SSD_HEADDIM

    causal = t_col >= t_row
    if n_seg > 1:
        same = (lax.broadcasted_iota(jnp.int32, (q, 1), 0) // rps) == (lax.broadcasted_iota(jnp.int32, (1, q), 1) // rps)
        causal = jnp.logical_and(same, causal)
    if n_valid is not None:
        dt = jnp.where(lax.broadcasted_iota(jnp.int32, dt.shape, 0) % rps < n_valid, dt, 0.0)
    a = dt * p["A"]
    cum = _dot_f32(causal.astype(F32), a)
    cum_t = cum.T
    total = cum[q - 1:q, :] if n_seg == 1 else _dot_f32(same.astype(F32), a)
    dt_x, to_end_x, ecum_x = _expand_heads([dt, jnp.exp(total - cum), jnp.exp(cum)], e2)

    xs = xc[:, :d_inner]
    xdt = xs * dt_x
    xw = xdt * to_end_x
    packed = rps % (2 * SUBLANES) == 0
    xdt_m = xdt.astype(BF16) if packed else xdt
    if packed:
        xw = xw.astype(BF16)
    lane_head = lax.broadcasted_iota(jnp.int32, (1, gw), 1) // SSD_HEADDIM
    y_groups = []
    for g in range(SSD_GROUPS):
        sl = slice(g * gw, (g + 1) * gw)
        bg = xc[:, d_inner + g * SSD_STATE:d_inner + (g + 1) * SSD_STATE]
        cg = xc[:, d_inner + gn + g * SSD_STATE:d_inner + gn + (g + 1) * SSD_STATE]
        if packed:
            bg, cg = bg.astype(BF16), cg.astype(BF16)
        cb = _dot_nt(cg.astype(BF16), bg.astype(BF16))
        m_heads, x_heads = [], []
        for hh in range(hpg):
            h = g * hpg + hh
            decay = jnp.exp(jnp.where(causal, cum[:, h:h + 1] - cum_t[h:h + 1, :], -jnp.inf))
            m_heads.append((cb * decay).astype(BF16))
            x_heads.append(jnp.where(lane_head == hh, xdt_m[:, sl], 0.0))
        y_diag = _dot(jnp.concatenate(m_heads, axis=1), jnp.concatenate(x_heads, axis=0).astype(BF16))
        y_off = []
        for b in range(n_seg):
            rows = slice(b * rps, (b + 1) * rps)
            s_old = s_get(b, g)
            y_off.append(_dot_nt(cg[rows].astype(BF16), s_old.astype(BF16)))
            s_dec = [s_old[hh * SSD_HEADDIM:(hh + 1) * SSD_HEADDIM, :]
                     * jnp.exp(cum_t[g * hpg + hh:g * hpg + hh + 1, (b + 1) * rps - 1:(b + 1) * rps])
                     for hh in range(hpg)]
            s_set(b, g, jnp.concatenate(s_dec, axis=0) + _dot_tn(xw[rows, sl].astype(BF16), bg[rows].astype(BF16)))
        y_off = y_off[0] if n_seg == 1 else jnp.concatenate(y_off, axis=0)
        y_groups.append(y_diag + y_off * ecum_x[:, sl])
    return jnp.concatenate(y_groups, axis=1) + p["D"] * xs


def _ssd_gate_norm(y, z, p):
    gw = y.shape[1] // SSD_GROUPS
    zf = z.astype(F32)
    out = []
    for g in range(SSD_GROUPS):
        sl = slice(g * gw, (g + 1) * gw)
        v = y[:, sl] * (zf[:, sl] * jax.nn.sigmoid(zf[:, sl]))
        out.append(v * lax.rsqrt(jnp.mean(v * v, axis=-1, keepdims=True) + EPS) * p["norm_g"][:, sl])
    return jnp.concatenate(out, axis=1)


def _ssd_conv(xpad, q, cw, cb):
    base = SUBLANES - (CONV_W - 1)
    y = cb + sum(xpad[base + k:base + k + q] * cw[k:k + 1] for k in range(CONV_W))
    return y * jax.nn.sigmoid(y)


def _softplus(x):
    return jax.nn.softplus(x)


def _ssd_params(cw_ref, cb_ref, dtb_ref, a_ref, d_ref, ng_ref):
    return {"cw": cw_ref[...], "cb": cb_ref[...], "dt_bias": dtb_ref[...], "A": a_ref[...],
            "D": d_ref[...], "norm_g": ng_ref[...]}


def _ssd_prompt_body(xbc_ref, z_ref, dt_ref, cw_ref, cb_ref, dtb_ref, a_ref, d_ref, ng_ref, e2_ref,
                     yb_ref, sout_ref, bufout_ref, x_scr, dt_scr, y_scr, s_scr, *, q):
    c = pl.program_id(1)
    rows = xbc_ref.shape[1]
    half = q // 2
    n_xslab = x_scr.shape[0]
    base = SUBLANES - (CONV_W - 1)

    @pl.when(c == 0)
    def _():
        x_scr[:, 0:SUBLANES, :] = jnp.zeros((n_xslab, SUBLANES, LANES), F32)
        s_scr[...] = jnp.zeros_like(s_scr)

    @pl.when(c > 0)
    def _():
        x_scr[:, 0:SUBLANES, :] = x_scr[:, rows:rows + SUBLANES, :]

    p = _ssd_params(cw_ref, cb_ref, dtb_ref, a_ref, d_ref, ng_ref)
    x = xbc_ref[0].astype(F32)
    for j in range(n_xslab):
        x_scr[j, SUBLANES:SUBLANES + rows, :] = x[:, j * LANES:(j + 1) * LANES]
    dt_scr[...] = _softplus(dt_ref[0].astype(F32) + p["dt_bias"])

    def times(shape, axis):
        pos = lax.broadcasted_iota(jnp.int32, shape, axis)
        return jnp.where(pos < half, 2 * pos, 2 * (pos - half) + 1)

    def s_set(b, g, v):
        s_scr[g] = v

    for ch in range(rows // q):
        r0 = ch * q
        cols = []
        for j in range(n_xslab):
            ls = slice(j * LANES, (j + 1) * LANES)
            halves = []
            for par in range(2):
                acc = p["cb"][:, ls]
                for k in range(CONV_W):
                    acc = acc + x_scr[j, pl.ds(r0 + base + k + par, half, stride=2), :] * p["cw"][k:k + 1, ls]
                halves.append(acc)
            cols.append(jnp.concatenate(halves, axis=0))
        xc = jnp.concatenate(cols, axis=1)
        xc = xc * jax.nn.sigmoid(xc)
        dt = jnp.concatenate([dt_scr[pl.ds(r0 + par, half, stride=2), :] for par in range(2)], axis=0)
        y = _ssd_chunk(xc, dt, p, e2_ref[...], lambda b, g: s_scr[g], s_set, times((q, 1), 0), times((1, q), 1))
        for j in range(y_scr.shape[0]):
            for par in range(2):
                y_scr[j, pl.ds(par, half, stride=2), :] = y[par * half:(par + 1) * half, j * LANES:(j + 1) * LANES]
        y = jnp.concatenate([y_scr[j] for j in range(y_scr.shape[0])], axis=1)
        yb_ref[0, r0:r0 + q, :] = _ssd_gate_norm(y, z_ref[0, r0:r0 + q, :], p).astype(yb_ref.dtype)

    @pl.when(c == pl.num_programs(1) - 1)
    def _():
        sout_ref[0] = s_scr[...]
        bufout_ref[0] = x[rows - (CONV_W - 1):rows]


def _ssd_prompt(proj3, sp, col_xbc, col_z, col_dt):
    b, l, _ = proj3.shape
    cdim = sp["ssd_conv_w"].shape[1]
    d_inner = sp["ssd_norm_g"].shape[1]
    gw = d_inner // SSD_GROUPS
    q = SSD_CHUNK if l % SSD_CHUNK == 0 else l
    assert q % (2 * SUBLANES) == 0
    rows = q * SSD_CHUNKS_PER_STEP if l % (q * SSD_CHUNKS_PER_STEP) == 0 else q
    return pl.pallas_call(
        functools.partial(_ssd_prompt_body, q=q),
        grid=(b, l // rows),
        in_specs=[pl.BlockSpec((1, rows, cdim), lambda i, c: (i, c, col_xbc // cdim)),
                  pl.BlockSpec((1, rows, d_inner), lambda i, c: (i, c, col_z // d_inner)),
                  pl.BlockSpec((1, rows, LANES), lambda i, c: (i, c, col_dt // LANES)),
                  _const_spec((CONV_W, cdim)), _const_spec((1, cdim)), _const_spec((1, LANES)),
                  _const_spec((1, LANES)), _const_spec((1, d_inner)), _const_spec((1, d_inner)),
                  _const_spec((2 * LANES, d_inner))],
        out_specs=[pl.BlockSpec((1, rows, d_inner), lambda i, c: (i, c, 0)),
                   pl.BlockSpec((1, SSD_GROUPS, gw, SSD_STATE), lambda i, c: (i, 0, 0, 0)),
                   pl.BlockSpec((1, CONV_W - 1, cdim), lambda i, c: (i, 0, 0))],
        out_shape=[jax.ShapeDtypeStruct((b, l, d_inner), BF16),
                   jax.ShapeDtypeStruct((b, SSD_GROUPS, gw, SSD_STATE), F32),
                   jax.ShapeDtypeStruct((b, CONV_W - 1, cdim), F32)],
        scratch_shapes=[pltpu.VMEM((cdim // LANES, rows + SUBLANES, LANES), F32),
                        pltpu.VMEM((rows, LANES), F32),
                        pltpu.VMEM((d_inner // LANES, q, LANES), F32),
                        pltpu.VMEM((SSD_GROUPS, gw, SSD_STATE), F32)],
        compiler_params=_cparams(("parallel", "arbitrary")),
        name="ssd_prompt",
    )(proj3, proj3, proj3, sp["ssd_conv_w"], sp["ssd_conv_b"], sp["dt_bias"], sp["A"], sp["D"], sp["ssd_norm_g"],
      sp["head_expand"])


def _ssd_step_body(xbc_ref, z_ref, dt_ref, buf_ref, s_ref, cw_ref, cb_ref, dtb_ref, a_ref, d_ref, ng_ref, e2_ref,
                   yb_ref, sout_ref, bufout_ref, *, steps, nb):
    p = _ssd_params(cw_ref, cb_ref, dtb_ref, a_ref, d_ref, ng_ref)
    rps = SUBLANES
    q = nb * rps
    x_all = xbc_ref[...].astype(F32)
    z_all = z_ref[...].astype(F32)
    dt_all = _softplus(dt_ref[...].astype(F32) + p["dt_bias"])
    cdim = x_all.shape[1]

    def padded(v, j):
        return jnp.concatenate([v[j * steps:(j + 1) * steps], jnp.zeros((rps - steps, v.shape[1]), v.dtype)], axis=0)

    xcs = []
    for j in range(nb):
        xpad = jnp.concatenate([jnp.zeros((SUBLANES - (CONV_W - 1), cdim), F32), buf_ref[j], padded(x_all, j)],
                               axis=0)
        xcs.append(_ssd_conv(xpad, rps, p["cw"], p["cb"]))
        bufout_ref[j] = xpad[SUBLANES + steps - (CONV_W - 1):SUBLANES + steps]
    xc = jnp.concatenate(xcs, axis=0)
    dt = jnp.concatenate([padded(dt_all, j) for j in range(nb)], axis=0)
    z = jnp.concatenate([padded(z_all, j) for j in range(nb)], axis=0)

    def s_set(b, g, v):
        sout_ref[b, g] = v

    t_col = lax.broadcasted_iota(jnp.int32, (q, 1), 0) % rps
    t_row = lax.broadcasted_iota(jnp.int32, (1, q), 1) % rps
    y = _ssd_chunk(xc, dt, p, e2_ref[...], lambda b, g: s_ref[b, g], s_set, t_col, t_row, n_seg=nb, n_valid=steps)
    y = _ssd_gate_norm(y, z, p)
    for j in range(nb):
        yb_ref[j * steps:(j + 1) * steps, :] = y[j * rps:j * rps + steps].astype(yb_ref.dtype)


def _ssd_step(proj, buf, s0, sp, steps, nb, col_xbc, col_z, col_dt):
    bsz = s0.shape[0]
    cdim = sp["ssd_conv_w"].shape[1]
    d_inner = sp["ssd_norm_g"].shape[1]
    gw = d_inner // SSD_GROUPS
    rows = nb * steps
    body = functools.partial(_ssd_step_body, steps=steps, nb=nb)
    return pl.pallas_call(
        body,
        grid=(bsz // nb,),
        in_specs=[pl.BlockSpec((rows, cdim), lambda i: (i, col_xbc // cdim)),
                  pl.BlockSpec((rows, d_inner), lambda i: (i, col_z // d_inner)),
                  pl.BlockSpec((rows, LANES), lambda i: (i, col_dt // LANES)),
                  pl.BlockSpec((None, nb, CONV_W - 1, cdim), lambda i: (0, i, 0, 0)),
                  pl.BlockSpec((nb, SSD_GROUPS, gw, SSD_STATE), lambda i: (i, 0, 0, 0)),
                  _const_spec((CONV_W, cdim)), _const_spec((1, cdim)), _const_spec((1, LANES)),
                  _const_spec((1, LANES)), _const_spec((1, d_inner)), _const_spec((1, d_inner)),
                  _const_spec((2 * LANES, d_inner))],
        out_specs=[pl.BlockSpec((rows, d_inner), lambda i: (i, 0)),
                   pl.BlockSpec((nb, SSD_GROUPS, gw, SSD_STATE), lambda i: (i, 0, 0, 0)),
                   pl.BlockSpec((None, nb, CONV_W - 1, cdim), lambda i: (0, i, 0, 0))],
        out_shape=[jax.ShapeDtypeStruct((bsz * steps, d_inner), BF16),
                   jax.ShapeDtypeStruct(s0.shape, F32),
                   jax.ShapeDtypeStruct(buf.shape, F32)],
        compiler_params=_cparams(("parallel",)),
        name="ssd_step",
    )(proj, proj, proj, buf, s0, sp["ssd_conv_w"], sp["ssd_conv_b"], sp["dt_bias"], sp["A"], sp["D"],
      sp["ssd_norm_g"], sp["head_expand"])


def _router(t, wr, br):
    t_hi = t.astype(BF16)
    t_lo = (t - t_hi.astype(F32)).astype(BF16)
    both = _dot(jnp.concatenate([t_hi, t_lo], axis=1), wr)
    logits = both[:, :LANES] + both[:, LANES:] + br
    lane = lax.broadcasted_iota(jnp.int32, logits.shape, 1)
    neg = -jnp.inf
    gl = jnp.where(lane < N_EGROUPS, logits, neg)
    gmax = jnp.max(gl, axis=-1, keepdims=True)
    g_idx = jnp.min(jnp.where(gl == gmax, lane, LANES), axis=-1, keepdims=True)
    g_w = 1.0 / jnp.sum(jnp.exp(gl - gmax), axis=-1, keepdims=True)
    in_grp = jnp.logical_and(jnp.logical_and(lane >= ROUTER_LANE0, lane < ROUTER_LANE0 + N_EXPERTS),
                             ((lane - ROUTER_LANE0) >> 2) == g_idx)
    el = jnp.where(in_grp, logits, neg)
    pe = jnp.exp(el - jnp.max(el, axis=-1, keepdims=True))
    pe = pe / jnp.sum(pe, axis=-1, keepdims=True)
    cand = jnp.where(in_grp, pe, -1.0)
    v1 = jnp.max(cand, axis=-1, keepdims=True)
    i1 = jnp.min(jnp.where(cand == v1, lane, LANES), axis=-1, keepdims=True)
    cand2 = jnp.where(lane == i1, -1.0, cand)
    v2 = jnp.max(cand2, axis=-1, keepdims=True)
    i2 = jnp.min(jnp.where(jnp.logical_and(cand2 == v2, in_grp), lane, LANES), axis=-1, keepdims=True)
    den = v1 + v2
    return lane, i1, i2, g_w * v1 / den, g_w * v2 / den


def _rows_to_tiles(ref, val):
    n, d = val.shape
    for k in range(d // LANES):
        ref[pl.ds(k, n, stride=d // LANES), :] = val[:, k * LANES:(k + 1) * LANES]


def _tiles_to_rows(ref, n, d, start=0):
    return jnp.concatenate([ref[pl.ds(start + k, n, stride=d // LANES), :] for k in range(d // LANES)], axis=1)


def _merge_body(x_ref, ya_ref, yb_ref, ga_ref, gb_ref, wl_ref, ws_ref, wo_ref, gf_ref, wr_ref, br_ref, cnt0_ref,
                x1_ref, t_ref, rt_ref, rtt_ref, cnt_ref, base_scr):
    step = pl.program_id(0)

    @pl.when(step == 0)
    def _():
        base_scr[...] = cnt0_ref[...]

    a = _dot(ya_ref[...], wl_ref[...])
    b = _dot(yb_ref[...], ws_ref[...])
    merged = jax.nn.sigmoid(ga_ref[...].astype(F32)) * a + jax.nn.sigmoid(gb_ref[...].astype(F32)) * b
    x1 = x_ref[...] + _dot(merged.astype(BF16), wo_ref[...])
    x1_ref[...] = x1
    t = _rms(x1, gf_ref[...])
    _rows_to_tiles(t_ref, t)
    lane, i1, i2, wg1, wg2 = _router(t, wr_ref[...], br_ref[...])

    tm = t.shape[0]
    onehot = jnp.where(jnp.logical_or(lane == i1, lane == i2), 1.0, 0.0).astype(BF16)
    tri = (lax.broadcasted_iota(jnp.int32, (tm, tm), 1) <= lax.broadcasted_iota(jnp.int32, (tm, tm), 0)).astype(BF16)
    cum = _dot(tri, onehot) + base_scr[...]
    r1 = jnp.sum(jnp.where(lane == i1, cum, 0.0), axis=-1, keepdims=True) - 1.0
    r2 = jnp.sum(jnp.where(lane == i2, cum, 0.0), axis=-1, keepdims=True) - 1.0
    cols = (wg1, wg2, r1, r2, (i1 - ROUTER_LANE0).astype(F32), (i2 - ROUTER_LANE0).astype(F32))
    rt = jnp.zeros(cum.shape, F32)
    for k, c in enumerate(cols):
        rt = jnp.where(lane == k, c, rt)
    rt_ref[...] = rt
    rtt_ref[...] = rt.T[0:SUBLANES, :]
    base_scr[...] = cum[tm - 1:tm, :]
    cnt_ref[...] = cum[tm - 1:tm, :]


def _merge(x, ya, yb, proj, mw, tm, col_ga, col_gb, counts_so_far):
    t, d = x.shape
    d_inner = yb.shape[1]
    return pl.pallas_call(
        _merge_body,
        grid=(t // tm,),
        in_specs=[pl.BlockSpec((tm, d), lambda i: (i, 0)),
                  pl.BlockSpec((tm, d), lambda i: (i, 0)),
                  pl.BlockSpec((tm, d_inner), lambda i: (i, 0)),
                  pl.BlockSpec((tm, d), lambda i: (i, col_ga // d)),
                  pl.BlockSpec((tm, d), lambda i: (i, col_gb // d)),
                  _const_spec((d, d)), _const_spec((d_inner, d)), _const_spec((d, d)),
                  _const_spec((1, d)), _const_spec((2 * d, 2 * LANES)), _const_spec((1, LANES)),
                  _const_spec((1, LANES))],
        out_specs=[pl.BlockSpec((tm, d), lambda i: (i, 0)),
                   pl.BlockSpec((tm * d // LANES, LANES), lambda i: (i, 0)),
                   pl.BlockSpec((tm, LANES), lambda i: (i, 0)),
                   pl.BlockSpec((SUBLANES, tm), lambda i: (0, i)),
                   pl.BlockSpec((1, LANES), lambda i: (0, 0))],
        out_shape=[jax.ShapeDtypeStruct((t, d), F32),
                   jax.ShapeDtypeStruct((t * d // LANES, LANES), F32),
                   jax.ShapeDtypeStruct((t, LANES), F32),
                   jax.ShapeDtypeStruct((SUBLANES, t), F32),
                   jax.ShapeDtypeStruct((1, LANES), F32)],
        scratch_shapes=[pltpu.VMEM((1, LANES), F32)],
        compiler_params=_cparams(("arbitrary",)),
        name="merge_router",
    )(x, ya, yb, proj, proj, mw["w_br_lru"], mw["w_br_ssd"], mw["w_out"], mw["g_ffn"], mw["w_router"],
      mw["b_router"], counts_so_far)


def _dispatch_body(zb_ref, *rest, n_groups, tms, n_toks, nk, tmg):
    dest_refs, t_refs = rest[:n_groups], rest[n_groups:2 * n_groups]
    o_ref, zero_scr, sem, zsem = rest[2 * n_groups:]
    step = pl.program_id(0)

    @pl.when(step == 0)
    def _():
        zero_scr[...] = jnp.zeros_like(zero_scr)
        blk = tmg * nk

        def zcopy(j):
            return pltpu.make_async_copy(zero_scr, o_ref.at[pl.ds(pl.multiple_of(zb_ref[j] * blk, blk), blk)], zsem)

        for j in range(zb_ref.shape[0]):
            pl.when(zb_ref[j] >= 0)(lambda j=j: zcopy(j).start())
        for j in range(zb_ref.shape[0]):
            pl.when(zb_ref[j] >= 0)(lambda j=j: zcopy(j).wait())

    first_step = 0
    for g in range(n_groups):
        tm, n_tok, steps = tms[g], n_toks[g], n_toks[g] // tms[g]

        @pl.when(jnp.logical_and(step >= first_step, step < first_step + steps))
        def _(g=g, tm=tm, n_tok=n_tok, first_step=first_step):
            t_ref, dest_ref = t_refs[g], dest_refs[g]

            def issue(r, carry):
                src = t_ref.at[pl.ds(pl.multiple_of(r * nk, nk), nk)]
                for k in range(2):
                    row = dest_ref[k * n_tok + (step - first_step) * tm + r]
                    pltpu.make_async_copy(src, o_ref.at[pl.ds(pl.multiple_of(row * nk, nk), nk)],
                                          sem).start(priority=k)
                return carry

            lax.fori_loop(0, tm, issue, 0, unroll=8)
            for k in range(2):
                pltpu.make_async_copy(t_ref, o_ref.at[pl.ds(0, tm * nk)], sem).wait()

        first_step += steps


def _dispatch(dests, zero_blocks, t_tiles, n_toks, tms, n_rows, tmg):
    n_groups = len(dests)
    nk = t_tiles[0].shape[0] // n_toks[0]
    steps = [n // tm for n, tm in zip(n_toks, tms)]
    starts = [sum(steps[:g]) for g in range(n_groups)]
    block_of = lambda g: (lambda i, *_: (jnp.clip(i - starts[g], 0, steps[g] - 1), 0))
    return pl.pallas_call(
        functools.partial(_dispatch_body, n_groups=n_groups, tms=tuple(tms), n_toks=tuple(n_toks), nk=nk, tmg=tmg),
        grid_spec=pltpu.PrefetchScalarGridSpec(
            num_scalar_prefetch=1 + n_groups,
            grid=(sum(steps),),
            in_specs=[pl.BlockSpec((tms[g] * nk, LANES), block_of(g)) for g in range(n_groups)],
            out_specs=pl.BlockSpec(memory_space=pl.ANY),
            scratch_shapes=[pltpu.VMEM((tmg * nk, LANES), F32), pltpu.SemaphoreType.DMA(()),
                            pltpu.SemaphoreType.DMA(())]),
        out_shape=jax.ShapeDtypeStruct((n_rows * nk, LANES), F32),
        compiler_params=_cparams(("arbitrary",)),
        name="moe_dispatch",
    )(zero_blocks, *dests, *t_tiles)


def _expert_body(te_ref, nt_ref, x_ref, w1_ref, w3_ref, w2_ref, y_ref, w1_scr, w3_scr, w2_scr, *, tmg):
    i = pl.program_id(0)
    real = i < nt_ref[0]
    d = w1_scr.shape[0]

    @pl.when(jnp.logical_or(i == 0, te_ref[i] != te_ref[jnp.maximum(i - 1, 0)]))
    def _():
        w1_scr[...] = w1_ref[0].astype(BF16)
        w3_scr[...] = w3_ref[0].astype(BF16)
        w2_scr[...] = w2_ref[0].astype(BF16)

    @pl.when(real)
    def _():
        x = _tiles_to_rows(x_ref, tmg, d).astype(BF16)
        h1 = _dot(x, w1_scr[...])
        h3 = _dot(x, w3_scr[...])
        _rows_to_tiles(y_ref, _dot((h1 * jax.nn.sigmoid(h1) * h3).astype(BF16), w2_scr[...]))

    @pl.when(jnp.logical_not(real))
    def _():
        y_ref[...] = jnp.zeros_like(y_ref)


def _experts(tile_expert, n_tiles, xs_tiles, w1, w3, w2, tmg):
    _, d, dff = w1.shape
    blk = tmg * d // LANES
    row_spec = pl.BlockSpec((blk, LANES), lambda i, te, nt: (i, 0))
    return pl.pallas_call(
        functools.partial(_expert_body, tmg=tmg),
        grid_spec=pltpu.PrefetchScalarGridSpec(
            num_scalar_prefetch=2,
            grid=(xs_tiles.shape[0] // blk,),
            in_specs=[row_spec,
                      pl.BlockSpec((1, d, dff), lambda i, te, nt: (te[i], 0, 0)),
                      pl.BlockSpec((1, d, dff), lambda i, te, nt: (te[i], 0, 0)),
                      pl.BlockSpec((1, dff, d), lambda i, te, nt: (te[i], 0, 0))],
            out_specs=row_spec,
            scratch_shapes=[pltpu.VMEM((d, dff), BF16), pltpu.VMEM((d, dff), BF16), pltpu.VMEM((dff, d), BF16)]),
        out_shape=jax.ShapeDtypeStruct(xs_tiles.shape, F32),
        compiler_params=_cparams(("arbitrary",)),
        name="moe_experts",
    )(tile_expert, n_tiles, xs_tiles, w1, w3, w2)


def _ple_body(dest_ref, x_ref, rt_ref, p_ref, wp_ref, gp_ref, gg_ref, wg_ref, gfin_ref, y_hbm, o_ref, gbuf, sem,
              *, n_tok):
    step = pl.program_id(0)
    n_steps = pl.num_programs(0)
    tm, d = x_ref.shape
    nk = d // LANES
    pr = tm // COMBINE_PIECES

    def issue_row(tile, slot, r):
        tok = tile * tm + r
        for k in range(2):
            row = dest_ref[k * n_tok + tok]
            pltpu.make_async_copy(y_hbm.at[pl.ds(pl.multiple_of(row * nk, nk), nk)],
                                  gbuf.at[slot, pl.ds(pl.multiple_of((k * tm + r) * nk, nk), nk)],
                                  sem.at[slot]).start(priority=k)

    def wait_slot(slot):
        pltpu.make_async_copy(y_hbm.at[pl.ds(0, 2 * tm * nk)], gbuf.at[slot], sem.at[slot]).wait()

    @pl.when(step == 0)
    def _():
        def issue(r, carry):
            issue_row(0, 0, r)
            return carry

        lax.fori_loop(0, tm, issue, 0, unroll=8)

    slot = step % 2
    wait_slot(slot)
    nxt_tile = jnp.minimum(step + 1, n_steps - 1)
    rows = gbuf.at[slot]
    for q in range(COMBINE_PIECES):
        sl = slice(q * pr, (q + 1) * pr)
        rt = rt_ref[sl, :]
        x = (x_ref[sl, :] + rt[:, 0:1] * _tiles_to_rows(rows, pr, d, start=q * pr * nk)
             + rt[:, 1:2] * _tiles_to_rows(rows, pr, d, start=(tm + q * pr) * nk))
        e = _rms(_dot(p_ref[sl, :].astype(BF16), wp_ref[...]), gp_ref[...])
        gate = jax.nn.sigmoid(_dot(_rms(x, gg_ref[...]).astype(BF16), wg_ref[...]))
        o_ref[sl, :] = _rms(x + gate * e, gfin_ref[...])
        for r in range(q * pr, (q + 1) * pr):
            issue_row(nxt_tile, 1 - slot, r)

    pl.when(step == n_steps - 1)(lambda: wait_slot(1 - slot))


def _ple(dest, x1, rt, p, y_sorted, pw, tm):
    n, d = x1.shape
    dp = p.shape[1]
    const = lambda shape: pl.BlockSpec(shape, lambda i, *_: (0,) * len(shape))
    return pl.pallas_call(
        functools.partial(_ple_body, n_tok=n),
        grid_spec=pltpu.PrefetchScalarGridSpec(
            num_scalar_prefetch=1,
            grid=(n // tm,),
            in_specs=[pl.BlockSpec((tm, d), lambda i, *_: (i, 0)),
                      pl.BlockSpec((tm, LANES), lambda i, *_: (i, 0)),
                      pl.BlockSpec((tm, dp), lambda i, *_: (i, 0)),
                      const((dp, d)), const((1, d)), const((1, d)), const((d, d)), const((1, d)),
                      pl.BlockSpec(memory_space=pl.ANY)],
            out_specs=pl.BlockSpec((tm, d), lambda i, *_: (i, 0)),
            scratch_shapes=[pltpu.VMEM((2, 2 * tm * d // LANES, LANES), F32), pltpu.SemaphoreType.DMA((2,))]),
        out_shape=jax.ShapeDtypeStruct((n, d), F32),
        compiler_params=_cparams(("arbitrary",)),
        name="combine_ple_final",
    )(dest, x1, rt, p, pw["w_ple_proj"], pw["g_ple"], pw["g_ple_gate"], pw["w_ple_gate"], pw["g_final"], y_sorted)


def _pick_tile(n, pref):
    t = min(n, pref)
    while n % t:
        t //= 2
    return t


def _moe_row_tile(n):
    return 512 if 2 * n // N_EXPERTS >= 1024 else 128
MOE_DISPATCH_TILE = 4096
COMBINE_TILE = 1024
SSD_CHUNKS_PER_STEP = 4
COMBINE_PIECES = 1
LRU_SLICE_ROWS = 128
LRU_PIECES = 2


def _split_router(w):
    hi = w.astype(BF16)
    lo = (w - hi.astype(F32)).astype(BF16)
    return jnp.concatenate([jnp.concatenate([hi, lo], axis=1),
                            jnp.concatenate([hi, jnp.zeros_like(hi)], axis=1)], axis=0)


def _token_tail(groups, cnt, lw):
    n_all = sum(g["x1"].shape[0] for g in groups)
    tmg = _moe_row_tile(n_all)
    n_blocks = pl.cdiv(2 * n_all, tmg) + N_EXPERTS
    counts = cnt[0, ROUTER_LANE0:ROUTER_LANE0 + N_EXPERTS].astype(jnp.int32)
    tiles = (counts + tmg - 1) // tmg
    ends = jnp.cumsum(tiles)
    n_tiles = ends[-1]
    first_row = (ends - tiles) * tmg
    blk = jnp.arange(n_blocks, dtype=jnp.int32)
    tile_expert = jnp.sum((jnp.minimum(blk, n_tiles - 1)[:, None] >= ends[None, :]).astype(jnp.int32), axis=1)
    tail = n_tiles + blk[:N_EXPERTS]
    zero_blocks = jnp.concatenate([jnp.where(tiles > 0, ends - 1, -1),
                                   jnp.where(tail < n_blocks, tail, -1)]).astype(jnp.int32)

    dests = []
    for g in groups:
        e_idx = g["rtt"][4:6].astype(jnp.int32)
        dest = g["rtt"][2:4].astype(jnp.int32) + sum(jnp.where(e_idx == e, first_row[e], 0) for e in range(N_EXPERTS))
        dests.append(dest.reshape(2 * g["x1"].shape[0]))
    n_toks = [g["x1"].shape[0] for g in groups]
    sorted_t = _dispatch(dests, zero_blocks, [g["t_tiles"] for g in groups], n_toks,
                         [_pick_tile(n, MOE_DISPATCH_TILE) for n in n_toks], n_blocks * tmg, tmg)
    y_sorted = _experts(tile_expert, n_tiles.reshape(1), sorted_t, lw["w1"], lw["w3"], lw["w2"], tmg)
    return [_ple(dest, g["x1"], g["rt"], g["p"], y_sorted, lw, g["tm"]) for g, dest in zip(groups, dests)]


def kernel(x_prompt, x_sample, state_lru_h, state_lru_conv, state_ssd, state_ssd_conv, p_prompt, p_sample, g_mix, w_in, lru_conv_w, lru_conv_b, lru_wa, lru_ba, lru_wx, lru_bx, lru_lambda, ssd_conv_w, ssd_conv_b, ssd_dt_bias, ssd_A_log, ssd_D, ssd_norm_g, w_br_lru, w_br_ssd, w_out, g_ffn, w_router_g, b_router_g, w_router_e, b_router_e, w1, w3, w2, w_ple_proj, g_ple, g_ple_gate, w_ple_gate, g_final):
    depth = w_in.shape[0]
    assert depth == 1, "one decoder layer per call"
    bp, lp, d = x_prompt.shape
    bs, ls, _ = x_sample.shape
    w_lru = state_lru_h.shape[-1]
    heads, hdim, nstate = state_ssd.shape[2:]
    d_inner = heads * hdim
    cdim = state_ssd_conv.shape[-1]
    assert hdim == SSD_HEADDIM and nstate == SSD_STATE and heads <= LANES and ls < SUBLANES
    gw = d_inner // SSD_GROUPS

    o_dt = 2 * w_lru + d_inner + cdim
    n_proj = o_dt + 2 * d + LANES
    tn = n_proj // 9 if n_proj % (9 * LANES) == 0 else LANES
    n_head = o_dt // tn
    w_head = w_in.astype(BF16)
    wi = w_head[0]
    w_tail = jnp.concatenate([wi[:, n_head * tn:o_dt], wi[:, o_dt + heads:], wi[:, o_dt:o_dt + heads],
                              jnp.zeros((d, LANES - heads), BF16)], axis=1)
    assert n_head >= 1 and w_tail.shape[1] == n_proj - n_head * tn and w_tail.shape[1] % tn == 0
    col_z, col_xbc = 2 * w_lru, 2 * w_lru + d_inner
    col_ga, col_gb, col_dt = o_dt, o_dt + d, o_dt + 2 * d
    row = lambda v: v.reshape(1, -1).astype(F32)
    pad_heads = lambda v: jnp.pad(v.astype(F32), (0, LANES - heads)).reshape(1, LANES)
    lw = {
        "lru_conv_w": lru_conv_w[0], "lru_conv_b": row(lru_conv_b[0]),
        "wax": jnp.concatenate([lru_wa[0], lru_wx[0]], axis=-1).astype(BF16),
        "lru_ba": row(lru_ba[0]), "lru_bx": row(lru_bx[0]), "lru_lambda": row(lru_lambda[0]),
        "ssd_conv_w": ssd_conv_w[0], "ssd_conv_b": row(ssd_conv_b[0]),
        "dt_bias": pad_heads(ssd_dt_bias[0]), "A": pad_heads(-jnp.exp(ssd_A_log[0].astype(F32))),
        "D": row(jnp.repeat(ssd_D[0], hdim)), "ssd_norm_g": row(ssd_norm_g[0]),
        "head_expand": jnp.tile(jnp.arange(LANES)[:, None] == jnp.arange(d_inner)[None, :] // hdim, (2, 1)).astype(BF16),
        "w_br_lru": w_br_lru[0].astype(BF16), "w_br_ssd": w_br_ssd[0].astype(BF16), "w_out": w_out[0].astype(BF16),
        "g_ffn": row(g_ffn[0]),
        "w_router": _split_router(jnp.concatenate([w_router_g[0], w_router_e[0],
                                                   jnp.zeros((d, LANES - N_EGROUPS - N_EXPERTS), F32)], axis=1)),
        "b_router": jnp.concatenate([b_router_g[0], b_router_e[0],
                                     jnp.zeros((LANES - N_EGROUPS - N_EXPERTS,), F32)]).reshape(1, LANES),
        "w1": w1[0], "w3": w3[0], "w2": w2[0],
        "w_ple_proj": w_ple_proj[0].astype(BF16), "g_ple": row(g_ple[0]), "g_ple_gate": row(g_ple_gate[0]),
        "w_ple_gate": w_ple_gate[0].astype(BF16), "g_final": row(g_final),
    }
    g_mix_r = row(g_mix[0])

    tp = bp * lp
    xp = x_prompt.reshape(tp, d)
    tm_p = _pick_tile(lp, 1024)
    proj_p, ya_p, hl_p, lbuf_p = _inproj_lru(xp, g_mix_r, w_head, w_tail, lw, lp, tm_p, tn, n_head,
                                             _pick_tile(tm_p, LRU_SLICE_ROWS))
    proj_p3 = proj_p.reshape(bp, lp, n_proj)
    yb_p, s_p, sbuf_p = _ssd_prompt(proj_p3, lw, col_xbc, col_z, col_dt)
    tm_tail_p = _pick_tile(tp, 512)
    x1_p, t_p, rt_p, rtt_p, cnt_p = _merge(xp, ya_p.reshape(tp, w_lru), yb_p.reshape(tp, d_inner), proj_p, lw,
                                           tm_tail_p, col_ga, col_gb, jnp.zeros((1, LANES), F32))
    group_p = {"x1": x1_p, "t_tiles": t_p, "rt": rt_p, "rtt": rtt_p, "p": p_prompt[0].reshape(tp, -1),
               "tm": _pick_tile(tp, COMBINE_TILE)}

    ts = bs * ls
    xs = x_sample.reshape(ts, d)
    tm_s = _pick_tile(ts, 512)
    proj_s = _inproj(xs, g_mix_r, w_head, w_tail, tm_s, tn, n_head)
    to_tmajor = lambda v, n: v.reshape(bs, n, -1).transpose(1, 0, 2).reshape(n * bs, -1)
    from_tmajor = lambda v, n: v.reshape(n, bs, -1).transpose(1, 0, 2)
    ya_t, hl_s, lbuf_t = _lru_step(to_tmajor(proj_s[:, :w_lru], ls), to_tmajor(proj_s[:, w_lru:2 * w_lru], ls),
                                   to_tmajor(state_lru_conv[0], CONV_W - 1), state_lru_h[0], lw, ls)
    ya_s = from_tmajor(ya_t, ls).reshape(ts, w_lru)
    lbuf_s = from_tmajor(lbuf_t, CONV_W - 1)
    yb_s, s_s, sbuf_s = _ssd_step(proj_s, state_ssd_conv, state_ssd[0].reshape(bs, SSD_GROUPS, gw, nstate), lw,
                                  ls, _pick_tile(bs, 8), col_xbc, col_z, col_dt)
    x1_s, t_s, rt_s, rtt_s, cnt_all = _merge(xs, ya_s, yb_s, proj_s, lw, tm_s, col_ga, col_gb, cnt_p)
    group_s = {"x1": x1_s, "t_tiles": t_s, "rt": rt_s, "rtt": rtt_s, "p": p_sample[0].reshape(ts, -1), "tm": tm_s}

    y_p, y_s = _token_tail([group_p, group_s], cnt_all, lw)

    return (y_p.reshape(bp, lp, d), y_s.reshape(bs, ls, d),
            hl_p.reshape(1, bp, w_lru), lbuf_p[None],
            s_p.reshape(1, bp, heads, hdim, nstate), sbuf_p[None],
            hl_s[None], lbuf_s[None],
            s_s.reshape(1, bs, heads, hdim, nstate), sbuf_s)
```

```python
import functools

import jax
import jax.numpy as jnp
from jax import lax
from jax.experimental import pallas as pl
from jax.experimental.pallas import tpu as pltpu

F32 = jnp.float32
BF16 = jnp.bfloat16

EPS = 1e-6
CONV_W = 4
LRU_BLOCKS = 8
LRU_C = 8.0
SSD_HEADDIM = 64
SSD_GROUPS = 8
SSD_STATE = 128
SSD_CHUNK = 128
N_EGROUPS = 4
EXP_PER_GROUP = 4
N_EXPERTS = N_EGROUPS * EXP_PER_GROUP

LANES = 128
SUBLANES = 8
VMEM_LIMIT = 52 * 1024 * 1024
ROUTER_LANE0 = N_EGROUPS


def _cparams(sem):
    return pltpu.CompilerParams(dimension_semantics=sem, vmem_limit_bytes=VMEM_LIMIT)


def _dot(a, b):
    return jnp.dot(a, b, preferred_element_type=F32)


def _dot_nt(a, b):
    return lax.dot_general(a, b, (((1,), (1,)), ((), ())), preferred_element_type=F32)


def _dot_tn(a, b):
    return lax.dot_general(a, b, (((0,), (0,)), ((), ())), preferred_element_type=F32)


def _dot_f32(a, b):
    return jnp.dot(a, b, precision=lax.Precision.HIGHEST, preferred_element_type=F32)


def _rms(x, g):
    return x * lax.rsqrt(jnp.mean(x * x, axis=-1, keepdims=True) + EPS) * g


def _const_spec(shape):
    nd = len(shape)
    return pl.BlockSpec(shape, lambda *_: (0,) * nd)


def _inproj_body(x_ref, g_ref, wh_ref, wt_ref, o_ref, h_scr, *, n_head):
    j = pl.program_id(1)

    @pl.when(j == 0)
    def _():
        h_scr[...] = _rms(x_ref[...], g_ref[...]).astype(BF16)

    @pl.when(j < n_head)
    def _():
        o_ref[...] = _dot(h_scr[...], wh_ref[0]).astype(o_ref.dtype)

    @pl.when(j >= n_head)
    def _():
        o_ref[...] = _dot(h_scr[...], wt_ref[...]).astype(o_ref.dtype)


def _inproj(x, g, w_head, w_tail, tm, tn, n_head):
    t, d = x.shape
    n_tail = w_tail.shape[1] // tn
    return pl.pallas_call(
        functools.partial(_inproj_body, n_head=n_head),
        grid=(t // tm, n_head + n_tail),
        in_specs=[pl.BlockSpec((tm, d), lambda i, j: (i, 0)),
                  pl.BlockSpec((1, d), lambda i, j: (0, 0)),
                  pl.BlockSpec((1, d, tn), lambda i, j: (0, 0, jnp.minimum(j, n_head - 1))),
                  pl.BlockSpec((d, tn), lambda i, j: (0, jnp.maximum(j - n_head, 0)))],
        out_specs=pl.BlockSpec((tm, tn), lambda i, j: (i, j)),
        out_shape=jax.ShapeDtypeStruct((t, (n_head + n_tail) * tn), BF16),
        scratch_shapes=[pltpu.VMEM((tm, d), BF16)],
        compiler_params=_cparams(("parallel", "arbitrary")),
        name="inproj",
    )(x, g, w_head, w_tail)


def _lru_gate_matmuls(u, wax_ref):
    bw = u.shape[1] // LRU_BLOCKS
    r_parts, i_parts = [], []
    for n in range(LRU_BLOCKS):
        ri = _dot(u[:, n * bw:(n + 1) * bw].astype(BF16), wax_ref[n])
        r_parts.append(ri[:, :bw])
        i_parts.append(ri[:, bw:])
    return jnp.concatenate(r_parts, axis=1), jnp.concatenate(i_parts, axis=1)


def _lru_gate_values(r_pre, i_pre, ba, bx, lam):
    r = jax.nn.sigmoid(r_pre + ba)
    i = jax.nn.sigmoid(i_pre + bx)
    log_a = LRU_C * r * jax.nn.log_sigmoid(lam)
    a = jnp.exp(log_a)
    m2 = -jnp.tanh(log_a) * (a * a + 1.0)
    mult = jnp.where(m2 > 0.0, m2 * lax.rsqrt(m2), 0.0)
    return a, i, mult


def _lru_gates(u, wax_ref, ba, bx, lam):
    return _lru_gate_values(*_lru_gate_matmuls(u, wax_ref), ba, bx, lam)


def _lru_conv(x, halo, seq_start, cw, cb):
    tt = x.shape[0]
    xpad = jnp.concatenate([jnp.where(seq_start, 0.0, halo), x], axis=0)
    base = SUBLANES - (CONV_W - 1)
    return cb + sum(xpad[base + k:base + k + tt] * cw[k:k + 1] for k in range(CONV_W))


def _lru_scan(u, r_pre, i_pre, gate, carry, seq_start, ba, bx, lam):
    tt, width = u.shape
    carry = jnp.where(seq_start, 0.0, carry)
    a, i, mult = _lru_gate_values(r_pre, i_pre, ba, bx, lam)
    first = jnp.logical_and(lax.broadcasted_iota(jnp.int32, a.shape, 0) == 0, seq_start)
    mult = jnp.where(first, 1.0, mult)
    a = jnp.where(first, 0.0, a)
    v = u * i * mult

    a = a.reshape(tt // SUBLANES, SUBLANES, width)
    v = v.reshape(tt // SUBLANES, SUBLANES, width)
    sub = lax.broadcasted_iota(jnp.int32, a.shape, 1)
    s = 1
    while s < SUBLANES:
        keep = sub >= s
        v = jnp.where(keep, a * pltpu.roll(v, s, axis=1) + v, v)
        a = jnp.where(keep, a * pltpu.roll(a, s, axis=1), a)
        s *= 2
    groups = []
    for g in range(tt // SUBLANES):
        hg = a[g] * carry + v[g]
        carry = hg[SUBLANES - 1:SUBLANES]
        groups.append(hg)
    h = jnp.concatenate(groups, axis=0)
    return h * jax.nn.gelu(gate), carry


def _inproj_lru_body(x_ref, g_ref, wh_ref, wt_ref, cw_ref, cb_ref, wax_ref, ba_ref, bx_ref, lam_ref,
                     o_ref, ya_ref, hlast_ref, bufout_ref, h_scr, lru_new, lru_cur, halo_scr, carry_scr,
                     *, n_head, n_j, n_tiles, tiles_per_seq, sub_rows):
    i = pl.program_id(0)
    j = pl.program_id(1)
    tm, tn = o_ref.shape
    w = ya_ref.shape[1]
    n_sub = tm // sub_rows
    prev_tile = jnp.maximum(i - 1, 0)

    @pl.when(jnp.logical_and(i == 0, j == 0))
    def _():
        lru_cur[...] = jnp.zeros_like(lru_cur)
        halo_scr[...] = jnp.zeros_like(halo_scr)
        carry_scr[...] = jnp.zeros_like(carry_scr)

    n_piece = LRU_PIECES
    piece_rows = sub_rows // n_piece
    mx_w = 2 * LANES
    col_cuts = [min(tn, mx_w * (q * (tn // mx_w) // n_piece)) for q in range(n_piece)] + [tn]

    def lru_piece(q, state):
        r0 = pl.multiple_of(j * sub_rows, sub_rows) + q * piece_rows
        seq_start = jnp.logical_and(jnp.logical_and(prev_tile % tiles_per_seq == 0, j == 0), q == 0)
        x = lru_cur[pl.ds(r0, piece_rows), 0:w].astype(F32)
        gate = lru_cur[pl.ds(r0, piece_rows), w:2 * w].astype(F32)
        halo, carry = state if state is not None else (halo_scr[...], carry_scr[...])
        u = _lru_conv(x, halo, seq_start, cw_ref[...], cb_ref[...])
        r_pre, i_pre = _lru_gate_matmuls(u, wax_ref)
        ya, carry = _lru_scan(u, r_pre, i_pre, gate, carry, seq_start, ba_ref[...], bx_ref[...], lam_ref[...])
        ya_ref[pl.ds(r0, piece_rows), :] = ya.astype(ya_ref.dtype)
        halo = x[piece_rows - SUBLANES:piece_rows]
        if q == n_piece - 1:
            halo_scr[...] = halo
            carry_scr[...] = carry

            @pl.when(jnp.logical_and(prev_tile % tiles_per_seq == tiles_per_seq - 1, j == n_sub - 1))
            def _():
                hlast_ref[0] = carry
                bufout_ref[0] = x[piece_rows - (CONV_W - 1):piece_rows]
        return halo, carry

    def project_piece(q, from_head, lru_lo):
        c0, c1 = col_cuts[q], col_cuts[q + 1]
        if c0 == c1:
            return
        w_cols = wh_ref[0, :, c0:c1] if from_head else wt_ref[:, c0:c1]
        o = _dot(h_scr[...], w_cols).astype(o_ref.dtype)
        o_ref[:, c0:c1] = o
        if lru_lo is not None:
            keep = min(lru_lo + c1, 2 * w) - (lru_lo + c0)
            if keep > 0:
                lru_new[:, lru_lo + c0:lru_lo + c0 + keep] = o[:, 0:keep]

    def column_steps(lo, hi, from_head, prologue=None, lru_lo=None):
        for a, b, with_lru in ((lo, min(hi, n_sub), True), (max(lo, n_sub), hi, False)):
            if a < b:
                @pl.when(jnp.logical_and(i < n_tiles, jnp.logical_and(j >= a, j < b)))
                def _(with_lru=with_lru):
                    if prologue is not None:
                        prologue()
                    state = None
                    for q in range(n_piece):
                        project_piece(q, from_head, lru_lo)
                        if with_lru:
                            state = lru_piece(q, state)
                    if b == n_j:
                        @pl.when(j == n_j - 1)
                        def _():
                            lru_cur[...] = lru_new[...]

    def normalise():
        h_scr[...] = _rms(x_ref[...], g_ref[...]).astype(BF16)

    column_steps(0, 1, True, prologue=normalise, lru_lo=0)
    column_steps(1, 2, True, lru_lo=tn)
    column_steps(2, n_head, True)
    column_steps(n_head, n_j, False)

    @pl.when(jnp.logical_and(i == n_tiles, j < n_sub))
    def _():
        state = None
        for q in range(n_piece):
            state = lru_piece(q, state)


def _inproj_lru(x, g, w_head, w_tail, lw, seq_len, tm, tn, n_head, sub_rows):
    t, d = x.shape
    w = lw["lru_lambda"].shape[1]
    n_tail = w_tail.shape[1] // tn
    n_tiles, n_j = t // tm, n_head + n_tail
    tiles_per_seq = seq_len // tm
    assert seq_len % tm == 0 and tm % sub_rows == 0 and tm // sub_rows < n_j and n_head >= 2
    assert w <= tn and 2 * w <= 2 * tn and 2 * w > tn
    cur_tile = lambda i: jnp.minimum(i, n_tiles - 1)
    lru_tile = lambda i: jnp.maximum(i - 1, 0)
    body = functools.partial(_inproj_lru_body, n_head=n_head, n_j=n_j, n_tiles=n_tiles,
                             tiles_per_seq=tiles_per_seq, sub_rows=sub_rows)
    out_col = lambda i, j: jnp.where(i < n_tiles, j, n_j - 1)
    w_col = lambda i, j: jnp.where(i < n_tiles, j, n_j - 1)
    return pl.pallas_call(
        body,
        grid=(n_tiles + 1, n_j),
        in_specs=[pl.BlockSpec((tm, d), lambda i, j: (cur_tile(i), 0)),
                  pl.BlockSpec((1, d), lambda i, j: (0, 0)),
                  pl.BlockSpec((1, d, tn), lambda i, j: (0, 0, jnp.minimum(w_col(i, j), n_head - 1))),
                  pl.BlockSpec((d, tn), lambda i, j: (0, jnp.maximum(w_col(i, j) - n_head, 0))),
                  _const_spec((CONV_W, w)), _const_spec((1, w)),
                  _const_spec(lw["wax"].shape), _const_spec((1, w)), _const_spec((1, w)), _const_spec((1, w))],
        out_specs=[pl.BlockSpec((tm, tn), lambda i, j: (cur_tile(i), out_col(i, j))),
                   pl.BlockSpec((tm, w), lambda i, j: (lru_tile(i), 0)),
                   pl.BlockSpec((1, 1, w), lambda i, j: (lru_tile(i) // tiles_per_seq, 0, 0)),
                   pl.BlockSpec((1, CONV_W - 1, w), lambda i, j: (lru_tile(i) // tiles_per_seq, 0, 0))],
        out_shape=[jax.ShapeDtypeStruct((t, n_j * tn), BF16),
                   jax.ShapeDtypeStruct((t, w), BF16),
                   jax.ShapeDtypeStruct((t // seq_len, 1, w), F32),
                   jax.ShapeDtypeStruct((t // seq_len, CONV_W - 1, w), F32)],
        scratch_shapes=[pltpu.VMEM((tm, d), BF16), pltpu.VMEM((tm, 2 * w), BF16), pltpu.VMEM((tm, 2 * w), BF16),
                        pltpu.VMEM((SUBLANES, w), F32), pltpu.VMEM((1, w), F32)],
        compiler_params=_cparams(("arbitrary", "arbitrary")),
        name="inproj_lru",
    )(x, g, w_head, w_tail, lw["lru_conv_w"], lw["lru_conv_b"], lw["wax"], lw["lru_ba"], lw["lru_bx"],
      lw["lru_lambda"])


def _lru_step_body(xin_ref, gate_ref, buf_ref, h0_ref, cw_ref, cb_ref, wax_ref, ba_ref, bx_ref, lam_ref,
                   ya_ref, hlast_ref, bufout_ref, *, steps):
    bsz = h0_ref.shape[0]
    n = steps * bsz
    x = xin_ref[...].astype(F32)
    xx = jnp.concatenate([buf_ref[...], x], axis=0)
    cw = cw_ref[...]
    u = cb_ref[...] + sum(xx[k * bsz:k * bsz + n] * cw[k:k + 1] for k in range(CONV_W))
    a, i, mult = _lru_gates(u, wax_ref, ba_ref[...], bx_ref[...], lam_ref[...])
    v = u * i * mult
    h = h0_ref[...]
    for t in range(steps):
        sl = slice(t * bsz, (t + 1) * bsz)
        h = a[sl] * h + v[sl]
        ya_ref[sl, :] = (h * jax.nn.gelu(gate_ref[sl, :].astype(F32))).astype(ya_ref.dtype)
    hlast_ref[...] = h
    bufout_ref[...] = xx[steps * bsz:(steps + CONV_W - 1) * bsz]


def _lru_step(xin_t, gate_t, buf_t, h0, lw, steps):
    bsz, w = h0.shape
    body = functools.partial(_lru_step_body, steps=steps)
    return pl.pallas_call(
        body,
        out_shape=[jax.ShapeDtypeStruct((steps * bsz, w), BF16),
                   jax.ShapeDtypeStruct((bsz, w), F32),
                   jax.ShapeDtypeStruct(((CONV_W - 1) * bsz, w), F32)],
        compiler_params=pltpu.CompilerParams(vmem_limit_bytes=VMEM_LIMIT),
        name="lru_step",
    )(xin_t, gate_t, buf_t, h0, lw["lru_conv_w"], lw["lru_conv_b"], lw["wax"],
      lw["lru_ba"], lw["lru_bx"], lw["lru_lambda"])


def _expand_heads(cols, e2):
    q = cols[0].shape[0]
    v = jnp.concatenate(cols, axis=0)
    hi = v.astype(BF16)
    lo = (v - hi.astype(F32)).astype(BF16)
    out = _dot(jnp.concatenate([hi, lo], axis=1), e2)
    return [out[i * q:(i + 1) * q] for i in range(len(cols))]


def _ssd_chunk(xc, dt, p, e2, s_get, s_set, t_col, t_row, n_seg=1, n_valid=None):
    q = xc.shape[0]
    rps = q // n_seg
    gn = SSD_GROUPS * SSD_STATE
    d_inner = xc.shape[1] - 2 * gn
    hpg = d_inner // SSD_HEADDIM // SSD_GROUPS
    gw = hpg * SSD_HEADDIM

    causal = t_col >= t_row
    if n_seg > 1:
        same = (lax.broadcasted_iota(jnp.int32, (q, 1), 0) // rps) == (lax.broadcasted_iota(jnp.int32, (1, q), 1) // rps)
        causal = jnp.logical_and(same, causal)
    if n_valid is not None:
        dt = jnp.where(lax.broadcasted_iota(jnp.int32, dt.shape, 0) % rps < n_valid, dt, 0.0)
    a = dt * p["A"]
    cum = _dot_f32(causal.astype(F32), a)
    cum_t = cum.T
    total = cum[q - 1:q, :] if n_seg == 1 else _dot_f32(same.astype(F32), a)
    dt_x, to_end_x, ecum_x = _expand_heads([dt, jnp.exp(total - cum), jnp.exp(cum)], e2)

    xs = xc[:, :d_inner]
    xdt = xs * dt_x
    xw = xdt * to_end_x
    packed = rps % (2 * SUBLANES) == 0
    xdt_m = xdt.astype(BF16) if packed else xdt
    if packed:
        xw = xw.astype(BF16)
    lane_head = lax.broadcasted_iota(jnp.int32, (1, gw), 1) // SSD_HEADDIM
    y_groups = []
    for g in range(SSD_GROUPS):
        sl = slice(g * gw, (g + 1) * gw)
        bg = xc[:, d_inner + g * SSD_STATE:d_inner + (g + 1) * SSD_STATE]
        cg = xc[:, d_inner + gn + g * SSD_STATE:d_inner + gn + (g + 1) * SSD_STATE]
        if packed:
            bg, cg = bg.astype(BF16), cg.astype(BF16)
        cb = _dot_nt(cg.astype(BF16), bg.astype(BF16))
        m_heads, x_heads = [], []
        for hh in range(hpg):
            h = g * hpg + hh
            decay = jnp.exp(jnp.where(causal, cum[:, h:h + 1] - cum_t[h:h + 1, :], -jnp.inf))
            m_heads.append((cb * decay).astype(BF16))
            x_heads.append(jnp.where(lane_head == hh, xdt_m[:, sl], 0.0))
        y_diag = _dot(jnp.concatenate(m_heads, axis=1), jnp.concatenate(x_heads, axis=0).astype(BF16))
        y_off = []
        for b in range(n_seg):
            rows = slice(b * rps, (b + 1) * rps)
            s_old = s_get(b, g)
            y_off.append(_dot_nt(cg[rows].astype(BF16), s_old.astype(BF16)))
            s_dec = [s_old[hh * SSD_HEADDIM:(hh + 1) * SSD_HEADDIM, :]
                     * jnp.exp(cum_t[g * hpg + hh:g * hpg + hh + 1, (b + 1) * rps - 1:(b + 1) * rps])
                     for hh in range(hpg)]
            s_set(b, g, jnp.concatenate(s_dec, axis=0) + _dot_tn(xw[rows, sl].astype(BF16), bg[rows].astype(BF16)))
        y_off = y_off[0] if n_seg == 1 else jnp.concatenate(y_off, axis=0)
        y_groups.append(y_diag + y_off * ecum_x[:, sl])
    return jnp.concatenate(y_groups, axis=1) + p["D"] * xs


def _ssd_gate_norm(y, z, p):
    gw = y.shape[1] // SSD_GROUPS
    zf = z.astype(F32)
    out = []
    for g in range(SSD_GROUPS):
        sl = slice(g * gw, (g + 1) * gw)
        v = y[:, sl] * (zf[:, sl] * jax.nn.sigmoid(zf[:, sl]))
        out.append(v * lax.rsqrt(jnp.mean(v * v, axis=-1, keepdims=True) + EPS) * p["norm_g"][:, sl])
    return jnp.concatenate(out, axis=1)


def _ssd_conv(xpad, q, cw, cb):
    base = SUBLANES - (CONV_W - 1)
    y = cb + sum(xpad[base + k:base + k + q] * cw[k:k + 1] for k in range(CONV_W))
    return y * jax.nn.sigmoid(y)


def _softplus(x):
    return jax.nn.softplus(x)


def _ssd_params(cw_ref, cb_ref, dtb_ref, a_ref, d_ref, ng_ref):
    return {"cw": cw_ref[...], "cb": cb_ref[...], "dt_bias": dtb_ref[...], "A": a_ref[...],
            "D": d_ref[...], "norm_g": ng_ref[...]}


def _ssd_prompt_body(xbc_ref, z_ref, dt_ref, cw_ref, cb_ref, dtb_ref, a_ref, d_ref, ng_ref, e2_ref,
                     yb_ref, sout_ref, bufout_ref, x_scr, dt_scr, y_scr, s_scr, *, q):
    c = pl.program_id(1)
    rows = xbc_ref.shape[1]
    half = q // 2
    n_xslab = x_scr.shape[0]
    base = SUBLANES - (CONV_W - 1)

    @pl.when(c == 0)
    def _():
        x_scr[:, 0:SUBLANES, :] = jnp.zeros((n_xslab, SUBLANES, LANES), F32)
        s_scr[...] = jnp.zeros_like(s_scr)

    @pl.when(c > 0)
    def _():
        x_scr[:, 0:SUBLANES, :] = x_scr[:, rows:rows + SUBLANES, :]

    p = _ssd_params(cw_ref, cb_ref, dtb_ref, a_ref, d_ref, ng_ref)
    x = xbc_ref[0].astype(F32)
    for j in range(n_xslab):
        x_scr[j, SUBLANES:SUBLANES + rows, :] = x[:, j * LANES:(j + 1) * LANES]
    dt_scr[...] = _softplus(dt_ref[0].astype(F32) + p["dt_bias"])

    def times(shape, axis):
        pos = lax.broadcasted_iota(jnp.int32, shape, axis)
        return jnp.where(pos < half, 2 * pos, 2 * (pos - half) + 1)

    def s_set(b, g, v):
        s_scr[g] = v

    for ch in range(rows // q):
        r0 = ch * q
        cols = []
        for j in range(n_xslab):
            ls = slice(j * LANES, (j + 1) * LANES)
            halves = []
            for par in range(2):
                acc = p["cb"][:, ls]
                for k in range(CONV_W):
                    acc = acc + x_scr[j, pl.ds(r0 + base + k + par, half, stride=2), :] * p["cw"][k:k + 1, ls]
                halves.append(acc)
            cols.append(jnp.concatenate(halves, axis=0))
        xc = jnp.concatenate(cols, axis=1)
        xc = xc * jax.nn.sigmoid(xc)
        dt = jnp.concatenate([dt_scr[pl.ds(r0 + par, half, stride=2), :] for par in range(2)], axis=0)
        y = _ssd_chunk(xc, dt, p, e2_ref[...], lambda b, g: s_scr[g], s_set, times((q, 1), 0), times((1, q), 1))
        for j in range(y_scr.shape[0]):
            for par in range(2):
                y_scr[j, pl.ds(par, half, stride=2), :] = y[par * half:(par + 1) * half, j * LANES:(j + 1) * LANES]
        y = jnp.concatenate([y_scr[j] for j in range(y_scr.shape[0])], axis=1)
        yb_ref[0, r0:r0 + q, :] = _ssd_gate_norm(y, z_ref[0, r0:r0 + q, :], p).astype(yb_ref.dtype)

    @pl.when(c == pl.num_programs(1) - 1)
    def _():
        sout_ref[0] = s_scr[...]
        bufout_ref[0] = x[rows - (CONV_W - 1):rows]


def _ssd_prompt(proj3, sp, col_xbc, col_z, col_dt):
    b, l, _ = proj3.shape
    cdim = sp["ssd_conv_w"].shape[1]
    d_inner = sp["ssd_norm_g"].shape[1]
    gw = d_inner // SSD_GROUPS
    q = SSD_CHUNK if l % SSD_CHUNK == 0 else l
    assert q % (2 * SUBLANES) == 0
    rows = q * SSD_CHUNKS_PER_STEP if l % (q * SSD_CHUNKS_PER_STEP) == 0 else q
    return pl.pallas_call(
        functools.partial(_ssd_prompt_body, q=q),
        grid=(b, l // rows),
        in_specs=[pl.BlockSpec((1, rows, cdim), lambda i, c: (i, c, col_xbc // cdim)),
                  pl.BlockSpec((1, rows, d_inner), lambda i, c: (i, c, col_z // d_inner)),
                  pl.BlockSpec((1, rows, LANES), lambda i, c: (i, c, col_dt // LANES)),
                  _const_spec((CONV_W, cdim)), _const_spec((1, cdim)), _const_spec((1, LANES)),
                  _const_spec((1, LANES)), _const_spec((1, d_inner)), _const_spec((1, d_inner)),
                  _const_spec((2 * LANES, d_inner))],
        out_specs=[pl.BlockSpec((1, rows, d_inner), lambda i, c: (i, c, 0)),
                   pl.BlockSpec((1, SSD_GROUPS, gw, SSD_STATE), lambda i, c: (i, 0, 0, 0)),
                   pl.BlockSpec((1, CONV_W - 1, cdim), lambda i, c: (i, 0, 0))],
        out_shape=[jax.ShapeDtypeStruct((b, l, d_inner), BF16),
                   jax.ShapeDtypeStruct((b, SSD_GROUPS, gw, SSD_STATE), F32),
                   jax.ShapeDtypeStruct((b, CONV_W - 1, cdim), F32)],
        scratch_shapes=[pltpu.VMEM((cdim // LANES, rows + SUBLANES, LANES), F32),
                        pltpu.VMEM((rows, LANES), F32),
                        pltpu.VMEM((d_inner // LANES, q, LANES), F32),
                        pltpu.VMEM((SSD_GROUPS, gw, SSD_STATE), F32)],
        compiler_params=_cparams(("parallel", "arbitrary")),
        name="ssd_prompt",
    )(proj3, proj3, proj3, sp["ssd_conv_w"], sp["ssd_conv_b"], sp["dt_bias"], sp["A"], sp["D"], sp["ssd_norm_g"],
      sp["head_expand"])


def _ssd_step_body(xbc_ref, z_ref, dt_ref, buf_ref, s_ref, cw_ref, cb_ref, dtb_ref, a_ref, d_ref, ng_ref, e2_ref,
                   yb_ref, sout_ref, bufout_ref, *, steps, nb):
    p = _ssd_params(cw_ref, cb_ref, dtb_ref, a_ref, d_ref, ng_ref)
    rps = SUBLANES
    q = nb * rps
    x_all = xbc_ref[...].astype(F32)
    z_all = z_ref[...].astype(F32)
    dt_all = _softplus(dt_ref[...].astype(F32) + p["dt_bias"])
    cdim = x_all.shape[1]

    def padded(v, j):
        return jnp.concatenate([v[j * steps:(j + 1) * steps], jnp.zeros((rps - steps, v.shape[1]), v.dtype)], axis=0)

    xcs = []
    for j in range(nb):
        xpad = jnp.concatenate([jnp.zeros((SUBLANES - (CONV_W - 1), cdim), F32), buf_ref[j], padded(x_all, j)],
                               axis=0)
        xcs.append(_ssd_conv(xpad, rps, p["cw"], p["cb"]))
        bufout_ref[j] = xpad[SUBLANES + steps - (CONV_W - 1):SUBLANES + steps]
    xc = jnp.concatenate(xcs, axis=0)
    dt = jnp.concatenate([padded(dt_all, j) for j in range(nb)], axis=0)
    z = jnp.concatenate([padded(z_all, j) for j in range(nb)], axis=0)

    def s_set(b, g, v):
        sout_ref[b, g] = v

    t_col = lax.broadcasted_iota(jnp.int32, (q, 1), 0) % rps
    t_row = lax.broadcasted_iota(jnp.int32, (1, q), 1) % rps
    y = _ssd_chunk(xc, dt, p, e2_ref[...], lambda b, g: s_ref[b, g], s_set, t_col, t_row, n_seg=nb, n_valid=steps)
    y = _ssd_gate_norm(y, z, p)
    for j in range(nb):
        yb_ref[j * steps:(j + 1) * steps, :] = y[j * rps:j * rps + steps].astype(yb_ref.dtype)


def _ssd_step(proj, buf, s0, sp, steps, nb, col_xbc, col_z, col_dt):
    bsz = s0.shape[0]
    cdim = sp["ssd_conv_w"].shape[1]
    d_inner = sp["ssd_norm_g"].shape[1]
    gw = d_inner // SSD_GROUPS
    rows = nb * steps
    body = functools.partial(_ssd_step_body, steps=steps, nb=nb)
    return pl.pallas_call(
        body,
        grid=(bsz // nb,),
        in_specs=[pl.BlockSpec((rows, cdim), lambda i: (i, col_xbc // cdim)),
                  pl.BlockSpec((rows, d_inner), lambda i: (i, col_z // d_inner)),
                  pl.BlockSpec((rows, LANES), lambda i: (i, col_dt // LANES)),
                  pl.BlockSpec((None, nb, CONV_W - 1, cdim), lambda i: (0, i, 0, 0)),
                  pl.BlockSpec((nb, SSD_GROUPS, gw, SSD_STATE), lambda i: (i, 0, 0, 0)),
                  _const_spec((CONV_W, cdim)), _const_spec((1, cdim)), _const_spec((1, LANES)),
                  _const_spec((1, LANES)), _const_spec((1, d_inner)), _const_spec((1, d_inner)),
                  _const_spec((2 * LANES, d_inner))],
        out_specs=[pl.BlockSpec((rows, d_inner), lambda i: (i, 0)),
                   pl.BlockSpec((nb, SSD_GROUPS, gw, SSD_STATE), lambda i: (i, 0, 0, 0)),
                   pl.BlockSpec((None, nb, CONV_W - 1, cdim), lambda i: (0, i, 0, 0))],
        out_shape=[jax.ShapeDtypeStruct((bsz * steps, d_inner), BF16),
                   jax.ShapeDtypeStruct(s0.shape, F32),
                   jax.ShapeDtypeStruct(buf.shape, F32)],
        compiler_params=_cparams(("parallel",)),
        name="ssd_step",
    )(proj, proj, proj, buf, s0, sp["ssd_conv_w"], sp["ssd_conv_b"], sp["dt_bias"], sp["A"], sp["D"],
      sp["ssd_norm_g"], sp["head_expand"])


def _router(t, wr, br):
    t_hi = t.astype(BF16)
    t_lo = (t - t_hi.astype(F32)).astype(BF16)
    both = _dot(jnp.concatenate([t_hi, t_lo], axis=1), wr)
    logits = both[:, :LANES] + both[:, LANES:] + br
    lane = lax.broadcasted_iota(jnp.int32, logits.shape, 1)
    neg = -jnp.inf
    gl = jnp.where(lane < N_EGROUPS, logits, neg)
    gmax = jnp.max(gl, axis=-1, keepdims=True)
    g_idx = jnp.min(jnp.where(gl == gmax, lane, LANES), axis=-1, keepdims=True)
    g_w = 1.0 / jnp.sum(jnp.exp(gl - gmax), axis=-1, keepdims=True)
    in_grp = jnp.logical_and(jnp.logical_and(lane >= ROUTER_LANE0, lane < ROUTER_LANE0 + N_EXPERTS),
                             ((lane - ROUTER_LANE0) >> 2) == g_idx)
    el = jnp.where(in_grp, logits, neg)
    pe = jnp.exp(el - jnp.max(el, axis=-1, keepdims=True))
    pe = pe / jnp.sum(pe, axis=-1, keepdims=True)
    cand = jnp.where(in_grp, pe, -1.0)
    v1 = jnp.max(cand, axis=-1, keepdims=True)
    i1 = jnp.min(jnp.where(cand == v1, lane, LANES), axis=-1, keepdims=True)
    cand2 = jnp.where(lane == i1, -1.0, cand)
    v2 = jnp.max(cand2, axis=-1, keepdims=True)
    i2 = jnp.min(jnp.where(jnp.logical_and(cand2 == v2, in_grp), lane, LANES), axis=-1, keepdims=True)
    den = v1 + v2
    return lane, i1, i2, g_w * v1 / den, g_w * v2 / den


def _rows_to_tiles(ref, val):
    n, d = val.shape
    for k in range(d // LANES):
        ref[pl.ds(k, n, stride=d // LANES), :] = val[:, k * LANES:(k + 1) * LANES]


def _tiles_to_rows(ref, n, d, start=0):
    return jnp.concatenate([ref[pl.ds(start + k, n, stride=d // LANES), :] for k in range(d // LANES)], axis=1)


def _merge_body(x_ref, ya_ref, yb_ref, ga_ref, gb_ref, wl_ref, ws_ref, wo_ref, gf_ref, wr_ref, br_ref, cnt0_ref,
                x1_ref, t_ref, rt_ref, rtt_ref, cnt_ref, base_scr):
    step = pl.program_id(0)

    @pl.when(step == 0)
    def _():
        base_scr[...] = cnt0_ref[...]

    a = _dot(ya_ref[...], wl_ref[...])
    b = _dot(yb_ref[...], ws_ref[...])
    merged = jax.nn.sigmoid(ga_ref[...].astype(F32)) * a + jax.nn.sigmoid(gb_ref[...].astype(F32)) * b
    x1 = x_ref[...] + _dot(merged.astype(BF16), wo_ref[...])
    x1_ref[...] = x1
    t = _rms(x1, gf_ref[...])
    _rows_to_tiles(t_ref, t)
    lane, i1, i2, wg1, wg2 = _router(t, wr_ref[...], br_ref[...])

    tm = t.shape[0]
    onehot = jnp.where(jnp.logical_or(lane == i1, lane == i2), 1.0, 0.0).astype(BF16)
    tri = (lax.broadcasted_iota(jnp.int32, (tm, tm), 1) <= lax.broadcasted_iota(jnp.int32, (tm, tm), 0)).astype(BF16)
    cum = _dot(tri, onehot) + base_scr[...]
    r1 = jnp.sum(jnp.where(lane == i1, cum, 0.0), axis=-1, keepdims=True) - 1.0
    r2 = jnp.sum(jnp.where(lane == i2, cum, 0.0), axis=-1, keepdims=True) - 1.0
    cols = (wg1, wg2, r1, r2, (i1 - ROUTER_LANE0).astype(F32), (i2 - ROUTER_LANE0).astype(F32))
    rt = jnp.zeros(cum.shape, F32)
    for k, c in enumerate(cols):
        rt = jnp.where(lane == k, c, rt)
    rt_ref[...] = rt
    rtt_ref[...] = rt.T[0:SUBLANES, :]
    base_scr[...] = cum[tm - 1:tm, :]
    cnt_ref[...] = cum[tm - 1:tm, :]


def _merge(x, ya, yb, proj, mw, tm, col_ga, col_gb, counts_so_far):
    t, d = x.shape
    d_inner = yb.shape[1]
    return pl.pallas_call(
        _merge_body,
        grid=(t // tm,),
        in_specs=[pl.BlockSpec((tm, d), lambda i: (i, 0)),
                  pl.BlockSpec((tm, d), lambda i: (i, 0)),
                  pl.BlockSpec((tm, d_inner), lambda i: (i, 0)),
                  pl.BlockSpec((tm, d), lambda i: (i, col_ga // d)),
                  pl.BlockSpec((tm, d), lambda i: (i, col_gb // d)),
                  _const_spec((d, d)), _const_spec((d_inner, d)), _const_spec((d, d)),
                  _const_spec((1, d)), _const_spec((2 * d, 2 * LANES)), _const_spec((1, LANES)),
                  _const_spec((1, LANES))],
        out_specs=[pl.BlockSpec((tm, d), lambda i: (i, 0)),
                   pl.BlockSpec((tm * d // LANES, LANES), lambda i: (i, 0)),
                   pl.BlockSpec((tm, LANES), lambda i: (i, 0)),
                   pl.BlockSpec((SUBLANES, tm), lambda i: (0, i)),
                   pl.BlockSpec((1, LANES), lambda i: (0, 0))],
        out_shape=[jax.ShapeDtypeStruct((t, d), F32),
                   jax.ShapeDtypeStruct((t * d // LANES, LANES), F32),
                   jax.ShapeDtypeStruct((t, LANES), F32),
                   jax.ShapeDtypeStruct((SUBLANES, t), F32),
                   jax.ShapeDtypeStruct((1, LANES), F32)],
        scratch_shapes=[pltpu.VMEM((1, LANES), F32)],
        compiler_params=_cparams(("arbitrary",)),
        name="merge_router",
    )(x, ya, yb, proj, proj, mw["w_br_lru"], mw["w_br_ssd"], mw["w_out"], mw["g_ffn"], mw["w_router"],
      mw["b_router"], counts_so_far)


def _dispatch_body(zb_ref, *rest, n_groups, tms, n_toks, nk, tmg):
    dest_refs, t_refs = rest[:n_groups], rest[n_groups:2 * n_groups]
    o_ref, zero_scr, sem, zsem = rest[2 * n_groups:]
    step = pl.program_id(0)

    @pl.when(step == 0)
    def _():
        zero_scr[...] = jnp.zeros_like(zero_scr)
        blk = tmg * nk

        def zcopy(j):
            return pltpu.make_async_copy(zero_scr, o_ref.at[pl.ds(pl.multiple_of(zb_ref[j] * blk, blk), blk)], zsem)

        for j in range(zb_ref.shape[0]):
            pl.when(zb_ref[j] >= 0)(lambda j=j: zcopy(j).start())
        for j in range(zb_ref.shape[0]):
            pl.when(zb_ref[j] >= 0)(lambda j=j: zcopy(j).wait())

    first_step = 0
    for g in range(n_groups):
        tm, n_tok, steps = tms[g], n_toks[g], n_toks[g] // tms[g]

        @pl.when(jnp.logical_and(step >= first_step, step < first_step + steps))
        def _(g=g, tm=tm, n_tok=n_tok, first_step=first_step):
            t_ref, dest_ref = t_refs[g], dest_refs[g]

            def issue(r, carry):
                src = t_ref.at[pl.ds(pl.multiple_of(r * nk, nk), nk)]
                for k in range(2):
                    row = dest_ref[k * n_tok + (step - first_step) * tm + r]
                    pltpu.make_async_copy(src, o_ref.at[pl.ds(pl.multiple_of(row * nk, nk), nk)],
                                          sem).start(priority=k)
                return carry

            lax.fori_loop(0, tm, issue, 0, unroll=8)
            for k in range(2):
                pltpu.make_async_copy(t_ref, o_ref.at[pl.ds(0, tm * nk)], sem).wait()

        first_step += steps


def _dispatch(dests, zero_blocks, t_tiles, n_toks, tms, n_rows, tmg):
    n_groups = len(dests)
    nk = t_tiles[0].shape[0] // n_toks[0]
    steps = [n // tm for n, tm in zip(n_toks, tms)]
    starts = [sum(steps[:g]) for g in range(n_groups)]
    block_of = lambda g: (lambda i, *_: (jnp.clip(i - starts[g], 0, steps[g] - 1), 0))
    return pl.pallas_call(
        functools.partial(_dispatch_body, n_groups=n_groups, tms=tuple(tms), n_toks=tuple(n_toks), nk=nk, tmg=tmg),
        grid_spec=pltpu.PrefetchScalarGridSpec(
            num_scalar_prefetch=1 + n_groups,
            grid=(sum(steps),),
            in_specs=[pl.BlockSpec((tms[g] * nk, LANES), block_of(g)) for g in range(n_groups)],
            out_specs=pl.BlockSpec(memory_space=pl.ANY),
            scratch_shapes=[pltpu.VMEM((tmg * nk, LANES), F32), pltpu.SemaphoreType.DMA(()),
                            pltpu.SemaphoreType.DMA(())]),
        out_shape=jax.ShapeDtypeStruct((n_rows * nk, LANES), F32),
        compiler_params=_cparams(("arbitrary",)),
        name="moe_dispatch",
    )(zero_blocks, *dests, *t_tiles)


def _expert_body(te_ref, nt_ref, x_ref, w1_ref, w3_ref, w2_ref, y_ref, w1_scr, w3_scr, w2_scr, *, tmg):
    i = pl.program_id(0)
    real = i < nt_ref[0]
    d = w1_scr.shape[0]

    @pl.when(jnp.logical_or(i == 0, te_ref[i] != te_ref[jnp.maximum(i - 1, 0)]))
    def _():
        w1_scr[...] = w1_ref[0].astype(BF16)
        w3_scr[...] = w3_ref[0].astype(BF16)
        w2_scr[...] = w2_ref[0].astype(BF16)

    @pl.when(real)
    def _():
        x = _tiles_to_rows(x_ref, tmg, d).astype(BF16)
        h1 = _dot(x, w1_scr[...])
        h3 = _dot(x, w3_scr[...])
        _rows_to_tiles(y_ref, _dot((h1 * jax.nn.sigmoid(h1) * h3).astype(BF16), w2_scr[...]))

    @pl.when(jnp.logical_not(real))
    def _():
        y_ref[...] = jnp.zeros_like(y_ref)


def _experts(tile_expert, n_tiles, xs_tiles, w1, w3, w2, tmg):
    _, d, dff = w1.shape
    blk = tmg * d // LANES
    row_spec = pl.BlockSpec((blk, LANES), lambda i, te, nt: (i, 0))
    return pl.pallas_call(
        functools.partial(_expert_body, tmg=tmg),
        grid_spec=pltpu.PrefetchScalarGridSpec(
            num_scalar_prefetch=2,
            grid=(xs_tiles.shape[0] // blk,),
            in_specs=[row_spec,
                      pl.BlockSpec((1, d, dff), lambda i, te, nt: (te[i], 0, 0)),
                      pl.BlockSpec((1, d, dff), lambda i, te, nt: (te[i], 0, 0)),
                      pl.BlockSpec((1, dff, d), lambda i, te, nt: (te[i], 0, 0))],
            out_specs=row_spec,
            scratch_shapes=[pltpu.VMEM((d, dff), BF16), pltpu.VMEM((d, dff), BF16), pltpu.VMEM((dff, d), BF16)]),
        out_shape=jax.ShapeDtypeStruct(xs_tiles.shape, F32),
        compiler_params=_cparams(("arbitrary",)),
        name="moe_experts",
    )(tile_expert, n_tiles, xs_tiles, w1, w3, w2)


def _ple_body(dest_ref, x_ref, rt_ref, p_ref, wp_ref, gp_ref, gg_ref, wg_ref, gfin_ref, y_hbm, o_ref, gbuf, sem,
              *, n_tok):
    step = pl.program_id(0)
    n_steps = pl.num_programs(0)
    tm, d = x_ref.shape
    nk = d // LANES
    pr = tm // COMBINE_PIECES

    def issue_row(tile, slot, r):
        tok = tile * tm + r
        for k in range(2):
            row = dest_ref[k * n_tok + tok]
            pltpu.make_async_copy(y_hbm.at[pl.ds(pl.multiple_of(row * nk, nk), nk)],
                                  gbuf.at[slot, pl.ds(pl.multiple_of((k * tm + r) * nk, nk), nk)],
                                  sem.at[slot]).start(priority=k)

    def wait_slot(slot):
        pltpu.make_async_copy(y_hbm.at[pl.ds(0, 2 * tm * nk)], gbuf.at[slot], sem.at[slot]).wait()

    @pl.when(step == 0)
    def _():
        def issue(r, carry):
            issue_row(0, 0, r)
            return carry

        lax.fori_loop(0, tm, issue, 0, unroll=8)

    slot = step % 2
    wait_slot(slot)
    nxt_tile = jnp.minimum(step + 1, n_steps - 1)
    rows = gbuf.at[slot]
    for q in range(COMBINE_PIECES):
        sl = slice(q * pr, (q + 1) * pr)
        rt = rt_ref[sl, :]
        x = (x_ref[sl, :] + rt[:, 0:1] * _tiles_to_rows(rows, pr, d, start=q * pr * nk)
             + rt[:, 1:2] * _tiles_to_rows(rows, pr, d, start=(tm + q * pr) * nk))
        e = _rms(_dot(p_ref[sl, :].astype(BF16), wp_ref[...]), gp_ref[...])
        gate = jax.nn.sigmoid(_dot(_rms(x, gg_ref[...]).astype(BF16), wg_ref[...]))
        o_ref[sl, :] = _rms(x + gate * e, gfin_ref[...])
        for r in range(q * pr, (q + 1) * pr):
            issue_row(nxt_tile, 1 - slot, r)

    pl.when(step == n_steps - 1)(lambda: wait_slot(1 - slot))


def _ple(dest, x1, rt, p, y_sorted, pw, tm):
    n, d = x1.shape
    dp = p.shape[1]
    const = lambda shape: pl.BlockSpec(shape, lambda i, *_: (0,) * len(shape))
    return pl.pallas_call(
        functools.partial(_ple_body, n_tok=n),
        grid_spec=pltpu.PrefetchScalarGridSpec(
            num_scalar_prefetch=1,
            grid=(n // tm,),
            in_specs=[pl.BlockSpec((tm, d), lambda i, *_: (i, 0)),
                      pl.BlockSpec((tm, LANES), lambda i, *_: (i, 0)),
                      pl.BlockSpec((tm, dp), lambda i, *_: (i, 0)),
                      const((dp, d)), const((1, d)), const((1, d)), const((d, d)), const((1, d)),
                      pl.BlockSpec(memory_space=pl.ANY)],
            out_specs=pl.BlockSpec((tm, d), lambda i, *_: (i, 0)),
            scratch_shapes=[pltpu.VMEM((2, 2 * tm * d // LANES, LANES), F32), pltpu.SemaphoreType.DMA((2,))]),
        out_shape=jax.ShapeDtypeStruct((n, d), F32),
        compiler_params=_cparams(("arbitrary",)),
        name="combine_ple_final",
    )(dest, x1, rt, p, pw["w_ple_proj"], pw["g_ple"], pw["g_ple_gate"], pw["w_ple_gate"], pw["g_final"], y_sorted)


def _pick_tile(n, pref):
    t = min(n, pref)
    while n % t:
        t //= 2
    return t


def _moe_row_tile(n):
    return 512 if 2 * n // N_EXPERTS >= 1024 else 128
MOE_DISPATCH_TILE = 4096
COMBINE_TILE = 1024
SSD_CHUNKS_PER_STEP = 4
COMBINE_PIECES = 1
LRU_SLICE_ROWS = 128
LRU_PIECES = 2


def _split_router(w):
    hi = w.astype(BF16)
    lo = (w - hi.astype(F32)).astype(BF16)
    return jnp.concatenate([jnp.concatenate([hi, lo], axis=1),
                            jnp.concatenate([hi, jnp.zeros_like(hi)], axis=1)], axis=0)


def _token_tail(groups, cnt, lw):
    n_all = sum(g["x1"].shape[0] for g in groups)
    tmg = _moe_row_tile(n_all)
    n_blocks = pl.cdiv(2 * n_all, tmg) + N_EXPERTS
    counts = cnt[0, ROUTER_LANE0:ROUTER_LANE0 + N_EXPERTS].astype(jnp.int32)
    tiles = (counts + tmg - 1) // tmg
    ends = jnp.cumsum(tiles)
    n_tiles = ends[-1]
    first_row = (ends - tiles) * tmg
    blk = jnp.arange(n_blocks, dtype=jnp.int32)
    tile_expert = jnp.sum((jnp.minimum(blk, n_tiles - 1)[:, None] >= ends[None, :]).astype(jnp.int32), axis=1)
    tail = n_tiles + blk[:N_EXPERTS]
    zero_blocks = jnp.concatenate([jnp.where(tiles > 0, ends - 1, -1),
                                   jnp.where(tail < n_blocks, tail, -1)]).astype(jnp.int32)

    dests = []
    for g in groups:
        e_idx = g["rtt"][4:6].astype(jnp.int32)
        expert = jnp.arange(N_EXPERTS, dtype=jnp.int32)[:, None, None]
        dest = g["rtt"][2:4].astype(jnp.int32) + jnp.sum(
            jnp.where(e_idx[None] == expert, first_row[:, None, None], 0), axis=0)
        dests.append(dest.reshape(2 * g["x1"].shape[0]))
    n_toks = [g["x1"].shape[0] for g in groups]
    sorted_t = _dispatch(dests, zero_blocks, [g["t_tiles"] for g in groups], n_toks,
                         [_pick_tile(n, MOE_DISPATCH_TILE) for n in n_toks], n_blocks * tmg, tmg)
    y_sorted = _experts(tile_expert, n_tiles.reshape(1), sorted_t, lw["w1"], lw["w3"], lw["w2"], tmg)
    return [_ple(dest, g["x1"], g["rt"], g["p"], y_sorted, lw, g["tm"]) for g, dest in zip(groups, dests)]


def kernel(x_prompt, x_sample, state_lru_h, state_lru_conv, state_ssd, state_ssd_conv, p_prompt, p_sample, g_mix, w_in, lru_conv_w, lru_conv_b, lru_wa, lru_ba, lru_wx, lru_bx, lru_lambda, ssd_conv_w, ssd_conv_b, ssd_dt_bias, ssd_A_log, ssd_D, ssd_norm_g, w_br_lru, w_br_ssd, w_out, g_ffn, w_router_g, b_router_g, w_router_e, b_router_e, w1, w3, w2, w_ple_proj, g_ple, g_ple_gate, w_ple_gate, g_final):
    depth = w_in.shape[0]
    assert depth == 1, "one decoder layer per call"
    bp, lp, d = x_prompt.shape
    bs, ls, _ = x_sample.shape
    w_lru = state_lru_h.shape[-1]
    heads, hdim, nstate = state_ssd.shape[2:]
    d_inner = heads * hdim
    cdim = state_ssd_conv.shape[-1]
    assert hdim == SSD_HEADDIM and nstate == SSD_STATE and heads <= LANES and ls < SUBLANES
    gw = d_inner // SSD_GROUPS

    o_dt = 2 * w_lru + d_inner + cdim
    n_proj = o_dt + 2 * d + LANES
    tn = n_proj // 9 if n_proj % (9 * LANES) == 0 else LANES
    n_head = o_dt // tn
    w_head = w_in.astype(BF16)
    wi = w_head[0]
    w_tail = jnp.concatenate([wi[:, n_head * tn:o_dt], wi[:, o_dt + heads:], wi[:, o_dt:o_dt + heads],
                              jnp.zeros((d, LANES - heads), BF16)], axis=1)
    assert n_head >= 1 and w_tail.shape[1] == n_proj - n_head * tn and w_tail.shape[1] % tn == 0
    col_z, col_xbc = 2 * w_lru, 2 * w_lru + d_inner
    col_ga, col_gb, col_dt = o_dt, o_dt + d, o_dt + 2 * d
    row = lambda v: v.reshape(1, -1).astype(F32)
    pad_heads = lambda v: jnp.pad(v.astype(F32), (0, LANES - heads)).reshape(1, LANES)
    lw = {
        "lru_conv_w": lru_conv_w[0], "lru_conv_b": row(lru_conv_b[0]),
        "wax": jnp.concatenate([lru_wa[0], lru_wx[0]], axis=-1).astype(BF16),
        "lru_ba": row(lru_ba[0]), "lru_bx": row(lru_bx[0]), "lru_lambda": row(lru_lambda[0]),
        "ssd_conv_w": ssd_conv_w[0], "ssd_conv_b": row(ssd_conv_b[0]),
        "dt_bias": pad_heads(ssd_dt_bias[0]), "A": pad_heads(-jnp.exp(ssd_A_log[0].astype(F32))),
        "D": row(jnp.repeat(ssd_D[0], hdim)), "ssd_norm_g": row(ssd_norm_g[0]),
        "head_expand": jnp.tile(jnp.arange(LANES)[:, None] == jnp.arange(d_inner)[None, :] // hdim, (2, 1)).astype(BF16),
        "w_br_lru": w_br_lru[0].astype(BF16), "w_br_ssd": w_br_ssd[0].astype(BF16), "w_out": w_out[0].astype(BF16),
        "g_ffn": row(g_ffn[0]),
        "w_router": _split_router(jnp.concatenate([w_router_g[0], w_router_e[0],
                                                   jnp.zeros((d, LANES - N_EGROUPS - N_EXPERTS), F32)], axis=1)),
        "b_router": jnp.concatenate([b_router_g[0], b_router_e[0],
                                     jnp.zeros((LANES - N_EGROUPS - N_EXPERTS,), F32)]).reshape(1, LANES),
        "w1": w1[0], "w3": w3[0], "w2": w2[0],
        "w_ple_proj": w_ple_proj[0].astype(BF16), "g_ple": row(g_ple[0]), "g_ple_gate": row(g_ple_gate[0]),
        "w_ple_gate": w_ple_gate[0].astype(BF16), "g_final": row(g_final),
    }
    g_mix_r = row(g_mix[0])

    tp = bp * lp
    xp = x_prompt.reshape(tp, d)
    tm_p = _pick_tile(lp, 1024)
    proj_p, ya_p, hl_p, lbuf_p = _inproj_lru(xp, g_mix_r, w_head, w_tail, lw, lp, tm_p, tn, n_head,
                                             _pick_tile(tm_p, LRU_SLICE_ROWS))
    proj_p3 = proj_p.reshape(bp, lp, n_proj)
    yb_p, s_p, sbuf_p = _ssd_prompt(proj_p3, lw, col_xbc, col_z, col_dt)
    tm_tail_p = _pick_tile(tp, 512)
    x1_p, t_p, rt_p, rtt_p, cnt_p = _merge(xp, ya_p.reshape(tp, w_lru), yb_p.reshape(tp, d_inner), proj_p, lw,
                                           tm_tail_p, col_ga, col_gb, jnp.zeros((1, LANES), F32))
    group_p = {"x1": x1_p, "t_tiles": t_p, "rt": rt_p, "rtt": rtt_p, "p": p_prompt[0].reshape(tp, -1),
               "tm": _pick_tile(tp, COMBINE_TILE)}

    ts = bs * ls
    xs = x_sample.reshape(ts, d)
    tm_s = _pick_tile(ts, 512)
    proj_s = _inproj(xs, g_mix_r, w_head, w_tail, tm_s, tn, n_head)
    to_tmajor = lambda v, n: v.reshape(bs, n, -1).transpose(1, 0, 2).reshape(n * bs, -1)
    from_tmajor = lambda v, n: v.reshape(n, bs, -1).transpose(1, 0, 2)
    ya_t, hl_s, lbuf_t = _lru_step(to_tmajor(proj_s[:, :w_lru], ls), to_tmajor(proj_s[:, w_lru:2 * w_lru], ls),
                                   to_tmajor(state_lru_conv[0], CONV_W - 1), state_lru_h[0], lw, ls)
    ya_s = from_tmajor(ya_t, ls).reshape(ts, w_lru)
    lbuf_s = from_tmajor(lbuf_t, CONV_W - 1)
    yb_s, s_s, sbuf_s = _ssd_step(proj_s, state_ssd_conv, state_ssd[0].reshape(bs, SSD_GROUPS, gw, nstate), lw,
                                  ls, _pick_tile(bs, 8), col_xbc, col_z, col_dt)
    x1_s, t_s, rt_s, rtt_s, cnt_all = _merge(xs, ya_s, yb_s, proj_s, lw, tm_s, col_ga, col_gb, cnt_p)
    group_s = {"x1": x1_s, "t_tiles": t_s, "rt": rt_s, "rtt": rtt_s, "p": p_sample[0].reshape(ts, -1), "tm": tm_s}

    y_p, y_s = _token_tail([group_p, group_s], cnt_all, lw)

    return (y_p.reshape(bp, lp, d), y_s.reshape(bs, ls, d),
            hl_p.reshape(1, bp, w_lru), lbuf_p[None],
            s_p.reshape(1, bp, heads, hdim, nstate), sbuf_p[None],
            hl_s[None], lbuf_s[None],
            s_s.reshape(1, bs, heads, hdim, nstate), sbuf_s)
```

```python
import functools

import jax
import jax.numpy as jnp
from jax import lax
from jax.experimental import pallas as pl
from jax.experimental.pallas import tpu as pltpu

F32 = jnp.float32
BF16 = jnp.bfloat16

EPS = 1e-6
CONV_W = 4
LRU_BLOCKS = 8
LRU_C = 8.0
SSD_HEADDIM = 64
SSD_GROUPS = 8
SSD_STATE = 128
SSD_CHUNK = 128
N_EGROUPS = 4
EXP_PER_GROUP = 4
N_EXPERTS = N_EGROUPS * EXP_PER_GROUP

LANES = 128
SUBLANES = 8
VMEM_LIMIT = 52 * 1024 * 1024
ROUTER_LANE0 = N_EGROUPS


def _cparams(sem):
    return pltpu.CompilerParams(dimension_semantics=sem, vmem_limit_bytes=VMEM_LIMIT)


def _dot(a, b):
    return jnp.dot(a, b, preferred_element_type=F32)


def _dot_nt(a, b):
    return lax.dot_general(a, b, (((1,), (1,)), ((), ())), preferred_element_type=F32)


def _dot_tn(a, b):
    return lax.dot_general(a, b, (((0,), (0,)), ((), ())), preferred_element_type=F32)


def _dot_f32(a, b):
    return jnp.dot(a, b, precision=lax.Precision.HIGHEST, preferred_element_type=F32)


def _rms(x, g):
    return x * lax.rsqrt(jnp.mean(x * x, axis=-1, keepdims=True) + EPS) * g


def _const_spec(shape):
    nd = len(shape)
    return pl.BlockSpec(shape, lambda *_: (0,) * nd)


def _inproj_body(x_ref, g_ref, wh_ref, wt_ref, o_ref, h_scr, *, n_head):
    j = pl.program_id(1)

    @pl.when(j == 0)
    def _():
        h_scr[...] = _rms(x_ref[...], g_ref[...]).astype(BF16)

    @pl.when(j < n_head)
    def _():
        o_ref[...] = _dot(h_scr[...], wh_ref[0]).astype(o_ref.dtype)

    @pl.when(j >= n_head)
    def _():
        o_ref[...] = _dot(h_scr[...], wt_ref[...]).astype(o_ref.dtype)


def _inproj(x, g, w_head, w_tail, tm, tn, n_head):
    t, d = x.shape
    n_tail = w_tail.shape[1] // tn
    return pl.pallas_call(
        functools.partial(_inproj_body, n_head=n_head),
        grid=(t // tm, n_head + n_tail),
        in_specs=[pl.BlockSpec((tm, d), lambda i, j: (i, 0)),
                  pl.BlockSpec((1, d), lambda i, j: (0, 0)),
                  pl.BlockSpec((1, d, tn), lambda i, j: (0, 0, jnp.minimum(j, n_head - 1))),
                  pl.BlockSpec((d, tn), lambda i, j: (0, jnp.maximum(j - n_head, 0)))],
        out_specs=pl.BlockSpec((tm, tn), lambda i, j: (i, j)),
        out_shape=jax.ShapeDtypeStruct((t, (n_head + n_tail) * tn), BF16),
        scratch_shapes=[pltpu.VMEM((tm, d), BF16)],
        compiler_params=_cparams(("parallel", "arbitrary")),
        name="inproj",
    )(x, g, w_head, w_tail)


def _lru_gate_matmuls(u, wax_ref):
    bw = u.shape[1] // LRU_BLOCKS
    r_parts, i_parts = [], []
    for n in range(LRU_BLOCKS):
        ri = _dot(u[:, n * bw:(n + 1) * bw].astype(BF16), wax_ref[n])
        r_parts.append(ri[:, :bw])
        i_parts.append(ri[:, bw:])
    return jnp.concatenate(r_parts, axis=1), jnp.concatenate(i_parts, axis=1)


def _lru_gate_values(r_pre, i_pre, ba, bx, lam):
    r = jax.nn.sigmoid(r_pre + ba)
    i = jax.nn.sigmoid(i_pre + bx)
    log_a = LRU_C * r * jax.nn.log_sigmoid(lam)
    a = jnp.exp(log_a)
    m2 = -jnp.tanh(log_a) * (a * a + 1.0)
    mult = jnp.where(m2 > 0.0, m2 * lax.rsqrt(m2), 0.0)
    return a, i, mult


def _lru_gates(u, wax_ref, ba, bx, lam):
    return _lru_gate_values(*_lru_gate_matmuls(u, wax_ref), ba, bx, lam)


def _lru_conv(x, halo, seq_start, cw, cb):
    tt = x.shape[0]
    xpad = jnp.concatenate([jnp.where(seq_start, 0.0, halo), x], axis=0)
    base = SUBLANES - (CONV_W - 1)
    return cb + sum(xpad[base + k:base + k + tt] * cw[k:k + 1] for k in range(CONV_W))


def _lru_scan(u, r_pre, i_pre, gate, carry, seq_start, ba, bx, lam):
    tt, width = u.shape
    carry = jnp.where(seq_start, 0.0, carry)
    a, i, mult = _lru_gate_values(r_pre, i_pre, ba, bx, lam)
    first = jnp.logical_and(lax.broadcasted_iota(jnp.int32, a.shape, 0) == 0, seq_start)
    mult = jnp.where(first, 1.0, mult)
    a = jnp.where(first, 0.0, a)
    v = u * i * mult

    a = a.reshape(tt // SUBLANES, SUBLANES, width)
    v = v.reshape(tt // SUBLANES, SUBLANES, width)
    sub = lax.broadcasted_iota(jnp.int32, a.shape, 1)
    s = 1
    while s < SUBLANES:
        keep = sub >= s
        v = jnp.where(keep, a * pltpu.roll(v, s, axis=1) + v, v)
        a = jnp.where(keep, a * pltpu.roll(a, s, axis=1), a)
        s *= 2
    groups = []
    for g in range(tt // SUBLANES):
        hg = a[g] * carry + v[g]
        carry = hg[SUBLANES - 1:SUBLANES]
        groups.append(hg)
    h = jnp.concatenate(groups, axis=0)
    return h * jax.nn.gelu(gate), carry


def _inproj_lru_body(x_ref, g_ref, wh_ref, wt_ref, cw_ref, cb_ref, wax_ref, ba_ref, bx_ref, lam_ref,
                     o_ref, ya_ref, hlast_ref, bufout_ref, h_scr, lru_new, lru_cur, halo_scr, carry_scr,
                     *, n_head, n_j, n_tiles, tiles_per_seq, sub_rows):
    i = pl.program_id(0)
    j = pl.program_id(1)
    tm, tn = o_ref.shape
    w = ya_ref.shape[1]
    n_sub = tm // sub_rows
    prev_tile = jnp.maximum(i - 1, 0)

    @pl.when(jnp.logical_and(i == 0, j == 0))
    def _():
        lru_cur[...] = jnp.zeros_like(lru_cur)
        halo_scr[...] = jnp.zeros_like(halo_scr)
        carry_scr[...] = jnp.zeros_like(carry_scr)

    n_piece = LRU_PIECES
    piece_rows = sub_rows // n_piece
    mx_w = 2 * LANES
    col_cuts = [min(tn, mx_w * (q * (tn // mx_w) // n_piece)) for q in range(n_piece)] + [tn]

    def lru_piece(q, state):
        r0 = pl.multiple_of(j * sub_rows, sub_rows) + q * piece_rows
        seq_start = jnp.logical_and(jnp.logical_and(prev_tile % tiles_per_seq == 0, j == 0), q == 0)
        x = lru_cur[pl.ds(r0, piece_rows), 0:w].astype(F32)
        gate = lru_cur[pl.ds(r0, piece_rows), w:2 * w].astype(F32)
        halo, carry = state if state is not None else (halo_scr[...], carry_scr[...])
        u = _lru_conv(x, halo, seq_start, cw_ref[...], cb_ref[...])
        r_pre, i_pre = _lru_gate_matmuls(u, wax_ref)
        ya, carry = _lru_scan(u, r_pre, i_pre, gate, carry, seq_start, ba_ref[...], bx_ref[...], lam_ref[...])
        ya_ref[pl.ds(r0, piece_rows), :] = ya.astype(ya_ref.dtype)
        halo = x[piece_rows - SUBLANES:piece_rows]
        if q == n_piece - 1:
            halo_scr[...] = halo
            carry_scr[...] = carry

            @pl.when(jnp.logical_and(prev_tile % tiles_per_seq == tiles_per_seq - 1, j == n_sub - 1))
            def _():
                hlast_ref[0] = carry
                bufout_ref[0] = x[piece_rows - (CONV_W - 1):piece_rows]
        return halo, carry

    def project_piece(q, from_head, lru_lo):
        c0, c1 = col_cuts[q], col_cuts[q + 1]
        if c0 == c1:
            return
        w_cols = wh_ref[0, :, c0:c1] if from_head else wt_ref[:, c0:c1]
        o = _dot(h_scr[...], w_cols).astype(o_ref.dtype)
        o_ref[:, c0:c1] = o
        if lru_lo is not None:
            keep = min(lru_lo + c1, 2 * w) - (lru_lo + c0)
            if keep > 0:
                lru_new[:, lru_lo + c0:lru_lo + c0 + keep] = o[:, 0:keep]

    def column_steps(lo, hi, from_head, prologue=None, lru_lo=None):
        for a, b, with_lru in ((lo, min(hi, n_sub), True), (max(lo, n_sub), hi, False)):
            if a < b:
                @pl.when(jnp.logical_and(i < n_tiles, jnp.logical_and(j >= a, j < b)))
                def _(with_lru=with_lru):
                    if prologue is not None:
                        prologue()
                    state = None
                    for q in range(n_piece):
                        project_piece(q, from_head, lru_lo)
                        if with_lru:
                            state = lru_piece(q, state)
                    if b == n_j:
                        @pl.when(j == n_j - 1)
                        def _():
                            lru_cur[...] = lru_new[...]

    def normalise():
        h_scr[...] = _rms(x_ref[...], g_ref[...]).astype(BF16)

    column_steps(0, 1, True, prologue=normalise, lru_lo=0)
    column_steps(1, 2, True, lru_lo=tn)
    column_steps(2, n_head, True)
    column_steps(n_head, n_j, False)

    @pl.when(jnp.logical_and(i == n_tiles, j < n_sub))
    def _():
        state = None
        for q in range(n_piece):
            state = lru_piece(q, state)


def _inproj_lru(x, g, w_head, w_tail, lw, seq_len, tm, tn, n_head, sub_rows):
    t, d = x.shape
    w = lw["lru_lambda"].shape[1]
    n_tail = w_tail.shape[1] // tn
    n_tiles, n_j = t // tm, n_head + n_tail
    tiles_per_seq = seq_len // tm
    assert seq_len % tm == 0 and tm % sub_rows == 0 and tm // sub_rows < n_j and n_head >= 2
    assert w <= tn and 2 * w <= 2 * tn and 2 * w > tn
    cur_tile = lambda i: jnp.minimum(i, n_tiles - 1)
    lru_tile = lambda i: jnp.maximum(i - 1, 0)
    body = functools.partial(_inproj_lru_body, n_head=n_head, n_j=n_j, n_tiles=n_tiles,
                             tiles_per_seq=tiles_per_seq, sub_rows=sub_rows)
    out_col = lambda i, j: jnp.where(i < n_tiles, j, n_j - 1)
    w_col = lambda i, j: jnp.where(i < n_tiles, j, n_j - 1)
    return pl.pallas_call(
        body,
        grid=(n_tiles + 1, n_j),
        in_specs=[pl.BlockSpec((tm, d), lambda i, j: (cur_tile(i), 0)),
                  pl.BlockSpec((1, d), lambda i, j: (0, 0)),
                  pl.BlockSpec((1, d, tn), lambda i, j: (0, 0, jnp.minimum(w_col(i, j), n_head - 1))),
                  pl.BlockSpec((d, tn), lambda i, j: (0, jnp.maximum(w_col(i, j) - n_head, 0))),
                  _const_spec((CONV_W, w)), _const_spec((1, w)),
                  _const_spec(lw["wax"].shape), _const_spec((1, w)), _const_spec((1, w)), _const_spec((1, w))],
        out_specs=[pl.BlockSpec((tm, tn), lambda i, j: (cur_tile(i), out_col(i, j))),
                   pl.BlockSpec((tm, w), lambda i, j: (lru_tile(i), 0)),
                   pl.BlockSpec((1, 1, w), lambda i, j: (lru_tile(i) // tiles_per_seq, 0, 0)),
                   pl.BlockSpec((1, CONV_W - 1, w), lambda i, j: (lru_tile(i) // tiles_per_seq, 0, 0))],
        out_shape=[jax.ShapeDtypeStruct((t, n_j * tn), BF16),
                   jax.ShapeDtypeStruct((t, w), BF16),
                   jax.ShapeDtypeStruct((t // seq_len, 1, w), F32),
                   jax.ShapeDtypeStruct((t // seq_len, CONV_W - 1, w), F32)],
        scratch_shapes=[pltpu.VMEM((tm, d), BF16), pltpu.VMEM((tm, 2 * w), BF16), pltpu.VMEM((tm, 2 * w), BF16),
                        pltpu.VMEM((SUBLANES, w), F32), pltpu.VMEM((1, w), F32)],
        compiler_params=_cparams(("arbitrary", "arbitrary")),
        name="inproj_lru",
    )(x, g, w_head, w_tail, lw["lru_conv_w"], lw["lru_conv_b"], lw["wax"], lw["lru_ba"], lw["lru_bx"],
      lw["lru_lambda"])


def _lru_step_body(xin_ref, gate_ref, buf_ref, h0_ref, cw_ref, cb_ref, wax_ref, ba_ref, bx_ref, lam_ref,
                   ya_ref, hlast_ref, bufout_ref, *, steps):
    bsz = h0_ref.shape[0]
    n = steps * bsz
    x = xin_ref[...].astype(F32)
    xx = jnp.concatenate([buf_ref[...], x], axis=0)
    cw = cw_ref[...]
    u = cb_ref[...] + sum(xx[k * bsz:k * bsz + n] * cw[k:k + 1] for k in range(CONV_W))
    a, i, mult = _lru_gates(u, wax_ref, ba_ref[...], bx_ref[...], lam_ref[...])
    v = u * i * mult
    h = h0_ref[...]
    for t in range(steps):
        sl = slice(t * bsz, (t + 1) * bsz)
        h = a[sl] * h + v[sl]
        ya_ref[sl, :] = (h * jax.nn.gelu(gate_ref[sl, :].astype(F32))).astype(ya_ref.dtype)
    hlast_ref[...] = h
    bufout_ref[...] = xx[steps * bsz:(steps + CONV_W - 1) * bsz]


def _lru_step(xin_t, gate_t, buf_t, h0, lw, steps):
    bsz, w = h0.shape
    body = functools.partial(_lru_step_body, steps=steps)
    return pl.pallas_call(
        body,
        out_shape=[jax.ShapeDtypeStruct((steps * bsz, w), BF16),
                   jax.ShapeDtypeStruct((bsz, w), F32),
                   jax.ShapeDtypeStruct(((CONV_W - 1) * bsz, w), F32)],
        compiler_params=pltpu.CompilerParams(vmem_limit_bytes=VMEM_LIMIT),
        name="lru_step",
    )(xin_t, gate_t, buf_t, h0, lw["lru_conv_w"], lw["lru_conv_b"], lw["wax"],
      lw["lru_ba"], lw["lru_bx"], lw["lru_lambda"])


def _expand_heads(cols, e2):
    q = cols[0].shape[0]
    v = jnp.concatenate(cols, axis=0)
    hi = v.astype(BF16)
    lo = (v - hi.astype(F32)).astype(BF16)
    out = _dot(jnp.concatenate([hi, lo], axis=1), e2)
    return [out[i * q:(i + 1) * q] for i in range(len(cols))]


def _ssd_chunk(xc, dt, p, e2, s_get, s_set, t_col, t_row, n_seg=1, n_valid=None):
    q = xc.shape[0]
    rps = q // n_seg
    gn = SSD_GROUPS * SSD_STATE
    d_inner = xc.shape[1] - 2 * gn
    hpg = d_inner // SSD_HEADDIM // SSD_GROUPS
    gw = hpg * SSD_HEADDIM

    causal = t_col >= t_row
    if n_seg > 1:
        same = (lax.broadcasted_iota(jnp.int32, (q, 1), 0) // rps) == (lax.broadcasted_iota(jnp.int32, (1, q), 1) // rps)
        causal = jnp.logical_and(same, causal)
    if n_valid is not None:
        dt = jnp.where(lax.broadcasted_iota(jnp.int32, dt.shape, 0) % rps < n_valid, dt, 0.0)
    a = dt * p["A"]
    cum = _dot_f32(causal.astype(F32), a)
    cum_t = cum.T
    total = cum[q - 1:q, :] if n_seg == 1 else _dot_f32(same.astype(F32), a)
    dt_x, to_end_x, ecum_x = _expand_heads([dt, jnp.exp(total - cum), jnp.exp(cum)], e2)

    xs = xc[:, :d_inner]
    xdt = xs * dt_x
    xw = xdt * to_end_x
    packed = rps % (2 * SUBLANES) == 0
    xdt_m = xdt.astype(BF16) if packed else xdt
    if packed:
        xw = xw.astype(BF16)
    lane_head = lax.broadcasted_iota(jnp.int32, (1, gw), 1) // SSD_HEADDIM
    y_groups = []
    for g in range(SSD_GROUPS):
        sl = slice(g * gw, (g + 1) * gw)
        bg = xc[:, d_inner + g * SSD_STATE:d_inner + (g + 1) * SSD_STATE]
        cg = xc[:, d_inner + gn + g * SSD_STATE:d_inner + gn + (g + 1) * SSD_STATE]
        if packed:
            bg, cg = bg.astype(BF16), cg.astype(BF16)
        cb = _dot_nt(cg.astype(BF16), bg.astype(BF16))
        m_heads, x_heads = [], []
        for hh in range(hpg):
            h = g * hpg + hh
            decay = jnp.exp(jnp.where(causal, cum[:, h:h + 1] - cum_t[h:h + 1, :], -jnp.inf))
            m_heads.append((cb * decay).astype(BF16))
            x_heads.append(jnp.where(lane_head == hh, xdt_m[:, sl], 0.0))
        y_diag = _dot(jnp.concatenate(m_heads, axis=1), jnp.concatenate(x_heads, axis=0).astype(BF16))
        y_off = []
        for b in range(n_seg):
            rows = slice(b * rps, (b + 1) * rps)
            s_old = s_get(b, g)
            y_off.append(_dot_nt(cg[rows].astype(BF16), s_old.astype(BF16)))
            s_dec = [s_old[hh * SSD_HEADDIM:(hh + 1) * SSD_HEADDIM, :]
                     * jnp.exp(cum_t[g * hpg + hh:g * hpg + hh + 1, (b + 1) * rps - 1:(b + 1) * rps])
                     for hh in range(hpg)]
            s_set(b, g, jnp.concatenate(s_dec, axis=0) + _dot_tn(xw[rows, sl].astype(BF16), bg[rows].astype(BF16)))
        y_off = y_off[0] if n_seg == 1 else jnp.concatenate(y_off, axis=0)
        y_groups.append(y_diag + y_off * ecum_x[:, sl])
    return jnp.concatenate(y_groups, axis=1) + p["D"] * xs


def _ssd_gate_norm(y, z, p):
    gw = y.shape[1] // SSD_GROUPS
    zf = z.astype(F32)
    out = []
    for g in range(SSD_GROUPS):
        sl = slice(g * gw, (g + 1) * gw)
        v = y[:, sl] * (zf[:, sl] * jax.nn.sigmoid(zf[:, sl]))
        out.append(v * lax.rsqrt(jnp.mean(v * v, axis=-1, keepdims=True) + EPS) * p["norm_g"][:, sl])
    return jnp.concatenate(out, axis=1)


def _ssd_conv(xpad, q, cw, cb):
    base = SUBLANES - (CONV_W - 1)
    y = cb + sum(xpad[base + k:base + k + q] * cw[k:k + 1] for k in range(CONV_W))
    return y * jax.nn.sigmoid(y)


def _softplus(x):
    return jax.nn.softplus(x)


def _ssd_params(cw_ref, cb_ref, dtb_ref, a_ref, d_ref, ng_ref):
    return {"cw": cw_ref[...], "cb": cb_ref[...], "dt_bias": dtb_ref[...], "A": a_ref[...],
            "D": d_ref[...], "norm_g": ng_ref[...]}


def _ssd_prompt_body(xbc_ref, z_ref, dt_ref, cw_ref, cb_ref, dtb_ref, a_ref, d_ref, ng_ref, e2_ref,
                     yb_ref, sout_ref, bufout_ref, x_scr, dt_scr, y_scr, s_scr, *, q):
    c = pl.program_id(1)
    rows = xbc_ref.shape[1]
    half = q // 2
    n_xslab = x_scr.shape[0]
    base = SUBLANES - (CONV_W - 1)

    @pl.when(c == 0)
    def _():
        x_scr[:, 0:SUBLANES, :] = jnp.zeros((n_xslab, SUBLANES, LANES), F32)
        s_scr[...] = jnp.zeros_like(s_scr)

    @pl.when(c > 0)
    def _():
        x_scr[:, 0:SUBLANES, :] = x_scr[:, rows:rows + SUBLANES, :]

    p = _ssd_params(cw_ref, cb_ref, dtb_ref, a_ref, d_ref, ng_ref)
    x = xbc_ref[0].astype(F32)
    for j in range(n_xslab):
        x_scr[j, SUBLANES:SUBLANES + rows, :] = x[:, j * LANES:(j + 1) * LANES]
    dt_scr[...] = _softplus(dt_ref[0].astype(F32) + p["dt_bias"])

    def times(shape, axis):
        pos = lax.broadcasted_iota(jnp.int32, shape, axis)
        return jnp.where(pos < half, 2 * pos, 2 * (pos - half) + 1)

    def s_set(b, g, v):
        s_scr[g] = v

    for ch in range(rows // q):
        r0 = ch * q
        cols = []
        for j in range(n_xslab):
            ls = slice(j * LANES, (j + 1) * LANES)
            halves = []
            for par in range(2):
                acc = p["cb"][:, ls]
                for k in range(CONV_W):
                    acc = acc + x_scr[j, pl.ds(r0 + base + k + par, half, stride=2), :] * p["cw"][k:k + 1, ls]
                halves.append(acc)
            cols.append(jnp.concatenate(halves, axis=0))
        xc = jnp.concatenate(cols, axis=1)
        xc = xc * jax.nn.sigmoid(xc)
        dt = jnp.concatenate([dt_scr[pl.ds(r0 + par, half, stride=2), :] for par in range(2)], axis=0)
        y = _ssd_chunk(xc, dt, p, e2_ref[...], lambda b, g: s_scr[g], s_set, times((q, 1), 0), times((1, q), 1))
        for j in range(y_scr.shape[0]):
            for par in range(2):
                y_scr[j, pl.ds(par, half, stride=2), :] = y[par * half:(par + 1) * half, j * LANES:(j + 1) * LANES]
        y = jnp.concatenate([y_scr[j] for j in range(y_scr.shape[0])], axis=1)
        yb_ref[0, r0:r0 + q, :] = _ssd_gate_norm(y, z_ref[0, r0:r0 + q, :], p).astype(yb_ref.dtype)

    @pl.when(c == pl.num_programs(1) - 1)
    def _():
        sout_ref[0] = s_scr[...]
        bufout_ref[0] = x[rows - (CONV_W - 1):rows]


def _ssd_prompt(proj3, sp, col_xbc, col_z, col_dt):
    b, l, _ = proj3.shape
    cdim = sp["ssd_conv_w"].shape[1]
    d_inner = sp["ssd_norm_g"].shape[1]
    gw = d_inner // SSD_GROUPS
    q = SSD_CHUNK if l % SSD_CHUNK == 0 else l
    assert q % (2 * SUBLANES) == 0
    rows = q * SSD_CHUNKS_PER_STEP if l % (q * SSD_CHUNKS_PER_STEP) == 0 else q
    return pl.pallas_call(
        functools.partial(_ssd_prompt_body, q=q),
        grid=(b, l // rows),
        in_specs=[pl.BlockSpec((1, rows, cdim), lambda i, c: (i, c, col_xbc // cdim)),
                  pl.BlockSpec((1, rows, d_inner), lambda i, c: (i, c, col_z // d_inner)),
                  pl.BlockSpec((1, rows, LANES), lambda i, c: (i, c, col_dt // LANES)),
                  _const_spec((CONV_W, cdim)), _const_spec((1, cdim)), _const_spec((1, LANES)),
                  _const_spec((1, LANES)), _const_spec((1, d_inner)), _const_spec((1, d_inner)),
                  _const_spec((2 * LANES, d_inner))],
        out_specs=[pl.BlockSpec((1, rows, d_inner), lambda i, c: (i, c, 0)),
                   pl.BlockSpec((1, SSD_GROUPS, gw, SSD_STATE), lambda i, c: (i, 0, 0, 0)),
                   pl.BlockSpec((1, CONV_W - 1, cdim), lambda i, c: (i, 0, 0))],
        out_shape=[jax.ShapeDtypeStruct((b, l, d_inner), BF16),
                   jax.ShapeDtypeStruct((b, SSD_GROUPS, gw, SSD_STATE), F32),
                   jax.ShapeDtypeStruct((b, CONV_W - 1, cdim), F32)],
        scratch_shapes=[pltpu.VMEM((cdim // LANES, rows + SUBLANES, LANES), F32),
                        pltpu.VMEM((rows, LANES), F32),
                        pltpu.VMEM((d_inner // LANES, q, LANES), F32),
                        pltpu.VMEM((SSD_GROUPS, gw, SSD_STATE), F32)],
        compiler_params=_cparams(("parallel", "arbitrary")),
        name="ssd_prompt",
    )(proj3, proj3, proj3, sp["ssd_conv_w"], sp["ssd_conv_b"], sp["dt_bias"], sp["A"], sp["D"], sp["ssd_norm_g"],
      sp["head_expand"])


def _ssd_step_body(xbc_ref, z_ref, dt_ref, buf_ref, s_ref, cw_ref, cb_ref, dtb_ref, a_ref, d_ref, ng_ref, e2_ref,
                   yb_ref, sout_ref, bufout_ref, *, steps, nb):
    p = _ssd_params(cw_ref, cb_ref, dtb_ref, a_ref, d_ref, ng_ref)
    rps = SUBLANES
    q = nb * rps
    x_all = xbc_ref[...].astype(F32)
    z_all = z_ref[...].astype(F32)
    dt_all = _softplus(dt_ref[...].astype(F32) + p["dt_bias"])
    cdim = x_all.shape[1]

    def padded(v, j):
        return jnp.concatenate([v[j * steps:(j + 1) * steps], jnp.zeros((rps - steps, v.shape[1]), v.dtype)], axis=0)

    xcs = []
    for j in range(nb):
        xpad = jnp.concatenate([jnp.zeros((SUBLANES - (CONV_W - 1), cdim), F32), buf_ref[j], padded(x_all, j)],
                               axis=0)
        xcs.append(_ssd_conv(xpad, rps, p["cw"], p["cb"]))
        bufout_ref[j] = xpad[SUBLANES + steps - (CONV_W - 1):SUBLANES + steps]
    xc = jnp.concatenate(xcs, axis=0)
    dt = jnp.concatenate([padded(dt_all, j) for j in range(nb)], axis=0)
    z = jnp.concatenate([padded(z_all, j) for j in range(nb)], axis=0)

    def s_set(b, g, v):
        sout_ref[b, g] = v

    t_col = lax.broadcasted_iota(jnp.int32, (q, 1), 0) % rps
    t_row = lax.broadcasted_iota(jnp.int32, (1, q), 1) % rps
    y = _ssd_chunk(xc, dt, p, e2_ref[...], lambda b, g: s_ref[b, g], s_set, t_col, t_row, n_seg=nb, n_valid=steps)
    y = _ssd_gate_norm(y, z, p)
    for j in range(nb):
        yb_ref[j * steps:(j + 1) * steps, :] = y[j * rps:j * rps + steps].astype(yb_ref.dtype)


def _ssd_step(proj, buf, s0, sp, steps, nb, col_xbc, col_z, col_dt):
    bsz = s0.shape[0]
    cdim = sp["ssd_conv_w"].shape[1]
    d_inner = sp["ssd_norm_g"].shape[1]
    gw = d_inner // SSD_GROUPS
    rows = nb * steps
    body = functools.partial(_ssd_step_body, steps=steps, nb=nb)
    return pl.pallas_call(
        body,
        grid=(bsz // nb,),
        in_specs=[pl.BlockSpec((rows, cdim), lambda i: (i, col_xbc // cdim)),
                  pl.BlockSpec((rows, d_inner), lambda i: (i, col_z // d_inner)),
                  pl.BlockSpec((rows, LANES), lambda i: (i, col_dt // LANES)),
                  pl.BlockSpec((None, nb, CONV_W - 1, cdim), lambda i: (0, i, 0, 0)),
                  pl.BlockSpec((nb, SSD_GROUPS, gw, SSD_STATE), lambda i: (i, 0, 0, 0)),
                  _const_spec((CONV_W, cdim)), _const_spec((1, cdim)), _const_spec((1, LANES)),
                  _const_spec((1, LANES)), _const_spec((1, d_inner)), _const_spec((1, d_inner)),
                  _const_spec((2 * LANES, d_inner))],
        out_specs=[pl.BlockSpec((rows, d_inner), lambda i: (i, 0)),
                   pl.BlockSpec((nb, SSD_GROUPS, gw, SSD_STATE), lambda i: (i, 0, 0, 0)),
                   pl.BlockSpec((None, nb, CONV_W - 1, cdim), lambda i: (0, i, 0, 0))],
        out_shape=[jax.ShapeDtypeStruct((bsz * steps, d_inner), BF16),
                   jax.ShapeDtypeStruct(s0.shape, F32),
                   jax.ShapeDtypeStruct(buf.shape, F32)],
        compiler_params=_cparams(("parallel",)),
        name="ssd_step",
    )(proj, proj, proj, buf, s0, sp["ssd_conv_w"], sp["ssd_conv_b"], sp["dt_bias"], sp["A"], sp["D"],
      sp["ssd_norm_g"], sp["head_expand"])


def _router(t, wr, br):
    t_hi = t.astype(BF16)
    t_lo = (t - t_hi.astype(F32)).astype(BF16)
    both = _dot(jnp.concatenate([t_hi, t_lo], axis=1), wr)
    logits = both[:, :LANES] + both[:, LANES:] + br
    lane = lax.broadcasted_iota(jnp.int32, logits.shape, 1)
    neg = -jnp.inf
    gl = jnp.where(lane < N_EGROUPS, logits, neg)
    gmax = jnp.max(gl, axis=-1, keepdims=True)
    g_idx = jnp.min(jnp.where(gl == gmax, lane, LANES), axis=-1, keepdims=True)
    g_w = 1.0 / jnp.sum(jnp.exp(gl - gmax), axis=-1, keepdims=True)
    in_grp = jnp.logical_and(jnp.logical_and(lane >= ROUTER_LANE0, lane < ROUTER_LANE0 + N_EXPERTS),
                             ((lane - ROUTER_LANE0) >> 2) == g_idx)
    el = jnp.where(in_grp, logits, neg)
    pe = jnp.exp(el - jnp.max(el, axis=-1, keepdims=True))
    pe = pe / jnp.sum(pe, axis=-1, keepdims=True)
    cand = jnp.where(in_grp, pe, -1.0)
    v1 = jnp.max(cand, axis=-1, keepdims=True)
    i1 = jnp.min(jnp.where(cand == v1, lane, LANES), axis=-1, keepdims=True)
    cand2 = jnp.where(lane == i1, -1.0, cand)
    v2 = jnp.max(cand2, axis=-1, keepdims=True)
    i2 = jnp.min(jnp.where(jnp.logical_and(cand2 == v2, in_grp), lane, LANES), axis=-1, keepdims=True)
    den = v1 + v2
    return lane, i1, i2, g_w * v1 / den, g_w * v2 / den


def _rows_to_tiles(ref, val):
    n, d = val.shape
    for k in range(d // LANES):
        ref[pl.ds(k, n, stride=d // LANES), :] = val[:, k * LANES:(k + 1) * LANES]


def _tiles_to_rows(ref, n, d, start=0):
    return jnp.concatenate([ref[pl.ds(start + k, n, stride=d // LANES), :] for k in range(d // LANES)], axis=1)


def _merge_body(x_ref, ya_ref, yb_ref, ga_ref, gb_ref, wl_ref, ws_ref, wo_ref, gf_ref, wr_ref, br_ref, cnt0_ref,
                x1_ref, t_ref, rt_ref, rtt_ref, cnt_ref, base_scr):
    step = pl.program_id(0)

    @pl.when(step == 0)
    def _():
        base_scr[...] = cnt0_ref[...]

    a = _dot(ya_ref[...], wl_ref[...])
    b = _dot(yb_ref[...], ws_ref[...])
    merged = jax.nn.sigmoid(ga_ref[...].astype(F32)) * a + jax.nn.sigmoid(gb_ref[...].astype(F32)) * b
    x1 = x_ref[...] + _dot(merged.astype(BF16), wo_ref[...])
    x1_ref[...] = x1
    t = _rms(x1, gf_ref[...])
    _rows_to_tiles(t_ref, t)
    lane, i1, i2, wg1, wg2 = _router(t, wr_ref[...], br_ref[...])

    tm = t.shape[0]
    onehot = jnp.where(jnp.logical_or(lane == i1, lane == i2), 1.0, 0.0).astype(BF16)
    tri = (lax.broadcasted_iota(jnp.int32, (tm, tm), 1) <= lax.broadcasted_iota(jnp.int32, (tm, tm), 0)).astype(BF16)
    cum = _dot(tri, onehot) + base_scr[...]
    r1 = jnp.sum(jnp.where(lane == i1, cum, 0.0), axis=-1, keepdims=True) - 1.0
    r2 = jnp.sum(jnp.where(lane == i2, cum, 0.0), axis=-1, keepdims=True) - 1.0
    cols = (wg1, wg2, r1, r2, (i1 - ROUTER_LANE0).astype(F32), (i2 - ROUTER_LANE0).astype(F32))
    rt = jnp.zeros(cum.shape, F32)
    for k, c in enumerate(cols):
        rt = jnp.where(lane == k, c, rt)
    rt_ref[...] = rt
    rtt_ref[...] = rt.T[0:SUBLANES, :]
    base_scr[...] = cum[tm - 1:tm, :]
    cnt_ref[...] = cum[tm - 1:tm, :]


def _merge(x, ya, yb, proj, mw, tm, col_ga, col_gb, counts_so_far):
    t, d = x.shape
    d_inner = yb.shape[1]
    return pl.pallas_call(
        _merge_body,
        grid=(t // tm,),
        in_specs=[pl.BlockSpec((tm, d), lambda i: (i, 0)),
                  pl.BlockSpec((tm, d), lambda i: (i, 0)),
                  pl.BlockSpec((tm, d_inner), lambda i: (i, 0)),
                  pl.BlockSpec((tm, d), lambda i: (i, col_ga // d)),
                  pl.BlockSpec((tm, d), lambda i: (i, col_gb // d)),
                  _const_spec((d, d)), _const_spec((d_inner, d)), _const_spec((d, d)),
                  _const_spec((1, d)), _const_spec((2 * d, 2 * LANES)), _const_spec((1, LANES)),
                  _const_spec((1, LANES))],
        out_specs=[pl.BlockSpec((tm, d), lambda i: (i, 0)),
                   pl.BlockSpec((tm * d // LANES, LANES), lambda i: (i, 0)),
                   pl.BlockSpec((tm, LANES), lambda i: (i, 0)),
                   pl.BlockSpec((SUBLANES, tm), lambda i: (0, i)),
                   pl.BlockSpec((1, LANES), lambda i: (0, 0))],
        out_shape=[jax.ShapeDtypeStruct((t, d), F32),
                   jax.ShapeDtypeStruct((t * d // LANES, LANES), F32),
                   jax.ShapeDtypeStruct((t, LANES), F32),
                   jax.ShapeDtypeStruct((SUBLANES, t), F32),
                   jax.ShapeDtypeStruct((1, LANES), F32)],
        scratch_shapes=[pltpu.VMEM((1, LANES), F32)],
        compiler_params=_cparams(("arbitrary",)),
        name="merge_router",
    )(x, ya, yb, proj, proj, mw["w_br_lru"], mw["w_br_ssd"], mw["w_out"], mw["g_ffn"], mw["w_router"],
      mw["b_router"], counts_so_far)


def _dispatch_body(zb_ref, *rest, n_groups, tms, n_toks, nk, tmg):
    dest_refs, t_refs = rest[:n_groups], rest[n_groups:2 * n_groups]
    o_ref, zero_scr, sem, zsem = rest[2 * n_groups:]
    step = pl.program_id(0)

    @pl.when(step == 0)
    def _():
        zero_scr[...] = jnp.zeros_like(zero_scr)
        blk = tmg * nk

        def zcopy(j):
            return pltpu.make_async_copy(zero_scr, o_ref.at[pl.ds(pl.multiple_of(zb_ref[j] * blk, blk), blk)], zsem)

        for j in range(zb_ref.shape[0]):
            pl.when(zb_ref[j] >= 0)(lambda j=j: zcopy(j).start())
        for j in range(zb_ref.shape[0]):
            pl.when(zb_ref[j] >= 0)(lambda j=j: zcopy(j).wait())

    first_step = 0
    for g in range(n_groups):
        tm, n_tok, steps = tms[g], n_toks[g], n_toks[g] // tms[g]

        @pl.when(jnp.logical_and(step >= first_step, step < first_step + steps))
        def _(g=g, tm=tm, n_tok=n_tok, first_step=first_step):
            t_ref, dest_ref = t_refs[g], dest_refs[g]

            def issue(r, carry):
                src = t_ref.at[pl.ds(pl.multiple_of(r * nk, nk), nk)]
                for k in range(2):
                    row = dest_ref[k * n_tok + (step - first_step) * tm + r]
                    pltpu.make_async_copy(src, o_ref.at[pl.ds(pl.multiple_of(row * nk, nk), nk)],
                                          sem).start(priority=k)
                return carry

            lax.fori_loop(0, tm, issue, 0, unroll=8)
            for k in range(2):
                pltpu.make_async_copy(t_ref, o_ref.at[pl.ds(0, tm * nk)], sem).wait()

        first_step += steps


def _dispatch(dests, zero_blocks, t_tiles, n_toks, tms, n_rows, tmg):
    n_groups = len(dests)
    nk = t_tiles[0].shape[0] // n_toks[0]
    steps = [n // tm for n, tm in zip(n_toks, tms)]
    starts = [sum(steps[:g]) for g in range(n_groups)]
    block_of = lambda g: (lambda i, *_: (jnp.clip(i - starts[g], 0, steps[g] - 1), 0))
    return pl.pallas_call(
        functools.partial(_dispatch_body, n_groups=n_groups, tms=tuple(tms), n_toks=tuple(n_toks), nk=nk, tmg=tmg),
        grid_spec=pltpu.PrefetchScalarGridSpec(
            num_scalar_prefetch=1 + n_groups,
            grid=(sum(steps),),
            in_specs=[pl.BlockSpec((tms[g] * nk, LANES), block_of(g)) for g in range(n_groups)],
            out_specs=pl.BlockSpec(memory_space=pl.ANY),
            scratch_shapes=[pltpu.VMEM((tmg * nk, LANES), F32), pltpu.SemaphoreType.DMA(()),
                            pltpu.SemaphoreType.DMA(())]),
        out_shape=jax.ShapeDtypeStruct((n_rows * nk, LANES), F32),
        compiler_params=_cparams(("arbitrary",)),
        name="moe_dispatch",
    )(zero_blocks, *dests, *t_tiles)


def _expert_body(te_ref, nt_ref, x_ref, w1_ref, w3_ref, w2_ref, y_ref, w1_scr, w3_scr, w2_scr, *, tmg):
    i = pl.program_id(0)
    real = i < nt_ref[0]
    d = w1_scr.shape[0]

    @pl.when(jnp.logical_or(i == 0, te_ref[i] != te_ref[jnp.maximum(i - 1, 0)]))
    def _():
        w1_scr[...] = w1_ref[0].astype(BF16)
        w3_scr[...] = w3_ref[0].astype(BF16)
        w2_scr[...] = w2_ref[0].astype(BF16)

    @pl.when(real)
    def _():
        x = _tiles_to_rows(x_ref, tmg, d).astype(BF16)
        h1 = _dot(x, w1_scr[...])
        h3 = _dot(x, w3_scr[...])
        _rows_to_tiles(y_ref, _dot((h1 * jax.nn.sigmoid(h1) * h3).astype(BF16), w2_scr[...]))

    @pl.when(jnp.logical_not(real))
    def _():
        y_ref[...] = jnp.zeros_like(y_ref)


def _experts(tile_expert, n_tiles, xs_tiles, w1, w3, w2, tmg):
    _, d, dff = w1.shape
    blk = tmg * d // LANES
    row_spec = pl.BlockSpec((blk, LANES), lambda i, te, nt: (i, 0))
    return pl.pallas_call(
        functools.partial(_expert_body, tmg=tmg),
        grid_spec=pltpu.PrefetchScalarGridSpec(
            num_scalar_prefetch=2,
            grid=(xs_tiles.shape[0] // blk,),
            in_specs=[row_spec,
                      pl.BlockSpec((1, d, dff), lambda i, te, nt: (te[i], 0, 0)),
                      pl.BlockSpec((1, d, dff), lambda i, te, nt: (te[i], 0, 0)),
                      pl.BlockSpec((1, dff, d), lambda i, te, nt: (te[i], 0, 0))],
            out_specs=row_spec,
            scratch_shapes=[pltpu.VMEM((d, dff), BF16), pltpu.VMEM((d, dff), BF16), pltpu.VMEM((dff, d), BF16)]),
        out_shape=jax.ShapeDtypeStruct(xs_tiles.shape, F32),
        compiler_params=_cparams(("arbitrary",)),
        name="moe_experts",
    )(tile_expert, n_tiles, xs_tiles, w1, w3, w2)


def _ple_body(dest_ref, x_ref, rt_ref, p_ref, wp_ref, gp_ref, gg_ref, wg_ref, gfin_ref, y_hbm, o_ref, gbuf, sem,
              *, n_tok):
    step = pl.program_id(0)
    n_steps = pl.num_programs(0)
    tm, d = x_ref.shape
    nk = d // LANES
    pr = tm // COMBINE_PIECES

    def issue_row(tile, slot, r):
        tok = tile * tm + r
        for k in range(2):
            row = dest_ref[k * n_tok + tok]
            pltpu.make_async_copy(y_hbm.at[pl.ds(pl.multiple_of(row * nk, nk), nk)],
                                  gbuf.at[slot, pl.ds(pl.multiple_of((k * tm + r) * nk, nk), nk)],
                                  sem.at[slot]).start(priority=k)

    def wait_slot(slot):
        pltpu.make_async_copy(y_hbm.at[pl.ds(0, 2 * tm * nk)], gbuf.at[slot], sem.at[slot]).wait()

    n_slots = gbuf.shape[0]
    last_tile = n_steps - 1

    @pl.when(step == 0)
    def _():
        for first in range(n_slots - 1):
            def issue(r, carry, first=first):
                issue_row(jnp.minimum(first, last_tile), first, r)
                return carry

            lax.fori_loop(0, tm, issue, 0, unroll=8)

    slot = step % n_slots
    wait_slot(slot)
    nxt_tile = jnp.minimum(step + n_slots - 1, last_tile)
    nxt_slot = (step + n_slots - 1) % n_slots
    rows = gbuf.at[slot]
    for q in range(COMBINE_PIECES):
        sl = slice(q * pr, (q + 1) * pr)
        rt = rt_ref[sl, :]
        x = (x_ref[sl, :] + rt[:, 0:1] * _tiles_to_rows(rows, pr, d, start=q * pr * nk)
             + rt[:, 1:2] * _tiles_to_rows(rows, pr, d, start=(tm + q * pr) * nk))
        e = _rms(_dot(p_ref[sl, :].astype(BF16), wp_ref[...]), gp_ref[...])
        gate = jax.nn.sigmoid(_dot(_rms(x, gg_ref[...]).astype(BF16), wg_ref[...]))
        o_ref[sl, :] = _rms(x + gate * e, gfin_ref[...])
        for r in range(q * pr, (q + 1) * pr):
            issue_row(nxt_tile, nxt_slot, r)

    @pl.when(step == last_tile)
    def _():
        for ahead in range(1, n_slots):
            wait_slot((step + ahead) % n_slots)


def _ple(dest, x1, rt, p, y_sorted, pw, tm):
    n, d = x1.shape
    dp = p.shape[1]
    const = lambda shape: pl.BlockSpec(shape, lambda i, *_: (0,) * len(shape))
    return pl.pallas_call(
        functools.partial(_ple_body, n_tok=n),
        grid_spec=pltpu.PrefetchScalarGridSpec(
            num_scalar_prefetch=1,
            grid=(n // tm,),
            in_specs=[pl.BlockSpec((tm, d), lambda i, *_: (i, 0)),
                      pl.BlockSpec((tm, LANES), lambda i, *_: (i, 0)),
                      pl.BlockSpec((tm, dp), lambda i, *_: (i, 0)),
                      const((dp, d)), const((1, d)), const((1, d)), const((d, d)), const((1, d)),
                      pl.BlockSpec(memory_space=pl.ANY)],
            out_specs=pl.BlockSpec((tm, d), lambda i, *_: (i, 0)),
            scratch_shapes=[pltpu.VMEM((COMBINE_SLOTS, 2 * tm * d // LANES, LANES), F32),
                            pltpu.SemaphoreType.DMA((COMBINE_SLOTS,))]),
        out_shape=jax.ShapeDtypeStruct((n, d), F32),
        compiler_params=_cparams(("arbitrary",)),
        name="combine_ple_final",
    )(dest, x1, rt, p, pw["w_ple_proj"], pw["g_ple"], pw["g_ple_gate"], pw["w_ple_gate"], pw["g_final"], y_sorted)


def _pick_tile(n, pref):
    t = min(n, pref)
    while n % t:
        t //= 2
    return t


def _moe_row_tile(n):
    return 512 if 2 * n // N_EXPERTS >= 1024 else 128
MOE_DISPATCH_TILE = 4096
COMBINE_TILE = 512
COMBINE_SLOTS = 3
SSD_CHUNKS_PER_STEP = 4
COMBINE_PIECES = 1
LRU_SLICE_ROWS = 128
LRU_PIECES = 2


def _split_router(w):
    hi = w.astype(BF16)
    lo = (w - hi.astype(F32)).astype(BF16)
    return jnp.concatenate([jnp.concatenate([hi, lo], axis=1),
                            jnp.concatenate([hi, jnp.zeros_like(hi)], axis=1)], axis=0)


def _token_tail(groups, cnt, lw):
    n_all = sum(g["x1"].shape[0] for g in groups)
    tmg = _moe_row_tile(n_all)
    n_blocks = pl.cdiv(2 * n_all, tmg) + N_EXPERTS
    counts = cnt[0, ROUTER_LANE0:ROUTER_LANE0 + N_EXPERTS].astype(jnp.int32)
    tiles = (counts + tmg - 1) // tmg
    ends = jnp.cumsum(tiles)
    n_tiles = ends[-1]
    first_row = (ends - tiles) * tmg
    blk = jnp.arange(n_blocks, dtype=jnp.int32)
    tile_expert = jnp.sum((jnp.minimum(blk, n_tiles - 1)[:, None] >= ends[None, :]).astype(jnp.int32), axis=1)
    tail = n_tiles + blk[:N_EXPERTS]
    zero_blocks = jnp.concatenate([jnp.where(tiles > 0, ends - 1, -1),
                                   jnp.where(tail < n_blocks, tail, -1)]).astype(jnp.int32)

    dests = []
    for g in groups:
        e_idx = g["rtt"][4:6].astype(jnp.int32)
        expert = jnp.arange(N_EXPERTS, dtype=jnp.int32)[:, None, None]
        dest = g["rtt"][2:4].astype(jnp.int32) + jnp.sum(
            jnp.where(e_idx[None] == expert, first_row[:, None, None], 0), axis=0)
        dests.append(dest.reshape(2 * g["x1"].shape[0]))
    n_toks = [g["x1"].shape[0] for g in groups]
    sorted_t = _dispatch(dests, zero_blocks, [g["t_tiles"] for g in groups], n_toks,
                         [_pick_tile(n, MOE_DISPATCH_TILE) for n in n_toks], n_blocks * tmg, tmg)
    y_sorted = _experts(tile_expert, n_tiles.reshape(1), sorted_t, lw["w1"], lw["w3"], lw["w2"], tmg)
    return [_ple(dest, g["x1"], g["rt"], g["p"], y_sorted, lw, g["tm"]) for g, dest in zip(groups, dests)]


def kernel(x_prompt, x_sample, state_lru_h, state_lru_conv, state_ssd, state_ssd_conv, p_prompt, p_sample, g_mix, w_in, lru_conv_w, lru_conv_b, lru_wa, lru_ba, lru_wx, lru_bx, lru_lambda, ssd_conv_w, ssd_conv_b, ssd_dt_bias, ssd_A_log, ssd_D, ssd_norm_g, w_br_lru, w_br_ssd, w_out, g_ffn, w_router_g, b_router_g, w_router_e, b_router_e, w1, w3, w2, w_ple_proj, g_ple, g_ple_gate, w_ple_gate, g_final):
    depth = w_in.shape[0]
    assert depth == 1, "one decoder layer per call"
    bp, lp, d = x_prompt.shape
    bs, ls, _ = x_sample.shape
    w_lru = state_lru_h.shape[-1]
    heads, hdim, nstate = state_ssd.shape[2:]
    d_inner = heads * hdim
    cdim = state_ssd_conv.shape[-1]
    assert hdim == SSD_HEADDIM and nstate == SSD_STATE and heads <= LANES and ls < SUBLANES
    gw = d_inner // SSD_GROUPS

    o_dt = 2 * w_lru + d_inner + cdim
    n_proj = o_dt + 2 * d + LANES
    tn = n_proj // 9 if n_proj % (9 * LANES) == 0 else LANES
    n_head = o_dt // tn
    w_head = w_in.astype(BF16)
    wi = w_head[0]
    w_tail = jnp.concatenate([wi[:, n_head * tn:o_dt], wi[:, o_dt + heads:], wi[:, o_dt:o_dt + heads],
                              jnp.zeros((d, LANES - heads), BF16)], axis=1)
    assert n_head >= 1 and w_tail.shape[1] == n_proj - n_head * tn and w_tail.shape[1] % tn == 0
    col_z, col_xbc = 2 * w_lru, 2 * w_lru + d_inner
    col_ga, col_gb, col_dt = o_dt, o_dt + d, o_dt + 2 * d
    row = lambda v: v.reshape(1, -1).astype(F32)
    pad_heads = lambda v: jnp.pad(v.astype(F32), (0, LANES - heads)).reshape(1, LANES)
    lw = {
        "lru_conv_w": lru_conv_w[0], "lru_conv_b": row(lru_conv_b[0]),
        "wax": jnp.concatenate([lru_wa[0], lru_wx[0]], axis=-1).astype(BF16),
        "lru_ba": row(lru_ba[0]), "lru_bx": row(lru_bx[0]), "lru_lambda": row(lru_lambda[0]),
        "ssd_conv_w": ssd_conv_w[0], "ssd_conv_b": row(ssd_conv_b[0]),
        "dt_bias": pad_heads(ssd_dt_bias[0]), "A": pad_heads(-jnp.exp(ssd_A_log[0].astype(F32))),
        "D": row(jnp.repeat(ssd_D[0], hdim)), "ssd_norm_g": row(ssd_norm_g[0]),
        "head_expand": jnp.tile(jnp.arange(LANES)[:, None] == jnp.arange(d_inner)[None, :] // hdim, (2, 1)).astype(BF16),
        "w_br_lru": w_br_lru[0].astype(BF16), "w_br_ssd": w_br_ssd[0].astype(BF16), "w_out": w_out[0].astype(BF16),
        "g_ffn": row(g_ffn[0]),
        "w_router": _split_router(jnp.concatenate([w_router_g[0], w_router_e[0],
                                                   jnp.zeros((d, LANES - N_EGROUPS - N_EXPERTS), F32)], axis=1)),
        "b_router": jnp.concatenate([b_router_g[0], b_router_e[0],
                                     jnp.zeros((LANES - N_EGROUPS - N_EXPERTS,), F32)]).reshape(1, LANES),
        "w1": w1[0], "w3": w3[0], "w2": w2[0],
        "w_ple_proj": w_ple_proj[0].astype(BF16), "g_ple": row(g_ple[0]), "g_ple_gate": row(g_ple_gate[0]),
        "w_ple_gate": w_ple_gate[0].astype(BF16), "g_final": row(g_final),
    }
    g_mix_r = row(g_mix[0])

    tp = bp * lp
    xp = x_prompt.reshape(tp, d)
    tm_p = _pick_tile(lp, 1024)
    proj_p, ya_p, hl_p, lbuf_p = _inproj_lru(xp, g_mix_r, w_head, w_tail, lw, lp, tm_p, tn, n_head,
                                             _pick_tile(tm_p, LRU_SLICE_ROWS))
    proj_p3 = proj_p.reshape(bp, lp, n_proj)
    yb_p, s_p, sbuf_p = _ssd_prompt(proj_p3, lw, col_xbc, col_z, col_dt)
    tm_tail_p = _pick_tile(tp, 512)
    x1_p, t_p, rt_p, rtt_p, cnt_p = _merge(xp, ya_p.reshape(tp, w_lru), yb_p.reshape(tp, d_inner), proj_p, lw,
                                           tm_tail_p, col_ga, col_gb, jnp.zeros((1, LANES), F32))
    group_p = {"x1": x1_p, "t_tiles": t_p, "rt": rt_p, "rtt": rtt_p, "p": p_prompt[0].reshape(tp, -1),
               "tm": _pick_tile(tp, COMBINE_TILE)}

    ts = bs * ls
    xs = x_sample.reshape(ts, d)
    tm_s = _pick_tile(ts, 512)
    proj_s = _inproj(xs, g_mix_r, w_head, w_tail, tm_s, tn, n_head)
    to_tmajor = lambda v, n: v.reshape(bs, n, -1).transpose(1, 0, 2).reshape(n * bs, -1)
    from_tmajor = lambda v, n: v.reshape(n, bs, -1).transpose(1, 0, 2)
    ya_t, hl_s, lbuf_t = _lru_step(to_tmajor(proj_s[:, :w_lru], ls), to_tmajor(proj_s[:, w_lru:2 * w_lru], ls),
                                   to_tmajor(state_lru_conv[0], CONV_W - 1), state_lru_h[0], lw, ls)
    ya_s = from_tmajor(ya_t, ls).reshape(ts, w_lru)
    lbuf_s = from_tmajor(lbuf_t, CONV_W - 1)
    yb_s, s_s, sbuf_s = _ssd_step(proj_s, state_ssd_conv, state_ssd[0].reshape(bs, SSD_GROUPS, gw, nstate), lw,
                                  ls, _pick_tile(bs, 8), col_xbc, col_z, col_dt)
    x1_s, t_s, rt_s, rtt_s, cnt_all = _merge(xs, ya_s, yb_s, proj_s, lw, tm_s, col_ga, col_gb, cnt_p)
    group_s = {"x1": x1_s, "t_tiles": t_s, "rt": rt_s, "rtt": rtt_s, "p": p_sample[0].reshape(ts, -1), "tm": tm_s}

    y_p, y_s = _token_tail([group_p, group_s], cnt_all, lw)

    return (y_p.reshape(bp, lp, d), y_s.reshape(bs, ls, d),
            hl_p.reshape(1, bp, w_lru), lbuf_p[None],
            s_p.reshape(1, bp, heads, hdim, nstate), sbuf_p[None],
            hl_s[None], lbuf_s[None],
            s_s.reshape(1, bs, heads, hdim, nstate), sbuf_s)
```

```python
import functools

import jax
import jax.numpy as jnp
from jax import lax
from jax.experimental import pallas as pl
from jax.experimental.pallas import tpu as pltpu

F32 = jnp.float32
BF16 = jnp.bfloat16

EPS = 1e-6
CONV_W = 4
LRU_BLOCKS = 8
LRU_C = 8.0
SSD_HEADDIM = 64
SSD_GROUPS = 8
SSD_STATE = 128
SSD_CHUNK = 128
N_EGROUPS = 4
EXP_PER_GROUP = 4
N_EXPERTS = N_EGROUPS * EXP_PER_GROUP

LANES = 128
SUBLANES = 8
VMEM_LIMIT = 52 * 1024 * 1024
ROUTER_LANE0 = N_EGROUPS


def _cparams(sem):
    return pltpu.CompilerParams(dimension_semantics=sem, vmem_limit_bytes=VMEM_LIMIT)


def _dot(a, b):
    return jnp.dot(a, b, preferred_element_type=F32)


def _dot_nt(a, b):
    return lax.dot_general(a, b, (((1,), (1,)), ((), ())), preferred_element_type=F32)


def _dot_tn(a, b):
    return lax.dot_general(a, b, (((0,), (0,)), ((), ())), preferred_element_type=F32)


def _dot_f32(a, b):
    return jnp.dot(a, b, precision=lax.Precision.HIGHEST, preferred_element_type=F32)


def _rms(x, g):
    return x * lax.rsqrt(jnp.mean(x * x, axis=-1, keepdims=True) + EPS) * g


def _const_spec(shape):
    nd = len(shape)
    return pl.BlockSpec(shape, lambda *_: (0,) * nd)


def _inproj_body(x_ref, g_ref, wh_ref, wt_ref, o_ref, h_scr, *, n_head):
    j = pl.program_id(1)

    @pl.when(j == 0)
    def _():
        h_scr[...] = _rms(x_ref[...], g_ref[...]).astype(BF16)

    @pl.when(j < n_head)
    def _():
        o_ref[...] = _dot(h_scr[...], wh_ref[0]).astype(o_ref.dtype)

    @pl.when(j >= n_head)
    def _():
        o_ref[...] = _dot(h_scr[...], wt_ref[...]).astype(o_ref.dtype)


def _inproj(x, g, w_head, w_tail, tm, tn, n_head):
    t, d = x.shape
    n_tail = w_tail.shape[1] // tn
    return pl.pallas_call(
        functools.partial(_inproj_body, n_head=n_head),
        grid=(t // tm, n_head + n_tail),
        in_specs=[pl.BlockSpec((tm, d), lambda i, j: (i, 0)),
                  pl.BlockSpec((1, d), lambda i, j: (0, 0)),
                  pl.BlockSpec((1, d, tn), lambda i, j: (0, 0, jnp.minimum(j, n_head - 1))),
                  pl.BlockSpec((d, tn), lambda i, j: (0, jnp.maximum(j - n_head, 0)))],
        out_specs=pl.BlockSpec((tm, tn), lambda i, j: (i, j)),
        out_shape=jax.ShapeDtypeStruct((t, (n_head + n_tail) * tn), BF16),
        scratch_shapes=[pltpu.VMEM((tm, d), BF16)],
        compiler_params=_cparams(("parallel", "arbitrary")),
        name="inproj",
    )(x, g, w_head, w_tail)


def _lru_gate_matmuls(u, wax_ref):
    bw = u.shape[1] // LRU_BLOCKS
    r_parts, i_parts = [], []
    for n in range(LRU_BLOCKS):
        ri = _dot(u[:, n * bw:(n + 1) * bw].astype(BF16), wax_ref[n])
        r_parts.append(ri[:, :bw])
        i_parts.append(ri[:, bw:])
    return jnp.concatenate(r_parts, axis=1), jnp.concatenate(i_parts, axis=1)


def _lru_gate_values(r_pre, i_pre, ba, bx, lam):
    r = jax.nn.sigmoid(r_pre + ba)
    i = jax.nn.sigmoid(i_pre + bx)
    log_a = LRU_C * r * jax.nn.log_sigmoid(lam)
    a = jnp.exp(log_a)
    m2 = -jnp.tanh(log_a) * (a * a + 1.0)
    mult = jnp.where(m2 > 0.0, m2 * lax.rsqrt(m2), 0.0)
    return a, i, mult


def _lru_gates(u, wax_ref, ba, bx, lam):
    return _lru_gate_values(*_lru_gate_matmuls(u, wax_ref), ba, bx, lam)


def _lru_conv(x, halo, seq_start, cw, cb):
    tt = x.shape[0]
    xpad = jnp.concatenate([jnp.where(seq_start, 0.0, halo), x], axis=0)
    base = SUBLANES - (CONV_W - 1)
    return cb + sum(xpad[base + k:base + k + tt] * cw[k:k + 1] for k in range(CONV_W))


def _lru_scan(u, r_pre, i_pre, gate, carry, seq_start, ba, bx, lam):
    tt, width = u.shape
    carry = jnp.where(seq_start, 0.0, carry)
    a, i, mult = _lru_gate_values(r_pre, i_pre, ba, bx, lam)
    first = jnp.logical_and(lax.broadcasted_iota(jnp.int32, a.shape, 0) == 0, seq_start)
    mult = jnp.where(first, 1.0, mult)
    a = jnp.where(first, 0.0, a)
    v = u * i * mult

    a = a.reshape(tt // SUBLANES, SUBLANES, width)
    v = v.reshape(tt // SUBLANES, SUBLANES, width)
    sub = lax.broadcasted_iota(jnp.int32, a.shape, 1)
    s = 1
    while s < SUBLANES:
        keep = sub >= s
        v = jnp.where(keep, a * pltpu.roll(v, s, axis=1) + v, v)
        a = jnp.where(keep, a * pltpu.roll(a, s, axis=1), a)
        s *= 2
    groups = []
    for g in range(tt // SUBLANES):
        hg = a[g] * carry + v[g]
        carry = hg[SUBLANES - 1:SUBLANES]
        groups.append(hg)
    h = jnp.concatenate(groups, axis=0)
    return h * jax.nn.gelu(gate), carry


def _inproj_lru_body(x_ref, g_ref, wh_ref, wt_ref, cw_ref, cb_ref, wax_ref, ba_ref, bx_ref, lam_ref,
                     o_ref, ya_ref, hlast_ref, bufout_ref, h_scr, lru_new, lru_cur, halo_scr, carry_scr,
                     *, n_head, n_j, n_tiles, tiles_per_seq, sub_rows):
    i = pl.program_id(0)
    j = pl.program_id(1)
    tm, tn = o_ref.shape
    w = ya_ref.shape[1]
    n_sub = tm // sub_rows
    prev_tile = jnp.maximum(i - 1, 0)

    @pl.when(jnp.logical_and(i == 0, j == 0))
    def _():
        lru_cur[...] = jnp.zeros_like(lru_cur)
        halo_scr[...] = jnp.zeros_like(halo_scr)
        carry_scr[...] = jnp.zeros_like(carry_scr)

    n_piece = LRU_PIECES
    piece_rows = sub_rows // n_piece
    mx_w = 2 * LANES
    col_cuts = [min(tn, mx_w * (q * (tn // mx_w) // n_piece)) for q in range(n_piece)] + [tn]

    def lru_piece(q, state):
        r0 = pl.multiple_of(j * sub_rows, sub_rows) + q * piece_rows
        seq_start = jnp.logical_and(jnp.logical_and(prev_tile % tiles_per_seq == 0, j == 0), q == 0)
        x = lru_cur[pl.ds(r0, piece_rows), 0:w].astype(F32)
        gate = lru_cur[pl.ds(r0, piece_rows), w:2 * w].astype(F32)
        halo, carry = state if state is not None else (halo_scr[...], carry_scr[...])
        u = _lru_conv(x, halo, seq_start, cw_ref[...], cb_ref[...])
        r_pre, i_pre = _lru_gate_matmuls(u, wax_ref)
        ya, carry = _lru_scan(u, r_pre, i_pre, gate, carry, seq_start, ba_ref[...], bx_ref[...], lam_ref[...])
        ya_ref[pl.ds(r0, piece_rows), :] = ya.astype(ya_ref.dtype)
        halo = x[piece_rows - SUBLANES:piece_rows]
        if q == n_piece - 1:
            halo_scr[...] = halo
            carry_scr[...] = carry

            @pl.when(jnp.logical_and(prev_tile % tiles_per_seq == tiles_per_seq - 1, j == n_sub - 1))
            def _():
                hlast_ref[0] = carry
                bufout_ref[0] = x[piece_rows - (CONV_W - 1):piece_rows]
        return halo, carry

    def project_piece(q, from_head, lru_lo):
        c0, c1 = col_cuts[q], col_cuts[q + 1]
        if c0 == c1:
            return
        w_cols = wh_ref[0, :, c0:c1] if from_head else wt_ref[:, c0:c1]
        o = _dot(h_scr[...], w_cols).astype(o_ref.dtype)
        o_ref[:, c0:c1] = o
        if lru_lo is not None:
            keep = min(lru_lo + c1, 2 * w) - (lru_lo + c0)
            if keep > 0:
                lru_new[:, lru_lo + c0:lru_lo + c0 + keep] = o[:, 0:keep]

    def column_steps(lo, hi, from_head, prologue=None, lru_lo=None):
        for a, b, with_lru in ((lo, min(hi, n_sub), True), (max(lo, n_sub), hi, False)):
            if a < b:
                @pl.when(jnp.logical_and(i < n_tiles, jnp.logical_and(j >= a, j < b)))
                def _(with_lru=with_lru):
                    if prologue is not None:
                        prologue()
                    state = None
                    for q in range(n_piece):
                        project_piece(q, from_head, lru_lo)
                        if with_lru:
                            state = lru_piece(q, state)
                    if b == n_j:
                        @pl.when(j == n_j - 1)
                        def _():
                            lru_cur[...] = lru_new[...]

    def normalise():
        h_scr[...] = _rms(x_ref[...], g_ref[...]).astype(BF16)

    column_steps(0, 1, True, prologue=normalise, lru_lo=0)
    column_steps(1, 2, True, lru_lo=tn)
    column_steps(2, n_head, True)
    column_steps(n_head, n_j, False)

    @pl.when(jnp.logical_and(i == n_tiles, j < n_sub))
    def _():
        state = None
        for q in range(n_piece):
            state = lru_piece(q, state)


def _inproj_lru(x, g, w_head, w_tail, lw, seq_len, tm, tn, n_head, sub_rows):
    t, d = x.shape
    w = lw["lru_lambda"].shape[1]
    n_tail = w_tail.shape[1] // tn
    n_tiles, n_j = t // tm, n_head + n_tail
    tiles_per_seq = seq_len // tm
    assert seq_len % tm == 0 and tm % sub_rows == 0 and tm // sub_rows < n_j and n_head >= 2
    assert w <= tn and 2 * w <= 2 * tn and 2 * w > tn
    cur_tile = lambda i: jnp.minimum(i, n_tiles - 1)
    lru_tile = lambda i: jnp.maximum(i - 1, 0)
    body = functools.partial(_inproj_lru_body, n_head=n_head, n_j=n_j, n_tiles=n_tiles,
                             tiles_per_seq=tiles_per_seq, sub_rows=sub_rows)
    out_col = lambda i, j: jnp.where(i < n_tiles, j, n_j - 1)
    w_col = lambda i, j: jnp.where(i < n_tiles, j, n_j - 1)
    return pl.pallas_call(
        body,
        grid=(n_tiles + 1, n_j),
        in_specs=[pl.BlockSpec((tm, d), lambda i, j: (cur_tile(i), 0)),
                  pl.BlockSpec((1, d), lambda i, j: (0, 0)),
                  pl.BlockSpec((1, d, tn), lambda i, j: (0, 0, jnp.minimum(w_col(i, j), n_head - 1))),
                  pl.BlockSpec((d, tn), lambda i, j: (0, jnp.maximum(w_col(i, j) - n_head, 0))),
                  _const_spec((CONV_W, w)), _const_spec((1, w)),
                  _const_spec(lw["wax"].shape), _const_spec((1, w)), _const_spec((1, w)), _const_spec((1, w))],
        out_specs=[pl.BlockSpec((tm, tn), lambda i, j: (cur_tile(i), out_col(i, j))),
                   pl.BlockSpec((tm, w), lambda i, j: (lru_tile(i), 0)),
                   pl.BlockSpec((1, 1, w), lambda i, j: (lru_tile(i) // tiles_per_seq, 0, 0)),
                   pl.BlockSpec((1, CONV_W - 1, w), lambda i, j: (lru_tile(i) // tiles_per_seq, 0, 0))],
        out_shape=[jax.ShapeDtypeStruct((t, n_j * tn), BF16),
                   jax.ShapeDtypeStruct((t, w), BF16),
                   jax.ShapeDtypeStruct((t // seq_len, 1, w), F32),
                   jax.ShapeDtypeStruct((t // seq_len, CONV_W - 1, w), F32)],
        scratch_shapes=[pltpu.VMEM((tm, d), BF16), pltpu.VMEM((tm, 2 * w), BF16), pltpu.VMEM((tm, 2 * w), BF16),
                        pltpu.VMEM((SUBLANES, w), F32), pltpu.VMEM((1, w), F32)],
        compiler_params=_cparams(("arbitrary", "arbitrary")),
        name="inproj_lru",
    )(x, g, w_head, w_tail, lw["lru_conv_w"], lw["lru_conv_b"], lw["wax"], lw["lru_ba"], lw["lru_bx"],
      lw["lru_lambda"])


def _lru_step_body(xin_ref, gate_ref, buf_ref, h0_ref, cw_ref, cb_ref, wax_ref, ba_ref, bx_ref, lam_ref,
                   ya_ref, hlast_ref, bufout_ref, *, steps):
    bsz = h0_ref.shape[0]
    n = steps * bsz
    x = xin_ref[...].astype(F32)
    xx = jnp.concatenate([buf_ref[...], x], axis=0)
    cw = cw_ref[...]
    u = cb_ref[...] + sum(xx[k * bsz:k * bsz + n] * cw[k:k + 1] for k in range(CONV_W))
    a, i, mult = _lru_gates(u, wax_ref, ba_ref[...], bx_ref[...], lam_ref[...])
    v = u * i * mult
    h = h0_ref[...]
    for t in range(steps):
        sl = slice(t * bsz, (t + 1) * bsz)
        h = a[sl] * h + v[sl]
        ya_ref[sl, :] = (h * jax.nn.gelu(gate_ref[sl, :].astype(F32))).astype(ya_ref.dtype)
    hlast_ref[...] = h
    bufout_ref[...] = xx[steps * bsz:(steps + CONV_W - 1) * bsz]


def _lru_step(xin_t, gate_t, buf_t, h0, lw, steps):
    bsz, w = h0.shape
    body = functools.partial(_lru_step_body, steps=steps)
    return pl.pallas_call(
        body,
        out_shape=[jax.ShapeDtypeStruct((steps * bsz, w), BF16),
                   jax.ShapeDtypeStruct((bsz, w), F32),
                   jax.ShapeDtypeStruct(((CONV_W - 1) * bsz, w), F32)],
        compiler_params=pltpu.CompilerParams(vmem_limit_bytes=VMEM_LIMIT),
        name="lru_step",
    )(xin_t, gate_t, buf_t, h0, lw["lru_conv_w"], lw["lru_conv_b"], lw["wax"],
      lw["lru_ba"], lw["lru_bx"], lw["lru_lambda"])


def _expand_heads(cols, e2):
    q = cols[0].shape[0]
    v = jnp.concatenate(cols, axis=0)
    hi = v.astype(BF16)
    lo = (v - hi.astype(F32)).astype(BF16)
    out = _dot(jnp.concatenate([hi, lo], axis=1), e2)
    return [out[i * q:(i + 1) * q] for i in range(len(cols))]


def _ssd_chunk(xc, dt, p, e2, s_get, s_set, t_col, t_row, n_seg=1, n_valid=None):
    q = xc.shape[0]
    rps = q // n_seg
    gn = SSD_GROUPS * SSD_STATE
    d_inner = xc.shape[1] - 2 * gn
    hpg = d_inner // SSD_HEADDIM // SSD_GROUPS
    gw = hpg * SSD_HEADDIM

    causal = t_col >= t_row
    if n_seg > 1:
        same = (lax.broadcasted_iota(jnp.int32, (q, 1), 0) // rps) == (lax.broadcasted_iota(jnp.int32, (1, q), 1) // rps)
        causal = jnp.logical_and(same, causal)
    if n_valid is not None:
        dt = jnp.where(lax.broadcasted_iota(jnp.int32, dt.shape, 0) % rps < n_valid, dt, 0.0)
    a = dt * p["A"]
    cum = _dot_f32(causal.astype(F32), a)
    cum_t = cum.T
    total = cum[q - 1:q, :] if n_seg == 1 else _dot_f32(same.astype(F32), a)
    dt_x, to_end_x, ecum_x = _expand_heads([dt, jnp.exp(total - cum), jnp.exp(cum)], e2)

    xs = xc[:, :d_inner]
    xdt = xs * dt_x
    xw = xdt * to_end_x
    packed = rps % (2 * SUBLANES) == 0
    xdt_m = xdt.astype(BF16) if packed else xdt
    if packed:
        xw = xw.astype(BF16)
    lane_head = lax.broadcasted_iota(jnp.int32, (1, gw), 1) // SSD_HEADDIM
    y_groups = []
    for g in range(SSD_GROUPS):
        sl = slice(g * gw, (g + 1) * gw)
        bg = xc[:, d_inner + g * SSD_STATE:d_inner + (g + 1) * SSD_STATE]
        cg = xc[:, d_inner + gn + g * SSD_STATE:d_inner + gn + (g + 1) * SSD_STATE]
        if packed:
            bg, cg = bg.astype(BF16), cg.astype(BF16)
        cb = _dot_nt(cg.astype(BF16), bg.astype(BF16))
        m_heads, x_heads = [], []
        for hh in range(hpg):
            h = g * hpg + hh
            decay = jnp.exp(jnp.where(causal, cum[:, h:h + 1] - cum_t[h:h + 1, :], -jnp.inf))
            m_heads.append((cb * decay).astype(BF16))
            x_heads.append(jnp.where(lane_head == hh, xdt_m[:, sl], 0.0))
        y_diag = _dot(jnp.concatenate(m_heads, axis=1), jnp.concatenate(x_heads, axis=0).astype(BF16))
        y_off = []
        for b in range(n_seg):
            rows = slice(b * rps, (b + 1) * rps)
            s_old = s_get(b, g)
            y_off.append(_dot_nt(cg[rows].astype(BF16), s_old.astype(BF16)))
            s_dec = [s_old[hh * SSD_HEADDIM:(hh + 1) * SSD_HEADDIM, :]
                     * jnp.exp(cum_t[g * hpg + hh:g * hpg + hh + 1, (b + 1) * rps - 1:(b + 1) * rps])
                     for hh in range(hpg)]
            s_set(b, g, jnp.concatenate(s_dec, axis=0) + _dot_tn(xw[rows, sl].astype(BF16), bg[rows].astype(BF16)))
        y_off = y_off[0] if n_seg == 1 else jnp.concatenate(y_off, axis=0)
        y_groups.append(y_diag + y_off * ecum_x[:, sl])
    return jnp.concatenate(y_groups, axis=1) + p["D"] * xs


def _ssd_gate_norm(y, z, p):
    gw = y.shape[1] // SSD_GROUPS
    zf = z.astype(F32)
    out = []
    for g in range(SSD_GROUPS):
        sl = slice(g * gw, (g + 1) * gw)
        v = y[:, sl] * (zf[:, sl] * jax.nn.sigmoid(zf[:, sl]))
        out.append(v * lax.rsqrt(jnp.mean(v * v, axis=-1, keepdims=True) + EPS) * p["norm_g"][:, sl])
    return jnp.concatenate(out, axis=1)


def _ssd_conv(xpad, q, cw, cb):
    base = SUBLANES - (CONV_W - 1)
    y = cb + sum(xpad[base + k:base + k + q] * cw[k:k + 1] for k in range(CONV_W))
    return y * jax.nn.sigmoid(y)


def _softplus(x):
    return jax.nn.softplus(x)


def _ssd_params(cw_ref, cb_ref, dtb_ref, a_ref, d_ref, ng_ref):
    return {"cw": cw_ref[...], "cb": cb_ref[...], "dt_bias": dtb_ref[...], "A": a_ref[...],
            "D": d_ref[...], "norm_g": ng_ref[...]}


def _ssd_prompt_body(xbc_ref, z_ref, dt_ref, cw_ref, cb_ref, dtb_ref, a_ref, d_ref, ng_ref, e2_ref,
                     yb_ref, sout_ref, bufout_ref, x_scr, dt_scr, y_scr, s_scr, *, q):
    c = pl.program_id(1)
    rows = xbc_ref.shape[1]
    half = q // 2
    n_xslab = x_scr.shape[0]
    base = SUBLANES - (CONV_W - 1)

    @pl.when(c == 0)
    def _():
        x_scr[:, 0:SUBLANES, :] = jnp.zeros((n_xslab, SUBLANES, LANES), F32)
        s_scr[...] = jnp.zeros_like(s_scr)

    @pl.when(c > 0)
    def _():
        x_scr[:, 0:SUBLANES, :] = x_scr[:, rows:rows + SUBLANES, :]

    p = _ssd_params(cw_ref, cb_ref, dtb_ref, a_ref, d_ref, ng_ref)
    x = xbc_ref[0].astype(F32)
    for j in range(n_xslab):
        x_scr[j, SUBLANES:SUBLANES + rows, :] = x[:, j * LANES:(j + 1) * LANES]
    dt_scr[...] = _softplus(dt_ref[0].astype(F32) + p["dt_bias"])

    def times(shape, axis):
        pos = lax.broadcasted_iota(jnp.int32, shape, axis)
        return jnp.where(pos < half, 2 * pos, 2 * (pos - half) + 1)

    def s_set(b, g, v):
        s_scr[g] = v

    for ch in range(rows // q):
        r0 = ch * q
        cols = []
        for j in range(n_xslab):
            ls = slice(j * LANES, (j + 1) * LANES)
            halves = []
            for par in range(2):
                acc = p["cb"][:, ls]
                for k in range(CONV_W):
                    acc = acc + x_scr[j, pl.ds(r0 + base + k + par, half, stride=2), :] * p["cw"][k:k + 1, ls]
                halves.append(acc)
            cols.append(jnp.concatenate(halves, axis=0))
        xc = jnp.concatenate(cols, axis=1)
        xc = xc * jax.nn.sigmoid(xc)
        dt = jnp.concatenate([dt_scr[pl.ds(r0 + par, half, stride=2), :] for par in range(2)], axis=0)
        y = _ssd_chunk(xc, dt, p, e2_ref[...], lambda b, g: s_scr[g], s_set, times((q, 1), 0), times((1, q), 1))
        for j in range(y_scr.shape[0]):
            for par in range(2):
                y_scr[j, pl.ds(par, half, stride=2), :] = y[par * half:(par + 1) * half, j * LANES:(j + 1) * LANES]
        y = jnp.concatenate([y_scr[j] for j in range(y_scr.shape[0])], axis=1)
        yb_ref[0, r0:r0 + q, :] = _ssd_gate_norm(y, z_ref[0, r0:r0 + q, :], p).astype(yb_ref.dtype)

    @pl.when(c == pl.num_programs(1) - 1)
    def _():
        sout_ref[0] = s_scr[...]
        bufout_ref[0] = x[rows - (CONV_W - 1):rows]


def _ssd_prompt(proj3, sp, col_xbc, col_z, col_dt):
    b, l, _ = proj3.shape
    cdim = sp["ssd_conv_w"].shape[1]
    d_inner = sp["ssd_norm_g"].shape[1]
    gw = d_inner // SSD_GROUPS
    q = SSD_CHUNK if l % SSD_CHUNK == 0 else l
    assert q % (2 * SUBLANES) == 0
    rows = q * SSD_CHUNKS_PER_STEP if l % (q * SSD_CHUNKS_PER_STEP) == 0 else q
    return pl.pallas_call(
        functools.partial(_ssd_prompt_body, q=q),
        grid=(b, l // rows),
        in_specs=[pl.BlockSpec((1, rows, cdim), lambda i, c: (i, c, col_xbc // cdim)),
                  pl.BlockSpec((1, rows, d_inner), lambda i, c: (i, c, col_z // d_inner)),
                  pl.BlockSpec((1, rows, LANES), lambda i, c: (i, c, col_dt // LANES)),
                  _const_spec((CONV_W, cdim)), _const_spec((1, cdim)), _const_spec((1, LANES)),
                  _const_spec((1, LANES)), _const_spec((1, d_inner)), _const_spec((1, d_inner)),
                  _const_spec((2 * LANES, d_inner))],
        out_specs=[pl.BlockSpec((1, rows, d_inner), lambda i, c: (i, c, 0)),
                   pl.BlockSpec((1, SSD_GROUPS, gw, SSD_STATE), lambda i, c: (i, 0, 0, 0)),
                   pl.BlockSpec((1, CONV_W - 1, cdim), lambda i, c: (i, 0, 0))],
        out_shape=[jax.ShapeDtypeStruct((b, l, d_inner), BF16),
                   jax.ShapeDtypeStruct((b, SSD_GROUPS, gw, SSD_STATE), F32),
                   jax.ShapeDtypeStruct((b, CONV_W - 1, cdim), F32)],
        scratch_shapes=[pltpu.VMEM((cdim // LANES, rows + SUBLANES, LANES), F32),
                        pltpu.VMEM((rows, LANES), F32),
                        pltpu.VMEM((d_inner // LANES, q, LANES), F32),
                        pltpu.VMEM((SSD_GROUPS, gw, SSD_STATE), F32)],
        compiler_params=_cparams(("parallel", "arbitrary")),
        name="ssd_prompt",
    )(proj3, proj3, proj3, sp["ssd_conv_w"], sp["ssd_conv_b"], sp["dt_bias"], sp["A"], sp["D"], sp["ssd_norm_g"],
      sp["head_expand"])


def _ssd_step_body(xbc_ref, z_ref, dt_ref, buf_ref, s_ref, cw_ref, cb_ref, dtb_ref, a_ref, d_ref, ng_ref, e2_ref,
                   yb_ref, sout_ref, bufout_ref, *, steps, nb):
    p = _ssd_params(cw_ref, cb_ref, dtb_ref, a_ref, d_ref, ng_ref)
    rps = SUBLANES
    q = nb * rps
    x_all = xbc_ref[...].astype(F32)
    z_all = z_ref[...].astype(F32)
    dt_all = _softplus(dt_ref[...].astype(F32) + p["dt_bias"])
    cdim = x_all.shape[1]

    def padded(v, j):
        return jnp.concatenate([v[j * steps:(j + 1) * steps], jnp.zeros((rps - steps, v.shape[1]), v.dtype)], axis=0)

    xcs = []
    for j in range(nb):
        xpad = jnp.concatenate([jnp.zeros((SUBLANES - (CONV_W - 1), cdim), F32), buf_ref[j], padded(x_all, j)],
                               axis=0)
        xcs.append(_ssd_conv(xpad, rps, p["cw"], p["cb"]))
        bufout_ref[j] = xpad[SUBLANES + steps - (CONV_W - 1):SUBLANES + steps]
    xc = jnp.concatenate(xcs, axis=0)
    dt = jnp.concatenate([padded(dt_all, j) for j in range(nb)], axis=0)
    z = jnp.concatenate([padded(z_all, j) for j in range(nb)], axis=0)

    def s_set(b, g, v):
        sout_ref[b, g] = v

    t_col = lax.broadcasted_iota(jnp.int32, (q, 1), 0) % rps
    t_row = lax.broadcasted_iota(jnp.int32, (1, q), 1) % rps
    y = _ssd_chunk(xc, dt, p, e2_ref[...], lambda b, g: s_ref[b, g], s_set, t_col, t_row, n_seg=nb, n_valid=steps)
    y = _ssd_gate_norm(y, z, p)
    for j in range(nb):
        yb_ref[j * steps:(j + 1) * steps, :] = y[j * rps:j * rps + steps].astype(yb_ref.dtype)


def _ssd_step(proj, buf, s0, sp, steps, nb, col_xbc, col_z, col_dt):
    bsz = s0.shape[0]
    cdim = sp["ssd_conv_w"].shape[1]
    d_inner = sp["ssd_norm_g"].shape[1]
    gw = d_inner // SSD_GROUPS
    rows = nb * steps
    body = functools.partial(_ssd_step_body, steps=steps, nb=nb)
    return pl.pallas_call(
        body,
        grid=(bsz // nb,),
        in_specs=[pl.BlockSpec((rows, cdim), lambda i: (i, col_xbc // cdim)),
                  pl.BlockSpec((rows, d_inner), lambda i: (i, col_z // d_inner)),
                  pl.BlockSpec((rows, LANES), lambda i: (i, col_dt // LANES)),
                  pl.BlockSpec((None, nb, CONV_W - 1, cdim), lambda i: (0, i, 0, 0)),
                  pl.BlockSpec((nb, SSD_GROUPS, gw, SSD_STATE), lambda i: (i, 0, 0, 0)),
                  _const_spec((CONV_W, cdim)), _const_spec((1, cdim)), _const_spec((1, LANES)),
                  _const_spec((1, LANES)), _const_spec((1, d_inner)), _const_spec((1, d_inner)),
                  _const_spec((2 * LANES, d_inner))],
        out_specs=[pl.BlockSpec((rows, d_inner), lambda i: (i, 0)),
                   pl.BlockSpec((nb, SSD_GROUPS, gw, SSD_STATE), lambda i: (i, 0, 0, 0)),
                   pl.BlockSpec((None, nb, CONV_W - 1, cdim), lambda i: (0, i, 0, 0))],
        out_shape=[jax.ShapeDtypeStruct((bsz * steps, d_inner), BF16),
                   jax.ShapeDtypeStruct(s0.shape, F32),
                   jax.ShapeDtypeStruct(buf.shape, F32)],
        compiler_params=_cparams(("parallel",)),
        name="ssd_step",
    )(proj, proj, proj, buf, s0, sp["ssd_conv_w"], sp["ssd_conv_b"], sp["dt_bias"], sp["A"], sp["D"],
      sp["ssd_norm_g"], sp["head_expand"])


def _router(t, wr, br):
    t_hi = t.astype(BF16)
    t_lo = (t - t_hi.astype(F32)).astype(BF16)
    both = _dot(jnp.concatenate([t_hi, t_lo], axis=1), wr)
    logits = both[:, :LANES] + both[:, LANES:] + br
    lane = lax.broadcasted_iota(jnp.int32, logits.shape, 1)
    neg = -jnp.inf
    gl = jnp.where(lane < N_EGROUPS, logits, neg)
    gmax = jnp.max(gl, axis=-1, keepdims=True)
    g_idx = jnp.min(jnp.where(gl == gmax, lane, LANES), axis=-1, keepdims=True)
    g_w = 1.0 / jnp.sum(jnp.exp(gl - gmax), axis=-1, keepdims=True)
    in_grp = jnp.logical_and(jnp.logical_and(lane >= ROUTER_LANE0, lane < ROUTER_LANE0 + N_EXPERTS),
                             ((lane - ROUTER_LANE0) >> 2) == g_idx)
    el = jnp.where(in_grp, logits, neg)
    pe = jnp.exp(el - jnp.max(el, axis=-1, keepdims=True))
    pe = pe / jnp.sum(pe, axis=-1, keepdims=True)
    cand = jnp.where(in_grp, pe, -1.0)
    v1 = jnp.max(cand, axis=-1, keepdims=True)
    i1 = jnp.min(jnp.where(cand == v1, lane, LANES), axis=-1, keepdims=True)
    cand2 = jnp.where(lane == i1, -1.0, cand)
    v2 = jnp.max(cand2, axis=-1, keepdims=True)
    i2 = jnp.min(jnp.where(jnp.logical_and(cand2 == v2, in_grp), lane, LANES), axis=-1, keepdims=True)
    den = v1 + v2
    return lane, i1, i2, g_w * v1 / den, g_w * v2 / den


def _rows_to_tiles(ref, val):
    n, d = val.shape
    for k in range(d // LANES):
        ref[pl.ds(k, n, stride=d // LANES), :] = val[:, k * LANES:(k + 1) * LANES]


def _tiles_to_rows(ref, n, d, start=0):
    return jnp.concatenate([ref[pl.ds(start + k, n, stride=d // LANES), :] for k in range(d // LANES)], axis=1)


def _merge_body(x_ref, ya_ref, yb_ref, ga_ref, gb_ref, wl_ref, ws_ref, wo_ref, gf_ref, wr_ref, br_ref, cnt0_ref,
                x1_ref, t_ref, rt_ref, rtt_ref, cnt_ref, base_scr):
    step = pl.program_id(0)

    @pl.when(step == 0)
    def _():
        base_scr[...] = cnt0_ref[...]

    a = _dot(ya_ref[...], wl_ref[...])
    b = _dot(yb_ref[...], ws_ref[...])
    merged = jax.nn.sigmoid(ga_ref[...].astype(F32)) * a + jax.nn.sigmoid(gb_ref[...].astype(F32)) * b
    x1 = x_ref[...] + _dot(merged.astype(BF16), wo_ref[...])
    t = _rms(x1, gf_ref[...])
    lane, i1, i2, wg1, wg2 = _router(t, wr_ref[...], br_ref[...])

    tm = t.shape[0]
    onehot = jnp.where(jnp.logical_or(lane == i1, lane == i2), 1.0, 0.0).astype(BF16)
    tri = (lax.broadcasted_iota(jnp.int32, (tm, tm), 1) <= lax.broadcasted_iota(jnp.int32, (tm, tm), 0)).astype(BF16)
    cum = _dot(tri, onehot) + base_scr[...]
    r1 = jnp.sum(jnp.where(lane == i1, cum, 0.0), axis=-1, keepdims=True) - 1.0
    r2 = jnp.sum(jnp.where(lane == i2, cum, 0.0), axis=-1, keepdims=True) - 1.0
    cols = (wg1, wg2, r1, r2, (i1 - ROUTER_LANE0).astype(F32), (i2 - ROUTER_LANE0).astype(F32))
    rt = jnp.zeros(cum.shape, F32)
    for k, c in enumerate(cols):
        rt = jnp.where(lane == k, c, rt)
    rt_ref[...] = rt
    rtt_ref[...] = rt.T[0:SUBLANES, :]
    base_scr[...] = cum[tm - 1:tm, :]
    cnt_ref[...] = cum[tm - 1:tm, :]
    x1_ref[...] = x1
    _rows_to_tiles(t_ref, t)


def _merge(x, ya, yb, proj, mw, tm, col_ga, col_gb, counts_so_far):
    t, d = x.shape
    d_inner = yb.shape[1]
    return pl.pallas_call(
        _merge_body,
        grid=(t // tm,),
        in_specs=[pl.BlockSpec((tm, d), lambda i: (i, 0)),
                  pl.BlockSpec((tm, d), lambda i: (i, 0)),
                  pl.BlockSpec((tm, d_inner), lambda i: (i, 0)),
                  pl.BlockSpec((tm, d), lambda i: (i, col_ga // d)),
                  pl.BlockSpec((tm, d), lambda i: (i, col_gb // d)),
                  _const_spec((d, d)), _const_spec((d_inner, d)), _const_spec((d, d)),
                  _const_spec((1, d)), _const_spec((2 * d, 2 * LANES)), _const_spec((1, LANES)),
                  _const_spec((1, LANES))],
        out_specs=[pl.BlockSpec((tm, d), lambda i: (i, 0)),
                   pl.BlockSpec((tm * d // LANES, LANES), lambda i: (i, 0)),
                   pl.BlockSpec((tm, LANES), lambda i: (i, 0)),
                   pl.BlockSpec((SUBLANES, tm), lambda i: (0, i)),
                   pl.BlockSpec((1, LANES), lambda i: (0, 0))],
        out_shape=[jax.ShapeDtypeStruct((t, d), F32),
                   jax.ShapeDtypeStruct((t * d // LANES, LANES), F32),
                   jax.ShapeDtypeStruct((t, LANES), F32),
                   jax.ShapeDtypeStruct((SUBLANES, t), F32),
                   jax.ShapeDtypeStruct((1, LANES), F32)],
        scratch_shapes=[pltpu.VMEM((1, LANES), F32)],
        compiler_params=_cparams(("arbitrary",)),
        name="merge_router",
    )(x, ya, yb, proj, proj, mw["w_br_lru"], mw["w_br_ssd"], mw["w_out"], mw["g_ffn"], mw["w_router"],
      mw["b_router"], counts_so_far)


def _dispatch_body(zb_ref, *rest, n_groups, tms, n_toks, nk, tmg):
    dest_refs, t_refs = rest[:n_groups], rest[n_groups:2 * n_groups]
    o_ref, zero_scr, sem, zsem = rest[2 * n_groups:]
    step = pl.program_id(0)

    @pl.when(step == 0)
    def _():
        zero_scr[...] = jnp.zeros_like(zero_scr)
        blk = tmg * nk

        def zcopy(j):
            return pltpu.make_async_copy(zero_scr, o_ref.at[pl.ds(pl.multiple_of(zb_ref[j] * blk, blk), blk)], zsem)

        for j in range(zb_ref.shape[0]):
            pl.when(zb_ref[j] >= 0)(lambda j=j: zcopy(j).start())
        for j in range(zb_ref.shape[0]):
            pl.when(zb_ref[j] >= 0)(lambda j=j: zcopy(j).wait())

    first_step = 0
    for g in range(n_groups):
        tm, n_tok, steps = tms[g], n_toks[g], n_toks[g] // tms[g]

        @pl.when(jnp.logical_and(step >= first_step, step < first_step + steps))
        def _(g=g, tm=tm, n_tok=n_tok, first_step=first_step):
            t_ref, dest_ref = t_refs[g], dest_refs[g]

            def issue(r, carry):
                src = t_ref.at[pl.ds(pl.multiple_of(r * nk, nk), nk)]
                for k in range(2):
                    row = dest_ref[k * n_tok + (step - first_step) * tm + r]
                    pltpu.make_async_copy(src, o_ref.at[pl.ds(pl.multiple_of(row * nk, nk), nk)],
                                          sem).start(priority=k)
                return carry

            lax.fori_loop(0, tm, issue, 0, unroll=8)
            for k in range(2):
                pltpu.make_async_copy(t_ref, o_ref.at[pl.ds(0, tm * nk)], sem).wait()

        first_step += steps


def _dispatch(dests, zero_blocks, t_tiles, n_toks, tms, n_rows, tmg):
    n_groups = len(dests)
    nk = t_tiles[0].shape[0] // n_toks[0]
    steps = [n // tm for n, tm in zip(n_toks, tms)]
    starts = [sum(steps[:g]) for g in range(n_groups)]
    block_of = lambda g: (lambda i, *_: (jnp.clip(i - starts[g], 0, steps[g] - 1), 0))
    return pl.pallas_call(
        functools.partial(_dispatch_body, n_groups=n_groups, tms=tuple(tms), n_toks=tuple(n_toks), nk=nk, tmg=tmg),
        grid_spec=pltpu.PrefetchScalarGridSpec(
            num_scalar_prefetch=1 + n_groups,
            grid=(sum(steps),),
            in_specs=[pl.BlockSpec((tms[g] * nk, LANES), block_of(g)) for g in range(n_groups)],
            out_specs=pl.BlockSpec(memory_space=pl.ANY),
            scratch_shapes=[pltpu.VMEM((tmg * nk, LANES), F32), pltpu.SemaphoreType.DMA(()),
                            pltpu.SemaphoreType.DMA(())]),
        out_shape=jax.ShapeDtypeStruct((n_rows * nk, LANES), F32),
        compiler_params=_cparams(("arbitrary",)),
        name="moe_dispatch",
    )(zero_blocks, *dests, *t_tiles)


def _expert_body(te_ref, nt_ref, x_ref, w1_ref, w3_ref, w2_ref, y_ref, w1_scr, w3_scr, w2_scr, *, tmg):
    i = pl.program_id(0)
    real = i < nt_ref[0]
    d = w1_scr.shape[0]

    @pl.when(jnp.logical_or(i == 0, te_ref[i] != te_ref[jnp.maximum(i - 1, 0)]))
    def _():
        w1_scr[...] = w1_ref[0].astype(BF16)
        w3_scr[...] = w3_ref[0].astype(BF16)
        w2_scr[...] = w2_ref[0].astype(BF16)

    @pl.when(real)
    def _():
        x = _tiles_to_rows(x_ref, tmg, d).astype(BF16)
        h1 = _dot(x, w1_scr[...])
        h3 = _dot(x, w3_scr[...])
        _rows_to_tiles(y_ref, _dot((h1 * jax.nn.sigmoid(h1) * h3).astype(BF16), w2_scr[...]))

    @pl.when(jnp.logical_not(real))
    def _():
        y_ref[...] = jnp.zeros_like(y_ref)


def _experts(tile_expert, n_tiles, xs_tiles, w1, w3, w2, tmg):
    _, d, dff = w1.shape
    blk = tmg * d // LANES
    row_spec = pl.BlockSpec((blk, LANES), lambda i, te, nt: (i, 0))
    return pl.pallas_call(
        functools.partial(_expert_body, tmg=tmg),
        grid_spec=pltpu.PrefetchScalarGridSpec(
            num_scalar_prefetch=2,
            grid=(xs_tiles.shape[0] // blk,),
            in_specs=[row_spec,
                      pl.BlockSpec((1, d, dff), lambda i, te, nt: (te[i], 0, 0)),
                      pl.BlockSpec((1, d, dff), lambda i, te, nt: (te[i], 0, 0)),
                      pl.BlockSpec((1, dff, d), lambda i, te, nt: (te[i], 0, 0))],
            out_specs=row_spec,
            scratch_shapes=[pltpu.VMEM((d, dff), BF16), pltpu.VMEM((d, dff), BF16), pltpu.VMEM((dff, d), BF16)]),
        out_shape=jax.ShapeDtypeStruct(xs_tiles.shape, F32),
        compiler_params=_cparams(("arbitrary",)),
        name="moe_experts",
    )(tile_expert, n_tiles, xs_tiles, w1, w3, w2)


def _ple_body(dest_ref, x_ref, rt_ref, p_ref, wp_ref, gp_ref, gg_ref, wg_ref, gfin_ref, y_hbm, o_ref, gbuf, sem,
              *, n_tok):
    step = pl.program_id(0)
    n_steps = pl.num_programs(0)
    tm, d = x_ref.shape
    nk = d // LANES
    pr = tm // COMBINE_PIECES

    def issue_row(tile, slot, r):
        tok = tile * tm + r
        for k in range(2):
            row = dest_ref[k * n_tok + tok]
            pltpu.make_async_copy(y_hbm.at[pl.ds(pl.multiple_of(row * nk, nk), nk)],
                                  gbuf.at[slot, pl.ds(pl.multiple_of((k * tm + r) * nk, nk), nk)],
                                  sem.at[slot]).start(priority=k)

    def wait_slot(slot):
        pltpu.make_async_copy(y_hbm.at[pl.ds(0, 2 * tm * nk)], gbuf.at[slot], sem.at[slot]).wait()

    n_slots = gbuf.shape[0]
    last_tile = n_steps - 1

    @pl.when(step == 0)
    def _():
        for first in range(n_slots - 1):
            def issue(r, carry, first=first):
                issue_row(jnp.minimum(first, last_tile), first, r)
                return carry

            lax.fori_loop(0, tm, issue, 0, unroll=8)

    slot = step % n_slots
    wait_slot(slot)
    nxt_tile = jnp.minimum(step + n_slots - 1, last_tile)
    nxt_slot = (step + n_slots - 1) % n_slots
    rows = gbuf.at[slot]
    for q in range(COMBINE_PIECES):
        sl = slice(q * pr, (q + 1) * pr)
        rt = rt_ref[sl, :]
        x = (x_ref[sl, :] + rt[:, 0:1] * _tiles_to_rows(rows, pr, d, start=q * pr * nk)
             + rt[:, 1:2] * _tiles_to_rows(rows, pr, d, start=(tm + q * pr) * nk))
        e = _rms(_dot(p_ref[sl, :].astype(BF16), wp_ref[...]), gp_ref[...])
        gate = jax.nn.sigmoid(_dot(_rms(x, gg_ref[...]).astype(BF16), wg_ref[...]))
        o_ref[sl, :] = _rms(x + gate * e, gfin_ref[...])
        for r in range(q * pr, (q + 1) * pr):
            issue_row(nxt_tile, nxt_slot, r)

    @pl.when(step == last_tile)
    def _():
        for ahead in range(1, n_slots):
            wait_slot((step + ahead) % n_slots)


def _ple(dest, x1, rt, p, y_sorted, pw, tm):
    n, d = x1.shape
    dp = p.shape[1]
    const = lambda shape: pl.BlockSpec(shape, lambda i, *_: (0,) * len(shape))
    return pl.pallas_call(
        functools.partial(_ple_body, n_tok=n),
        grid_spec=pltpu.PrefetchScalarGridSpec(
            num_scalar_prefetch=1,
            grid=(n // tm,),
            in_specs=[pl.BlockSpec((tm, d), lambda i, *_: (i, 0)),
                      pl.BlockSpec((tm, LANES), lambda i, *_: (i, 0)),
                      pl.BlockSpec((tm, dp), lambda i, *_: (i, 0)),
                      const((dp, d)), const((1, d)), const((1, d)), const((d, d)), const((1, d)),
                      pl.BlockSpec(memory_space=pl.ANY)],
            out_specs=pl.BlockSpec((tm, d), lambda i, *_: (i, 0)),
            scratch_shapes=[pltpu.VMEM((COMBINE_SLOTS, 2 * tm * d // LANES, LANES), F32),
                            pltpu.SemaphoreType.DMA((COMBINE_SLOTS,))]),
        out_shape=jax.ShapeDtypeStruct((n, d), F32),
        compiler_params=_cparams(("arbitrary",)),
        name="combine_ple_final",
    )(dest, x1, rt, p, pw["w_ple_proj"], pw["g_ple"], pw["g_ple_gate"], pw["w_ple_gate"], pw["g_final"], y_sorted)


def _pick_tile(n, pref):
    t = min(n, pref)
    while n % t:
        t //= 2
    return t


def _moe_row_tile(n):
    return 512 if 2 * n // N_EXPERTS >= 1024 else 128
MOE_DISPATCH_TILE = 4096
COMBINE_TILE = 512
COMBINE_SLOTS = 3
SSD_CHUNKS_PER_STEP = 4
COMBINE_PIECES = 1
LRU_SLICE_ROWS = 128
LRU_PIECES = 2


def _split_router(w):
    hi = w.astype(BF16)
    lo = (w - hi.astype(F32)).astype(BF16)
    return jnp.concatenate([jnp.concatenate([hi, lo], axis=1),
                            jnp.concatenate([hi, jnp.zeros_like(hi)], axis=1)], axis=0)


def _token_tail(groups, cnt, lw):
    n_all = sum(g["x1"].shape[0] for g in groups)
    tmg = _moe_row_tile(n_all)
    n_blocks = pl.cdiv(2 * n_all, tmg) + N_EXPERTS
    counts = cnt[0, ROUTER_LANE0:ROUTER_LANE0 + N_EXPERTS].astype(jnp.int32)
    tiles = (counts + tmg - 1) // tmg
    ends = jnp.cumsum(tiles)
    n_tiles = ends[-1]
    first_row = (ends - tiles) * tmg
    blk = jnp.arange(n_blocks, dtype=jnp.int32)
    tile_expert = jnp.sum((jnp.minimum(blk, n_tiles - 1)[:, None] >= ends[None, :]).astype(jnp.int32), axis=1)
    tail = n_tiles + blk[:N_EXPERTS]
    zero_blocks = jnp.concatenate([jnp.where(tiles > 0, ends - 1, -1),
                                   jnp.where(tail < n_blocks, tail, -1)]).astype(jnp.int32)

    dests = []
    for g in groups:
        e_idx = g["rtt"][4:6].astype(jnp.int32)
        expert = jnp.arange(N_EXPERTS, dtype=jnp.int32)[:, None, None]
        dest = g["rtt"][2:4].astype(jnp.int32) + jnp.sum(
            jnp.where(e_idx[None] == expert, first_row[:, None, None], 0), axis=0)
        dests.append(dest.reshape(2 * g["x1"].shape[0]))
    n_toks = [g["x1"].shape[0] for g in groups]
    sorted_t = _dispatch(dests, zero_blocks, [g["t_tiles"] for g in groups], n_toks,
                         [_pick_tile(n, MOE_DISPATCH_TILE) for n in n_toks], n_blocks * tmg, tmg)
    y_sorted = _experts(tile_expert, n_tiles.reshape(1), sorted_t, lw["w1"], lw["w3"], lw["w2"], tmg)
    return [_ple(dest, g["x1"], g["rt"], g["p"], y_sorted, lw, g["tm"]) for g, dest in zip(groups, dests)]


def kernel(x_prompt, x_sample, state_lru_h, state_lru_conv, state_ssd, state_ssd_conv, p_prompt, p_sample, g_mix, w_in, lru_conv_w, lru_conv_b, lru_wa, lru_ba, lru_wx, lru_bx, lru_lambda, ssd_conv_w, ssd_conv_b, ssd_dt_bias, ssd_A_log, ssd_D, ssd_norm_g, w_br_lru, w_br_ssd, w_out, g_ffn, w_router_g, b_router_g, w_router_e, b_router_e, w1, w3, w2, w_ple_proj, g_ple, g_ple_gate, w_ple_gate, g_final):
    depth = w_in.shape[0]
    assert depth == 1, "one decoder layer per call"
    bp, lp, d = x_prompt.shape
    bs, ls, _ = x_sample.shape
    w_lru = state_lru_h.shape[-1]
    heads, hdim, nstate = state_ssd.shape[2:]
    d_inner = heads * hdim
    cdim = state_ssd_conv.shape[-1]
    assert hdim == SSD_HEADDIM and nstate == SSD_STATE and heads <= LANES and ls < SUBLANES
    gw = d_inner // SSD_GROUPS

    o_dt = 2 * w_lru + d_inner + cdim
    n_proj = o_dt + 2 * d + LANES
    tn = n_proj // 9 if n_proj % (9 * LANES) == 0 else LANES
    n_head = o_dt // tn
    w_head = w_in.astype(BF16)
    wi = w_head[0]
    w_tail = jnp.concatenate([wi[:, n_head * tn:o_dt], wi[:, o_dt + heads:], wi[:, o_dt:o_dt + heads],
                              jnp.zeros((d, LANES - heads), BF16)], axis=1)
    assert n_head >= 1 and w_tail.shape[1] == n_proj - n_head * tn and w_tail.shape[1] % tn == 0
    col_z, col_xbc = 2 * w_lru, 2 * w_lru + d_inner
    col_ga, col_gb, col_dt = o_dt, o_dt + d, o_dt + 2 * d
    row = lambda v: v.reshape(1, -1).astype(F32)
    pad_heads = lambda v: jnp.pad(v.astype(F32), (0, LANES - heads)).reshape(1, LANES)
    lw = {
        "lru_conv_w": lru_conv_w[0], "lru_conv_b": row(lru_conv_b[0]),
        "wax": jnp.concatenate([lru_wa[0], lru_wx[0]], axis=-1).astype(BF16),
        "lru_ba": row(lru_ba[0]), "lru_bx": row(lru_bx[0]), "lru_lambda": row(lru_lambda[0]),
        "ssd_conv_w": ssd_conv_w[0], "ssd_conv_b": row(ssd_conv_b[0]),
        "dt_bias": pad_heads(ssd_dt_bias[0]), "A": pad_heads(-jnp.exp(ssd_A_log[0].astype(F32))),
        "D": row(jnp.repeat(ssd_D[0], hdim)), "ssd_norm_g": row(ssd_norm_g[0]),
        "head_expand": jnp.tile(jnp.arange(LANES)[:, None] == jnp.arange(d_inner)[None, :] // hdim, (2, 1)).astype(BF16),
        "w_br_lru": w_br_lru[0].astype(BF16), "w_br_ssd": w_br_ssd[0].astype(BF16), "w_out": w_out[0].astype(BF16),
        "g_ffn": row(g_ffn[0]),
        "w_router": _split_router(jnp.concatenate([w_router_g[0], w_router_e[0],
                                                   jnp.zeros((d, LANES - N_EGROUPS - N_EXPERTS), F32)], axis=1)),
        "b_router": jnp.concatenate([b_router_g[0], b_router_e[0],
                                     jnp.zeros((LANES - N_EGROUPS - N_EXPERTS,), F32)]).reshape(1, LANES),
        "w1": w1[0], "w3": w3[0], "w2": w2[0],
        "w_ple_proj": w_ple_proj[0].astype(BF16), "g_ple": row(g_ple[0]), "g_ple_gate": row(g_ple_gate[0]),
        "w_ple_gate": w_ple_gate[0].astype(BF16), "g_final": row(g_final),
    }
    g_mix_r = row(g_mix[0])

    tp = bp * lp
    xp = x_prompt.reshape(tp, d)
    tm_p = _pick_tile(lp, 1024)
    proj_p, ya_p, hl_p, lbuf_p = _inproj_lru(xp, g_mix_r, w_head, w_tail, lw, lp, tm_p, tn, n_head,
                                             _pick_tile(tm_p, LRU_SLICE_ROWS))
    proj_p3 = proj_p.reshape(bp, lp, n_proj)
    yb_p, s_p, sbuf_p = _ssd_prompt(proj_p3, lw, col_xbc, col_z, col_dt)
    tm_tail_p = _pick_tile(tp, 512)
    x1_p, t_p, rt_p, rtt_p, cnt_p = _merge(xp, ya_p.reshape(tp, w_lru), yb_p.reshape(tp, d_inner), proj_p, lw,
                                           tm_tail_p, col_ga, col_gb, jnp.zeros((1, LANES), F32))
    group_p = {"x1": x1_p, "t_tiles": t_p, "rt": rt_p, "rtt": rtt_p, "p": p_prompt[0].reshape(tp, -1),
               "tm": _pick_tile(tp, COMBINE_TILE)}

    ts = bs * ls
    xs = x_sample.reshape(ts, d)
    tm_s = _pick_tile(ts, 512)
    proj_s = _inproj(xs, g_mix_r, w_head, w_tail, tm_s, tn, n_head)
    to_tmajor = lambda v, n: v.reshape(bs, n, -1).transpose(1, 0, 2).reshape(n * bs, -1)
    from_tmajor = lambda v, n: v.reshape(n, bs, -1).transpose(1, 0, 2)
    ya_t, hl_s, lbuf_t = _lru_step(to_tmajor(proj_s[:, :w_lru], ls), to_tmajor(proj_s[:, w_lru:2 * w_lru], ls),
                                   to_tmajor(state_lru_conv[0], CONV_W - 1), state_lru_h[0], lw, ls)
    ya_s = from_tmajor(ya_t, ls).reshape(ts, w_lru)
    lbuf_s = from_tmajor(lbuf_t, CONV_W - 1)
    yb_s, s_s, sbuf_s = _ssd_step(proj_s, state_ssd_conv, state_ssd[0].reshape(bs, SSD_GROUPS, gw, nstate), lw,
                                  ls, _pick_tile(bs, 8), col_xbc, col_z, col_dt)
    x1_s, t_s, rt_s, rtt_s, cnt_all = _merge(xs, ya_s, yb_s, proj_s, lw, tm_s, col_ga, col_gb, cnt_p)
    group_s = {"x1": x1_s, "t_tiles": t_s, "rt": rt_s, "rtt": rtt_s, "p": p_sample[0].reshape(ts, -1), "tm": tm_s}

    y_p, y_s = _token_tail([group_p, group_s], cnt_all, lw)

    return (y_p.reshape(bp, lp, d), y_s.reshape(bs, ls, d),
            hl_p.reshape(1, bp, w_lru), lbuf_p[None],
            s_p.reshape(1, bp, heads, hdim, nstate), sbuf_p[None],
            hl_s[None], lbuf_s[None],
            s_s.reshape(1, bs, heads, hdim, nstate), sbuf_s)
```
